```python
import jax, jax.numpy as jnp
from jax import lax
import numpy as np

D_MODEL = 2048
BATCH = 1
SEQ = 8192
DEPTH = 1

HEAD_DIM = 128
DSA_HEADS = 8
MOBA_HEADS = 8
DSA_W = DSA_HEADS * HEAD_DIM
MOBA_W = MOBA_HEADS * HEAD_DIM
ROPE_DIM = HEAD_DIM // 4
ROPE_THETA = 500000.0
IDX_HEADS = 16
IDX_DIM = 64
IDX_ROPE_DIM = IDX_DIM // 4
DSA_TOPK = 256
MOBA_BLOCK = 256
MOBA_TOPK = 3
Q_BLOCK = 128
N_EXPERTS = 32
EXPERT_TOPK = 4
D_FF = 2048
SWIGLU_LIMIT = 7.0
SWIGLU_ALPHA = 1.702
PLE_DIM = 256
MOE_ROW_BLOCK = 256
EPS = 1e-6
IN_SIZES = (DSA_W, DSA_W, DSA_W, IDX_HEADS * IDX_DIM, IDX_DIM, IDX_HEADS,
            MOBA_W, MOBA_W, MOBA_W, D_MODEL, D_MODEL)
IN_WIDTH = sum(IN_SIZES)

kernel_name = "hybrid_dsa_moba_gated_moe_ple"


def rms_norm(x, g):
    xf = x.astype(jnp.float32)
    y = xf * lax.rsqrt(jnp.mean(xf * xf, axis=-1, keepdims=True) + EPS)
    return (y * g.astype(jnp.float32)).astype(x.dtype)


def layer_norm(x, g, b):
    xf = x.astype(jnp.float32)
    mu = jnp.mean(xf, axis=-1, keepdims=True)
    var = jnp.mean(jnp.square(xf - mu), axis=-1, keepdims=True)
    y = (xf - mu) * lax.rsqrt(var + EPS)
    return (y * g.astype(jnp.float32) + b.astype(jnp.float32)).astype(x.dtype)


def rope(x, pos, rot_dim):
    half = rot_dim // 2
    inv = 1.0 / (ROPE_THETA ** (jnp.arange(half, dtype=jnp.float32) / half))
    ang = pos.astype(jnp.float32)[:, None] * inv[None, :]
    cos = jnp.cos(ang)[:, None, :].astype(x.dtype)
    sin = jnp.sin(ang)[:, None, :].astype(x.dtype)
    x1 = x[..., :half]
    x2 = x[..., half:rot_dim]
    return jnp.concatenate([x1 * cos - x2 * sin, x2 * cos + x1 * sin, x[..., rot_dim:]], axis=-1)


def dsa_attention(q, k, v, q_idx, k_idx, w_idx):
    B, S, H, dh = q.shape
    n_sel = min(DSA_TOPK, S // 4)
    key_pos = jnp.arange(S)
    scale = dh ** -0.5

    def block(qb):
        start = qb * Q_BLOCK
        qpos = start + jnp.arange(Q_BLOCK)
        qi = lax.dynamic_slice_in_dim(q_idx, start, Q_BLOCK, axis=1)
        wi = lax.dynamic_slice_in_dim(w_idx, start, Q_BLOCK, axis=1)
        logits = jnp.einsum('bqhd,bsd->bqhs', qi, k_idx)
        score = jnp.einsum('bqhs,bqh->bqs', jax.nn.relu(logits), wi).astype(jnp.float32)
        score = jnp.where((key_pos[None, :] <= qpos[:, None])[None], score, -jnp.inf)
        _, sel = lax.top_k(score, n_sel)
        valid = sel <= qpos[None, :, None]
        k_sel = jax.vmap(lambda kb, ib: kb[ib])(k, sel)
        v_sel = jax.vmap(lambda vb, ib: vb[ib])(v, sel)
        qq = lax.dynamic_slice_in_dim(q, start, Q_BLOCK, axis=1)
        s = jnp.einsum('bqhd,bqkhd->bhqk', qq, k_sel).astype(jnp.float32) * scale
        s = jnp.where(valid[:, None], s, -jnp.inf)
        pr = jax.nn.softmax(s, axis=-1).astype(v.dtype)
        return jnp.einsum('bhqk,bqkhd->bqhd', pr, v_sel)

    out = lax.map(block, jnp.arange(S // Q_BLOCK))
    return out.transpose(1, 0, 2, 3, 4).reshape(B, S, H * dh)


def moba_attention(q, k, v):
    B, S, H, dh = q.shape
    nkb = -(-S // MOBA_BLOCK)
    s_pad = nkb * MOBA_BLOCK
    n_top = min(MOBA_TOPK, nkb)
    scale = dh ** -0.5
    pad = ((0, 0), (0, s_pad - S), (0, 0), (0, 0))
    k_blocks = jnp.pad(k, pad).reshape(B, nkb, MOBA_BLOCK, H, dh).transpose(0, 3, 1, 2, 4)
    v_blocks = jnp.pad(v, pad).reshape(B, nkb, MOBA_BLOCK, H, dh).transpose(0, 3, 1, 2, 4)
    k_mean = jnp.mean(k_blocks.astype(jnp.float32), axis=3)
    blk_idx = jnp.arange(nkb)
    gather2 = jax.vmap(jax.vmap(lambda kb, ib: kb[ib]))

    def block(qb):
        start = qb * Q_BLOCK
        qpos = start + jnp.arange(Q_BLOCK)
        own = start // MOBA_BLOCK
        qq = lax.dynamic_slice_in_dim(q, start, Q_BLOCK, axis=1)
        gate = jnp.einsum('bqhd,bhnd->bhqn', qq.astype(jnp.float32), k_mean)
        gate = jnp.where(blk_idx < own, gate, -jnp.inf)
        _, sel = lax.top_k(gate, n_top)
        sel_valid = sel < own
        k_sel = gather2(k_blocks, sel)
        v_sel = gather2(v_blocks, sel)
        s_sel = jnp.einsum('bqhd,bhqnkd->bhqnk', qq, k_sel).astype(jnp.float32) * scale
        s_sel = jnp.where(sel_valid[..., None], s_sel, -jnp.inf).reshape(B, H, Q_BLOCK, n_top * MOBA_BLOCK)
        k_own = lax.dynamic_index_in_dim(k_blocks, own, axis=2, keepdims=False)
        v_own = lax.dynamic_index_in_dim(v_blocks, own, axis=2, keepdims=False)
        s_own = jnp.einsum('bqhd,bhkd->bhqk', qq, k_own).astype(jnp.float32) * scale
        own_pos = own * MOBA_BLOCK + jnp.arange(MOBA_BLOCK)
        s_own = jnp.where(own_pos[None, :] <= qpos[:, None], s_own, -jnp.inf)
        pr = jax.nn.softmax(jnp.concatenate([s_sel, s_own], axis=-1), axis=-1).astype(v.dtype)
        p_sel = pr[..., :n_top * MOBA_BLOCK].reshape(B, H, Q_BLOCK, n_top, MOBA_BLOCK)
        p_own = pr[..., n_top * MOBA_BLOCK:]
        return (jnp.einsum('bhqnk,bhqnkd->bqhd', p_sel, v_sel)
                + jnp.einsum('bhqk,bhkd->bqhd', p_own, v_own))

    out = lax.map(block, jnp.arange(S // Q_BLOCK))
    return out.transpose(1, 0, 2, 3, 4).reshape(B, S, H * dh)


def moe(h, w_router, b_router, w_gate_up, b_gate_up, w_down, b_down):
    B, S, D = h.shape
    T = B * S
    xt = h.reshape(T, D)
    logits = (xt @ w_router + b_router).astype(jnp.float32)
    top_val, top_idx = lax.top_k(logits, EXPERT_TOPK)
    gates = jax.nn.softmax(top_val, axis=-1)
    n_slots = T * EXPERT_TOPK
    e_flat = top_idx.reshape(-1)
    tok_flat = jnp.arange(n_slots, dtype=jnp.int32) // EXPERT_TOPK
    g_flat = gates.reshape(-1)
    order = jnp.argsort(e_flat)
    e_sorted = e_flat[order]
    counts = jnp.bincount(e_flat, length=N_EXPERTS)
    padded = (counts + MOE_ROW_BLOCK - 1) // MOE_ROW_BLOCK * MOE_ROW_BLOCK
    pad_end = jnp.cumsum(padded)
    pad_start = pad_end - padded
    start = jnp.cumsum(counts) - counts
    dest = pad_start[e_sorted] + jnp.arange(n_slots, dtype=jnp.int32) - start[e_sorted]
    n_blocks = -(-n_slots // MOE_ROW_BLOCK) + N_EXPERTS
    n_rows = n_blocks * MOE_ROW_BLOCK
    row_tok = jnp.zeros((n_rows,), jnp.int32).at[dest].set(tok_flat[order])
    row_gate = jnp.zeros((n_rows,), jnp.float32).at[dest].set(g_flat[order])
    block_expert = jnp.minimum(
        jnp.searchsorted(pad_end, jnp.arange(n_blocks) * MOE_ROW_BLOCK, side='right'), N_EXPERTS - 1)
    x_rows = xt[row_tok].reshape(n_blocks, MOE_ROW_BLOCK, D)

    def expert_block(args):
        xb, e = args
        gu = xb @ w_gate_up[e] + b_gate_up[e]
        g = jnp.minimum(gu[..., ::2], SWIGLU_LIMIT)
        lin = jnp.clip(gu[..., 1::2], -SWIGLU_LIMIT, SWIGLU_LIMIT)
        glu = g * jax.nn.sigmoid(g * SWIGLU_ALPHA)
        return ((lin + 1.0) * glu) @ w_down[e] + b_down[e]

    y = lax.map(expert_block, (x_rows, block_expert)).reshape(n_rows, D)
    out = jnp.zeros((T, D), xt.dtype).at[row_tok].add(y * row_gate[:, None].astype(y.dtype))
    return out.reshape(B, S, D)


def setup_inputs(seed: int = 0) -> dict:
    key = jax.random.key(seed)
    ks = jax.random.split(key, 24)
    L, D = DEPTH, D_MODEL
    nrm = lambda k, shape, fan: jax.random.normal(k, shape, jnp.float32) * (fan ** -0.5)
    gain = lambda k, shape: 1.0 + 0.02 * jax.random.normal(k, shape, jnp.float32)
    small = lambda k, shape, s: s * jax.random.normal(k, shape, jnp.float32)
    return {
        "x": jax.random.normal(ks[0], (BATCH, SEQ, D), jnp.float32),
        "p": jax.random.normal(ks[1], (DEPTH, BATCH, SEQ, PLE_DIM), jnp.float32),
        "g_mix": gain(ks[2], (L, D)),
        "w_in": nrm(ks[3], (L, D, IN_WIDTH), D),
        "g_idx_k": gain(ks[4], (L, IDX_DIM)),
        "b_idx_k": small(ks[5], (L, IDX_DIM), 0.02),
        "w_br_a": nrm(ks[6], (L, DSA_W, D), DSA_W),
        "w_br_b": nrm(ks[7], (L, MOBA_W, D), MOBA_W),
        "w_out": nrm(ks[8], (L, D, D), D),
        "g_ffn": gain(ks[9], (L, D)),
        "w_router": nrm(ks[10], (L, D, N_EXPERTS), D),
        "b_router": small(ks[11], (L, N_EXPERTS), 0.01),
        "w_gate_up": nrm(ks[12], (L, N_EXPERTS, D, 2 * D_FF), D),
        "b_gate_up": small(ks[13], (L, N_EXPERTS, 2 * D_FF), 0.02),
        "w_down": nrm(ks[14], (L, N_EXPERTS, D_FF, D), D_FF),
        "b_down": small(ks[15], (L, N_EXPERTS, D), 0.02),
        "w_ple_proj": nrm(ks[16], (L, PLE_DIM, D), PLE_DIM),
        "w_ple_gate": nrm(ks[17], (L, D, D), D),
        "g_ple": gain(ks[18], (L, D)),
        "g_final": gain(ks[19], (D,)),
    }


def reference(x, p, g_mix, w_in, g_idx_k, b_idx_k, w_br_a, w_br_b, w_out, g_ffn,
              w_router, b_router, w_gate_up, b_gate_up, w_down, b_down,
              w_ple_proj, w_ple_gate, g_ple, g_final):
    B, S, D = x.shape
    pos = jnp.arange(S)
    split_pts = [int(c) for c in np.cumsum(IN_SIZES)[:-1]]
    idx_w_scale = (IDX_HEADS ** -0.5) * (IDX_DIM ** -0.5)
    for i in range(DEPTH):
        h = rms_norm(x, g_mix[i])
        proj = h @ w_in[i]
        qa, ka, va, qi, ki, wi, qb, kb, vb, ga, gb = jnp.split(proj, split_pts, axis=-1)
        qa = rope(qa.reshape(B, S, DSA_HEADS, HEAD_DIM), pos, ROPE_DIM)
        ka = rope(ka.reshape(B, S, DSA_HEADS, HEAD_DIM), pos, ROPE_DIM)
        va = va.reshape(B, S, DSA_HEADS, HEAD_DIM)
        qi = rope(qi.reshape(B, S, IDX_HEADS, IDX_DIM), pos, IDX_ROPE_DIM)
        ki = rope(layer_norm(ki, g_idx_k[i], b_idx_k[i])[:, :, None, :], pos, IDX_ROPE_DIM)[:, :, 0, :]
        wi = wi * idx_w_scale
        o_a = dsa_attention(qa, ka, va, qi, ki, wi)
        qb = rope(qb.reshape(B, S, MOBA_HEADS, HEAD_DIM), pos, ROPE_DIM)
        kb = rope(kb.reshape(B, S, MOBA_HEADS, HEAD_DIM), pos, ROPE_DIM)
        vb = vb.reshape(B, S, MOBA_HEADS, HEAD_DIM)
        o_b = moba_attention(qb, kb, vb)
        merged = jax.nn.sigmoid(ga) * (o_a @ w_br_a[i]) + jax.nn.sigmoid(gb) * (o_b @ w_br_b[i])
        x = x + merged @ w_out[i]
        h2 = rms_norm(x, g_ffn[i])
        x = x + moe(h2, w_router[i], b_router[i], w_gate_up[i], b_gate_up[i], w_down[i], b_down[i])
        ple = p[i] @ w_ple_proj[i]
        x = x + rms_norm(jax.nn.sigmoid(x @ w_ple_gate[i]) * ple, g_ple[i])
    return rms_norm(x, g_final)
```

```python
import functools

import numpy as np
import jax
import jax.numpy as jnp
from jax import lax
from jax.experimental import pallas as pl
from jax.experimental.pallas import tpu as pltpu

F32 = jnp.float32
BF16 = jnp.bfloat16
I32 = jnp.int32
NEG_INF = float("-inf")

HEAD_DIM = 128
DSA_HEADS = 8
MOBA_HEADS = 8
ROPE_DIM = HEAD_DIM // 4
ROPE_THETA = 500000.0
IDX_HEADS = 16
IDX_DIM = 64
IDX_ROPE_DIM = IDX_DIM // 4
DSA_TOPK = 256
MOBA_BLOCK = 256
MOBA_TOPK = 3
N_EXPERTS = 32
EXPERT_TOPK = 4
SWIGLU_LIMIT = 7.0
SWIGLU_ALPHA = 1.702
EPS = 1e-6

LANES = 128
VMEM_CAP_BYTES = 60000 * 1024

PROJ_TM = 512
PROJ_TN = 512
IDX_TQ = 128
IDX_CK = 512
ATT_T = 512
POST_TM = 256
MOE_TM = 512
MOE_TF = 512
FIN_TM = 256

INT_MIN = -2 ** 31
KEY_NEG_INF = 0x807FFFFF - 2 ** 32


def _cparams(sem, vmem_bytes):
    return pltpu.CompilerParams(dimension_semantics=sem,
                                vmem_limit_bytes=int(min(vmem_bytes, VMEM_CAP_BYTES)))


def _resident(shape, index_map):
    return pl.BlockSpec(shape, index_map, pipeline_mode=pl.Buffered(1))


def _sigmoid(x):
    return 1.0 / (1.0 + jnp.exp(-x))


def _rms(xf, g):
    ms = jnp.mean(xf * xf, axis=-1, keepdims=True)
    return xf * lax.rsqrt(ms + EPS) * g


def _dot_nt(a, b):
    return lax.dot_general(a, b, (((1,), (1,)), ((), ())), preferred_element_type=F32)


def _rope_tables(seq, rot_dim, period):
    half = rot_dim // 2
    inv = 1.0 / (ROPE_THETA ** (jnp.arange(half, dtype=F32) / half))
    ang = jnp.arange(seq).astype(F32)[:, None] * inv[None, :]
    cos, sin = jnp.cos(ang), jnp.sin(ang)
    z = lambda n: jnp.zeros((seq, n), F32)
    c = jnp.concatenate([cos, cos, jnp.ones((seq, period - rot_dim), F32)], axis=-1)
    s1 = jnp.concatenate([-sin, z(period - half)], axis=-1)
    s2 = jnp.concatenate([z(half), sin, z(period - rot_dim)], axis=-1)
    rep = LANES // period
    return tuple(jnp.tile(t, (1, rep)) for t in (c, s1, s2))


def _rope(a, c, s1, s2, half):
    return a * c + pltpu.roll(a, LANES - half, 1) * s1 + pltpu.roll(a, half, 1) * s2


def _proj_kernel(x_ref, g_ref, w_ref, *rest, half):
    if half is None:
        o_ref, h_ref = rest
    else:
        c_ref, s1_ref, s2_ref, o_ref, h_ref = rest

    @pl.when(pl.program_id(1) == 0)
    def _():
        h_ref[...] = _rms(x_ref[...], g_ref[...]).astype(BF16)

    acc = jnp.dot(h_ref[...], w_ref[...], preferred_element_type=F32)
    if half is None:
        o_ref[...] = acc.astype(o_ref.dtype)
    else:
        c, s1, s2 = c_ref[...], s1_ref[...], s2_ref[...]
        for k in range(acc.shape[1] // LANES):
            sl = slice(k * LANES, (k + 1) * LANES)
            o_ref[:, sl] = _rope(acc[:, sl], c, s1, s2, half).astype(o_ref.dtype)


def _project(x2, g, w, out_dtype, tables=None, half=None):
    seq, d = x2.shape
    n = w.shape[1]
    tm, tn = min(PROJ_TM, seq), min(PROJ_TN, n)
    in_specs = [pl.BlockSpec((tm, d), lambda i, j: (i, 0)),
                pl.BlockSpec((1, d), lambda i, j: (0, 0)),
                pl.BlockSpec((d, tn), lambda i, j: (0, j))]
    args = [x2, g, w]
    if tables is not None:
        in_specs += [pl.BlockSpec((tm, LANES), lambda i, j: (i, 0))] * 3
        args += list(tables)
    vmem = 2 * (tm * d * 4 + d * tn * 2 + tm * tn * 4 + 3 * tm * LANES * 4) + tm * d * 2 + 4 * tm * tn * 4
    return pl.pallas_call(
        functools.partial(_proj_kernel, half=half),
        out_shape=jax.ShapeDtypeStruct((seq, n), out_dtype),
        grid=(seq // tm, n // tn),
        in_specs=in_specs,
        out_specs=pl.BlockSpec((tm, tn), lambda i, j: (i, j)),
        scratch_shapes=[pltpu.VMEM((tm, d), BF16)],
        compiler_params=_cparams(("parallel", "arbitrary"), vmem),
        name="proj_rope" if half is not None else "proj_plain",
    )(*args)


def _proj_kiwi_kernel(x_ref, g_ref, w_ref, lg_ref, lb_ref, c_ref, s1_ref, s2_ref, ki_ref, wi_ref, *, scale):
    h = _rms(x_ref[...], g_ref[...]).astype(BF16)
    acc = jnp.dot(h, w_ref[...], preferred_element_type=F32)
    a = acc[:, :LANES]
    valid = lax.broadcasted_iota(I32, a.shape, 1) < IDX_DIM
    mu = jnp.sum(jnp.where(valid, a, 0.0), axis=-1, keepdims=True) / IDX_DIM
    dlt = jnp.where(valid, a - mu, 0.0)
    var = jnp.sum(dlt * dlt, axis=-1, keepdims=True) / IDX_DIM
    y = dlt * lax.rsqrt(var + EPS) * lg_ref[...] + lb_ref[...]
    ki_ref[...] = _rope(y, c_ref[...], s1_ref[...], s2_ref[...], IDX_ROPE_DIM // 2).astype(BF16)
    wi_ref[...] = acc[:, LANES:LANES + IDX_HEADS] * scale


def _project_kiwi(x2, g, w_kiwi, lg, lb, tables, scale):
    seq, d = x2.shape
    tm = min(PROJ_TM, seq)
    row = lambda i: (i, 0)
    fix = lambda i: (0, 0)
    vmem = 2 * (tm * d * 4 + d * 2 * LANES * 2 + 5 * tm * LANES * 4) + 8 * tm * 2 * LANES * 4 + tm * d * 6
    return pl.pallas_call(
        functools.partial(_proj_kiwi_kernel, scale=scale),
        out_shape=(jax.ShapeDtypeStruct((seq, LANES), BF16), jax.ShapeDtypeStruct((seq, IDX_HEADS), F32)),
        grid=(seq // tm,),
        in_specs=[pl.BlockSpec((tm, d), row), pl.BlockSpec((1, d), fix), pl.BlockSpec((d, 2 * LANES), fix),
                  pl.BlockSpec((1, LANES), fix), pl.BlockSpec((1, LANES), fix),
                  pl.BlockSpec((tm, LANES), row), pl.BlockSpec((tm, LANES), row), pl.BlockSpec((tm, LANES), row)],
        out_specs=(pl.BlockSpec((tm, LANES), row), pl.BlockSpec((tm, IDX_HEADS), row)),
        compiler_params=_cparams(("parallel",), vmem),
        name="proj_kiwi",
    )(x2, g, w_kiwi, lg, lb, *tables)


def _key_to_float(key):
    bits = key ^ ((key >> 31) & 0x7FFFFFFF)
    return lax.bitcast_convert_type(bits, F32)


def _indexer_kernel(qi_ref, ki_ref, wi_ref, out_ref, s_ref, qh_ref, wb_ref, cst_ref, *, n_sel, idx_bits):
    tq, ck = IDX_TQ, IDX_CK
    nslab = ck // LANES
    n_chunks_total = out_ref.shape[1]
    i = pl.program_id(0)
    nc = ((i + 1) * tq + ck - 1) // ck
    lane = lax.broadcasted_iota(I32, (tq, LANES), 1)
    row = i * tq + lax.broadcasted_iota(I32, (tq, LANES), 0)

    low = lane < IDX_DIM
    w = wi_ref[...]
    for p in range(IDX_HEADS // 2):
        pair = qi_ref[:, p * LANES:(p + 1) * LANES].astype(F32)
        qh_ref[2 * p] = jnp.where(low, pair, 0.0).astype(BF16)
        qh_ref[2 * p + 1] = jnp.where(low, pltpu.roll(pair, IDX_DIM, 1), 0.0).astype(BF16)
    for h in range(IDX_HEADS):
        wb_ref[h] = jnp.broadcast_to(w[:, h:h + 1], (tq, LANES))

    def score_chunk(c, carry):
        kc = ki_ref[pl.ds(pl.multiple_of(c * ck, ck), ck), :]
        slabs = [jnp.zeros((tq, LANES), F32) for _ in range(nslab)]
        for h in range(IDX_HEADS):
            logit = _dot_nt(qh_ref[h], kc)
            wbh = wb_ref[h]
            for k in range(nslab):
                slabs[k] = slabs[k] + jnp.maximum(logit[:, k * LANES:(k + 1) * LANES], 0.0) * wbh
        for k in range(nslab):
            col = c * ck + k * LANES + lane
            slabs[k] = jnp.where(col <= row, slabs[k], NEG_INF)
        s_ref[c] = jnp.concatenate(slabs, axis=1)
        return carry

    lax.fori_loop(0, nc, score_chunk, 0)

    def count(pred):
        def body(c, acc):
            sc = s_ref[c]
            for k in range(nslab):
                col = c * ck + k * LANES + lane
                acc = acc + jnp.where(pred(sc[:, k * LANES:(k + 1) * LANES], col), 1.0, 0.0)
            return acc
        acc = lax.fori_loop(0, nc, body, jnp.zeros((tq, LANES), F32))
        return jnp.broadcast_to(jnp.sum(acc, axis=-1, keepdims=True), (tq, LANES))

    def bit_step(b, carry):
        prefix, cnt_at = carry
        cand = prefix + lax.shift_left(jnp.int32(1), 31 - b)
        cand_f = _key_to_float(cand)
        cnt = count(lambda s, col: s >= cand_f)
        take = cnt >= n_sel
        return jnp.where(take, cand, prefix), jnp.where(take, cnt, cnt_at)

    prefix, cnt_at = lax.fori_loop(
        0, 32, bit_step, (jnp.full((tq, LANES), INT_MIN, I32), jnp.full((tq, LANES), float(n_sel), F32)))
    tau = jnp.where(prefix < KEY_NEG_INF, NEG_INF, _key_to_float(prefix))

    need = jnp.logical_and(cnt_at > n_sel, tau > NEG_INF)
    any_tie = jnp.max(jnp.where(need, 1.0, 0.0)) > 0.0
    idx_all = jnp.full((tq, LANES), 2 ** 30, I32)

    @pl.when(any_tie)
    def _():
        rem = n_sel - count(lambda s, col: s > tau)
        cut = jnp.zeros((tq, LANES), I32)
        for b in range(idx_bits - 1, -1, -1):
            cand = cut + (1 << b)
            below = count(lambda s, col: jnp.logical_and(s == tau, col < cand))
            cut = jnp.where(below < rem, cand, cut)
        cst_ref[...] = jnp.where(need, cut, idx_all)

    @pl.when(jnp.logical_not(any_tie))
    def _():
        cst_ref[...] = idx_all

    cut = cst_ref[...]

    def emit(c, carry):
        sc = s_ref[c]
        outs = []
        for k in range(nslab):
            s = sc[:, k * LANES:(k + 1) * LANES]
            col = c * ck + k * LANES + lane
            tie = jnp.where(col <= cut, 0.0, NEG_INF)
            b = jnp.where(s > tau, 0.0, jnp.where(s == tau, tie, NEG_INF))
            outs.append(jnp.where(col <= row, b, NEG_INF))
        out_ref[0, c] = jnp.concatenate(outs, axis=1).astype(BF16)
        return carry

    lax.fori_loop(0, nc, emit, 0)

    def fill(c, carry):
        out_ref[0, c] = jnp.full((tq, ck), NEG_INF, BF16)
        return carry

    lax.fori_loop(nc, n_chunks_total, fill, 0)


def _dsa_mask(qi, ki, wi, n_sel):
    seq = qi.shape[0]
    tq, ck = IDX_TQ, IDX_CK
    nq, nchunk = seq // tq, seq // ck
    idx_bits = max(1, int(seq - 1).bit_length())
    vmem = (2 * (tq * qi.shape[1] * 2 + tq * LANES * 4 + nchunk * tq * ck * 2) + seq * LANES * 2
            + nchunk * tq * ck * 4 + IDX_HEADS * tq * LANES * 6 + tq * LANES * 4 + 24 * tq * ck * 4)
    return pl.pallas_call(
        functools.partial(_indexer_kernel, n_sel=n_sel, idx_bits=idx_bits),
        out_shape=jax.ShapeDtypeStruct((nq, nchunk, tq, ck), BF16),
        grid=(nq,),
        in_specs=[pl.BlockSpec((tq, qi.shape[1]), lambda i: (i, 0)),
                  _resident((seq, LANES), lambda i: (0, 0)),
                  pl.BlockSpec((tq, IDX_HEADS), lambda i: (i, 0))],
        out_specs=pl.BlockSpec((1, nchunk, tq, ck), lambda i: (i, 0, 0, 0)),
        scratch_shapes=[pltpu.VMEM((nchunk, tq, ck), F32),
                        pltpu.VMEM((IDX_HEADS, tq, LANES), BF16),
                        pltpu.VMEM((IDX_HEADS, tq, LANES), F32),
                        pltpu.VMEM((tq, LANES), I32)],
        compiler_params=_cparams(("parallel",), vmem),
        name="dsa_indexer",
    )(qi, ki, wi)


def _tri_steps(n):
    qs = [i for i in range(n) for _ in range(i + 1)]
    ks = [j for i in range(n) for j in range(i + 1)]
    return jnp.asarray(qs, I32), jnp.asarray(ks, I32)


def _softmax_step(h, s, v_h, m_ref, l_ref, acc_ref):
    sl = slice(h * HEAD_DIM, (h + 1) * HEAD_DIM)
    m_prev = m_ref[h]
    m_new = jnp.maximum(m_prev, jnp.max(s, axis=-1, keepdims=True))
    m_safe = jnp.where(m_new == NEG_INF, 0.0, m_new)
    alpha = jnp.exp(m_prev - m_safe)
    p = jnp.exp(s - m_safe[:, :1])
    l_ref[h] = alpha * l_ref[h] + jnp.sum(p, axis=-1, keepdims=True)
    acc_ref[:, sl] = alpha * acc_ref[:, sl] + jnp.dot(p.astype(BF16), v_h, preferred_element_type=F32)
    m_ref[h] = m_new


def _attn_init(m_ref, l_ref, acc_ref):
    m_ref[...] = jnp.full(m_ref.shape, NEG_INF, F32)
    l_ref[...] = jnp.zeros(l_ref.shape, F32)
    acc_ref[...] = jnp.zeros(acc_ref.shape, F32)


def _attn_finish(o_ref, l_ref, acc_ref, nh):
    for h in range(nh):
        sl = slice(h * HEAD_DIM, (h + 1) * HEAD_DIM)
        o_ref[:, sl] = (acc_ref[:, sl] / l_ref[h]).astype(o_ref.dtype)


def _dsa_attn_kernel(qs_ref, ks_ref, q_ref, k_ref, v_ref, b_ref, o_ref, m_ref, l_ref, acc_ref, *, scale):
    s_id = pl.program_id(0)
    qi, kj = qs_ref[s_id], ks_ref[s_id]
    t = q_ref.shape[0]

    @pl.when(kj == 0)
    def _():
        _attn_init(m_ref, l_ref, acc_ref)

    bias = b_ref[...].reshape(t, t).astype(F32)
    for h in range(DSA_HEADS):
        sl = slice(h * HEAD_DIM, (h + 1) * HEAD_DIM)
        s = _dot_nt(q_ref[:, sl], k_ref[:, sl]) * scale + bias
        _softmax_step(h, s, v_ref[:, sl], m_ref, l_ref, acc_ref)

    @pl.when(kj == qi)
    def _():
        _attn_finish(o_ref, l_ref, acc_ref, DSA_HEADS)


def _dsa_attention(q, k, v, bias4):
    seq, width = q.shape
    t = min(ATT_T, seq)
    nq = seq // t
    qs, ks = _tri_steps(nq)
    sub = t // IDX_TQ
    qmap = lambda s, qs, ks: (qs[s], 0)
    kmap = lambda s, qs, ks: (ks[s], 0)
    vmem = (2 * (3 * t * width * 2 + t * t * 2 + t * width * 2) + t * width * 4
            + 2 * DSA_HEADS * t * LANES * 4 + 24 * t * t * 4)
    return pl.pallas_call(
        functools.partial(_dsa_attn_kernel, scale=HEAD_DIM ** -0.5),
        out_shape=jax.ShapeDtypeStruct((seq, width), BF16),
        grid_spec=pltpu.PrefetchScalarGridSpec(
            num_scalar_prefetch=2,
            grid=(int(qs.shape[0]),),
            in_specs=[pl.BlockSpec((t, width), qmap), pl.BlockSpec((t, width), kmap),
                      pl.BlockSpec((t, width), kmap),
                      pl.BlockSpec((sub, 1, IDX_TQ, IDX_CK), lambda s, qs, ks: (qs[s], ks[s], 0, 0))],
            out_specs=pl.BlockSpec((t, width), qmap),
            scratch_shapes=[pltpu.VMEM((DSA_HEADS, t, LANES), F32), pltpu.VMEM((DSA_HEADS, t, LANES), F32),
                            pltpu.VMEM((t, width), F32)]),
        compiler_params=_cparams(("arbitrary",), vmem),
        name="dsa_attention",
    )(qs, ks, q, k, v, bias4)


def _kmean_kernel(k_ref, o_ref):
    o_ref[0] = jnp.mean(k_ref[...].astype(F32), axis=0, keepdims=True)


def _block_means(k):
    seq, width = k.shape
    nblk = seq // MOBA_BLOCK
    out = pl.pallas_call(
        _kmean_kernel,
        out_shape=jax.ShapeDtypeStruct((nblk, 1, width), F32),
        grid=(nblk,),
        in_specs=[pl.BlockSpec((MOBA_BLOCK, width), lambda i: (i, 0))],
        out_specs=pl.BlockSpec((1, 1, width), lambda i: (i, 0, 0)),
        compiler_params=_cparams(("parallel",), 8 * MOBA_BLOCK * width * 4),
        name="moba_block_means",
    )(k)
    return out.reshape(nblk, width)


def _moba_attn_kernel(qs_ref, ks_ref, q_ref, k_ref, v_ref, km_ref, o_ref, m_ref, l_ref, acc_ref, sel_ref,
                      *, scale, n_top):
    s_id = pl.program_id(0)
    qi, kj = qs_ref[s_id], ks_ref[s_id]
    t = q_ref.shape[0]
    nblk = km_ref.shape[0]
    per_tile = t // MOBA_BLOCK
    blk = lax.broadcasted_iota(I32, (t, nblk), 1)

    @pl.when(kj == 0)
    def _():
        _attn_init(m_ref, l_ref, acc_ref)
        own = (qi * t + lax.broadcasted_iota(I32, (t, nblk), 0)) // MOBA_BLOCK
        blk_f = blk.astype(F32)
        for h in range(MOBA_HEADS):
            sl = slice(h * HEAD_DIM, (h + 1) * HEAD_DIM)
            g = _dot_nt(q_ref[:, sl], km_ref[:, sl].astype(BF16))
            g = jnp.where(blk < own, g, NEG_INF)
            sel = jnp.full((t, nblk), NEG_INF, F32)
            for _ in range(n_top):
                mx = jnp.max(g, axis=-1, keepdims=True)
                is_max = jnp.logical_and(g == mx, mx > NEG_INF)
                first = jnp.min(jnp.where(is_max, blk_f, float(nblk)), axis=-1, keepdims=True)
                pick = blk_f == first
                sel = jnp.where(pick, 0.0, sel)
                g = jnp.where(pick, NEG_INF, g)
            sel_ref[h] = sel

    def block_bias(h):
        sel = sel_ref[h]
        cols = []
        for b in range(per_tile):
            cb = jnp.max(jnp.where(blk == kj * per_tile + b, sel, NEG_INF), axis=-1, keepdims=True)
            cols.append(jnp.broadcast_to(cb, (t, MOBA_BLOCK)))
        return jnp.concatenate(cols, axis=1)

    def run(diag):
        if diag:
            r = lax.broadcasted_iota(I32, (t, t), 0)
            c = lax.broadcasted_iota(I32, (t, t), 1)
            own_blk = (r // MOBA_BLOCK) == (c // MOBA_BLOCK)
            causal = jnp.where(c <= r, 0.0, NEG_INF)
        for h in range(MOBA_HEADS):
            sl = slice(h * HEAD_DIM, (h + 1) * HEAD_DIM)
            bias = block_bias(h)
            if diag:
                bias = jnp.where(own_blk, causal, bias)
            s = _dot_nt(q_ref[:, sl], k_ref[:, sl]) * scale + bias
            _softmax_step(h, s, v_ref[:, sl], m_ref, l_ref, acc_ref)

    @pl.when(kj < qi)
    def _():
        run(False)

    @pl.when(kj == qi)
    def _():
        run(True)
        _attn_finish(o_ref, l_ref, acc_ref, MOBA_HEADS)


def _moba_attention(q, k, v, kmean):
    seq, width = q.shape
    t = min(ATT_T, seq)
    nq = seq // t
    nblk = kmean.shape[0]
    qs, ks = _tri_steps(nq)
    qmap = lambda s, qs, ks: (qs[s], 0)
    kmap = lambda s, qs, ks: (ks[s], 0)
    vmem = (2 * (4 * t * width * 2) + nblk * width * 4 + t * width * 4
            + 3 * MOBA_HEADS * t * LANES * 4 + 24 * t * t * 4)
    return pl.pallas_call(
        functools.partial(_moba_attn_kernel, scale=HEAD_DIM ** -0.5, n_top=min(MOBA_TOPK, nblk)),
        out_shape=jax.ShapeDtypeStruct((seq, width), BF16),
        grid_spec=pltpu.PrefetchScalarGridSpec(
            num_scalar_prefetch=2,
            grid=(int(qs.shape[0]),),
            in_specs=[pl.BlockSpec((t, width), qmap), pl.BlockSpec((t, width), kmap),
                      pl.BlockSpec((t, width), kmap),
                      _resident((nblk, width), lambda s, qs, ks: (0, 0))],
            out_specs=pl.BlockSpec((t, width), qmap),
            scratch_shapes=[pltpu.VMEM((MOBA_HEADS, t, LANES), F32), pltpu.VMEM((MOBA_HEADS, t, LANES), F32),
                            pltpu.VMEM((t, width), F32), pltpu.VMEM((MOBA_HEADS, t, nblk), F32)]),
        compiler_params=_cparams(("arbitrary",), vmem),
        name="moba_attention",
    )(qs, ks, q, k, v, kmean)


def _post_kernel(oa_ref, ob_ref, ga_ref, gb_ref, x_ref, wa_ref, wb_ref, wo_ref, gf_ref, wr_ref, br_ref,
                 x1_ref, h2_ref, ti_ref, tg_ref):
    ta = jnp.dot(oa_ref[...], wa_ref[...], preferred_element_type=F32)
    tb = jnp.dot(ob_ref[...], wb_ref[...], preferred_element_type=F32)
    merged = _sigmoid(ga_ref[...]) * ta + _sigmoid(gb_ref[...]) * tb
    x1 = x_ref[...] + jnp.dot(merged.astype(BF16), wo_ref[...], preferred_element_type=F32)
    x1_ref[...] = x1
    h2 = _rms(x1, gf_ref[...])
    h2_ref[...] = h2.astype(BF16)
    logits = jnp.dot(h2.astype(BF16), wr_ref[...], preferred_element_type=F32) + br_ref[...]
    lane = lax.broadcasted_iota(I32, logits.shape, 1)
    lane_f = lane.astype(F32)
    idx_out = jnp.zeros(logits.shape, I32)
    val_out = jnp.zeros(logits.shape, F32)
    top = None
    for r in range(EXPERT_TOPK):
        mx = jnp.max(logits, axis=-1, keepdims=True)
        ix = jnp.min(jnp.where(logits == mx, lane_f, float(LANES)), axis=-1, keepdims=True).astype(I32)
        if top is None:
            top = mx
        idx_out = jnp.where(lane == r, ix, idx_out)
        val_out = jnp.where(lane == r, jnp.exp(mx - top), val_out)
        logits = jnp.where(lane == ix, NEG_INF, logits)
    ti_ref[...] = idx_out
    tg_ref[...] = val_out / jnp.sum(val_out, axis=-1, keepdims=True)


def _post_attention(oa, ob, ga, gb, x2, wa, wb, wo, gf, wr, br):
    seq, d = x2.shape
    w = oa.shape[1]
    tm = min(POST_TM, seq)
    row = lambda i: (i, 0)
    fix = lambda i: (0, 0)
    vmem = (2 * (2 * tm * w * 2 + 3 * tm * d * 4 + tm * d * 4 + tm * d * 2 + 2 * tm * LANES * 4)
            + 2 * w * d * 2 + d * d * 2 + d * LANES * 2 + 8 * tm * d * 4)
    return pl.pallas_call(
        _post_kernel,
        out_shape=(jax.ShapeDtypeStruct((seq, d), F32), jax.ShapeDtypeStruct((seq, d), BF16),
                   jax.ShapeDtypeStruct((seq, LANES), I32), jax.ShapeDtypeStruct((seq, LANES), F32)),
        grid=(seq // tm,),
        in_specs=[pl.BlockSpec((tm, w), row), pl.BlockSpec((tm, w), row),
                  pl.BlockSpec((tm, d), row), pl.BlockSpec((tm, d), row), pl.BlockSpec((tm, d), row),
                  _resident((w, d), fix), _resident((w, d), fix), _resident((d, d), fix),
                  _resident((1, d), fix), _resident((d, LANES), fix), _resident((1, LANES), fix)],
        out_specs=(pl.BlockSpec((tm, d), row), pl.BlockSpec((tm, d), row),
                   pl.BlockSpec((tm, LANES), row), pl.BlockSpec((tm, LANES), row)),
        compiler_params=_cparams(("parallel",), vmem),
        name="merge_outproj_router",
    )(oa, ob, ga, gb, x2, wa, wb, wo, gf, wr, br)


def _moe_kernel(te_ref, nu_ref, x_ref, wg_ref, wl_ref, bg_ref, bl_ref, wd_ref, bd_ref, y_ref):
    t, j = pl.program_id(0), pl.program_id(1)

    @pl.when(t < nu_ref[0])
    def _():
        @pl.when(j == 0)
        def _():
            y_ref[...] = jnp.broadcast_to(bd_ref[0], y_ref.shape)

        x = x_ref[...]
        g = jnp.dot(x, wg_ref[0], preferred_element_type=F32) + bg_ref[0]
        lin = jnp.dot(x, wl_ref[0], preferred_element_type=F32) + bl_ref[0]
        g = jnp.minimum(g, SWIGLU_LIMIT)
        lin = jnp.clip(lin, -SWIGLU_LIMIT, SWIGLU_LIMIT)
        hid = (lin + 1.0) * (g * _sigmoid(g * SWIGLU_ALPHA))
        y_ref[...] += jnp.dot(hid.astype(BF16), wd_ref[0], preferred_element_type=F32)


def _moe_experts(x_rows, tile_expert, n_used, wg, wl, bg, bl, wd, bd):
    n_rows, d = x_rows.shape
    f = wg.shape[2]
    tm, tf = MOE_TM, min(MOE_TF, f)
    n_tiles, nf = n_rows // tm, f // tf

    def tile(t, nu):
        return jnp.minimum(t, nu[0] - 1)

    def ftile(t, j, nu):
        return jnp.where(t < nu[0], j, nf - 1)

    vmem = 2 * (tm * d * 2 + 2 * d * tf * 2 + tf * d * 2 + tm * d * 4 + 2 * tf * 4 + d * 4) + 8 * tm * tf * 4
    return pl.pallas_call(
        _moe_kernel,
        out_shape=jax.ShapeDtypeStruct((n_rows, d), F32),
        grid_spec=pltpu.PrefetchScalarGridSpec(
            num_scalar_prefetch=2,
            grid=(n_tiles, nf),
            in_specs=[pl.BlockSpec((tm, d), lambda t, j, te, nu: (tile(t, nu), 0)),
                      pl.BlockSpec((1, d, tf), lambda t, j, te, nu: (te[tile(t, nu)], 0, ftile(t, j, nu))),
                      pl.BlockSpec((1, d, tf), lambda t, j, te, nu: (te[tile(t, nu)], 0, ftile(t, j, nu))),
                      pl.BlockSpec((1, 1, tf), lambda t, j, te, nu: (te[tile(t, nu)], 0, ftile(t, j, nu))),
                      pl.BlockSpec((1, 1, tf), lambda t, j, te, nu: (te[tile(t, nu)], 0, ftile(t, j, nu))),
                      pl.BlockSpec((1, tf, d), lambda t, j, te, nu: (te[tile(t, nu)], ftile(t, j, nu), 0)),
                      pl.BlockSpec((1, 1, d), lambda t, j, te, nu: (te[tile(t, nu)], 0, 0))],
            out_specs=pl.BlockSpec((tm, d), lambda t, j, te, nu: (tile(t, nu), 0))),
        compiler_params=_cparams(("arbitrary", "arbitrary"), vmem),
        name="moe_experts",
    )(tile_expert, n_used, x_rows, wg, wl, bg, bl, wd, bd)


def _route(top_idx, tm):
    n_tok = top_idx.shape[0]
    n_slots = n_tok * EXPERT_TOPK
    e_flat = top_idx.reshape(-1)
    order = jnp.argsort(e_flat)
    e_sorted = e_flat[order]
    counts = jnp.bincount(e_flat, length=N_EXPERTS)
    padded = (counts + tm - 1) // tm * tm
    pad_end = jnp.cumsum(padded)
    pad_start = pad_end - padded
    start = jnp.cumsum(counts) - counts
    dest = (pad_start[e_sorted] + jnp.arange(n_slots, dtype=I32) - start[e_sorted]).astype(I32)
    n_tiles = n_slots // tm + N_EXPERTS
    row_tok = jnp.zeros((n_tiles * tm,), I32).at[dest].set((order // EXPERT_TOPK).astype(I32))
    slot_row = jnp.zeros((n_slots,), I32).at[order].set(dest).reshape(n_tok, EXPERT_TOPK)
    tile_expert = jnp.minimum(
        jnp.searchsorted(pad_end, jnp.arange(n_tiles) * tm, side="right"), N_EXPERTS - 1).astype(I32)
    n_used = (pad_end[-1] // tm).astype(I32).reshape(1)
    return row_tok, slot_row, tile_expert, n_used


def _final_kernel(x1_ref, y_ref, tg_ref, p_ref, wpp_ref, wpg_ref, gp_ref, gfin_ref, o_ref):
    x2 = x1_ref[...]
    gates = tg_ref[...]
    for r in range(EXPERT_TOPK):
        x2 = x2 + y_ref[r] * gates[:, r:r + 1]
    ple = jnp.dot(p_ref[...].astype(BF16), wpp_ref[...], preferred_element_type=F32)
    gate = _sigmoid(jnp.dot(x2.astype(BF16), wpg_ref[...], preferred_element_type=F32))
    x3 = x2 + _rms(gate * ple, gp_ref[...])
    o_ref[...] = _rms(x3, gfin_ref[...])


def _final(x1, y_sel, gates, p2, wpp, wpg, gp, gfin):
    seq, d = x1.shape
    pd = p2.shape[1]
    tm = min(FIN_TM, seq)
    row = lambda i: (i, 0)
    fix = lambda i: (0, 0)
    vmem = (2 * (2 * tm * d * 4 + EXPERT_TOPK * tm * d * 4 + tm * LANES * 4 + tm * pd * 4)
            + pd * d * 2 + d * d * 2 + 8 * tm * d * 4)
    return pl.pallas_call(
        _final_kernel,
        out_shape=jax.ShapeDtypeStruct((seq, d), F32),
        grid=(seq // tm,),
        in_specs=[pl.BlockSpec((tm, d), row),
                  pl.BlockSpec((EXPERT_TOPK, tm, d), lambda i: (0, i, 0)),
                  pl.BlockSpec((tm, LANES), row), pl.BlockSpec((tm, pd), row),
                  _resident((pd, d), fix), _resident((d, d), fix),
                  _resident((1, d), fix), _resident((1, d), fix)],
        out_specs=pl.BlockSpec((tm, d), row),
        compiler_params=_cparams(("parallel",), vmem),
        name="combine_ple_norm",
    )(x1, y_sel, gates, p2, wpp, wpg, gp, gfin)


def _layer(x2, p2, g_mix, w_in, g_idx_k, b_idx_k, w_br_a, w_br_b, w_out, g_ffn, w_router, b_router,
           w_gate_up, b_gate_up, w_down, b_down, w_ple_proj, w_ple_gate, g_ple, g_final):
    seq, d = x2.shape
    dsa_w, moba_w = DSA_HEADS * HEAD_DIM, MOBA_HEADS * HEAD_DIM
    sizes = (dsa_w, dsa_w, dsa_w, IDX_HEADS * IDX_DIM, IDX_DIM, IDX_HEADS, moba_w, moba_w, moba_w, d, d)
    off = np.concatenate([[0], np.cumsum(sizes)])
    col = lambda k: w_in[:, off[k]:off[k + 1]]
    row1 = lambda v: v.reshape(1, -1)
    g_mix2 = row1(g_mix)

    w_rope = jnp.concatenate([col(0), col(1), col(6), col(7)], axis=1).astype(BF16)
    w_qi = col(3).astype(BF16)
    w_kiwi = jnp.zeros((d, 2 * LANES), F32).at[:, :IDX_DIM].set(col(4)).at[:, LANES:LANES + IDX_HEADS].set(col(5))
    w_val = jnp.concatenate([col(2), col(8)], axis=1).astype(BF16)
    w_gate = jnp.concatenate([col(9), col(10)], axis=1).astype(BF16)

    rope_tabs = _rope_tables(seq, ROPE_DIM, HEAD_DIM)
    idx_tabs = _rope_tables(seq, IDX_ROPE_DIM, IDX_DIM)
    qk = _project(x2, g_mix2, w_rope, BF16, rope_tabs, ROPE_DIM // 2)
    qa, ka = qk[:, :dsa_w], qk[:, dsa_w:2 * dsa_w]
    qb, kb = qk[:, 2 * dsa_w:2 * dsa_w + moba_w], qk[:, 2 * dsa_w + moba_w:]
    qi = _project(x2, g_mix2, w_qi, BF16, idx_tabs, IDX_ROPE_DIM // 2)
    pad = lambda v: jnp.zeros((1, LANES), F32).at[0, :IDX_DIM].set(v)
    ki, wi = _project_kiwi(x2, g_mix2, w_kiwi.astype(BF16), pad(g_idx_k), pad(b_idx_k), idx_tabs,
                           (IDX_HEADS ** -0.5) * (IDX_DIM ** -0.5))
    vals = _project(x2, g_mix2, w_val, BF16)
    va, vb = vals[:, :dsa_w], vals[:, dsa_w:]
    gates_ab = _project(x2, g_mix2, w_gate, F32)
    ga, gb = gates_ab[:, :d], gates_ab[:, d:]

    bias4 = _dsa_mask(qi, ki, wi, min(DSA_TOPK, seq // 4))
    o_a = _dsa_attention(qa, ka, va, bias4)
    o_b = _moba_attention(qb, kb, vb, _block_means(kb))

    w_r = jnp.zeros((d, LANES), F32).at[:, :N_EXPERTS].set(w_router).astype(BF16)
    b_r = jnp.full((1, LANES), NEG_INF, F32).at[0, :N_EXPERTS].set(b_router)
    x1, h2, top_idx, top_gate = _post_attention(
        o_a, o_b, ga, gb, x2, w_br_a.astype(BF16), w_br_b.astype(BF16), w_out.astype(BF16),
        row1(g_ffn), w_r, b_r)

    row_tok, slot_row, tile_expert, n_used = _route(top_idx[:, :EXPERT_TOPK], MOE_TM)
    x_rows = h2[row_tok]
    bias_row = lambda b: b.reshape(N_EXPERTS, 1, -1)
    y = _moe_experts(x_rows, tile_expert, n_used,
                     w_gate_up[:, :, 0::2].astype(BF16), w_gate_up[:, :, 1::2].astype(BF16),
                     bias_row(b_gate_up[:, 0::2]), bias_row(b_gate_up[:, 1::2]),
                     w_down.astype(BF16), bias_row(b_down))
    y_sel = y[slot_row.T]

    return _final(x1, y_sel, top_gate, p2, w_ple_proj.astype(BF16), w_ple_gate.astype(BF16), row1(g_ple),
                  row1(g_final))


def kernel(x, p, g_mix, w_in, g_idx_k, b_idx_k, w_br_a, w_br_b, w_out, g_ffn, w_router, b_router,
           w_gate_up, b_gate_up, w_down, b_down, w_ple_proj, w_ple_gate, g_ple, g_final):
    batch, seq, d = x.shape
    depth = w_in.shape[0]
    assert batch == 1 and depth == 1, "kernel handles the single-sequence, single-layer block"
    assert seq % ATT_T == 0 and seq % PROJ_TM == 0
    out = _layer(x[0], p[0, 0], g_mix[0], w_in[0], g_idx_k[0], b_idx_k[0], w_br_a[0], w_br_b[0], w_out[0],
                 g_ffn[0], w_router[0], b_router[0], w_gate_up[0], b_gate_up[0], w_down[0], b_down[0],
                 w_ple_proj[0], w_ple_gate[0], g_ple[0], g_final)
    return out[None]
```

```python
import functools

import numpy as np
import jax
import jax.numpy as jnp
from jax import lax
from jax.experimental import pallas as pl
from jax.experimental.pallas import tpu as pltpu

F32 = jnp.float32
BF16 = jnp.bfloat16
I32 = jnp.int32
NEG_INF = float("-inf")

HEAD_DIM = 128
DSA_HEADS = 8
MOBA_HEADS = 8
ROPE_DIM = HEAD_DIM // 4
ROPE_THETA = 500000.0
IDX_HEADS = 16
IDX_DIM = 64
IDX_ROPE_DIM = IDX_DIM // 4
DSA_TOPK = 256
MOBA_BLOCK = 256
MOBA_TOPK = 3
N_EXPERTS = 32
EXPERT_TOPK = 4
SWIGLU_LIMIT = 7.0
SWIGLU_ALPHA = 1.702
EPS = 1e-6

LANES = 128
VMEM_CAP_BYTES = 60000 * 1024

PROJ_TM = 512
PROJ_TN = 512
IDX_TQ = 128
IDX_CK = 512
ATT_T = 512
POST_TM = 256
MOE_TM = 512
MOE_TF = 512
FIN_TM = 256

INT_MIN = -2 ** 31
KEY_NEG_INF = 0x807FFFFF - 2 ** 32


def _cparams(sem, vmem_bytes):
    return pltpu.CompilerParams(dimension_semantics=sem,
                                vmem_limit_bytes=int(min(vmem_bytes, VMEM_CAP_BYTES)))


def _resident(shape, index_map):
    return pl.BlockSpec(shape, index_map, pipeline_mode=pl.Buffered(1))


def _sigmoid(x):
    return 1.0 / (1.0 + jnp.exp(-x))


def _rms(xf, g):
    ms = jnp.mean(xf * xf, axis=-1, keepdims=True)
    return xf * lax.rsqrt(ms + EPS) * g


def _dot_nt(a, b):
    return lax.dot_general(a, b, (((1,), (1,)), ((), ())), preferred_element_type=F32)


def _rope_tables(seq, rot_dim, period):
    half = rot_dim // 2
    inv = 1.0 / (ROPE_THETA ** (jnp.arange(half, dtype=F32) / half))
    ang = jnp.arange(seq).astype(F32)[:, None] * inv[None, :]
    cos, sin = jnp.cos(ang), jnp.sin(ang)
    z = lambda n: jnp.zeros((seq, n), F32)
    c = jnp.concatenate([cos, cos, jnp.ones((seq, period - rot_dim), F32)], axis=-1)
    s1 = jnp.concatenate([-sin, z(period - half)], axis=-1)
    s2 = jnp.concatenate([z(half), sin, z(period - rot_dim)], axis=-1)
    rep = LANES // period
    return tuple(jnp.tile(t, (1, rep)) for t in (c, s1, s2))


def _rope(a, c, s1, s2, half):
    return a * c + pltpu.roll(a, LANES - half, 1) * s1 + pltpu.roll(a, half, 1) * s2


def _proj_kernel(x_ref, g_ref, w_ref, *rest, half):
    if half is None:
        o_ref, h_ref = rest
    else:
        c_ref, s1_ref, s2_ref, o_ref, h_ref = rest

    @pl.when(pl.program_id(1) == 0)
    def _():
        h_ref[...] = _rms(x_ref[...], g_ref[...]).astype(BF16)

    acc = jnp.dot(h_ref[...], w_ref[...], preferred_element_type=F32)
    if half is None:
        o_ref[...] = acc.astype(o_ref.dtype)
    else:
        c, s1, s2 = c_ref[...], s1_ref[...], s2_ref[...]
        for k in range(acc.shape[1] // LANES):
            sl = slice(k * LANES, (k + 1) * LANES)
            o_ref[:, sl] = _rope(acc[:, sl], c, s1, s2, half).astype(o_ref.dtype)


def _project(x2, g, w, out_dtype, tables=None, half=None):
    seq, d = x2.shape
    n = w.shape[1]
    tm, tn = min(PROJ_TM, seq), min(PROJ_TN, n)
    in_specs = [pl.BlockSpec((tm, d), lambda i, j: (i, 0)),
                pl.BlockSpec((1, d), lambda i, j: (0, 0)),
                pl.BlockSpec((d, tn), lambda i, j: (0, j))]
    args = [x2, g, w]
    if tables is not None:
        in_specs += [pl.BlockSpec((tm, LANES), lambda i, j: (i, 0))] * 3
        args += list(tables)
    vmem = 2 * (tm * d * 4 + d * tn * 2 + tm * tn * 4 + 3 * tm * LANES * 4) + tm * d * 2 + 4 * tm * tn * 4
    return pl.pallas_call(
        functools.partial(_proj_kernel, half=half),
        out_shape=jax.ShapeDtypeStruct((seq, n), out_dtype),
        grid=(seq // tm, n // tn),
        in_specs=in_specs,
        out_specs=pl.BlockSpec((tm, tn), lambda i, j: (i, j)),
        scratch_shapes=[pltpu.VMEM((tm, d), BF16)],
        compiler_params=_cparams(("parallel", "arbitrary"), vmem),
        name="proj_rope" if half is not None else "proj_plain",
    )(*args)


def _proj_kiwi_kernel(x_ref, g_ref, w_ref, lg_ref, lb_ref, c_ref, s1_ref, s2_ref, ki_ref, wi_ref, *, scale):
    h = _rms(x_ref[...], g_ref[...]).astype(BF16)
    acc = jnp.dot(h, w_ref[...], preferred_element_type=F32)
    a = acc[:, :LANES]
    valid = lax.broadcasted_iota(I32, a.shape, 1) < IDX_DIM
    mu = jnp.sum(jnp.where(valid, a, 0.0), axis=-1, keepdims=True) / IDX_DIM
    dlt = jnp.where(valid, a - mu, 0.0)
    var = jnp.sum(dlt * dlt, axis=-1, keepdims=True) / IDX_DIM
    y = dlt * lax.rsqrt(var + EPS) * lg_ref[...] + lb_ref[...]
    ki_ref[...] = _rope(y, c_ref[...], s1_ref[...], s2_ref[...], IDX_ROPE_DIM // 2).astype(BF16)
    wi_ref[...] = acc[:, LANES:LANES + IDX_HEADS] * scale


def _project_kiwi(x2, g, w_kiwi, lg, lb, tables, scale):
    seq, d = x2.shape
    tm = min(PROJ_TM, seq)
    row = lambda i: (i, 0)
    fix = lambda i: (0, 0)
    vmem = 2 * (tm * d * 4 + d * 2 * LANES * 2 + 5 * tm * LANES * 4) + 8 * tm * 2 * LANES * 4 + tm * d * 6
    return pl.pallas_call(
        functools.partial(_proj_kiwi_kernel, scale=scale),
        out_shape=(jax.ShapeDtypeStruct((seq, LANES), BF16), jax.ShapeDtypeStruct((seq, IDX_HEADS), F32)),
        grid=(seq // tm,),
        in_specs=[pl.BlockSpec((tm, d), row), pl.BlockSpec((1, d), fix), pl.BlockSpec((d, 2 * LANES), fix),
                  pl.BlockSpec((1, LANES), fix), pl.BlockSpec((1, LANES), fix),
                  pl.BlockSpec((tm, LANES), row), pl.BlockSpec((tm, LANES), row), pl.BlockSpec((tm, LANES), row)],
        out_specs=(pl.BlockSpec((tm, LANES), row), pl.BlockSpec((tm, IDX_HEADS), row)),
        compiler_params=_cparams(("parallel",), vmem),
        name="proj_kiwi",
    )(x2, g, w_kiwi, lg, lb, *tables)


def _key_to_float(key):
    bits = key ^ ((key >> 31) & 0x7FFFFFFF)
    return lax.bitcast_convert_type(bits, F32)


def _indexer_kernel(qi_ref, ki_ref, wi_ref, out_ref, s_ref, qh_ref, wb_ref, cst_ref, *, n_sel, idx_bits):
    tq, ck = IDX_TQ, IDX_CK
    nslab = ck // LANES
    n_chunks_total = out_ref.shape[1]
    i = pl.program_id(0)
    nc = ((i + 1) * tq + ck - 1) // ck
    lane = lax.broadcasted_iota(I32, (tq, LANES), 1)
    row = i * tq + lax.broadcasted_iota(I32, (tq, LANES), 0)

    low = lane < IDX_DIM
    w = wi_ref[...]
    for p in range(IDX_HEADS // 2):
        pair = qi_ref[:, p * LANES:(p + 1) * LANES].astype(F32)
        qh_ref[2 * p] = jnp.where(low, pair, 0.0).astype(BF16)
        qh_ref[2 * p + 1] = jnp.where(low, pltpu.roll(pair, IDX_DIM, 1), 0.0).astype(BF16)
    for h in range(IDX_HEADS):
        wb_ref[h] = jnp.broadcast_to(w[:, h:h + 1], (tq, LANES))

    def score_chunk(c, carry):
        kc = ki_ref[pl.ds(pl.multiple_of(c * ck, ck), ck), :]
        slabs = [jnp.zeros((tq, LANES), F32) for _ in range(nslab)]
        for h in range(IDX_HEADS):
            logit = _dot_nt(qh_ref[h], kc)
            wbh = wb_ref[h]
            for k in range(nslab):
                slabs[k] = slabs[k] + jnp.maximum(logit[:, k * LANES:(k + 1) * LANES], 0.0) * wbh
        for k in range(nslab):
            col = c * ck + k * LANES + lane
            slabs[k] = jnp.where(col <= row, slabs[k], NEG_INF)
        s_ref[c] = jnp.concatenate(slabs, axis=1)
        return carry

    lax.fori_loop(0, nc, score_chunk, 0)

    def count(pred):
        def body(c, acc):
            sc = s_ref[c]
            for k in range(nslab):
                col = c * ck + k * LANES + lane
                acc = acc + jnp.where(pred(sc[:, k * LANES:(k + 1) * LANES], col), 1.0, 0.0)
            return acc
        acc = lax.fori_loop(0, nc, body, jnp.zeros((tq, LANES), F32))
        return jnp.broadcast_to(jnp.sum(acc, axis=-1, keepdims=True), (tq, LANES))

    def bit_step(b, carry):
        prefix, cnt_at = carry
        cand = prefix + lax.shift_left(jnp.int32(1), 31 - b)
        cand_f = _key_to_float(cand)
        cnt = count(lambda s, col: s >= cand_f)
        take = cnt >= n_sel
        return jnp.where(take, cand, prefix), jnp.where(take, cnt, cnt_at)

    prefix, cnt_at = lax.fori_loop(
        0, 32, bit_step, (jnp.full((tq, LANES), INT_MIN, I32), jnp.full((tq, LANES), float(n_sel), F32)))
    tau = jnp.where(prefix < KEY_NEG_INF, NEG_INF, _key_to_float(prefix))

    need = jnp.logical_and(cnt_at > n_sel, tau > NEG_INF)
    any_tie = jnp.max(jnp.where(need, 1.0, 0.0)) > 0.0
    idx_all = jnp.full((tq, LANES), 2 ** 30, I32)

    @pl.when(any_tie)
    def _():
        rem = n_sel - count(lambda s, col: s > tau)
        cut = jnp.zeros((tq, LANES), I32)
        for b in range(idx_bits - 1, -1, -1):
            cand = cut + (1 << b)
            below = count(lambda s, col: jnp.logical_and(s == tau, col < cand))
            cut = jnp.where(below < rem, cand, cut)
        cst_ref[...] = jnp.where(need, cut, idx_all)

    @pl.when(jnp.logical_not(any_tie))
    def _():
        cst_ref[...] = idx_all

    cut = cst_ref[...]

    def emit(c, carry):
        sc = s_ref[c]
        outs = []
        for k in range(nslab):
            s = sc[:, k * LANES:(k + 1) * LANES]
            col = c * ck + k * LANES + lane
            tie = jnp.where(col <= cut, 0.0, NEG_INF)
            b = jnp.where(s > tau, 0.0, jnp.where(s == tau, tie, NEG_INF))
            outs.append(jnp.where(col <= row, b, NEG_INF))
        out_ref[0, c] = jnp.concatenate(outs, axis=1).astype(BF16)
        return carry

    lax.fori_loop(0, nc, emit, 0)

    def fill(c, carry):
        out_ref[0, c] = jnp.full((tq, ck), NEG_INF, BF16)
        return carry

    lax.fori_loop(nc, n_chunks_total, fill, 0)


def _dsa_mask(qi, ki, wi, n_sel):
    seq = qi.shape[0]
    tq, ck = IDX_TQ, IDX_CK
    nq, nchunk = seq // tq, seq // ck
    idx_bits = max(1, int(seq - 1).bit_length())
    vmem = (2 * (tq * qi.shape[1] * 2 + tq * LANES * 4 + nchunk * tq * ck * 2) + seq * LANES * 2
            + nchunk * tq * ck * 4 + IDX_HEADS * tq * LANES * 6 + tq * LANES * 4 + 24 * tq * ck * 4)
    return pl.pallas_call(
        functools.partial(_indexer_kernel, n_sel=n_sel, idx_bits=idx_bits),
        out_shape=jax.ShapeDtypeStruct((nq, nchunk, tq, ck), BF16),
        grid=(nq,),
        in_specs=[pl.BlockSpec((tq, qi.shape[1]), lambda i: (i, 0)),
                  _resident((seq, LANES), lambda i: (0, 0)),
                  pl.BlockSpec((tq, IDX_HEADS), lambda i: (i, 0))],
        out_specs=pl.BlockSpec((1, nchunk, tq, ck), lambda i: (i, 0, 0, 0)),
        scratch_shapes=[pltpu.VMEM((nchunk, tq, ck), F32),
                        pltpu.VMEM((IDX_HEADS, tq, LANES), BF16),
                        pltpu.VMEM((IDX_HEADS, tq, LANES), F32),
                        pltpu.VMEM((tq, LANES), I32)],
        compiler_params=_cparams(("parallel",), vmem),
        name="dsa_indexer",
    )(qi, ki, wi)


def _tri_steps(n):
    qs = [i for i in range(n) for _ in range(i + 1)]
    ks = [j for i in range(n) for j in range(i + 1)]
    return jnp.asarray(qs, I32), jnp.asarray(ks, I32)


def _softmax_step(h, s, v_h, m_ref, l_ref, acc_ref):
    sl = slice(h * HEAD_DIM, (h + 1) * HEAD_DIM)
    m_prev = m_ref[h]
    m_new = jnp.maximum(m_prev, jnp.max(s, axis=-1, keepdims=True))
    m_safe = jnp.where(m_new == NEG_INF, 0.0, m_new)
    alpha = jnp.exp(m_prev - m_safe)
    p = jnp.exp(s - m_safe[:, :1])
    l_ref[h] = alpha * l_ref[h] + jnp.sum(p, axis=-1, keepdims=True)
    acc_ref[:, sl] = alpha * acc_ref[:, sl] + jnp.dot(p.astype(BF16), v_h, preferred_element_type=F32)
    m_ref[h] = m_new


def _attn_init(m_ref, l_ref, acc_ref):
    m_ref[...] = jnp.full(m_ref.shape, NEG_INF, F32)
    l_ref[...] = jnp.zeros(l_ref.shape, F32)
    acc_ref[...] = jnp.zeros(acc_ref.shape, F32)


def _attn_finish(o_ref, l_ref, acc_ref, nh):
    for h in range(nh):
        sl = slice(h * HEAD_DIM, (h + 1) * HEAD_DIM)
        o_ref[:, sl] = (acc_ref[:, sl] / l_ref[h]).astype(o_ref.dtype)


def _dsa_attn_kernel(qs_ref, ks_ref, q_ref, k_ref, v_ref, b_ref, o_ref, m_ref, l_ref, acc_ref, *, scale):
    s_id = pl.program_id(0)
    qi, kj = qs_ref[s_id], ks_ref[s_id]
    t = q_ref.shape[0]

    @pl.when(kj == 0)
    def _():
        _attn_init(m_ref, l_ref, acc_ref)

    bias = b_ref[...].reshape(t, t).astype(F32)
    for h in range(DSA_HEADS):
        sl = slice(h * HEAD_DIM, (h + 1) * HEAD_DIM)
        s = _dot_nt(q_ref[:, sl], k_ref[:, sl]) * scale + bias
        _softmax_step(h, s, v_ref[:, sl], m_ref, l_ref, acc_ref)

    @pl.when(kj == qi)
    def _():
        _attn_finish(o_ref, l_ref, acc_ref, DSA_HEADS)


def _dsa_attention(q, k, v, bias4):
    seq, width = q.shape
    t = min(ATT_T, seq)
    nq = seq // t
    qs, ks = _tri_steps(nq)
    sub = t // IDX_TQ
    qmap = lambda s, qs, ks: (qs[s], 0)
    kmap = lambda s, qs, ks: (ks[s], 0)
    vmem = (2 * (3 * t * width * 2 + t * t * 2 + t * width * 2) + t * width * 4
            + 2 * DSA_HEADS * t * LANES * 4 + 24 * t * t * 4)
    return pl.pallas_call(
        functools.partial(_dsa_attn_kernel, scale=HEAD_DIM ** -0.5),
        out_shape=jax.ShapeDtypeStruct((seq, width), BF16),
        grid_spec=pltpu.PrefetchScalarGridSpec(
            num_scalar_prefetch=2,
            grid=(int(qs.shape[0]),),
            in_specs=[pl.BlockSpec((t, width), qmap), pl.BlockSpec((t, width), kmap),
                      pl.BlockSpec((t, width), kmap),
                      pl.BlockSpec((sub, 1, IDX_TQ, IDX_CK), lambda s, qs, ks: (qs[s], ks[s], 0, 0))],
            out_specs=pl.BlockSpec((t, width), qmap),
            scratch_shapes=[pltpu.VMEM((DSA_HEADS, t, LANES), F32), pltpu.VMEM((DSA_HEADS, t, LANES), F32),
                            pltpu.VMEM((t, width), F32)]),
        compiler_params=_cparams(("arbitrary",), vmem),
        name="dsa_attention",
    )(qs, ks, q, k, v, bias4)


def _kmean_kernel(k_ref, o_ref):
    o_ref[0] = jnp.mean(k_ref[...].astype(F32), axis=0, keepdims=True)


def _block_means(k):
    seq, width = k.shape
    nblk = seq // MOBA_BLOCK
    out = pl.pallas_call(
        _kmean_kernel,
        out_shape=jax.ShapeDtypeStruct((nblk, 1, width), F32),
        grid=(nblk,),
        in_specs=[pl.BlockSpec((MOBA_BLOCK, width), lambda i: (i, 0))],
        out_specs=pl.BlockSpec((1, 1, width), lambda i: (i, 0, 0)),
        compiler_params=_cparams(("parallel",), 8 * MOBA_BLOCK * width * 4),
        name="moba_block_means",
    )(k)
    return out.reshape(nblk, width)


def _moba_attn_kernel(qs_ref, ks_ref, q_ref, k_ref, v_ref, km_ref, o_ref, m_ref, l_ref, acc_ref, sel_ref,
                      *, scale, n_top):
    s_id = pl.program_id(0)
    qi, kj = qs_ref[s_id], ks_ref[s_id]
    t = q_ref.shape[0]
    nblk = km_ref.shape[0]
    per_tile = t // MOBA_BLOCK
    blk = lax.broadcasted_iota(I32, (t, nblk), 1)

    @pl.when(kj == 0)
    def _():
        _attn_init(m_ref, l_ref, acc_ref)
        own = (qi * t + lax.broadcasted_iota(I32, (t, nblk), 0)) // MOBA_BLOCK
        blk_f = blk.astype(F32)
        for h in range(MOBA_HEADS):
            sl = slice(h * HEAD_DIM, (h + 1) * HEAD_DIM)
            g = _dot_nt(q_ref[:, sl], km_ref[:, sl].astype(BF16))
            g = jnp.where(blk < own, g, NEG_INF)
            sel = jnp.full((t, nblk), NEG_INF, F32)
            for _ in range(n_top):
                mx = jnp.max(g, axis=-1, keepdims=True)
                is_max = jnp.logical_and(g == mx, mx > NEG_INF)
                first = jnp.min(jnp.where(is_max, blk_f, float(nblk)), axis=-1, keepdims=True)
                pick = blk_f == first
                sel = jnp.where(pick, 0.0, sel)
                g = jnp.where(pick, NEG_INF, g)
            sel_ref[h] = sel

    def block_bias(h):
        sel = sel_ref[h]
        cols = []
        for b in range(per_tile):
            cb = jnp.max(jnp.where(blk == kj * per_tile + b, sel, NEG_INF), axis=-1, keepdims=True)
            cols.append(jnp.broadcast_to(cb, (t, MOBA_BLOCK)))
        return jnp.concatenate(cols, axis=1)

    def run(diag):
        if diag:
            r = lax.broadcasted_iota(I32, (t, t), 0)
            c = lax.broadcasted_iota(I32, (t, t), 1)
            own_blk = (r // MOBA_BLOCK) == (c // MOBA_BLOCK)
            causal = jnp.where(c <= r, 0.0, NEG_INF)
        for h in range(MOBA_HEADS):
            sl = slice(h * HEAD_DIM, (h + 1) * HEAD_DIM)
            bias = block_bias(h)
            if diag:
                bias = jnp.where(own_blk, causal, bias)
            s = _dot_nt(q_ref[:, sl], k_ref[:, sl]) * scale + bias
            _softmax_step(h, s, v_ref[:, sl], m_ref, l_ref, acc_ref)

    @pl.when(kj < qi)
    def _():
        run(False)

    @pl.when(kj == qi)
    def _():
        run(True)
        _attn_finish(o_ref, l_ref, acc_ref, MOBA_HEADS)


def _moba_attention(q, k, v, kmean):
    seq, width = q.shape
    t = min(ATT_T, seq)
    nq = seq // t
    nblk = kmean.shape[0]
    qs, ks = _tri_steps(nq)
    qmap = lambda s, qs, ks: (qs[s], 0)
    kmap = lambda s, qs, ks: (ks[s], 0)
    vmem = (2 * (4 * t * width * 2) + nblk * width * 4 + t * width * 4
            + 3 * MOBA_HEADS * t * LANES * 4 + 24 * t * t * 4)
    return pl.pallas_call(
        functools.partial(_moba_attn_kernel, scale=HEAD_DIM ** -0.5, n_top=min(MOBA_TOPK, nblk)),
        out_shape=jax.ShapeDtypeStruct((seq, width), BF16),
        grid_spec=pltpu.PrefetchScalarGridSpec(
            num_scalar_prefetch=2,
            grid=(int(qs.shape[0]),),
            in_specs=[pl.BlockSpec((t, width), qmap), pl.BlockSpec((t, width), kmap),
                      pl.BlockSpec((t, width), kmap),
                      _resident((nblk, width), lambda s, qs, ks: (0, 0))],
            out_specs=pl.BlockSpec((t, width), qmap),
            scratch_shapes=[pltpu.VMEM((MOBA_HEADS, t, LANES), F32), pltpu.VMEM((MOBA_HEADS, t, LANES), F32),
                            pltpu.VMEM((t, width), F32), pltpu.VMEM((MOBA_HEADS, t, nblk), F32)]),
        compiler_params=_cparams(("arbitrary",), vmem),
        name="moba_attention",
    )(qs, ks, q, k, v, kmean)


def _post_kernel(oa_ref, ob_ref, ga_ref, gb_ref, x_ref, wa_ref, wb_ref, wo_ref, gf_ref, wr_ref, br_ref,
                 x1_ref, h2_ref, ti_ref, tg_ref):
    ta = jnp.dot(oa_ref[...], wa_ref[...], preferred_element_type=F32)
    tb = jnp.dot(ob_ref[...], wb_ref[...], preferred_element_type=F32)
    merged = _sigmoid(ga_ref[...]) * ta + _sigmoid(gb_ref[...]) * tb
    x1 = x_ref[...] + jnp.dot(merged.astype(BF16), wo_ref[...], preferred_element_type=F32)
    x1_ref[...] = x1
    h2 = _rms(x1, gf_ref[...])
    h2_ref[...] = h2.astype(BF16)
    logits = jnp.dot(h2.astype(BF16), wr_ref[...], preferred_element_type=F32) + br_ref[...]
    lane = lax.broadcasted_iota(I32, logits.shape, 1)
    lane_f = lane.astype(F32)
    idx_out = jnp.zeros(logits.shape, I32)
    val_out = jnp.zeros(logits.shape, F32)
    top = None
    for r in range(EXPERT_TOPK):
        mx = jnp.max(logits, axis=-1, keepdims=True)
        ix = jnp.min(jnp.where(logits == mx, lane_f, float(LANES)), axis=-1, keepdims=True).astype(I32)
        if top is None:
            top = mx
        idx_out = jnp.where(lane == r, ix, idx_out)
        val_out = jnp.where(lane == r, jnp.exp(mx - top), val_out)
        logits = jnp.where(lane == ix, NEG_INF, logits)
    ti_ref[...] = idx_out
    tg_ref[...] = val_out / jnp.sum(val_out, axis=-1, keepdims=True)


def _post_attention(oa, ob, ga, gb, x2, wa, wb, wo, gf, wr, br):
    seq, d = x2.shape
    w = oa.shape[1]
    tm = min(POST_TM, seq)
    row = lambda i: (i, 0)
    fix = lambda i: (0, 0)
    vmem = (2 * (2 * tm * w * 2 + 3 * tm * d * 4 + tm * d * 4 + tm * d * 2 + 2 * tm * LANES * 4)
            + 2 * w * d * 2 + d * d * 2 + d * LANES * 2 + 8 * tm * d * 4)
    return pl.pallas_call(
        _post_kernel,
        out_shape=(jax.ShapeDtypeStruct((seq, d), F32), jax.ShapeDtypeStruct((seq, d), BF16),
                   jax.ShapeDtypeStruct((seq, LANES), I32), jax.ShapeDtypeStruct((seq, LANES), F32)),
        grid=(seq // tm,),
        in_specs=[pl.BlockSpec((tm, w), row), pl.BlockSpec((tm, w), row),
                  pl.BlockSpec((tm, d), row), pl.BlockSpec((tm, d), row), pl.BlockSpec((tm, d), row),
                  _resident((w, d), fix), _resident((w, d), fix), _resident((d, d), fix),
                  _resident((1, d), fix), _resident((d, LANES), fix), _resident((1, LANES), fix)],
        out_specs=(pl.BlockSpec((tm, d), row), pl.BlockSpec((tm, d), row),
                   pl.BlockSpec((tm, LANES), row), pl.BlockSpec((tm, LANES), row)),
        compiler_params=_cparams(("parallel",), vmem),
        name="merge_outproj_router",
    )(oa, ob, ga, gb, x2, wa, wb, wo, gf, wr, br)


def _split_kernel(w_ref, g_ref, l_ref):
    grp = 2 * LANES
    r = lax.broadcasted_iota(I32, (grp, grp), 0)
    c = lax.broadcasted_iota(I32, (grp, grp), 1)
    src = jnp.where(c < LANES, 2 * c, 2 * (c - LANES) + 1)
    sel = jnp.where(r == src, 1.0, 0.0).astype(BF16)
    for k in range(w_ref.shape[2] // grp):
        res = jnp.dot(w_ref[0, :, k * grp:(k + 1) * grp].astype(BF16), sel, preferred_element_type=F32)
        g_ref[0, :, k * LANES:(k + 1) * LANES] = res[:, :LANES].astype(BF16)
        l_ref[0, :, k * LANES:(k + 1) * LANES] = res[:, LANES:].astype(BF16)


def _split_gate_up(w_gate_up):
    n_e, d, f2 = w_gate_up.shape
    tc = min(2 * MOE_TF, f2)
    half = jax.ShapeDtypeStruct((n_e, d, f2 // 2), BF16)
    return pl.pallas_call(
        _split_kernel,
        out_shape=(half, half),
        grid=(n_e, f2 // tc),
        in_specs=[pl.BlockSpec((1, d, tc), lambda e, j: (e, 0, j))],
        out_specs=(pl.BlockSpec((1, d, tc // 2), lambda e, j: (e, 0, j)),
                   pl.BlockSpec((1, d, tc // 2), lambda e, j: (e, 0, j))),
        compiler_params=_cparams(("parallel", "parallel"), 2 * (d * tc * 4 + d * tc * 2) + 6 * d * tc * 2),
        name="moe_split_gate_up",
    )(w_gate_up)


def _moe_kernel(te_ref, nu_ref, x_ref, wg_ref, wl_ref, bg_ref, bl_ref, wd_ref, bd_ref, y_ref):
    t, j = pl.program_id(0), pl.program_id(1)

    @pl.when(t < nu_ref[0])
    def _():
        @pl.when(j == 0)
        def _():
            y_ref[...] = jnp.broadcast_to(bd_ref[0], y_ref.shape)

        x = x_ref[...]
        g = jnp.dot(x, wg_ref[0], preferred_element_type=F32) + bg_ref[0]
        lin = jnp.dot(x, wl_ref[0], preferred_element_type=F32) + bl_ref[0]
        g = jnp.minimum(g, SWIGLU_LIMIT)
        lin = jnp.clip(lin, -SWIGLU_LIMIT, SWIGLU_LIMIT)
        hid = (lin + 1.0) * (g * _sigmoid(g * SWIGLU_ALPHA))
        y_ref[...] += jnp.dot(hid.astype(BF16), wd_ref[0].astype(BF16), preferred_element_type=F32)


def _moe_experts(x_rows, tile_expert, n_used, wg, wl, bg, bl, wd, bd):
    n_rows, d = x_rows.shape
    f = wg.shape[2]
    tm, tf = MOE_TM, min(MOE_TF, f)
    n_tiles, nf = n_rows // tm, f // tf

    def tile(t, nu):
        return jnp.minimum(t, nu[0] - 1)

    def ftile(t, j, nu):
        return jnp.where(t < nu[0], j, nf - 1)

    vmem = 2 * (tm * d * 2 + 2 * d * tf * 2 + tf * d * 4 + tm * d * 4 + 2 * tf * 4 + d * 4) + 8 * tm * tf * 4
    return pl.pallas_call(
        _moe_kernel,
        out_shape=jax.ShapeDtypeStruct((n_rows, d), F32),
        grid_spec=pltpu.PrefetchScalarGridSpec(
            num_scalar_prefetch=2,
            grid=(n_tiles, nf),
            in_specs=[pl.BlockSpec((tm, d), lambda t, j, te, nu: (tile(t, nu), 0)),
                      pl.BlockSpec((1, d, tf), lambda t, j, te, nu: (te[tile(t, nu)], 0, ftile(t, j, nu))),
                      pl.BlockSpec((1, d, tf), lambda t, j, te, nu: (te[tile(t, nu)], 0, ftile(t, j, nu))),
                      pl.BlockSpec((1, 1, tf), lambda t, j, te, nu: (te[tile(t, nu)], 0, ftile(t, j, nu))),
                      pl.BlockSpec((1, 1, tf), lambda t, j, te, nu: (te[tile(t, nu)], 0, ftile(t, j, nu))),
                      pl.BlockSpec((1, tf, d), lambda t, j, te, nu: (te[tile(t, nu)], ftile(t, j, nu), 0)),
                      pl.BlockSpec((1, 1, d), lambda t, j, te, nu: (te[tile(t, nu)], 0, 0))],
            out_specs=pl.BlockSpec((tm, d), lambda t, j, te, nu: (tile(t, nu), 0))),
        compiler_params=_cparams(("arbitrary", "arbitrary"), vmem),
        name="moe_experts",
    )(tile_expert, n_used, x_rows, wg, wl, bg, bl, wd, bd)


def _route(top_idx, tm):
    n_tok = top_idx.shape[0]
    n_slots = n_tok * EXPERT_TOPK
    e_flat = top_idx.reshape(-1)
    order = jnp.argsort(e_flat)
    e_sorted = e_flat[order]
    counts = jnp.bincount(e_flat, length=N_EXPERTS)
    padded = (counts + tm - 1) // tm * tm
    pad_end = jnp.cumsum(padded)
    pad_start = pad_end - padded
    start = jnp.cumsum(counts) - counts
    dest = (pad_start[e_sorted] + jnp.arange(n_slots, dtype=I32) - start[e_sorted]).astype(I32)
    n_tiles = n_slots // tm + N_EXPERTS
    row_tok = jnp.zeros((n_tiles * tm,), I32).at[dest].set((order // EXPERT_TOPK).astype(I32))
    slot_row = jnp.zeros((n_slots,), I32).at[order].set(dest).reshape(n_tok, EXPERT_TOPK)
    tile_expert = jnp.minimum(
        jnp.searchsorted(pad_end, jnp.arange(n_tiles) * tm, side="right"), N_EXPERTS - 1).astype(I32)
    n_used = (pad_end[-1] // tm).astype(I32).reshape(1)
    return row_tok, slot_row, tile_expert, n_used


def _final_kernel(x1_ref, y_ref, tg_ref, p_ref, wpp_ref, wpg_ref, gp_ref, gfin_ref, o_ref):
    x2 = x1_ref[...]
    d = x2.shape[1]
    gates = tg_ref[...]
    for r in range(EXPERT_TOPK):
        x2 = x2 + y_ref[:, r * d:(r + 1) * d] * gates[:, r:r + 1]
    ple = jnp.dot(p_ref[...].astype(BF16), wpp_ref[...], preferred_element_type=F32)
    gate = _sigmoid(jnp.dot(x2.astype(BF16), wpg_ref[...], preferred_element_type=F32))
    x3 = x2 + _rms(gate * ple, gp_ref[...])
    o_ref[...] = _rms(x3, gfin_ref[...])


def _final(x1, y_sel, gates, p2, wpp, wpg, gp, gfin):
    seq, d = x1.shape
    pd = p2.shape[1]
    tm = min(FIN_TM, seq)
    row = lambda i: (i, 0)
    fix = lambda i: (0, 0)
    vmem = (2 * (2 * tm * d * 4 + EXPERT_TOPK * tm * d * 4 + tm * LANES * 4 + tm * pd * 4)
            + pd * d * 2 + d * d * 2 + 8 * tm * d * 4)
    return pl.pallas_call(
        _final_kernel,
        out_shape=jax.ShapeDtypeStruct((seq, d), F32),
        grid=(seq // tm,),
        in_specs=[pl.BlockSpec((tm, d), row),
                  pl.BlockSpec((tm, EXPERT_TOPK * d), row),
                  pl.BlockSpec((tm, LANES), row), pl.BlockSpec((tm, pd), row),
                  _resident((pd, d), fix), _resident((d, d), fix),
                  _resident((1, d), fix), _resident((1, d), fix)],
        out_specs=pl.BlockSpec((tm, d), row),
        compiler_params=_cparams(("parallel",), vmem),
        name="combine_ple_norm",
    )(x1, y_sel, gates, p2, wpp, wpg, gp, gfin)


def _layer(x2, p2, g_mix, w_in, g_idx_k, b_idx_k, w_br_a, w_br_b, w_out, g_ffn, w_router, b_router,
           w_gate_up, b_gate_up, w_down, b_down, w_ple_proj, w_ple_gate, g_ple, g_final):
    seq, d = x2.shape
    dsa_w, moba_w = DSA_HEADS * HEAD_DIM, MOBA_HEADS * HEAD_DIM
    sizes = (dsa_w, dsa_w, dsa_w, IDX_HEADS * IDX_DIM, IDX_DIM, IDX_HEADS, moba_w, moba_w, moba_w, d, d)
    off = np.concatenate([[0], np.cumsum(sizes)])
    col = lambda k: w_in[:, off[k]:off[k + 1]]
    row1 = lambda v: v.reshape(1, -1)
    g_mix2 = row1(g_mix)

    w_rope = jnp.concatenate([col(0), col(1), col(6), col(7)], axis=1).astype(BF16)
    w_qi = col(3).astype(BF16)
    w_kiwi = jnp.zeros((d, 2 * LANES), F32).at[:, :IDX_DIM].set(col(4)).at[:, LANES:LANES + IDX_HEADS].set(col(5))
    w_val = jnp.concatenate([col(2), col(8)], axis=1).astype(BF16)
    w_gate = jnp.concatenate([col(9), col(10)], axis=1).astype(BF16)

    rope_tabs = _rope_tables(seq, ROPE_DIM, HEAD_DIM)
    idx_tabs = _rope_tables(seq, IDX_ROPE_DIM, IDX_DIM)
    qk = _project(x2, g_mix2, w_rope, BF16, rope_tabs, ROPE_DIM // 2)
    qa, ka = qk[:, :dsa_w], qk[:, dsa_w:2 * dsa_w]
    qb, kb = qk[:, 2 * dsa_w:2 * dsa_w + moba_w], qk[:, 2 * dsa_w + moba_w:]
    qi = _project(x2, g_mix2, w_qi, BF16, idx_tabs, IDX_ROPE_DIM // 2)
    pad = lambda v: jnp.zeros((1, LANES), F32).at[0, :IDX_DIM].set(v)
    ki, wi = _project_kiwi(x2, g_mix2, w_kiwi.astype(BF16), pad(g_idx_k), pad(b_idx_k), idx_tabs,
                           (IDX_HEADS ** -0.5) * (IDX_DIM ** -0.5))
    vals = _project(x2, g_mix2, w_val, BF16)
    va, vb = vals[:, :dsa_w], vals[:, dsa_w:]
    gates_ab = _project(x2, g_mix2, w_gate, F32)
    ga, gb = gates_ab[:, :d], gates_ab[:, d:]

    bias4 = _dsa_mask(qi, ki, wi, min(DSA_TOPK, seq // 4))
    o_a = _dsa_attention(qa, ka, va, bias4)
    o_b = _moba_attention(qb, kb, vb, _block_means(kb))

    w_r = jnp.zeros((d, LANES), F32).at[:, :N_EXPERTS].set(w_router).astype(BF16)
    b_r = jnp.full((1, LANES), NEG_INF, F32).at[0, :N_EXPERTS].set(b_router)
    x1, h2, top_idx, top_gate = _post_attention(
        o_a, o_b, ga, gb, x2, w_br_a.astype(BF16), w_br_b.astype(BF16), w_out.astype(BF16),
        row1(g_ffn), w_r, b_r)

    row_tok, slot_row, tile_expert, n_used = _route(top_idx[:, :EXPERT_TOPK], MOE_TM)
    x_rows = h2[row_tok]
    bias_row = lambda b: b.reshape(N_EXPERTS, 1, -1)
    y = _moe_experts(x_rows, tile_expert, n_used,
                     *_split_gate_up(w_gate_up),
                     bias_row(b_gate_up[:, 0::2]), bias_row(b_gate_up[:, 1::2]),
                     w_down, bias_row(b_down))
    y_sel = y[slot_row.reshape(-1)].reshape(seq, EXPERT_TOPK * d)

    return _final(x1, y_sel, top_gate, p2, w_ple_proj.astype(BF16), w_ple_gate.astype(BF16), row1(g_ple),
                  row1(g_final))


def kernel(x, p, g_mix, w_in, g_idx_k, b_idx_k, w_br_a, w_br_b, w_out, g_ffn, w_router, b_router,
           w_gate_up, b_gate_up, w_down, b_down, w_ple_proj, w_ple_gate, g_ple, g_final):
    batch, seq, d = x.shape
    depth = w_in.shape[0]
    assert batch == 1 and depth == 1, "kernel handles the single-sequence, single-layer block"
    assert seq % ATT_T == 0 and seq % PROJ_TM == 0
    out = _layer(x[0], p[0, 0], g_mix[0], w_in[0], g_idx_k[0], b_idx_k[0], w_br_a[0], w_br_b[0], w_out[0],
                 g_ffn[0], w_router[0], b_router[0], w_gate_up[0], b_gate_up[0], w_down[0], b_down[0],
                 w_ple_proj[0], w_ple_gate[0], g_ple[0], g_final)
    return out[None]
```

```python
import functools

import numpy as np
import jax
import jax.numpy as jnp
from jax import lax
from jax.experimental import pallas as pl
from jax.experimental.pallas import tpu as pltpu

F32 = jnp.float32
BF16 = jnp.bfloat16
I32 = jnp.int32
NEG_INF = float("-inf")

HEAD_DIM = 128
DSA_HEADS = 8
MOBA_HEADS = 8
ROPE_DIM = HEAD_DIM // 4
ROPE_THETA = 500000.0
IDX_HEADS = 16
IDX_DIM = 64
IDX_ROPE_DIM = IDX_DIM // 4
DSA_TOPK = 256
MOBA_BLOCK = 256
MOBA_TOPK = 3
N_EXPERTS = 32
EXPERT_TOPK = 4
SWIGLU_LIMIT = 7.0
SWIGLU_ALPHA = 1.702
EPS = 1e-6

LANES = 128
VMEM_CAP_BYTES = 60000 * 1024

PROJ_TM = 512
PROJ_TN = 512
IDX_TQ = 128
IDX_CK = 512
ATT_T = 512
POST_TM = 256
MOE_TM = 512
MOE_TF = 512
FIN_TM = 256

INT_MIN = -2 ** 31
KEY_NEG_INF = 0x807FFFFF - 2 ** 32


def _cparams(sem, vmem_bytes):
    return pltpu.CompilerParams(dimension_semantics=sem,
                                vmem_limit_bytes=int(min(vmem_bytes, VMEM_CAP_BYTES)))


def _resident(shape, index_map):
    return pl.BlockSpec(shape, index_map, pipeline_mode=pl.Buffered(1))


def _sigmoid(x):
    return 1.0 / (1.0 + jnp.exp(-x))


def _rms(xf, g):
    ms = jnp.mean(xf * xf, axis=-1, keepdims=True)
    return xf * lax.rsqrt(ms + EPS) * g


def _dot_nt(a, b):
    return lax.dot_general(a, b, (((1,), (1,)), ((), ())), preferred_element_type=F32)


def _rope_tables(seq, rot_dim, period):
    half = rot_dim // 2
    inv = 1.0 / (ROPE_THETA ** (jnp.arange(half, dtype=F32) / half))
    ang = jnp.arange(seq).astype(F32)[:, None] * inv[None, :]
    cos, sin = jnp.cos(ang), jnp.sin(ang)
    z = lambda n: jnp.zeros((seq, n), F32)
    c = jnp.concatenate([cos, cos, jnp.ones((seq, period - rot_dim), F32)], axis=-1)
    s1 = jnp.concatenate([-sin, z(period - half)], axis=-1)
    s2 = jnp.concatenate([z(half), sin, z(period - rot_dim)], axis=-1)
    rep = LANES // period
    return tuple(jnp.tile(t, (1, rep)) for t in (c, s1, s2))


def _rope(a, c, s1, s2, half):
    return a * c + pltpu.roll(a, LANES - half, 1) * s1 + pltpu.roll(a, half, 1) * s2


def _proj_kernel(x_ref, g_ref, w_ref, *rest, half):
    if half is None:
        o_ref, h_ref = rest
    else:
        c_ref, s1_ref, s2_ref, o_ref, h_ref = rest

    @pl.when(pl.program_id(1) == 0)
    def _():
        h_ref[...] = _rms(x_ref[...], g_ref[...]).astype(BF16)

    acc = jnp.dot(h_ref[...], w_ref[...], preferred_element_type=F32)
    if half is None:
        o_ref[...] = acc.astype(o_ref.dtype)
    else:
        c, s1, s2 = c_ref[...], s1_ref[...], s2_ref[...]
        for k in range(acc.shape[1] // LANES):
            sl = slice(k * LANES, (k + 1) * LANES)
            o_ref[:, sl] = _rope(acc[:, sl], c, s1, s2, half).astype(o_ref.dtype)


def _project(x2, g, w, out_dtype, tables=None, half=None):
    seq, d = x2.shape
    n = w.shape[1]
    tm, tn = min(PROJ_TM, seq), min(PROJ_TN, n)
    in_specs = [pl.BlockSpec((tm, d), lambda i, j: (i, 0)),
                pl.BlockSpec((1, d), lambda i, j: (0, 0)),
                pl.BlockSpec((d, tn), lambda i, j: (0, j))]
    args = [x2, g, w]
    if tables is not None:
        in_specs += [pl.BlockSpec((tm, LANES), lambda i, j: (i, 0))] * 3
        args += list(tables)
    vmem = 2 * (tm * d * 4 + d * tn * 2 + tm * tn * 4 + 3 * tm * LANES * 4) + tm * d * 2 + 4 * tm * tn * 4
    return pl.pallas_call(
        functools.partial(_proj_kernel, half=half),
        out_shape=jax.ShapeDtypeStruct((seq, n), out_dtype),
        grid=(seq // tm, n // tn),
        in_specs=in_specs,
        out_specs=pl.BlockSpec((tm, tn), lambda i, j: (i, j)),
        scratch_shapes=[pltpu.VMEM((tm, d), BF16)],
        compiler_params=_cparams(("parallel", "arbitrary"), vmem),
        name="proj_rope" if half is not None else "proj_plain",
    )(*args)


def _proj_kiwi_kernel(x_ref, g_ref, w_ref, lg_ref, lb_ref, c_ref, s1_ref, s2_ref, ki_ref, wi_ref, *, scale):
    h = _rms(x_ref[...], g_ref[...]).astype(BF16)
    acc = jnp.dot(h, w_ref[...], preferred_element_type=F32)
    a = acc[:, :LANES]
    valid = lax.broadcasted_iota(I32, a.shape, 1) < IDX_DIM
    mu = jnp.sum(jnp.where(valid, a, 0.0), axis=-1, keepdims=True) / IDX_DIM
    dlt = jnp.where(valid, a - mu, 0.0)
    var = jnp.sum(dlt * dlt, axis=-1, keepdims=True) / IDX_DIM
    y = dlt * lax.rsqrt(var + EPS) * lg_ref[...] + lb_ref[...]
    ki_ref[...] = _rope(y, c_ref[...], s1_ref[...], s2_ref[...], IDX_ROPE_DIM // 2).astype(BF16)
    wi_ref[...] = acc[:, LANES:LANES + IDX_HEADS] * scale


def _project_kiwi(x2, g, w_kiwi, lg, lb, tables, scale):
    seq, d = x2.shape
    tm = min(PROJ_TM, seq)
    row = lambda i: (i, 0)
    fix = lambda i: (0, 0)
    vmem = 2 * (tm * d * 4 + d * 2 * LANES * 2 + 5 * tm * LANES * 4) + 8 * tm * 2 * LANES * 4 + tm * d * 6
    return pl.pallas_call(
        functools.partial(_proj_kiwi_kernel, scale=scale),
        out_shape=(jax.ShapeDtypeStruct((seq, LANES), BF16), jax.ShapeDtypeStruct((seq, IDX_HEADS), F32)),
        grid=(seq // tm,),
        in_specs=[pl.BlockSpec((tm, d), row), pl.BlockSpec((1, d), fix), pl.BlockSpec((d, 2 * LANES), fix),
                  pl.BlockSpec((1, LANES), fix), pl.BlockSpec((1, LANES), fix),
                  pl.BlockSpec((tm, LANES), row), pl.BlockSpec((tm, LANES), row), pl.BlockSpec((tm, LANES), row)],
        out_specs=(pl.BlockSpec((tm, LANES), row), pl.BlockSpec((tm, IDX_HEADS), row)),
        compiler_params=_cparams(("parallel",), vmem),
        name="proj_kiwi",
    )(x2, g, w_kiwi, lg, lb, *tables)


def _key_to_float(key):
    bits = key ^ ((key >> 31) & 0x7FFFFFFF)
    return lax.bitcast_convert_type(bits, F32)


def _indexer_kernel(qi_ref, ki_ref, wi_ref, out_ref, s_ref, qh_ref, wb_ref, cst_ref, *, n_sel, idx_bits):
    tq, ck = IDX_TQ, IDX_CK
    nslab = ck // LANES
    n_chunks_total = out_ref.shape[1]
    i = pl.program_id(0)
    nc = ((i + 1) * tq + ck - 1) // ck
    lane = lax.broadcasted_iota(I32, (tq, LANES), 1)
    row = i * tq + lax.broadcasted_iota(I32, (tq, LANES), 0)

    low = lane < IDX_DIM
    w = wi_ref[...]
    for p in range(IDX_HEADS // 2):
        pair = qi_ref[:, p * LANES:(p + 1) * LANES].astype(F32)
        qh_ref[2 * p] = jnp.where(low, pair, 0.0).astype(BF16)
        qh_ref[2 * p + 1] = jnp.where(low, pltpu.roll(pair, IDX_DIM, 1), 0.0).astype(BF16)
    for h in range(IDX_HEADS):
        wb_ref[h] = jnp.broadcast_to(w[:, h:h + 1], (tq, LANES))

    def score_chunk(c, carry):
        kc = ki_ref[pl.ds(pl.multiple_of(c * ck, ck), ck), :]
        slabs = [jnp.zeros((tq, LANES), F32) for _ in range(nslab)]
        for h in range(IDX_HEADS):
            logit = _dot_nt(qh_ref[h], kc)
            wbh = wb_ref[h]
            for k in range(nslab):
                slabs[k] = slabs[k] + jnp.maximum(logit[:, k * LANES:(k + 1) * LANES], 0.0) * wbh
        for k in range(nslab):
            col = c * ck + k * LANES + lane
            slabs[k] = jnp.where(col <= row, slabs[k], NEG_INF)
        s_ref[c] = jnp.concatenate(slabs, axis=1)
        return carry

    lax.fori_loop(0, nc, score_chunk, 0)

    def count(pred):
        def body(c, acc):
            sc = s_ref[c]
            for k in range(nslab):
                col = c * ck + k * LANES + lane
                acc = acc + jnp.where(pred(sc[:, k * LANES:(k + 1) * LANES], col), 1.0, 0.0)
            return acc
        acc = lax.fori_loop(0, nc, body, jnp.zeros((tq, LANES), F32))
        return jnp.broadcast_to(jnp.sum(acc, axis=-1, keepdims=True), (tq, LANES))

    def bit_step(b, carry):
        prefix, cnt_at = carry
        cand = prefix + lax.shift_left(jnp.int32(1), 31 - b)
        cand_f = _key_to_float(cand)
        cnt = count(lambda s, col: s >= cand_f)
        take = cnt >= n_sel
        return jnp.where(take, cand, prefix), jnp.where(take, cnt, cnt_at)

    prefix, cnt_at = lax.fori_loop(
        0, 32, bit_step, (jnp.full((tq, LANES), INT_MIN, I32), jnp.full((tq, LANES), float(n_sel), F32)))
    tau = jnp.where(prefix < KEY_NEG_INF, NEG_INF, _key_to_float(prefix))

    need = jnp.logical_and(cnt_at > n_sel, tau > NEG_INF)
    any_tie = jnp.max(jnp.where(need, 1.0, 0.0)) > 0.0
    idx_all = jnp.full((tq, LANES), 2 ** 30, I32)

    @pl.when(any_tie)
    def _():
        rem = n_sel - count(lambda s, col: s > tau)
        cut = jnp.zeros((tq, LANES), I32)
        for b in range(idx_bits - 1, -1, -1):
            cand = cut + (1 << b)
            below = count(lambda s, col: jnp.logical_and(s == tau, col < cand))
            cut = jnp.where(below < rem, cand, cut)
        cst_ref[...] = jnp.where(need, cut, idx_all)

    @pl.when(jnp.logical_not(any_tie))
    def _():
        cst_ref[...] = idx_all

    cut = cst_ref[...]

    def emit(c, carry):
        sc = s_ref[c]
        outs = []
        for k in range(nslab):
            s = sc[:, k * LANES:(k + 1) * LANES]
            col = c * ck + k * LANES + lane
            tie = jnp.where(col <= cut, 0.0, NEG_INF)
            b = jnp.where(s > tau, 0.0, jnp.where(s == tau, tie, NEG_INF))
            outs.append(jnp.where(col <= row, b, NEG_INF))
        out_ref[0, c] = jnp.concatenate(outs, axis=1).astype(BF16)
        return carry

    lax.fori_loop(0, nc, emit, 0)

    def fill(c, carry):
        out_ref[0, c] = jnp.full((tq, ck), NEG_INF, BF16)
        return carry

    lax.fori_loop(nc, n_chunks_total, fill, 0)


def _dsa_mask(qi, ki, wi, n_sel):
    seq = qi.shape[0]
    tq, ck = IDX_TQ, IDX_CK
    nq, nchunk = seq // tq, seq // ck
    idx_bits = max(1, int(seq - 1).bit_length())
    vmem = (2 * (tq * qi.shape[1] * 2 + tq * LANES * 4 + nchunk * tq * ck * 2) + seq * LANES * 2
            + nchunk * tq * ck * 4 + IDX_HEADS * tq * LANES * 6 + tq * LANES * 4 + 24 * tq * ck * 4)
    return pl.pallas_call(
        functools.partial(_indexer_kernel, n_sel=n_sel, idx_bits=idx_bits),
        out_shape=jax.ShapeDtypeStruct((nq, nchunk, tq, ck), BF16),
        grid=(nq,),
        in_specs=[pl.BlockSpec((tq, qi.shape[1]), lambda i: (i, 0)),
                  _resident((seq, LANES), lambda i: (0, 0)),
                  pl.BlockSpec((tq, IDX_HEADS), lambda i: (i, 0))],
        out_specs=pl.BlockSpec((1, nchunk, tq, ck), lambda i: (i, 0, 0, 0)),
        scratch_shapes=[pltpu.VMEM((nchunk, tq, ck), F32),
                        pltpu.VMEM((IDX_HEADS, tq, LANES), BF16),
                        pltpu.VMEM((IDX_HEADS, tq, LANES), F32),
                        pltpu.VMEM((tq, LANES), I32)],
        compiler_params=_cparams(("parallel",), vmem),
        name="dsa_indexer",
    )(qi, ki, wi)


def _tri_steps(n):
    qs = [i for i in range(n) for _ in range(i + 1)]
    ks = [j for i in range(n) for j in range(i + 1)]
    return jnp.asarray(qs, I32), jnp.asarray(ks, I32)


def _softmax_step(h, s, v_h, m_ref, l_ref, acc_ref):
    sl = slice(h * HEAD_DIM, (h + 1) * HEAD_DIM)
    m_prev = m_ref[h]
    m_new = jnp.maximum(m_prev, jnp.max(s, axis=-1, keepdims=True))
    m_safe = jnp.where(m_new == NEG_INF, 0.0, m_new)
    alpha = jnp.exp(m_prev - m_safe)
    p = jnp.exp(s - m_safe[:, :1])
    l_ref[h] = alpha * l_ref[h] + jnp.sum(p, axis=-1, keepdims=True)
    acc_ref[:, sl] = alpha * acc_ref[:, sl] + jnp.dot(p.astype(BF16), v_h, preferred_element_type=F32)
    m_ref[h] = m_new


def _attn_init(m_ref, l_ref, acc_ref):
    m_ref[...] = jnp.full(m_ref.shape, NEG_INF, F32)
    l_ref[...] = jnp.zeros(l_ref.shape, F32)
    acc_ref[...] = jnp.zeros(acc_ref.shape, F32)


def _attn_finish(o_ref, l_ref, acc_ref, nh):
    for h in range(nh):
        sl = slice(h * HEAD_DIM, (h + 1) * HEAD_DIM)
        o_ref[:, sl] = (acc_ref[:, sl] / l_ref[h]).astype(o_ref.dtype)


def _dsa_attn_kernel(qs_ref, ks_ref, q_ref, k_ref, v_ref, b_ref, o_ref, m_ref, l_ref, acc_ref, *, scale):
    s_id = pl.program_id(0)
    qi, kj = qs_ref[s_id], ks_ref[s_id]
    t = q_ref.shape[0]

    @pl.when(kj == 0)
    def _():
        _attn_init(m_ref, l_ref, acc_ref)

    bias = b_ref[...].reshape(t, t).astype(F32)
    for h in range(DSA_HEADS):
        sl = slice(h * HEAD_DIM, (h + 1) * HEAD_DIM)
        s = _dot_nt(q_ref[:, sl], k_ref[:, sl]) * scale + bias
        _softmax_step(h, s, v_ref[:, sl], m_ref, l_ref, acc_ref)

    @pl.when(kj == qi)
    def _():
        _attn_finish(o_ref, l_ref, acc_ref, DSA_HEADS)


def _dsa_attention(q, k, v, bias4):
    seq, width = q.shape
    t = min(ATT_T, seq)
    nq = seq // t
    qs, ks = _tri_steps(nq)
    sub = t // IDX_TQ
    qmap = lambda s, qs, ks: (qs[s], 0)
    kmap = lambda s, qs, ks: (ks[s], 0)
    vmem = (2 * (3 * t * width * 2 + t * t * 2 + t * width * 2) + t * width * 4
            + 2 * DSA_HEADS * t * LANES * 4 + 24 * t * t * 4)
    return pl.pallas_call(
        functools.partial(_dsa_attn_kernel, scale=HEAD_DIM ** -0.5),
        out_shape=jax.ShapeDtypeStruct((seq, width), BF16),
        grid_spec=pltpu.PrefetchScalarGridSpec(
            num_scalar_prefetch=2,
            grid=(int(qs.shape[0]),),
            in_specs=[pl.BlockSpec((t, width), qmap), pl.BlockSpec((t, width), kmap),
                      pl.BlockSpec((t, width), kmap),
                      pl.BlockSpec((sub, 1, IDX_TQ, IDX_CK), lambda s, qs, ks: (qs[s], ks[s], 0, 0))],
            out_specs=pl.BlockSpec((t, width), qmap),
            scratch_shapes=[pltpu.VMEM((DSA_HEADS, t, LANES), F32), pltpu.VMEM((DSA_HEADS, t, LANES), F32),
                            pltpu.VMEM((t, width), F32)]),
        compiler_params=_cparams(("arbitrary",), vmem),
        name="dsa_attention",
    )(qs, ks, q, k, v, bias4)


def _kmean_kernel(k_ref, o_ref):
    o_ref[0] = jnp.mean(k_ref[...].astype(F32), axis=0, keepdims=True)


def _block_means(k):
    seq, width = k.shape
    nblk = seq // MOBA_BLOCK
    out = pl.pallas_call(
        _kmean_kernel,
        out_shape=jax.ShapeDtypeStruct((nblk, 1, width), F32),
        grid=(nblk,),
        in_specs=[pl.BlockSpec((MOBA_BLOCK, width), lambda i: (i, 0))],
        out_specs=pl.BlockSpec((1, 1, width), lambda i: (i, 0, 0)),
        compiler_params=_cparams(("parallel",), 8 * MOBA_BLOCK * width * 4),
        name="moba_block_means",
    )(k)
    return out.reshape(nblk, width)


def _moba_attn_kernel(qs_ref, ks_ref, q_ref, k_ref, v_ref, km_ref, o_ref, m_ref, l_ref, acc_ref, sel_ref,
                      *, scale, n_top):
    s_id = pl.program_id(0)
    qi, kj = qs_ref[s_id], ks_ref[s_id]
    t = q_ref.shape[0]
    nblk = km_ref.shape[0]
    per_tile = t // MOBA_BLOCK
    blk = lax.broadcasted_iota(I32, (t, nblk), 1)

    @pl.when(kj == 0)
    def _():
        _attn_init(m_ref, l_ref, acc_ref)
        own = (qi * t + lax.broadcasted_iota(I32, (t, nblk), 0)) // MOBA_BLOCK
        blk_f = blk.astype(F32)
        for h in range(MOBA_HEADS):
            sl = slice(h * HEAD_DIM, (h + 1) * HEAD_DIM)
            g = _dot_nt(q_ref[:, sl], km_ref[:, sl].astype(BF16))
            g = jnp.where(blk < own, g, NEG_INF)
            sel = jnp.full((t, nblk), NEG_INF, F32)
            for _ in range(n_top):
                mx = jnp.max(g, axis=-1, keepdims=True)
                is_max = jnp.logical_and(g == mx, mx > NEG_INF)
                first = jnp.min(jnp.where(is_max, blk_f, float(nblk)), axis=-1, keepdims=True)
                pick = blk_f == first
                sel = jnp.where(pick, 0.0, sel)
                g = jnp.where(pick, NEG_INF, g)
            sel_ref[h] = sel

    def block_bias(h):
        sel = sel_ref[h]
        cols = []
        for b in range(per_tile):
            cb = jnp.max(jnp.where(blk == kj * per_tile + b, sel, NEG_INF), axis=-1, keepdims=True)
            cols.append(jnp.broadcast_to(cb, (t, MOBA_BLOCK)))
        return jnp.concatenate(cols, axis=1)

    def run(diag):
        if diag:
            r = lax.broadcasted_iota(I32, (t, t), 0)
            c = lax.broadcasted_iota(I32, (t, t), 1)
            own_blk = (r // MOBA_BLOCK) == (c // MOBA_BLOCK)
            causal = jnp.where(c <= r, 0.0, NEG_INF)
        for h in range(MOBA_HEADS):
            sl = slice(h * HEAD_DIM, (h + 1) * HEAD_DIM)
            bias = block_bias(h)
            if diag:
                bias = jnp.where(own_blk, causal, bias)
            s = _dot_nt(q_ref[:, sl], k_ref[:, sl]) * scale + bias
            _softmax_step(h, s, v_ref[:, sl], m_ref, l_ref, acc_ref)

    @pl.when(kj < qi)
    def _():
        run(False)

    @pl.when(kj == qi)
    def _():
        run(True)
        _attn_finish(o_ref, l_ref, acc_ref, MOBA_HEADS)


def _moba_attention(q, k, v, kmean):
    seq, width = q.shape
    t = min(ATT_T, seq)
    nq = seq // t
    nblk = kmean.shape[0]
    qs, ks = _tri_steps(nq)
    qmap = lambda s, qs, ks: (qs[s], 0)
    kmap = lambda s, qs, ks: (ks[s], 0)
    vmem = (2 * (4 * t * width * 2) + nblk * width * 4 + t * width * 4
            + 3 * MOBA_HEADS * t * LANES * 4 + 24 * t * t * 4)
    return pl.pallas_call(
        functools.partial(_moba_attn_kernel, scale=HEAD_DIM ** -0.5, n_top=min(MOBA_TOPK, nblk)),
        out_shape=jax.ShapeDtypeStruct((seq, width), BF16),
        grid_spec=pltpu.PrefetchScalarGridSpec(
            num_scalar_prefetch=2,
            grid=(int(qs.shape[0]),),
            in_specs=[pl.BlockSpec((t, width), qmap), pl.BlockSpec((t, width), kmap),
                      pl.BlockSpec((t, width), kmap),
                      _resident((nblk, width), lambda s, qs, ks: (0, 0))],
            out_specs=pl.BlockSpec((t, width), qmap),
            scratch_shapes=[pltpu.VMEM((MOBA_HEADS, t, LANES), F32), pltpu.VMEM((MOBA_HEADS, t, LANES), F32),
                            pltpu.VMEM((t, width), F32), pltpu.VMEM((MOBA_HEADS, t, nblk), F32)]),
        compiler_params=_cparams(("arbitrary",), vmem),
        name="moba_attention",
    )(qs, ks, q, k, v, kmean)


def _post_kernel(oa_ref, ob_ref, ga_ref, gb_ref, x_ref, wa_ref, wb_ref, wo_ref, gf_ref, wr_ref, br_ref,
                 x1_ref, h2_ref, ti_ref, tg_ref):
    ta = jnp.dot(oa_ref[...], wa_ref[...], preferred_element_type=F32)
    tb = jnp.dot(ob_ref[...], wb_ref[...], preferred_element_type=F32)
    merged = _sigmoid(ga_ref[...]) * ta + _sigmoid(gb_ref[...]) * tb
    x1 = x_ref[...] + jnp.dot(merged.astype(BF16), wo_ref[...], preferred_element_type=F32)
    x1_ref[...] = x1
    h2 = _rms(x1, gf_ref[...])
    h2_ref[...] = h2
    logits = jnp.dot(h2.astype(BF16), wr_ref[...], preferred_element_type=F32) + br_ref[...]
    lane = lax.broadcasted_iota(I32, logits.shape, 1)
    lane_f = lane.astype(F32)
    idx_out = jnp.zeros(logits.shape, I32)
    val_out = jnp.zeros(logits.shape, F32)
    top = None
    for r in range(EXPERT_TOPK):
        mx = jnp.max(logits, axis=-1, keepdims=True)
        ix = jnp.min(jnp.where(logits == mx, lane_f, float(LANES)), axis=-1, keepdims=True).astype(I32)
        if top is None:
            top = mx
        idx_out = jnp.where(lane == r, ix, idx_out)
        val_out = jnp.where(lane == r, jnp.exp(mx - top), val_out)
        logits = jnp.where(lane == ix, NEG_INF, logits)
    ti_ref[...] = idx_out
    tg_ref[...] = val_out / jnp.sum(val_out, axis=-1, keepdims=True)


def _post_attention(oa, ob, ga, gb, x2, wa, wb, wo, gf, wr, br):
    seq, d = x2.shape
    w = oa.shape[1]
    tm = min(POST_TM, seq)
    row = lambda i: (i, 0)
    fix = lambda i: (0, 0)
    vmem = (2 * (2 * tm * w * 2 + 3 * tm * d * 4 + 2 * tm * d * 4 + 2 * tm * LANES * 4)
            + 2 * w * d * 2 + d * d * 2 + d * LANES * 2 + 8 * tm * d * 4)
    return pl.pallas_call(
        _post_kernel,
        out_shape=(jax.ShapeDtypeStruct((seq, d), F32), jax.ShapeDtypeStruct((seq, d), F32),
                   jax.ShapeDtypeStruct((seq, LANES), I32), jax.ShapeDtypeStruct((seq, LANES), F32)),
        grid=(seq // tm,),
        in_specs=[pl.BlockSpec((tm, w), row), pl.BlockSpec((tm, w), row),
                  pl.BlockSpec((tm, d), row), pl.BlockSpec((tm, d), row), pl.BlockSpec((tm, d), row),
                  _resident((w, d), fix), _resident((w, d), fix), _resident((d, d), fix),
                  _resident((1, d), fix), _resident((d, LANES), fix), _resident((1, LANES), fix)],
        out_specs=(pl.BlockSpec((tm, d), row), pl.BlockSpec((tm, d), row),
                   pl.BlockSpec((tm, LANES), row), pl.BlockSpec((tm, LANES), row)),
        compiler_params=_cparams(("parallel",), vmem),
        name="merge_outproj_router",
    )(oa, ob, ga, gb, x2, wa, wb, wo, gf, wr, br)


def _split_kernel(w_ref, g_ref, l_ref):
    grp = 2 * LANES
    r = lax.broadcasted_iota(I32, (grp, grp), 0)
    c = lax.broadcasted_iota(I32, (grp, grp), 1)
    src = jnp.where(c < LANES, 2 * c, 2 * (c - LANES) + 1)
    sel = jnp.where(r == src, 1.0, 0.0).astype(BF16)
    for k in range(w_ref.shape[2] // grp):
        res = jnp.dot(w_ref[0, :, k * grp:(k + 1) * grp].astype(BF16), sel, preferred_element_type=F32)
        g_ref[0, :, k * LANES:(k + 1) * LANES] = res[:, :LANES].astype(BF16)
        l_ref[0, :, k * LANES:(k + 1) * LANES] = res[:, LANES:].astype(BF16)


def _split_gate_up(w_gate_up):
    n_e, d, f2 = w_gate_up.shape
    tc = min(2 * MOE_TF, f2)
    half = jax.ShapeDtypeStruct((n_e, d, f2 // 2), BF16)
    return pl.pallas_call(
        _split_kernel,
        out_shape=(half, half),
        grid=(n_e, f2 // tc),
        in_specs=[pl.BlockSpec((1, d, tc), lambda e, j: (e, 0, j))],
        out_specs=(pl.BlockSpec((1, d, tc // 2), lambda e, j: (e, 0, j)),
                   pl.BlockSpec((1, d, tc // 2), lambda e, j: (e, 0, j))),
        compiler_params=_cparams(("parallel", "parallel"), 2 * (d * tc * 4 + d * tc * 2) + 6 * d * tc * 2),
        name="moe_split_gate_up",
    )(w_gate_up)


def _moe_kernel(te_ref, nu_ref, idx_hbm, h_hbm, wg_ref, wl_ref, bg_ref, bl_ref, wd_ref, bd_ref, y_hbm,
                idx_ref, xf_ref, xb_ref, acc_ref, isem, gsem, ssem):
    t, j = pl.program_id(0), pl.program_id(1)
    n_tiles, nf = pl.num_programs(0), pl.num_programs(1)
    nu = nu_ref[0]
    tm, d = xb_ref.shape
    slot = lax.rem(t, 2)
    other = 1 - slot

    def idx_copy(tile, s):
        return pltpu.make_async_copy(idx_hbm.at[tile], idx_ref.at[s], isem.at[s])

    def gather_start(s):
        def body(r, carry):
            tok = idx_ref[s, 0, r]
            pltpu.make_async_copy(h_hbm.at[pl.ds(tok, 1), :], xf_ref.at[s, pl.ds(r, 1), :], gsem.at[s]).start()
            return carry
        lax.fori_loop(0, tm, body, 0)

    def gather_wait(s):
        pltpu.make_async_copy(h_hbm.at[pl.ds(0, tm), :], xf_ref.at[s], gsem.at[s]).wait()

    def scatter_start(s):
        def body(r, carry):
            dst = idx_ref[s, 1, r]
            pltpu.make_async_copy(acc_ref.at[s, pl.ds(r, 1), :], y_hbm.at[pl.ds(dst, 1), :], ssem.at[s]).start()
            return carry
        lax.fori_loop(0, tm, body, 0)

    def scatter_wait(s):
        pltpu.make_async_copy(acc_ref.at[s], y_hbm.at[pl.ds(0, tm), :], ssem.at[s]).wait()

    @pl.when(jnp.logical_and(t == 0, j == 0))
    def _():
        first = idx_copy(0, 0)
        first.start()
        acc_ref[...] = jnp.zeros(acc_ref.shape, F32)
        for s in range(2):
            fill = pltpu.make_async_copy(
                acc_ref.at[s], y_hbm.at[pl.ds(y_hbm.shape[0] - (2 - s) * tm, tm), :], ssem.at[s])
            fill.start()
            fill.wait()
        first.wait()
        gather_start(0)

    @pl.when(jnp.logical_and(j == 0, t + 1 < nu))
    def _():
        idx_copy(t + 1, other).start()

    @pl.when(jnp.logical_and(j == 0, t < nu))
    def _():
        gather_wait(slot)
        xb_ref[...] = xf_ref[slot].astype(BF16)
        acc_ref[slot] = jnp.broadcast_to(bd_ref[0], (tm, d))

    @pl.when(jnp.logical_and(j == 1, jnp.logical_and(t >= 1, t <= nu)))
    def _():
        scatter_wait(other)

    @pl.when(jnp.logical_and(j == 1, t + 1 < nu))
    def _():
        idx_copy(t + 1, other).wait()
        gather_start(other)

    @pl.when(t < nu)
    def _():
        x = xb_ref[...]
        g = jnp.dot(x, wg_ref[0], preferred_element_type=F32) + bg_ref[0]
        lin = jnp.dot(x, wl_ref[0], preferred_element_type=F32) + bl_ref[0]
        g = jnp.minimum(g, SWIGLU_LIMIT)
        lin = jnp.clip(lin, -SWIGLU_LIMIT, SWIGLU_LIMIT)
        hid = (lin + 1.0) * (g * _sigmoid(g * SWIGLU_ALPHA))
        acc_ref[slot] += jnp.dot(hid.astype(BF16), wd_ref[0].astype(BF16), preferred_element_type=F32)

    @pl.when(jnp.logical_and(j == nf - 1, t < nu))
    def _():
        scatter_start(slot)

        @pl.when(t == n_tiles - 1)
        def _():
            scatter_wait(slot)


def _moe_experts(h2, tile_idx, tile_expert, n_used, n_out_rows, wg, wl, bg, bl, wd, bd):
    d = h2.shape[1]
    f = wg.shape[2]
    n_tiles, _, tm = tile_idx.shape
    tf = min(MOE_TF, f)
    nf = f // tf
    assert nf >= 2, "the DMA schedule uses hidden-tile steps 0 and 1 of every row tile"

    def tile(t, nu):
        return jnp.minimum(t, nu[0] - 1)

    def ftile(t, j, nu):
        return jnp.where(t < nu[0], j, nf - 1)

    vmem = (2 * (2 * d * tf * 2 + tf * d * 4 + 2 * tf * 4 + d * 4) + 2 * tm * d * 4 + tm * d * 2 + 2 * tm * d * 4
            + 8 * tm * tf * 4 + tm * d * 4)
    any_spec = pl.BlockSpec(memory_space=pl.ANY)
    return pl.pallas_call(
        _moe_kernel,
        out_shape=jax.ShapeDtypeStruct((n_out_rows, d), F32),
        grid_spec=pltpu.PrefetchScalarGridSpec(
            num_scalar_prefetch=2,
            grid=(n_tiles, nf),
            in_specs=[any_spec, any_spec,
                      pl.BlockSpec((1, d, tf), lambda t, j, te, nu: (te[tile(t, nu)], 0, ftile(t, j, nu))),
                      pl.BlockSpec((1, d, tf), lambda t, j, te, nu: (te[tile(t, nu)], 0, ftile(t, j, nu))),
                      pl.BlockSpec((1, 1, tf), lambda t, j, te, nu: (te[tile(t, nu)], 0, ftile(t, j, nu))),
                      pl.BlockSpec((1, 1, tf), lambda t, j, te, nu: (te[tile(t, nu)], 0, ftile(t, j, nu))),
                      pl.BlockSpec((1, tf, d), lambda t, j, te, nu: (te[tile(t, nu)], ftile(t, j, nu), 0)),
                      pl.BlockSpec((1, 1, d), lambda t, j, te, nu: (te[tile(t, nu)], 0, 0))],
            out_specs=any_spec,
            scratch_shapes=[pltpu.SMEM((2, 2, tm), I32),
                            pltpu.VMEM((2, tm, d), F32), pltpu.VMEM((tm, d), BF16), pltpu.VMEM((2, tm, d), F32),
                            pltpu.SemaphoreType.DMA((2,)), pltpu.SemaphoreType.DMA((2,)),
                            pltpu.SemaphoreType.DMA((2,))]),
        compiler_params=_cparams(("arbitrary", "arbitrary"), vmem),
        name="moe_experts",
    )(tile_expert, n_used, tile_idx, h2, wg, wl, bg, bl, wd, bd)


def _route(top_idx, tm):
    n_tok = top_idx.shape[0]
    n_slots = n_tok * EXPERT_TOPK
    e_flat = top_idx.reshape(-1)
    order = jnp.argsort(e_flat).astype(I32)
    e_sorted = e_flat[order]
    counts = jnp.bincount(e_flat, length=N_EXPERTS)
    padded = (counts + tm - 1) // tm * tm
    pad_end = jnp.cumsum(padded)
    pad_start = pad_end - padded
    start = jnp.cumsum(counts) - counts
    dest = (pad_start[e_sorted] + jnp.arange(n_slots, dtype=I32) - start[e_sorted]).astype(I32)
    n_tiles = n_slots // tm + N_EXPERTS
    rows = jnp.arange(n_tiles * tm, dtype=I32)
    spill = n_slots + ((rows // tm) % 2) * tm + rows % tm
    row_tok = jnp.zeros((n_tiles * tm,), I32).at[dest].set(order // EXPERT_TOPK)
    row_dst = spill.at[dest].set(order)
    tile_idx = jnp.stack([row_tok.reshape(n_tiles, tm), row_dst.reshape(n_tiles, tm)], axis=1)
    tile_expert = jnp.minimum(
        jnp.searchsorted(pad_end, jnp.arange(n_tiles) * tm, side="right"), N_EXPERTS - 1).astype(I32)
    n_used = (pad_end[-1] // tm).astype(I32).reshape(1)
    return tile_idx, tile_expert, n_used, n_slots + 2 * tm


def _final_kernel(x1_ref, y_ref, tg_ref, p_ref, wpp_ref, wpg_ref, gp_ref, gfin_ref, o_ref):
    x2 = x1_ref[...]
    d = x2.shape[1]
    gates = tg_ref[...]
    for r in range(EXPERT_TOPK):
        x2 = x2 + y_ref[:, r * d:(r + 1) * d] * gates[:, r:r + 1]
    ple = jnp.dot(p_ref[...].astype(BF16), wpp_ref[...], preferred_element_type=F32)
    gate = _sigmoid(jnp.dot(x2.astype(BF16), wpg_ref[...], preferred_element_type=F32))
    x3 = x2 + _rms(gate * ple, gp_ref[...])
    o_ref[...] = _rms(x3, gfin_ref[...])


def _final(x1, y_sel, gates, p2, wpp, wpg, gp, gfin):
    seq, d = x1.shape
    pd = p2.shape[1]
    tm = min(FIN_TM, seq)
    row = lambda i: (i, 0)
    fix = lambda i: (0, 0)
    vmem = (2 * (2 * tm * d * 4 + EXPERT_TOPK * tm * d * 4 + tm * LANES * 4 + tm * pd * 4)
            + pd * d * 2 + d * d * 2 + 8 * tm * d * 4)
    return pl.pallas_call(
        _final_kernel,
        out_shape=jax.ShapeDtypeStruct((seq, d), F32),
        grid=(seq // tm,),
        in_specs=[pl.BlockSpec((tm, d), row),
                  pl.BlockSpec((tm, EXPERT_TOPK * d), row),
                  pl.BlockSpec((tm, LANES), row), pl.BlockSpec((tm, pd), row),
                  _resident((pd, d), fix), _resident((d, d), fix),
                  _resident((1, d), fix), _resident((1, d), fix)],
        out_specs=pl.BlockSpec((tm, d), row),
        compiler_params=_cparams(("parallel",), vmem),
        name="combine_ple_norm",
    )(x1, y_sel, gates, p2, wpp, wpg, gp, gfin)


def _layer(x2, p2, g_mix, w_in, g_idx_k, b_idx_k, w_br_a, w_br_b, w_out, g_ffn, w_router, b_router,
           w_gate_up, b_gate_up, w_down, b_down, w_ple_proj, w_ple_gate, g_ple, g_final):
    seq, d = x2.shape
    dsa_w, moba_w = DSA_HEADS * HEAD_DIM, MOBA_HEADS * HEAD_DIM
    sizes = (dsa_w, dsa_w, dsa_w, IDX_HEADS * IDX_DIM, IDX_DIM, IDX_HEADS, moba_w, moba_w, moba_w, d, d)
    off = np.concatenate([[0], np.cumsum(sizes)])
    col = lambda k: w_in[:, off[k]:off[k + 1]]
    row1 = lambda v: v.reshape(1, -1)
    g_mix2 = row1(g_mix)

    w_rope = jnp.concatenate([col(0), col(1), col(6), col(7)], axis=1).astype(BF16)
    w_qi = col(3).astype(BF16)
    w_kiwi = jnp.zeros((d, 2 * LANES), F32).at[:, :IDX_DIM].set(col(4)).at[:, LANES:LANES + IDX_HEADS].set(col(5))
    w_val = jnp.concatenate([col(2), col(8)], axis=1).astype(BF16)
    w_gate = jnp.concatenate([col(9), col(10)], axis=1).astype(BF16)

    rope_tabs = _rope_tables(seq, ROPE_DIM, HEAD_DIM)
    idx_tabs = _rope_tables(seq, IDX_ROPE_DIM, IDX_DIM)
    qk = _project(x2, g_mix2, w_rope, BF16, rope_tabs, ROPE_DIM // 2)
    qa, ka = qk[:, :dsa_w], qk[:, dsa_w:2 * dsa_w]
    qb, kb = qk[:, 2 * dsa_w:2 * dsa_w + moba_w], qk[:, 2 * dsa_w + moba_w:]
    qi = _project(x2, g_mix2, w_qi, BF16, idx_tabs, IDX_ROPE_DIM // 2)
    pad = lambda v: jnp.zeros((1, LANES), F32).at[0, :IDX_DIM].set(v)
    ki, wi = _project_kiwi(x2, g_mix2, w_kiwi.astype(BF16), pad(g_idx_k), pad(b_idx_k), idx_tabs,
                           (IDX_HEADS ** -0.5) * (IDX_DIM ** -0.5))
    vals = _project(x2, g_mix2, w_val, BF16)
    va, vb = vals[:, :dsa_w], vals[:, dsa_w:]
    gates_ab = _project(x2, g_mix2, w_gate, F32)
    ga, gb = gates_ab[:, :d], gates_ab[:, d:]

    bias4 = _dsa_mask(qi, ki, wi, min(DSA_TOPK, seq // 4))
    o_a = _dsa_attention(qa, ka, va, bias4)
    o_b = _moba_attention(qb, kb, vb, _block_means(kb))

    w_r = jnp.zeros((d, LANES), F32).at[:, :N_EXPERTS].set(w_router).astype(BF16)
    b_r = jnp.full((1, LANES), NEG_INF, F32).at[0, :N_EXPERTS].set(b_router)
    x1, h2, top_idx, top_gate = _post_attention(
        o_a, o_b, ga, gb, x2, w_br_a.astype(BF16), w_br_b.astype(BF16), w_out.astype(BF16),
        row1(g_ffn), w_r, b_r)

    tile_idx, tile_expert, n_used, n_out_rows = _route(top_idx[:, :EXPERT_TOPK], MOE_TM)
    bias_row = lambda b: b.reshape(N_EXPERTS, 1, -1)
    y = _moe_experts(h2, tile_idx, tile_expert, n_used, n_out_rows,
                     *_split_gate_up(w_gate_up),
                     bias_row(b_gate_up[:, 0::2]), bias_row(b_gate_up[:, 1::2]),
                     w_down, bias_row(b_down))
    y_sel = y.reshape(n_out_rows // EXPERT_TOPK, EXPERT_TOPK * d)

    return _final(x1, y_sel, top_gate, p2, w_ple_proj.astype(BF16), w_ple_gate.astype(BF16), row1(g_ple),
                  row1(g_final))


def kernel(x, p, g_mix, w_in, g_idx_k, b_idx_k, w_br_a, w_br_b, w_out, g_ffn, w_router, b_router,
           w_gate_up, b_gate_up, w_down, b_down, w_ple_proj, w_ple_gate, g_ple, g_final):
    batch, seq, d = x.shape
    depth = w_in.shape[0]
    assert batch == 1 and depth == 1, "kernel handles the single-sequence, single-layer block"
    assert seq % ATT_T == 0 and seq % PROJ_TM == 0
    out = _layer(x[0], p[0, 0], g_mix[0], w_in[0], g_idx_k[0], b_idx_k[0], w_br_a[0], w_br_b[0], w_out[0],
                 g_ffn[0], w_router[0], b_router[0], w_gate_up[0], b_gate_up[0], w_down[0], b_down[0],
                 w_ple_proj[0], w_ple_gate[0], g_ple[0], g_final)
    return out[None]
```

```python
import functools

import numpy as np
import jax
import jax.numpy as jnp
from jax import lax
from jax.experimental import pallas as pl
from jax.experimental.pallas import tpu as pltpu

F32 = jnp.float32
BF16 = jnp.bfloat16
I32 = jnp.int32
NEG_INF = float("-inf")

HEAD_DIM = 128
DSA_HEADS = 8
MOBA_HEADS = 8
ROPE_DIM = HEAD_DIM // 4
ROPE_THETA = 500000.0
IDX_HEADS = 16
IDX_DIM = 64
IDX_ROPE_DIM = IDX_DIM // 4
DSA_TOPK = 256
MOBA_BLOCK = 256
MOBA_TOPK = 3
N_EXPERTS = 32
EXPERT_TOPK = 4
SWIGLU_LIMIT = 7.0
SWIGLU_ALPHA = 1.702
EPS = 1e-6

LANES = 128
VMEM_CAP_BYTES = 60000 * 1024

PROJ_TM = 512
PROJ_TN = 512
IDX_TQ = 128
IDX_CK = 512
ATT_T = 512
POST_TM = 256
MOE_TM = 512
MOE_TF = 512
MOE_ROW_UNROLL = 8
FIN_TM = 256

INT_MIN = -2 ** 31
KEY_NEG_INF = 0x807FFFFF - 2 ** 32


def _cparams(sem, vmem_bytes):
    return pltpu.CompilerParams(dimension_semantics=sem,
                                vmem_limit_bytes=int(min(vmem_bytes, VMEM_CAP_BYTES)))


def _resident(shape, index_map):
    return pl.BlockSpec(shape, index_map, pipeline_mode=pl.Buffered(1))


def _sigmoid(x):
    return 1.0 / (1.0 + jnp.exp(-x))


def _rms(xf, g):
    ms = jnp.mean(xf * xf, axis=-1, keepdims=True)
    return xf * lax.rsqrt(ms + EPS) * g


def _dot_nt(a, b):
    return lax.dot_general(a, b, (((1,), (1,)), ((), ())), preferred_element_type=F32)


def _rope_tables(seq, rot_dim, period):
    half = rot_dim // 2
    inv = 1.0 / (ROPE_THETA ** (jnp.arange(half, dtype=F32) / half))
    ang = jnp.arange(seq).astype(F32)[:, None] * inv[None, :]
    cos, sin = jnp.cos(ang), jnp.sin(ang)
    z = lambda n: jnp.zeros((seq, n), F32)
    c = jnp.concatenate([cos, cos, jnp.ones((seq, period - rot_dim), F32)], axis=-1)
    s1 = jnp.concatenate([-sin, z(period - half)], axis=-1)
    s2 = jnp.concatenate([z(half), sin, z(period - rot_dim)], axis=-1)
    rep = LANES // period
    return tuple(jnp.tile(t, (1, rep)) for t in (c, s1, s2))


def _rope(a, c, s1, s2, half):
    return a * c + pltpu.roll(a, LANES - half, 1) * s1 + pltpu.roll(a, half, 1) * s2


def _proj_kernel(x_ref, g_ref, w_ref, *rest, half):
    if half is None:
        o_ref, h_ref = rest
    else:
        c_ref, s1_ref, s2_ref, o_ref, h_ref = rest

    @pl.when(pl.program_id(1) == 0)
    def _():
        h_ref[...] = _rms(x_ref[...], g_ref[...]).astype(BF16)

    acc = jnp.dot(h_ref[...], w_ref[...], preferred_element_type=F32)
    if half is None:
        o_ref[...] = acc.astype(o_ref.dtype)
    else:
        c, s1, s2 = c_ref[...], s1_ref[...], s2_ref[...]
        for k in range(acc.shape[1] // LANES):
            sl = slice(k * LANES, (k + 1) * LANES)
            o_ref[:, sl] = _rope(acc[:, sl], c, s1, s2, half).astype(o_ref.dtype)


def _project(x2, g, w, out_dtype, tables=None, half=None):
    seq, d = x2.shape
    n = w.shape[1]
    tm, tn = min(PROJ_TM, seq), min(PROJ_TN, n)
    in_specs = [pl.BlockSpec((tm, d), lambda i, j: (i, 0)),
                pl.BlockSpec((1, d), lambda i, j: (0, 0)),
                pl.BlockSpec((d, tn), lambda i, j: (0, j))]
    args = [x2, g, w]
    if tables is not None:
        in_specs += [pl.BlockSpec((tm, LANES), lambda i, j: (i, 0))] * 3
        args += list(tables)
    vmem = 2 * (tm * d * 4 + d * tn * 2 + tm * tn * 4 + 3 * tm * LANES * 4) + tm * d * 2 + 4 * tm * tn * 4
    return pl.pallas_call(
        functools.partial(_proj_kernel, half=half),
        out_shape=jax.ShapeDtypeStruct((seq, n), out_dtype),
        grid=(seq // tm, n // tn),
        in_specs=in_specs,
        out_specs=pl.BlockSpec((tm, tn), lambda i, j: (i, j)),
        scratch_shapes=[pltpu.VMEM((tm, d), BF16)],
        compiler_params=_cparams(("parallel", "arbitrary"), vmem),
        name="proj_rope" if half is not None else "proj_plain",
    )(*args)


def _proj_kiwi_kernel(x_ref, g_ref, w_ref, lg_ref, lb_ref, c_ref, s1_ref, s2_ref, ki_ref, wi_ref, *, scale):
    h = _rms(x_ref[...], g_ref[...]).astype(BF16)
    acc = jnp.dot(h, w_ref[...], preferred_element_type=F32)
    a = acc[:, :LANES]
    valid = lax.broadcasted_iota(I32, a.shape, 1) < IDX_DIM
    mu = jnp.sum(jnp.where(valid, a, 0.0), axis=-1, keepdims=True) / IDX_DIM
    dlt = jnp.where(valid, a - mu, 0.0)
    var = jnp.sum(dlt * dlt, axis=-1, keepdims=True) / IDX_DIM
    y = dlt * lax.rsqrt(var + EPS) * lg_ref[...] + lb_ref[...]
    ki_ref[...] = _rope(y, c_ref[...], s1_ref[...], s2_ref[...], IDX_ROPE_DIM // 2).astype(BF16)
    wi_ref[...] = acc[:, LANES:LANES + IDX_HEADS] * scale


def _project_kiwi(x2, g, w_kiwi, lg, lb, tables, scale):
    seq, d = x2.shape
    tm = min(PROJ_TM, seq)
    row = lambda i: (i, 0)
    fix = lambda i: (0, 0)
    vmem = 2 * (tm * d * 4 + d * 2 * LANES * 2 + 5 * tm * LANES * 4) + 8 * tm * 2 * LANES * 4 + tm * d * 6
    return pl.pallas_call(
        functools.partial(_proj_kiwi_kernel, scale=scale),
        out_shape=(jax.ShapeDtypeStruct((seq, LANES), BF16), jax.ShapeDtypeStruct((seq, IDX_HEADS), F32)),
        grid=(seq // tm,),
        in_specs=[pl.BlockSpec((tm, d), row), pl.BlockSpec((1, d), fix), pl.BlockSpec((d, 2 * LANES), fix),
                  pl.BlockSpec((1, LANES), fix), pl.BlockSpec((1, LANES), fix),
                  pl.BlockSpec((tm, LANES), row), pl.BlockSpec((tm, LANES), row), pl.BlockSpec((tm, LANES), row)],
        out_specs=(pl.BlockSpec((tm, LANES), row), pl.BlockSpec((tm, IDX_HEADS), row)),
        compiler_params=_cparams(("parallel",), vmem),
        name="proj_kiwi",
    )(x2, g, w_kiwi, lg, lb, *tables)


def _key_to_float(key):
    bits = key ^ ((key >> 31) & 0x7FFFFFFF)
    return lax.bitcast_convert_type(bits, F32)


def _indexer_kernel(qi_ref, ki_ref, wi_ref, out_ref, s_ref, qh_ref, wb_ref, cst_ref, *, n_sel, idx_bits):
    tq, ck = IDX_TQ, IDX_CK
    nslab = ck // LANES
    n_chunks_total = out_ref.shape[1]
    i = pl.program_id(0)
    nc = ((i + 1) * tq + ck - 1) // ck
    lane = lax.broadcasted_iota(I32, (tq, LANES), 1)
    row = i * tq + lax.broadcasted_iota(I32, (tq, LANES), 0)

    low = lane < IDX_DIM
    w = wi_ref[...]
    for p in range(IDX_HEADS // 2):
        pair = qi_ref[:, p * LANES:(p + 1) * LANES].astype(F32)
        qh_ref[2 * p] = jnp.where(low, pair, 0.0).astype(BF16)
        qh_ref[2 * p + 1] = jnp.where(low, pltpu.roll(pair, IDX_DIM, 1), 0.0).astype(BF16)
    for h in range(IDX_HEADS):
        wb_ref[h] = jnp.broadcast_to(w[:, h:h + 1], (tq, LANES))

    def score_chunk(c, carry):
        kc = ki_ref[pl.ds(pl.multiple_of(c * ck, ck), ck), :]
        slabs = [jnp.zeros((tq, LANES), F32) for _ in range(nslab)]
        for h in range(IDX_HEADS):
            logit = _dot_nt(qh_ref[h], kc)
            wbh = wb_ref[h]
            for k in range(nslab):
                slabs[k] = slabs[k] + jnp.maximum(logit[:, k * LANES:(k + 1) * LANES], 0.0) * wbh
        for k in range(nslab):
            col = c * ck + k * LANES + lane
            slabs[k] = jnp.where(col <= row, slabs[k], NEG_INF)
        s_ref[c] = jnp.concatenate(slabs, axis=1)
        return carry

    lax.fori_loop(0, nc, score_chunk, 0)

    def count(pred):
        def body(c, acc):
            sc = s_ref[c]
            for k in range(nslab):
                col = c * ck + k * LANES + lane
                acc = acc + jnp.where(pred(sc[:, k * LANES:(k + 1) * LANES], col), 1.0, 0.0)
            return acc
        acc = lax.fori_loop(0, nc, body, jnp.zeros((tq, LANES), F32))
        return jnp.broadcast_to(jnp.sum(acc, axis=-1, keepdims=True), (tq, LANES))

    def bit_step(b, carry):
        prefix, cnt_at = carry
        cand = prefix + lax.shift_left(jnp.int32(1), 31 - b)
        cand_f = _key_to_float(cand)
        cnt = count(lambda s, col: s >= cand_f)
        take = cnt >= n_sel
        return jnp.where(take, cand, prefix), jnp.where(take, cnt, cnt_at)

    prefix, cnt_at = lax.fori_loop(
        0, 32, bit_step, (jnp.full((tq, LANES), INT_MIN, I32), jnp.full((tq, LANES), float(n_sel), F32)))
    tau = jnp.where(prefix < KEY_NEG_INF, NEG_INF, _key_to_float(prefix))

    need = jnp.logical_and(cnt_at > n_sel, tau > NEG_INF)
    any_tie = jnp.max(jnp.where(need, 1.0, 0.0)) > 0.0
    idx_all = jnp.full((tq, LANES), 2 ** 30, I32)

    @pl.when(any_tie)
    def _():
        rem = n_sel - count(lambda s, col: s > tau)
        cut = jnp.zeros((tq, LANES), I32)
        for b in range(idx_bits - 1, -1, -1):
            cand = cut + (1 << b)
            below = count(lambda s, col: jnp.logical_and(s == tau, col < cand))
            cut = jnp.where(below < rem, cand, cut)
        cst_ref[...] = jnp.where(need, cut, idx_all)

    @pl.when(jnp.logical_not(any_tie))
    def _():
        cst_ref[...] = idx_all

    cut = cst_ref[...]

    def emit(c, carry):
        sc = s_ref[c]
        outs = []
        for k in range(nslab):
            s = sc[:, k * LANES:(k + 1) * LANES]
            col = c * ck + k * LANES + lane
            tie = jnp.where(col <= cut, 0.0, NEG_INF)
            b = jnp.where(s > tau, 0.0, jnp.where(s == tau, tie, NEG_INF))
            outs.append(jnp.where(col <= row, b, NEG_INF))
        out_ref[0, c] = jnp.concatenate(outs, axis=1).astype(BF16)
        return carry

    lax.fori_loop(0, nc, emit, 0)

    def fill(c, carry):
        out_ref[0, c] = jnp.full((tq, ck), NEG_INF, BF16)
        return carry

    lax.fori_loop(nc, n_chunks_total, fill, 0)


def _dsa_mask(qi, ki, wi, n_sel):
    seq = qi.shape[0]
    tq, ck = IDX_TQ, IDX_CK
    nq, nchunk = seq // tq, seq // ck
    idx_bits = max(1, int(seq - 1).bit_length())
    vmem = (2 * (tq * qi.shape[1] * 2 + tq * LANES * 4 + nchunk * tq * ck * 2) + seq * LANES * 2
            + nchunk * tq * ck * 4 + IDX_HEADS * tq * LANES * 6 + tq * LANES * 4 + 24 * tq * ck * 4)
    return pl.pallas_call(
        functools.partial(_indexer_kernel, n_sel=n_sel, idx_bits=idx_bits),
        out_shape=jax.ShapeDtypeStruct((nq, nchunk, tq, ck), BF16),
        grid=(nq,),
        in_specs=[pl.BlockSpec((tq, qi.shape[1]), lambda i: (i, 0)),
                  _resident((seq, LANES), lambda i: (0, 0)),
                  pl.BlockSpec((tq, IDX_HEADS), lambda i: (i, 0))],
        out_specs=pl.BlockSpec((1, nchunk, tq, ck), lambda i: (i, 0, 0, 0)),
        scratch_shapes=[pltpu.VMEM((nchunk, tq, ck), F32),
                        pltpu.VMEM((IDX_HEADS, tq, LANES), BF16),
                        pltpu.VMEM((IDX_HEADS, tq, LANES), F32),
                        pltpu.VMEM((tq, LANES), I32)],
        compiler_params=_cparams(("parallel",), vmem),
        name="dsa_indexer",
    )(qi, ki, wi)


def _tri_steps(n):
    qs = [i for i in range(n) for _ in range(i + 1)]
    ks = [j for i in range(n) for j in range(i + 1)]
    return jnp.asarray(qs, I32), jnp.asarray(ks, I32)


def _softmax_step(h, s, v_h, m_ref, l_ref, acc_ref):
    sl = slice(h * HEAD_DIM, (h + 1) * HEAD_DIM)
    m_prev = m_ref[h]
    m_new = jnp.maximum(m_prev, jnp.max(s, axis=-1, keepdims=True))
    m_safe = jnp.where(m_new == NEG_INF, 0.0, m_new)
    alpha = jnp.exp(m_prev - m_safe)
    p = jnp.exp(s - m_safe[:, :1])
    l_ref[h] = alpha * l_ref[h] + jnp.sum(p, axis=-1, keepdims=True)
    acc_ref[:, sl] = alpha * acc_ref[:, sl] + jnp.dot(p.astype(BF16), v_h, preferred_element_type=F32)
    m_ref[h] = m_new


def _attn_init(m_ref, l_ref, acc_ref):
    m_ref[...] = jnp.full(m_ref.shape, NEG_INF, F32)
    l_ref[...] = jnp.zeros(l_ref.shape, F32)
    acc_ref[...] = jnp.zeros(acc_ref.shape, F32)


def _attn_finish(o_ref, l_ref, acc_ref, nh):
    for h in range(nh):
        sl = slice(h * HEAD_DIM, (h + 1) * HEAD_DIM)
        o_ref[:, sl] = (acc_ref[:, sl] / l_ref[h]).astype(o_ref.dtype)


def _dsa_attn_kernel(qs_ref, ks_ref, q_ref, k_ref, v_ref, b_ref, o_ref, m_ref, l_ref, acc_ref, *, scale):
    s_id = pl.program_id(0)
    qi, kj = qs_ref[s_id], ks_ref[s_id]
    t = q_ref.shape[0]

    @pl.when(kj == 0)
    def _():
        _attn_init(m_ref, l_ref, acc_ref)

    bias = b_ref[...].reshape(t, t).astype(F32)
    for h in range(DSA_HEADS):
        sl = slice(h * HEAD_DIM, (h + 1) * HEAD_DIM)
        s = _dot_nt(q_ref[:, sl], k_ref[:, sl]) * scale + bias
        _softmax_step(h, s, v_ref[:, sl], m_ref, l_ref, acc_ref)

    @pl.when(kj == qi)
    def _():
        _attn_finish(o_ref, l_ref, acc_ref, DSA_HEADS)


def _dsa_attention(qk, vals, bias4, qcol, kcol, vcol):
    seq, width = qk.shape[0], DSA_HEADS * HEAD_DIM
    t = min(ATT_T, seq)
    nq = seq // t
    qs, ks = _tri_steps(nq)
    sub = t // IDX_TQ
    omap = lambda s, qs, ks: (qs[s], 0)
    qmap = lambda s, qs, ks: (qs[s], qcol)
    kmap = lambda s, qs, ks: (ks[s], kcol)
    vmap = lambda s, qs, ks: (ks[s], vcol)
    vmem = (2 * (3 * t * width * 2 + t * t * 2 + t * width * 2) + t * width * 4
            + 2 * DSA_HEADS * t * LANES * 4 + 24 * t * t * 4)
    return pl.pallas_call(
        functools.partial(_dsa_attn_kernel, scale=HEAD_DIM ** -0.5),
        out_shape=jax.ShapeDtypeStruct((seq, width), BF16),
        grid_spec=pltpu.PrefetchScalarGridSpec(
            num_scalar_prefetch=2,
            grid=(int(qs.shape[0]),),
            in_specs=[pl.BlockSpec((t, width), qmap), pl.BlockSpec((t, width), kmap),
                      pl.BlockSpec((t, width), vmap),
                      pl.BlockSpec((sub, 1, IDX_TQ, IDX_CK), lambda s, qs, ks: (qs[s], ks[s], 0, 0))],
            out_specs=pl.BlockSpec((t, width), omap),
            scratch_shapes=[pltpu.VMEM((DSA_HEADS, t, LANES), F32), pltpu.VMEM((DSA_HEADS, t, LANES), F32),
                            pltpu.VMEM((t, width), F32)]),
        compiler_params=_cparams(("arbitrary",), vmem),
        name="dsa_attention",
    )(qs, ks, qk, qk, vals, bias4)


def _kmean_kernel(k_ref, o_ref):
    o_ref[0] = jnp.mean(k_ref[...].astype(F32), axis=0, keepdims=True)


def _block_means(qk, kcol):
    seq, width = qk.shape[0], MOBA_HEADS * HEAD_DIM
    nblk = seq // MOBA_BLOCK
    out = pl.pallas_call(
        _kmean_kernel,
        out_shape=jax.ShapeDtypeStruct((nblk, 1, width), F32),
        grid=(nblk,),
        in_specs=[pl.BlockSpec((MOBA_BLOCK, width), lambda i: (i, kcol))],
        out_specs=pl.BlockSpec((1, 1, width), lambda i: (i, 0, 0)),
        compiler_params=_cparams(("parallel",), 8 * MOBA_BLOCK * width * 4),
        name="moba_block_means",
    )(qk)
    return out.reshape(nblk, width)


def _moba_attn_kernel(qs_ref, ks_ref, q_ref, k_ref, v_ref, km_ref, o_ref, m_ref, l_ref, acc_ref, sel_ref,
                      *, scale, n_top):
    s_id = pl.program_id(0)
    qi, kj = qs_ref[s_id], ks_ref[s_id]
    t = q_ref.shape[0]
    nblk = km_ref.shape[0]
    per_tile = t // MOBA_BLOCK
    blk = lax.broadcasted_iota(I32, (t, nblk), 1)

    @pl.when(kj == 0)
    def _():
        _attn_init(m_ref, l_ref, acc_ref)
        own = (qi * t + lax.broadcasted_iota(I32, (t, nblk), 0)) // MOBA_BLOCK
        blk_f = blk.astype(F32)
        for h in range(MOBA_HEADS):
            sl = slice(h * HEAD_DIM, (h + 1) * HEAD_DIM)
            g = _dot_nt(q_ref[:, sl], km_ref[:, sl].astype(BF16))
            g = jnp.where(blk < own, g, NEG_INF)
            sel = jnp.full((t, nblk), NEG_INF, F32)
            for _ in range(n_top):
                mx = jnp.max(g, axis=-1, keepdims=True)
                is_max = jnp.logical_and(g == mx, mx > NEG_INF)
                first = jnp.min(jnp.where(is_max, blk_f, float(nblk)), axis=-1, keepdims=True)
                pick = blk_f == first
                sel = jnp.where(pick, 0.0, sel)
                g = jnp.where(pick, NEG_INF, g)
            sel_ref[h] = sel

    def block_bias(h):
        sel = sel_ref[h]
        cols = []
        for b in range(per_tile):
            cb = jnp.max(jnp.where(blk == kj * per_tile + b, sel, NEG_INF), axis=-1, keepdims=True)
            cols.append(jnp.broadcast_to(cb, (t, MOBA_BLOCK)))
        return jnp.concatenate(cols, axis=1)

    def run(diag):
        if diag:
            r = lax.broadcasted_iota(I32, (t, t), 0)
            c = lax.broadcasted_iota(I32, (t, t), 1)
            own_blk = (r // MOBA_BLOCK) == (c // MOBA_BLOCK)
            causal = jnp.where(c <= r, 0.0, NEG_INF)
        for h in range(MOBA_HEADS):
            sl = slice(h * HEAD_DIM, (h + 1) * HEAD_DIM)
            bias = block_bias(h)
            if diag:
                bias = jnp.where(own_blk, causal, bias)
            s = _dot_nt(q_ref[:, sl], k_ref[:, sl]) * scale + bias
            _softmax_step(h, s, v_ref[:, sl], m_ref, l_ref, acc_ref)

    @pl.when(kj < qi)
    def _():
        run(False)

    @pl.when(kj == qi)
    def _():
        run(True)
        _attn_finish(o_ref, l_ref, acc_ref, MOBA_HEADS)


def _moba_attention(qk, vals, kmean, qcol, kcol, vcol):
    seq, width = qk.shape[0], MOBA_HEADS * HEAD_DIM
    t = min(ATT_T, seq)
    nq = seq // t
    nblk = kmean.shape[0]
    qs, ks = _tri_steps(nq)
    omap = lambda s, qs, ks: (qs[s], 0)
    qmap = lambda s, qs, ks: (qs[s], qcol)
    kmap = lambda s, qs, ks: (ks[s], kcol)
    vmap = lambda s, qs, ks: (ks[s], vcol)
    vmem = (2 * (4 * t * width * 2) + nblk * width * 4 + t * width * 4
            + 3 * MOBA_HEADS * t * LANES * 4 + 24 * t * t * 4)
    return pl.pallas_call(
        functools.partial(_moba_attn_kernel, scale=HEAD_DIM ** -0.5, n_top=min(MOBA_TOPK, nblk)),
        out_shape=jax.ShapeDtypeStruct((seq, width), BF16),
        grid_spec=pltpu.PrefetchScalarGridSpec(
            num_scalar_prefetch=2,
            grid=(int(qs.shape[0]),),
            in_specs=[pl.BlockSpec((t, width), qmap), pl.BlockSpec((t, width), kmap),
                      pl.BlockSpec((t, width), vmap),
                      _resident((nblk, width), lambda s, qs, ks: (0, 0))],
            out_specs=pl.BlockSpec((t, width), omap),
            scratch_shapes=[pltpu.VMEM((MOBA_HEADS, t, LANES), F32), pltpu.VMEM((MOBA_HEADS, t, LANES), F32),
                            pltpu.VMEM((t, width), F32), pltpu.VMEM((MOBA_HEADS, t, nblk), F32)]),
        compiler_params=_cparams(("arbitrary",), vmem),
        name="moba_attention",
    )(qs, ks, qk, qk, vals, kmean)


def _post_kernel(oa_ref, ob_ref, ga_ref, gb_ref, x_ref, wa_ref, wb_ref, wo_ref, gf_ref, wr_ref, br_ref,
                 x1_ref, h2_ref, ti_ref, tg_ref):
    ta = jnp.dot(oa_ref[...], wa_ref[...], preferred_element_type=F32)
    tb = jnp.dot(ob_ref[...], wb_ref[...], preferred_element_type=F32)
    merged = _sigmoid(ga_ref[...]) * ta + _sigmoid(gb_ref[...]) * tb
    x1 = x_ref[...] + jnp.dot(merged.astype(BF16), wo_ref[...], preferred_element_type=F32)
    x1_ref[...] = x1
    h2 = _rms(x1, gf_ref[...])
    h2_ref[...] = h2
    logits = jnp.dot(h2.astype(BF16), wr_ref[...], preferred_element_type=F32) + br_ref[...]
    lane = lax.broadcasted_iota(I32, logits.shape, 1)
    lane_f = lane.astype(F32)
    idx_out = jnp.zeros(logits.shape, I32)
    val_out = jnp.zeros(logits.shape, F32)
    top = None
    for r in range(EXPERT_TOPK):
        mx = jnp.max(logits, axis=-1, keepdims=True)
        ix = jnp.min(jnp.where(logits == mx, lane_f, float(LANES)), axis=-1, keepdims=True).astype(I32)
        if top is None:
            top = mx
        idx_out = jnp.where(lane == r, ix, idx_out)
        val_out = jnp.where(lane == r, jnp.exp(mx - top), val_out)
        logits = jnp.where(lane == ix, NEG_INF, logits)
    ti_ref[...] = idx_out
    tg_ref[...] = val_out / jnp.sum(val_out, axis=-1, keepdims=True)


def _post_attention(oa, ob, gates_ab, x2, wa, wb, wo, gf, wr, br):
    seq, d = x2.shape
    w = oa.shape[1]
    tm = min(POST_TM, seq)
    row = lambda i: (i, 0)
    fix = lambda i: (0, 0)
    vmem = (2 * (2 * tm * w * 2 + 3 * tm * d * 4 + 2 * tm * d * 4 + 2 * tm * LANES * 4)
            + 2 * w * d * 2 + d * d * 2 + d * LANES * 2 + 8 * tm * d * 4)
    return pl.pallas_call(
        _post_kernel,
        out_shape=(jax.ShapeDtypeStruct((seq, d), F32), jax.ShapeDtypeStruct((seq, d), F32),
                   jax.ShapeDtypeStruct((seq, LANES), I32), jax.ShapeDtypeStruct((seq, LANES), F32)),
        grid=(seq // tm,),
        in_specs=[pl.BlockSpec((tm, w), row), pl.BlockSpec((tm, w), row),
                  pl.BlockSpec((tm, d), row), pl.BlockSpec((tm, d), lambda i: (i, 1)), pl.BlockSpec((tm, d), row),
                  _resident((w, d), fix), _resident((w, d), fix), _resident((d, d), fix),
                  _resident((1, d), fix), _resident((d, LANES), fix), _resident((1, LANES), fix)],
        out_specs=(pl.BlockSpec((tm, d), row), pl.BlockSpec((tm, d), row),
                   pl.BlockSpec((tm, LANES), row), pl.BlockSpec((tm, LANES), row)),
        compiler_params=_cparams(("parallel",), vmem),
        name="merge_outproj_router",
    )(oa, ob, gates_ab, gates_ab, x2, wa, wb, wo, gf, wr, br)


def _split_kernel(w_ref, g_ref, l_ref):
    grp = 2 * LANES
    r = lax.broadcasted_iota(I32, (grp, grp), 0)
    c = lax.broadcasted_iota(I32, (grp, grp), 1)
    src = jnp.where(c < LANES, 2 * c, 2 * (c - LANES) + 1)
    sel = jnp.where(r == src, 1.0, 0.0).astype(BF16)
    for k in range(w_ref.shape[2] // grp):
        res = jnp.dot(w_ref[0, :, k * grp:(k + 1) * grp].astype(BF16), sel, preferred_element_type=F32)
        g_ref[0, :, k * LANES:(k + 1) * LANES] = res[:, :LANES].astype(BF16)
        l_ref[0, :, k * LANES:(k + 1) * LANES] = res[:, LANES:].astype(BF16)


def _split_gate_up(w_gate_up):
    n_e, d, f2 = w_gate_up.shape
    tc = min(2 * MOE_TF, f2)
    half = jax.ShapeDtypeStruct((n_e, d, f2 // 2), BF16)
    return pl.pallas_call(
        _split_kernel,
        out_shape=(half, half),
        grid=(n_e, f2 // tc),
        in_specs=[pl.BlockSpec((1, d, tc), lambda e, j: (e, 0, j))],
        out_specs=(pl.BlockSpec((1, d, tc // 2), lambda e, j: (e, 0, j)),
                   pl.BlockSpec((1, d, tc // 2), lambda e, j: (e, 0, j))),
        compiler_params=_cparams(("parallel", "parallel"), 2 * (d * tc * 4 + d * tc * 2) + 6 * d * tc * 2),
        name="moe_split_gate_up",
    )(w_gate_up)


def _moe_kernel(te_ref, nu_ref, idx_hbm, h_hbm, wg_ref, wl_ref, bg_ref, bl_ref, wd_ref, bd_ref, y_hbm,
                idx0_ref, idx1_ref, xf_ref, xb_ref, acc_ref, isem, gsem, ssem):
    t, j = pl.program_id(0), pl.program_id(1)
    n_tiles, nf = pl.num_programs(0), pl.num_programs(1)
    nu = nu_ref[0]
    tm, d = xb_ref.shape
    sub = xf_ref.shape[2]
    slot = lax.rem(t, 2)
    other = 1 - slot
    idx_refs = (idx0_ref, idx1_ref)

    def on_slot(dyn_slot, fn):
        for s in range(2):
            pl.when(dyn_slot == s)(functools.partial(fn, s))

    def idx_copy(tile, s):
        return pltpu.make_async_copy(idx_hbm.at[pl.ds(pl.multiple_of(tile * 2 * tm, 2 * tm), 2 * tm)],
                                     idx_refs[s], isem.at[s])

    def row_loop(issue):
        def body(i, carry):
            for k in range(sub):
                issue(i, k)
            return carry
        lax.fori_loop(0, tm // sub, body, 0)

    def gather_start(s):
        def issue(i, k):
            tok = idx_refs[s][i * sub + k]
            pltpu.make_async_copy(h_hbm.at[pl.ds(tok, 1), :], xf_ref.at[s, i, pl.ds(k, 1), :], gsem.at[s]).start()
        row_loop(issue)

    def gather_wait(s):
        pltpu.make_async_copy(xf_ref.at[s], xf_ref.at[s], gsem.at[s]).wait()

    def scatter_start(s):
        def issue(i, k):
            dst = idx_refs[s][tm + i * sub + k]
            pltpu.make_async_copy(acc_ref.at[s, i, pl.ds(k, 1), :], y_hbm.at[pl.ds(dst, 1), :], ssem.at[s]).start()
        row_loop(issue)

    def scatter_wait(s):
        pltpu.make_async_copy(acc_ref.at[s], acc_ref.at[s], ssem.at[s]).wait()

    @pl.when(jnp.logical_and(t == 0, j == 0))
    def _():
        first = idx_copy(0, 0)
        first.start()
        acc_ref[...] = jnp.zeros(acc_ref.shape, F32)
        spill0 = y_hbm.shape[0] - 2 * tm
        for s in range(2):
            def fill(i, carry):
                row = pl.multiple_of(spill0 + s * tm + i * sub, sub)
                pltpu.make_async_copy(acc_ref.at[s, i], y_hbm.at[pl.ds(row, sub), :], ssem.at[s]).start()
                return carry
            lax.fori_loop(0, tm // sub, fill, 0)
            scatter_wait(s)
        first.wait()
        gather_start(0)

    @pl.when(jnp.logical_and(j == 0, t + 1 < nu))
    def _():
        on_slot(other, lambda s: idx_copy(t + 1, s).start())

    @pl.when(jnp.logical_and(j == 0, t < nu))
    def _():
        on_slot(slot, gather_wait)
        xb_ref[...] = xf_ref[slot].reshape(tm, d).astype(BF16)
        acc_ref[slot] = jnp.broadcast_to(bd_ref[0], (tm // sub, sub, d))

    @pl.when(jnp.logical_and(j == 1, jnp.logical_and(t >= 1, t <= nu)))
    def _():
        on_slot(other, scatter_wait)

    @pl.when(jnp.logical_and(j == 1, t + 1 < nu))
    def _():
        def fetch(s):
            idx_copy(t + 1, s).wait()
            gather_start(s)
        on_slot(other, fetch)

    @pl.when(t < nu)
    def _():
        x = xb_ref[...]
        g = jnp.dot(x, wg_ref[0], preferred_element_type=F32) + bg_ref[0]
        lin = jnp.dot(x, wl_ref[0], preferred_element_type=F32) + bl_ref[0]
        g = jnp.minimum(g, SWIGLU_LIMIT)
        lin = jnp.clip(lin, -SWIGLU_LIMIT, SWIGLU_LIMIT)
        hid = (lin + 1.0) * (g * _sigmoid(g * SWIGLU_ALPHA))
        y = jnp.dot(hid.astype(BF16), wd_ref[0].astype(BF16), preferred_element_type=F32)
        acc_ref[slot] += y.reshape(tm // sub, sub, d)

    @pl.when(jnp.logical_and(j == nf - 1, t < nu))
    def _():
        on_slot(slot, scatter_start)

        @pl.when(t == n_tiles - 1)
        def _():
            on_slot(slot, scatter_wait)


def _moe_experts(h2, tile_idx, tile_expert, n_used, n_out_rows, wg, wl, bg, bl, wd, bd):
    d = h2.shape[1]
    f = wg.shape[2]
    n_tiles, _, tm = tile_idx.shape
    tf = min(MOE_TF, f)
    nf = f // tf
    assert nf >= 2, "the DMA schedule uses hidden-tile steps 0 and 1 of every row tile"

    def tile(t, nu):
        return jnp.minimum(t, nu[0] - 1)

    def ftile(t, j, nu):
        return jnp.where(t < nu[0], j, nf - 1)

    vmem = (2 * (2 * d * tf * 2 + tf * d * 4 + 2 * tf * 4 + d * 4) + 2 * tm * d * 4 + tm * d * 2 + 2 * tm * d * 4
            + 8 * tm * tf * 4 + tm * d * 4)
    any_spec = pl.BlockSpec(memory_space=pl.ANY)
    return pl.pallas_call(
        _moe_kernel,
        out_shape=jax.ShapeDtypeStruct((n_out_rows, d), F32),
        grid_spec=pltpu.PrefetchScalarGridSpec(
            num_scalar_prefetch=2,
            grid=(n_tiles, nf),
            in_specs=[any_spec, any_spec,
                      pl.BlockSpec((1, d, tf), lambda t, j, te, nu: (te[tile(t, nu)], 0, ftile(t, j, nu))),
                      pl.BlockSpec((1, d, tf), lambda t, j, te, nu: (te[tile(t, nu)], 0, ftile(t, j, nu))),
                      pl.BlockSpec((1, 1, tf), lambda t, j, te, nu: (te[tile(t, nu)], 0, ftile(t, j, nu))),
                      pl.BlockSpec((1, 1, tf), lambda t, j, te, nu: (te[tile(t, nu)], 0, ftile(t, j, nu))),
                      pl.BlockSpec((1, tf, d), lambda t, j, te, nu: (te[tile(t, nu)], ftile(t, j, nu), 0)),
                      pl.BlockSpec((1, 1, d), lambda t, j, te, nu: (te[tile(t, nu)], 0, 0))],
            out_specs=any_spec,
            scratch_shapes=[pltpu.SMEM((2 * tm,), I32), pltpu.SMEM((2 * tm,), I32),
                            pltpu.VMEM((2, tm // MOE_ROW_UNROLL, MOE_ROW_UNROLL, d), F32),
                            pltpu.VMEM((tm, d), BF16),
                            pltpu.VMEM((2, tm // MOE_ROW_UNROLL, MOE_ROW_UNROLL, d), F32),
                            pltpu.SemaphoreType.DMA((2,)), pltpu.SemaphoreType.DMA((2,)),
                            pltpu.SemaphoreType.DMA((2,))]),
        compiler_params=_cparams(("arbitrary", "arbitrary"), vmem),
        name="moe_experts",
    )(tile_expert, n_used, tile_idx.reshape(-1), h2, wg, wl, bg, bl, wd, bd)


def _route(top_idx, tm):
    n_tok = top_idx.shape[0]
    n_slots = n_tok * EXPERT_TOPK
    e_flat = top_idx.reshape(-1)
    order = jnp.argsort(e_flat).astype(I32)
    e_sorted = e_flat[order]
    counts = jnp.bincount(e_flat, length=N_EXPERTS)
    padded = (counts + tm - 1) // tm * tm
    pad_end = jnp.cumsum(padded)
    pad_start = pad_end - padded
    start = jnp.cumsum(counts) - counts
    dest = (pad_start[e_sorted] + jnp.arange(n_slots, dtype=I32) - start[e_sorted]).astype(I32)
    n_tiles = n_slots // tm + N_EXPERTS
    rows = jnp.arange(n_tiles * tm, dtype=I32)
    spill = n_slots + ((rows // tm) % 2) * tm + rows % tm
    row_tok = jnp.zeros((n_tiles * tm,), I32).at[dest].set(order // EXPERT_TOPK)
    row_dst = spill.at[dest].set((order % EXPERT_TOPK) * n_tok + order // EXPERT_TOPK)
    tile_idx = jnp.stack([row_tok.reshape(n_tiles, tm), row_dst.reshape(n_tiles, tm)], axis=1)
    tile_expert = jnp.minimum(
        jnp.searchsorted(pad_end, jnp.arange(n_tiles) * tm, side="right"), N_EXPERTS - 1).astype(I32)
    n_used = (pad_end[-1] // tm).astype(I32).reshape(1)
    return tile_idx, tile_expert, n_used, n_slots + 2 * tm


def _final_kernel(x1_ref, *rest):
    y_refs = rest[:EXPERT_TOPK]
    tg_ref, p_ref, wpp_ref, wpg_ref, gp_ref, gfin_ref, o_ref = rest[EXPERT_TOPK:]
    x2 = x1_ref[...]
    gates = tg_ref[...]
    for r in range(EXPERT_TOPK):
        x2 = x2 + y_refs[r][...] * gates[:, r:r + 1]
    ple = jnp.dot(p_ref[...].astype(BF16), wpp_ref[...], preferred_element_type=F32)
    gate = _sigmoid(jnp.dot(x2.astype(BF16), wpg_ref[...], preferred_element_type=F32))
    x3 = x2 + _rms(gate * ple, gp_ref[...])
    o_ref[...] = _rms(x3, gfin_ref[...])


def _final(x1, y, gates, p2, wpp, wpg, gp, gfin):
    seq, d = x1.shape
    pd = p2.shape[1]
    tm = min(FIN_TM, seq)
    row = lambda i: (i, 0)
    fix = lambda i: (0, 0)
    vmem = (2 * (2 * tm * d * 4 + EXPERT_TOPK * tm * d * 4 + tm * LANES * 4 + tm * pd * 4)
            + pd * d * 2 + d * d * 2 + 8 * tm * d * 4)
    return pl.pallas_call(
        _final_kernel,
        out_shape=jax.ShapeDtypeStruct((seq, d), F32),
        grid=(seq // tm,),
        in_specs=[pl.BlockSpec((tm, d), row),
                  *[pl.BlockSpec((tm, d), functools.partial(lambda i, r: (r * (seq // tm) + i, 0), r=r))
                    for r in range(EXPERT_TOPK)],
                  pl.BlockSpec((tm, LANES), row), pl.BlockSpec((tm, pd), row),
                  _resident((pd, d), fix), _resident((d, d), fix),
                  _resident((1, d), fix), _resident((1, d), fix)],
        out_specs=pl.BlockSpec((tm, d), row),
        compiler_params=_cparams(("parallel",), vmem),
        name="combine_ple_norm",
    )(x1, *([y] * EXPERT_TOPK), gates, p2, wpp, wpg, gp, gfin)


def _layer(x2, p2, g_mix, w_in, g_idx_k, b_idx_k, w_br_a, w_br_b, w_out, g_ffn, w_router, b_router,
           w_gate_up, b_gate_up, w_down, b_down, w_ple_proj, w_ple_gate, g_ple, g_final):
    seq, d = x2.shape
    dsa_w, moba_w = DSA_HEADS * HEAD_DIM, MOBA_HEADS * HEAD_DIM
    sizes = (dsa_w, dsa_w, dsa_w, IDX_HEADS * IDX_DIM, IDX_DIM, IDX_HEADS, moba_w, moba_w, moba_w, d, d)
    off = np.concatenate([[0], np.cumsum(sizes)])
    col = lambda k: w_in[:, off[k]:off[k + 1]]
    row1 = lambda v: v.reshape(1, -1)
    g_mix2 = row1(g_mix)

    w_rope = jnp.concatenate([col(0), col(1), col(6), col(7)], axis=1).astype(BF16)
    w_qi = col(3).astype(BF16)
    w_kiwi = jnp.zeros((d, 2 * LANES), F32).at[:, :IDX_DIM].set(col(4)).at[:, LANES:LANES + IDX_HEADS].set(col(5))
    w_val = jnp.concatenate([col(2), col(8)], axis=1).astype(BF16)
    w_gate = jnp.concatenate([col(9), col(10)], axis=1).astype(BF16)

    rope_tabs = _rope_tables(seq, ROPE_DIM, HEAD_DIM)
    idx_tabs = _rope_tables(seq, IDX_ROPE_DIM, IDX_DIM)
    qk = _project(x2, g_mix2, w_rope, BF16, rope_tabs, ROPE_DIM // 2)
    qi = _project(x2, g_mix2, w_qi, BF16, idx_tabs, IDX_ROPE_DIM // 2)
    pad = lambda v: jnp.zeros((1, LANES), F32).at[0, :IDX_DIM].set(v)
    ki, wi = _project_kiwi(x2, g_mix2, w_kiwi.astype(BF16), pad(g_idx_k), pad(b_idx_k), idx_tabs,
                           (IDX_HEADS ** -0.5) * (IDX_DIM ** -0.5))
    vals = _project(x2, g_mix2, w_val, BF16)
    gates_ab = _project(x2, g_mix2, w_gate, F32)

    bias4 = _dsa_mask(qi, ki, wi, min(DSA_TOPK, seq // 4))
    o_a = _dsa_attention(qk, vals, bias4, 0, 1, 0)
    o_b = _moba_attention(qk, vals, _block_means(qk, 3), 2, 3, 1)

    w_r = jnp.zeros((d, LANES), F32).at[:, :N_EXPERTS].set(w_router).astype(BF16)
    b_r = jnp.full((1, LANES), NEG_INF, F32).at[0, :N_EXPERTS].set(b_router)
    x1, h2, top_idx, top_gate = _post_attention(
        o_a, o_b, gates_ab, x2, w_br_a.astype(BF16), w_br_b.astype(BF16), w_out.astype(BF16),
        row1(g_ffn), w_r, b_r)

    tile_idx, tile_expert, n_used, n_out_rows = _route(top_idx[:, :EXPERT_TOPK], MOE_TM)
    bias_row = lambda b: b.reshape(N_EXPERTS, 1, -1)
    y = _moe_experts(h2, tile_idx, tile_expert, n_used, n_out_rows,
                     *_split_gate_up(w_gate_up),
                     bias_row(b_gate_up[:, 0::2]), bias_row(b_gate_up[:, 1::2]),
                     w_down, bias_row(b_down))

    return _final(x1, y, top_gate, p2, w_ple_proj.astype(BF16), w_ple_gate.astype(BF16), row1(g_ple),
                  row1(g_final))


def kernel(x, p, g_mix, w_in, g_idx_k, b_idx_k, w_br_a, w_br_b, w_out, g_ffn, w_router, b_router,
           w_gate_up, b_gate_up, w_down, b_down, w_ple_proj, w_ple_gate, g_ple, g_final):
    batch, seq, d = x.shape
    depth = w_in.shape[0]
    assert batch == 1 and depth == 1, "kernel handles the single-sequence, single-layer block"
    assert seq % ATT_T == 0 and seq % PROJ_TM == 0
    out = _layer(x[0], p[0, 0], g_mix[0], w_in[0], g_idx_k[0], b_idx_k[0], w_br_a[0], w_br_b[0], w_out[0],
                 g_ffn[0], w_router[0], b_router[0], w_gate_up[0], b_gate_up[0], w_down[0], b_down[0],
                 w_ple_proj[0], w_ple_gate[0], g_ple[0], g_final)
    return out[None]
```

```python
import functools

import numpy as np
import jax
import jax.numpy as jnp
from jax import lax
from jax.experimental import pallas as pl
from jax.experimental.pallas import tpu as pltpu

F32 = jnp.float32
BF16 = jnp.bfloat16
I32 = jnp.int32
NEG_INF = float("-inf")
LOG2_E = 1.4426950408889634

HEAD_DIM = 128
DSA_HEADS = 8
MOBA_HEADS = 8
ROPE_DIM = HEAD_DIM // 4
ROPE_THETA = 500000.0
IDX_HEADS = 16
IDX_DIM = 64
IDX_ROPE_DIM = IDX_DIM // 4
DSA_TOPK = 256
MOBA_BLOCK = 256
MOBA_TOPK = 3
N_EXPERTS = 32
EXPERT_TOPK = 4
SWIGLU_LIMIT = 7.0
SWIGLU_ALPHA = 1.702
EPS = 1e-6

LANES = 128
VMEM_CAP_BYTES = 60000 * 1024

PROJ_TM = 1024
PROJ_TN = 512
IDX_TQ = 128
IDX_CK = 512
ATT_T = 512
POST_TM = 256
MOE_TM = 512
MOE_TF = 512
MOE_ROW_UNROLL = 8
FIN_TM = 256

INT_MIN = -2 ** 31
KEY_NEG_INF = 0x807FFFFF - 2 ** 32


def _cparams(sem, vmem_bytes):
    return pltpu.CompilerParams(dimension_semantics=sem,
                                vmem_limit_bytes=int(min(vmem_bytes, VMEM_CAP_BYTES)))


def _resident(shape, index_map):
    return pl.BlockSpec(shape, index_map, pipeline_mode=pl.Buffered(1))


def _sigmoid(x):
    return 1.0 / (1.0 + jnp.exp(-x))


def _rms(xf, g):
    ms = jnp.mean(xf * xf, axis=-1, keepdims=True)
    return xf * lax.rsqrt(ms + EPS) * g


def _dot_nt(a, b):
    return lax.dot_general(a, b, (((1,), (1,)), ((), ())), preferred_element_type=F32)


def _rope_tables(seq, rot_dim, period):
    half = rot_dim // 2
    inv = 1.0 / (ROPE_THETA ** (jnp.arange(half, dtype=F32) / half))
    ang = jnp.arange(seq).astype(F32)[:, None] * inv[None, :]
    cos, sin = jnp.cos(ang), jnp.sin(ang)
    z = lambda n: jnp.zeros((seq, n), F32)
    c = jnp.concatenate([cos, cos, jnp.ones((seq, period - rot_dim), F32)], axis=-1)
    s1 = jnp.concatenate([-sin, z(period - half)], axis=-1)
    s2 = jnp.concatenate([z(half), sin, z(period - rot_dim)], axis=-1)
    rep = LANES // period
    return tuple(jnp.tile(t, (1, rep)) for t in (c, s1, s2))


def _rope(a, c, s1, s2, half):
    return a * c + pltpu.roll(a, LANES - half, 1) * s1 + pltpu.roll(a, half, 1) * s2


def _proj_kernel(x_ref, g_ref, w_ref, *rest, half):
    if half is None:
        o_ref, h_ref = rest
    else:
        c_ref, s1_ref, s2_ref, o_ref, h_ref = rest

    @pl.when(pl.program_id(1) == 0)
    def _():
        h_ref[...] = _rms(x_ref[...], g_ref[...]).astype(BF16)

    acc = jnp.dot(h_ref[...], w_ref[...], preferred_element_type=F32)
    if half is None:
        o_ref[...] = acc.astype(o_ref.dtype)
    else:
        c, s1, s2 = c_ref[...], s1_ref[...], s2_ref[...]
        for k in range(acc.shape[1] // LANES):
            sl = slice(k * LANES, (k + 1) * LANES)
            o_ref[:, sl] = _rope(acc[:, sl], c, s1, s2, half).astype(o_ref.dtype)


def _project(x2, g, w, out_dtype, tables=None, half=None):
    seq, d = x2.shape
    n = w.shape[1]
    tm, tn = min(PROJ_TM, seq), min(PROJ_TN, n)
    in_specs = [pl.BlockSpec((tm, d), lambda i, j: (i, 0)),
                pl.BlockSpec((1, d), lambda i, j: (0, 0)),
                pl.BlockSpec((d, tn), lambda i, j: (0, j))]
    args = [x2, g, w]
    if tables is not None:
        in_specs += [pl.BlockSpec((tm, LANES), lambda i, j: (i, 0))] * 3
        args += list(tables)
    vmem = 2 * (tm * d * 4 + d * tn * 2 + tm * tn * 4 + 3 * tm * LANES * 4) + tm * d * 2 + 4 * tm * tn * 4
    return pl.pallas_call(
        functools.partial(_proj_kernel, half=half),
        out_shape=jax.ShapeDtypeStruct((seq, n), out_dtype),
        grid=(seq // tm, n // tn),
        in_specs=in_specs,
        out_specs=pl.BlockSpec((tm, tn), lambda i, j: (i, j)),
        scratch_shapes=[pltpu.VMEM((tm, d), BF16)],
        compiler_params=_cparams(("parallel", "arbitrary"), vmem),
        name="proj_rope" if half is not None else "proj_plain",
    )(*args)


def _proj_kiwi_kernel(x_ref, g_ref, w_ref, lg_ref, lb_ref, c_ref, s1_ref, s2_ref, ki_ref, wi_ref, *, scale):
    h = _rms(x_ref[...], g_ref[...]).astype(BF16)
    acc = jnp.dot(h, w_ref[...], preferred_element_type=F32)
    a = acc[:, :LANES]
    valid = lax.broadcasted_iota(I32, a.shape, 1) < IDX_DIM
    mu = jnp.sum(jnp.where(valid, a, 0.0), axis=-1, keepdims=True) / IDX_DIM
    dlt = jnp.where(valid, a - mu, 0.0)
    var = jnp.sum(dlt * dlt, axis=-1, keepdims=True) / IDX_DIM
    y = dlt * lax.rsqrt(var + EPS) * lg_ref[...] + lb_ref[...]
    ki_ref[...] = _rope(y, c_ref[...], s1_ref[...], s2_ref[...], IDX_ROPE_DIM // 2).astype(BF16)
    wi_ref[...] = acc[:, LANES:LANES + IDX_HEADS] * scale


def _project_kiwi(x2, g, w_kiwi, lg, lb, tables, scale):
    seq, d = x2.shape
    tm = min(PROJ_TM, seq)
    row = lambda i: (i, 0)
    fix = lambda i: (0, 0)
    vmem = 2 * (tm * d * 4 + d * 2 * LANES * 2 + 5 * tm * LANES * 4) + 8 * tm * 2 * LANES * 4 + tm * d * 6
    return pl.pallas_call(
        functools.partial(_proj_kiwi_kernel, scale=scale),
        out_shape=(jax.ShapeDtypeStruct((seq, LANES), BF16), jax.ShapeDtypeStruct((seq, IDX_HEADS), F32)),
        grid=(seq // tm,),
        in_specs=[pl.BlockSpec((tm, d), row), pl.BlockSpec((1, d), fix), pl.BlockSpec((d, 2 * LANES), fix),
                  pl.BlockSpec((1, LANES), fix), pl.BlockSpec((1, LANES), fix),
                  pl.BlockSpec((tm, LANES), row), pl.BlockSpec((tm, LANES), row), pl.BlockSpec((tm, LANES), row)],
        out_specs=(pl.BlockSpec((tm, LANES), row), pl.BlockSpec((tm, IDX_HEADS), row)),
        compiler_params=_cparams(("parallel",), vmem),
        name="proj_kiwi",
    )(x2, g, w_kiwi, lg, lb, *tables)


def _key_to_float(key):
    bits = key ^ ((key >> 31) & 0x7FFFFFFF)
    return lax.bitcast_convert_type(bits, F32)


def _indexer_kernel(qi_ref, ki_ref, wi_ref, out_ref, s_ref, qh_ref, wb_ref, cst_ref, *, n_sel, idx_bits):
    tq, ck = IDX_TQ, IDX_CK
    nslab = ck // LANES
    n_chunks_total = out_ref.shape[1]
    i = pl.program_id(0)
    nc = ((i + 1) * tq + ck - 1) // ck
    lane = lax.broadcasted_iota(I32, (tq, LANES), 1)
    row = i * tq + lax.broadcasted_iota(I32, (tq, LANES), 0)

    low = lane < IDX_DIM
    w = wi_ref[...]
    for p in range(IDX_HEADS // 2):
        pair = qi_ref[:, p * LANES:(p + 1) * LANES].astype(F32)
        qh_ref[2 * p] = jnp.where(low, pair, 0.0).astype(BF16)
        qh_ref[2 * p + 1] = jnp.where(low, pltpu.roll(pair, IDX_DIM, 1), 0.0).astype(BF16)
    for h in range(IDX_HEADS):
        wb_ref[h] = jnp.broadcast_to(w[:, h:h + 1], (tq, LANES))

    def score_chunk(c, carry):
        kc = ki_ref[pl.ds(pl.multiple_of(c * ck, ck), ck), :]
        slabs = [jnp.zeros((tq, LANES), F32) for _ in range(nslab)]
        for h in range(IDX_HEADS):
            logit = _dot_nt(qh_ref[h], kc)
            wbh = wb_ref[h]
            for k in range(nslab):
                slabs[k] = slabs[k] + jnp.maximum(logit[:, k * LANES:(k + 1) * LANES], 0.0) * wbh
        for k in range(nslab):
            col = c * ck + k * LANES + lane
            slabs[k] = jnp.where(col <= row, slabs[k], NEG_INF)
        s_ref[c] = jnp.concatenate(slabs, axis=1)
        return carry

    lax.fori_loop(0, nc, score_chunk, 0)

    def count(pred):
        def body(c, acc):
            sc = s_ref[c]
            for k in range(nslab):
                col = c * ck + k * LANES + lane
                acc = acc + jnp.where(pred(sc[:, k * LANES:(k + 1) * LANES], col), 1.0, 0.0)
            return acc
        acc = lax.fori_loop(0, nc, body, jnp.zeros((tq, LANES), F32))
        return jnp.broadcast_to(jnp.sum(acc, axis=-1, keepdims=True), (tq, LANES))

    def bit_step(b, carry):
        prefix, cnt_at = carry
        cand = prefix + lax.shift_left(jnp.int32(1), 31 - b)
        cand_f = _key_to_float(cand)
        cnt = count(lambda s, col: s >= cand_f)
        take = cnt >= n_sel
        return jnp.where(take, cand, prefix), jnp.where(take, cnt, cnt_at)

    prefix, cnt_at = lax.fori_loop(
        0, 32, bit_step, (jnp.full((tq, LANES), INT_MIN, I32), jnp.full((tq, LANES), float(n_sel), F32)))
    tau = jnp.where(prefix < KEY_NEG_INF, NEG_INF, _key_to_float(prefix))

    need = jnp.logical_and(cnt_at > n_sel, tau > NEG_INF)
    any_tie = jnp.max(jnp.where(need, 1.0, 0.0)) > 0.0
    idx_all = jnp.full((tq, LANES), 2 ** 30, I32)

    @pl.when(any_tie)
    def _():
        rem = n_sel - count(lambda s, col: s > tau)
        cut = jnp.zeros((tq, LANES), I32)
        for b in range(idx_bits - 1, -1, -1):
            cand = cut + (1 << b)
            below = count(lambda s, col: jnp.logical_and(s == tau, col < cand))
            cut = jnp.where(below < rem, cand, cut)
        cst_ref[...] = jnp.where(need, cut, idx_all)

    @pl.when(jnp.logical_not(any_tie))
    def _():
        cst_ref[...] = idx_all

    cut = cst_ref[...]

    def emit(c, carry):
        sc = s_ref[c]
        outs = []
        for k in range(nslab):
            s = sc[:, k * LANES:(k + 1) * LANES]
            col = c * ck + k * LANES + lane
            tie = jnp.where(col <= cut, 0.0, NEG_INF)
            b = jnp.where(s > tau, 0.0, jnp.where(s == tau, tie, NEG_INF))
            outs.append(jnp.where(col <= row, b, NEG_INF))
        out_ref[0, c] = jnp.concatenate(outs, axis=1).astype(BF16)
        return carry

    lax.fori_loop(0, nc, emit, 0)

    def fill(c, carry):
        out_ref[0, c] = jnp.full((tq, ck), NEG_INF, BF16)
        return carry

    lax.fori_loop(nc, n_chunks_total, fill, 0)


def _dsa_mask(qi, ki, wi, n_sel):
    seq = qi.shape[0]
    tq, ck = IDX_TQ, IDX_CK
    nq, nchunk = seq // tq, seq // ck
    idx_bits = max(1, int(seq - 1).bit_length())
    vmem = (2 * (tq * qi.shape[1] * 2 + tq * LANES * 4 + nchunk * tq * ck * 2) + seq * LANES * 2
            + nchunk * tq * ck * 4 + IDX_HEADS * tq * LANES * 6 + tq * LANES * 4 + 24 * tq * ck * 4)
    return pl.pallas_call(
        functools.partial(_indexer_kernel, n_sel=n_sel, idx_bits=idx_bits),
        out_shape=jax.ShapeDtypeStruct((nq, nchunk, tq, ck), BF16),
        grid=(nq,),
        in_specs=[pl.BlockSpec((tq, qi.shape[1]), lambda i: (i, 0)),
                  _resident((seq, LANES), lambda i: (0, 0)),
                  pl.BlockSpec((tq, IDX_HEADS), lambda i: (i, 0))],
        out_specs=pl.BlockSpec((1, nchunk, tq, ck), lambda i: (i, 0, 0, 0)),
        scratch_shapes=[pltpu.VMEM((nchunk, tq, ck), F32),
                        pltpu.VMEM((IDX_HEADS, tq, LANES), BF16),
                        pltpu.VMEM((IDX_HEADS, tq, LANES), F32),
                        pltpu.VMEM((tq, LANES), I32)],
        compiler_params=_cparams(("parallel",), vmem),
        name="dsa_indexer",
    )(qi, ki, wi)


def _tri_steps(n):
    qs = [i for i in range(n) for _ in range(i + 1)]
    ks = [j for i in range(n) for j in range(i + 1)]
    return jnp.asarray(qs, I32), jnp.asarray(ks, I32)


def _softmax_step(h, s, v_h, m_ref, l_ref, acc_ref):
    sl = slice(h * HEAD_DIM, (h + 1) * HEAD_DIM)
    m_prev = m_ref[h]
    m_new = jnp.maximum(m_prev, jnp.max(s, axis=-1, keepdims=True))
    m_safe = jnp.where(m_new == NEG_INF, 0.0, m_new)
    alpha = jnp.exp2(m_prev - m_safe)
    p = jnp.exp2(s - m_safe[:, :1])
    l_ref[h] = alpha * l_ref[h] + jnp.sum(p, axis=-1, keepdims=True)
    acc_ref[:, sl] = alpha * acc_ref[:, sl] + jnp.dot(p.astype(BF16), v_h, preferred_element_type=F32)
    m_ref[h] = m_new


def _attn_init(m_ref, l_ref, acc_ref):
    m_ref[...] = jnp.full(m_ref.shape, NEG_INF, F32)
    l_ref[...] = jnp.zeros(l_ref.shape, F32)
    acc_ref[...] = jnp.zeros(acc_ref.shape, F32)


def _attn_finish(o_ref, l_ref, acc_ref, nh):
    for h in range(nh):
        sl = slice(h * HEAD_DIM, (h + 1) * HEAD_DIM)
        o_ref[:, sl] = (acc_ref[:, sl] / l_ref[h]).astype(o_ref.dtype)


def _dsa_attn_kernel(qs_ref, ks_ref, q_ref, k_ref, v_ref, b_ref, o_ref, m_ref, l_ref, acc_ref, *, scale):
    s_id = pl.program_id(0)
    qi, kj = qs_ref[s_id], ks_ref[s_id]
    t = q_ref.shape[0]

    @pl.when(kj == 0)
    def _():
        _attn_init(m_ref, l_ref, acc_ref)

    bias = b_ref[...].reshape(t, t).astype(F32)
    for h in range(DSA_HEADS):
        sl = slice(h * HEAD_DIM, (h + 1) * HEAD_DIM)
        s = _dot_nt(q_ref[:, sl], k_ref[:, sl]) * scale + bias
        _softmax_step(h, s, v_ref[:, sl], m_ref, l_ref, acc_ref)

    @pl.when(kj == qi)
    def _():
        _attn_finish(o_ref, l_ref, acc_ref, DSA_HEADS)


def _dsa_attention(qk, vals, bias4, qcol, kcol, vcol):
    seq, width = qk.shape[0], DSA_HEADS * HEAD_DIM
    t = min(ATT_T, seq)
    nq = seq // t
    qs, ks = _tri_steps(nq)
    sub = t // IDX_TQ
    omap = lambda s, qs, ks: (qs[s], 0)
    qmap = lambda s, qs, ks: (qs[s], qcol)
    kmap = lambda s, qs, ks: (ks[s], kcol)
    vmap = lambda s, qs, ks: (ks[s], vcol)
    vmem = (2 * (3 * t * width * 2 + t * t * 2 + t * width * 2) + t * width * 4
            + 2 * DSA_HEADS * t * LANES * 4 + 24 * t * t * 4)
    return pl.pallas_call(
        functools.partial(_dsa_attn_kernel, scale=HEAD_DIM ** -0.5 * LOG2_E),
        out_shape=jax.ShapeDtypeStruct((seq, width), BF16),
        grid_spec=pltpu.PrefetchScalarGridSpec(
            num_scalar_prefetch=2,
            grid=(int(qs.shape[0]),),
            in_specs=[pl.BlockSpec((t, width), qmap), pl.BlockSpec((t, width), kmap),
                      pl.BlockSpec((t, width), vmap),
                      pl.BlockSpec((sub, 1, IDX_TQ, IDX_CK), lambda s, qs, ks: (qs[s], ks[s], 0, 0))],
            out_specs=pl.BlockSpec((t, width), omap),
            scratch_shapes=[pltpu.VMEM((DSA_HEADS, t, LANES), F32), pltpu.VMEM((DSA_HEADS, t, LANES), F32),
                            pltpu.VMEM((t, width), F32)]),
        compiler_params=_cparams(("arbitrary",), vmem),
        name="dsa_attention",
    )(qs, ks, qk, qk, vals, bias4)


def _kmean_kernel(k_ref, o_ref):
    o_ref[0] = jnp.mean(k_ref[...].astype(F32), axis=0, keepdims=True)


def _block_means(qk, kcol):
    seq, width = qk.shape[0], MOBA_HEADS * HEAD_DIM
    nblk = seq // MOBA_BLOCK
    out = pl.pallas_call(
        _kmean_kernel,
        out_shape=jax.ShapeDtypeStruct((nblk, 1, width), F32),
        grid=(nblk,),
        in_specs=[pl.BlockSpec((MOBA_BLOCK, width), lambda i: (i, kcol))],
        out_specs=pl.BlockSpec((1, 1, width), lambda i: (i, 0, 0)),
        compiler_params=_cparams(("parallel",), 8 * MOBA_BLOCK * width * 4),
        name="moba_block_means",
    )(qk)
    return out.reshape(nblk, width)


def _moba_attn_kernel(qs_ref, ks_ref, q_ref, k_ref, v_ref, km_ref, o_ref, m_ref, l_ref, acc_ref, sel_ref,
                      *, scale, n_top):
    s_id = pl.program_id(0)
    qi, kj = qs_ref[s_id], ks_ref[s_id]
    t = q_ref.shape[0]
    nblk = km_ref.shape[0]
    per_tile = t // MOBA_BLOCK
    blk = lax.broadcasted_iota(I32, (t, nblk), 1)

    @pl.when(kj == 0)
    def _():
        _attn_init(m_ref, l_ref, acc_ref)
        own = (qi * t + lax.broadcasted_iota(I32, (t, nblk), 0)) // MOBA_BLOCK
        blk_f = blk.astype(F32)
        for h in range(MOBA_HEADS):
            sl = slice(h * HEAD_DIM, (h + 1) * HEAD_DIM)
            g = _dot_nt(q_ref[:, sl], km_ref[:, sl].astype(BF16))
            g = jnp.where(blk < own, g, NEG_INF)
            sel = jnp.full((t, nblk), NEG_INF, F32)
            for _ in range(n_top):
                mx = jnp.max(g, axis=-1, keepdims=True)
                is_max = jnp.logical_and(g == mx, mx > NEG_INF)
                first = jnp.min(jnp.where(is_max, blk_f, float(nblk)), axis=-1, keepdims=True)
                pick = blk_f == first
                sel = jnp.where(pick, 0.0, sel)
                g = jnp.where(pick, NEG_INF, g)
            sel_ref[h] = sel

    def block_bias(h):
        sel = sel_ref[h]
        cols = []
        for b in range(per_tile):
            cb = jnp.max(jnp.where(blk == kj * per_tile + b, sel, NEG_INF), axis=-1, keepdims=True)
            cols.append(jnp.broadcast_to(cb, (t, MOBA_BLOCK)))
        return jnp.concatenate(cols, axis=1)

    def run(diag):
        if diag:
            r = lax.broadcasted_iota(I32, (t, t), 0)
            c = lax.broadcasted_iota(I32, (t, t), 1)
            own_blk = (r // MOBA_BLOCK) == (c // MOBA_BLOCK)
            causal = jnp.where(c <= r, 0.0, NEG_INF)
        for h in range(MOBA_HEADS):
            sl = slice(h * HEAD_DIM, (h + 1) * HEAD_DIM)
            bias = block_bias(h)
            if diag:
                bias = jnp.where(own_blk, causal, bias)
            s = _dot_nt(q_ref[:, sl], k_ref[:, sl]) * scale + bias
            _softmax_step(h, s, v_ref[:, sl], m_ref, l_ref, acc_ref)

    @pl.when(kj < qi)
    def _():
        run(False)

    @pl.when(kj == qi)
    def _():
        run(True)
        _attn_finish(o_ref, l_ref, acc_ref, MOBA_HEADS)


def _moba_attention(qk, vals, kmean, qcol, kcol, vcol):
    seq, width = qk.shape[0], MOBA_HEADS * HEAD_DIM
    t = min(ATT_T, seq)
    nq = seq // t
    nblk = kmean.shape[0]
    qs, ks = _tri_steps(nq)
    omap = lambda s, qs, ks: (qs[s], 0)
    qmap = lambda s, qs, ks: (qs[s], qcol)
    kmap = lambda s, qs, ks: (ks[s], kcol)
    vmap = lambda s, qs, ks: (ks[s], vcol)
    vmem = (2 * (4 * t * width * 2) + nblk * width * 4 + t * width * 4
            + 3 * MOBA_HEADS * t * LANES * 4 + 24 * t * t * 4)
    return pl.pallas_call(
        functools.partial(_moba_attn_kernel, scale=HEAD_DIM ** -0.5 * LOG2_E, n_top=min(MOBA_TOPK, nblk)),
        out_shape=jax.ShapeDtypeStruct((seq, width), BF16),
        grid_spec=pltpu.PrefetchScalarGridSpec(
            num_scalar_prefetch=2,
            grid=(int(qs.shape[0]),),
            in_specs=[pl.BlockSpec((t, width), qmap), pl.BlockSpec((t, width), kmap),
                      pl.BlockSpec((t, width), vmap),
                      _resident((nblk, width), lambda s, qs, ks: (0, 0))],
            out_specs=pl.BlockSpec((t, width), omap),
            scratch_shapes=[pltpu.VMEM((MOBA_HEADS, t, LANES), F32), pltpu.VMEM((MOBA_HEADS, t, LANES), F32),
                            pltpu.VMEM((t, width), F32), pltpu.VMEM((MOBA_HEADS, t, nblk), F32)]),
        compiler_params=_cparams(("arbitrary",), vmem),
        name="moba_attention",
    )(qs, ks, qk, qk, vals, kmean)


def _post_kernel(oa_ref, ob_ref, ga_ref, gb_ref, x_ref, wa_ref, wb_ref, wo_ref, gf_ref, wr_ref, br_ref,
                 x1_ref, h2_ref, ti_ref, tg_ref):
    ta = jnp.dot(oa_ref[...], wa_ref[...], preferred_element_type=F32)
    tb = jnp.dot(ob_ref[...], wb_ref[...], preferred_element_type=F32)
    merged = _sigmoid(ga_ref[...]) * ta + _sigmoid(gb_ref[...]) * tb
    x1 = x_ref[...] + jnp.dot(merged.astype(BF16), wo_ref[...], preferred_element_type=F32)
    x1_ref[...] = x1
    h2 = _rms(x1, gf_ref[...])
    h2_ref[...] = h2
    logits = jnp.dot(h2.astype(BF16), wr_ref[...], preferred_element_type=F32) + br_ref[...]
    lane = lax.broadcasted_iota(I32, logits.shape, 1)
    lane_f = lane.astype(F32)
    idx_out = jnp.zeros(logits.shape, I32)
    val_out = jnp.zeros(logits.shape, F32)
    top = None
    for r in range(EXPERT_TOPK):
        mx = jnp.max(logits, axis=-1, keepdims=True)
        ix = jnp.min(jnp.where(logits == mx, lane_f, float(LANES)), axis=-1, keepdims=True).astype(I32)
        if top is None:
            top = mx
        idx_out = jnp.where(lane == r, ix, idx_out)
        val_out = jnp.where(lane == r, jnp.exp(mx - top), val_out)
        logits = jnp.where(lane == ix, NEG_INF, logits)
    ti_ref[...] = idx_out
    tg_ref[...] = val_out / jnp.sum(val_out, axis=-1, keepdims=True)


def _post_attention(oa, ob, gates_ab, x2, wa, wb, wo, gf, wr, br):
    seq, d = x2.shape
    w = oa.shape[1]
    tm = min(POST_TM, seq)
    row = lambda i: (i, 0)
    fix = lambda i: (0, 0)
    vmem = (2 * (2 * tm * w * 2 + 3 * tm * d * 4 + 2 * tm * d * 4 + 2 * tm * LANES * 4)
            + 2 * w * d * 2 + d * d * 2 + d * LANES * 2 + 8 * tm * d * 4)
    return pl.pallas_call(
        _post_kernel,
        out_shape=(jax.ShapeDtypeStruct((seq, d), F32), jax.ShapeDtypeStruct((seq, d), F32),
                   jax.ShapeDtypeStruct((seq, LANES), I32), jax.ShapeDtypeStruct((seq, LANES), F32)),
        grid=(seq // tm,),
        in_specs=[pl.BlockSpec((tm, w), row), pl.BlockSpec((tm, w), row),
                  pl.BlockSpec((tm, d), row), pl.BlockSpec((tm, d), lambda i: (i, 1)), pl.BlockSpec((tm, d), row),
                  _resident((w, d), fix), _resident((w, d), fix), _resident((d, d), fix),
                  _resident((1, d), fix), _resident((d, LANES), fix), _resident((1, LANES), fix)],
        out_specs=(pl.BlockSpec((tm, d), row), pl.BlockSpec((tm, d), row),
                   pl.BlockSpec((tm, LANES), row), pl.BlockSpec((tm, LANES), row)),
        compiler_params=_cparams(("parallel",), vmem),
        name="merge_outproj_router",
    )(oa, ob, gates_ab, gates_ab, x2, wa, wb, wo, gf, wr, br)


def _split_kernel(w_ref, g_ref, l_ref):
    grp = 2 * LANES
    r = lax.broadcasted_iota(I32, (grp, grp), 0)
    c = lax.broadcasted_iota(I32, (grp, grp), 1)
    src = jnp.where(c < LANES, 2 * c, 2 * (c - LANES) + 1)
    sel = jnp.where(r == src, 1.0, 0.0).astype(BF16)
    for k in range(w_ref.shape[2] // grp):
        res = jnp.dot(w_ref[0, :, k * grp:(k + 1) * grp].astype(BF16), sel, preferred_element_type=F32)
        g_ref[0, :, k * LANES:(k + 1) * LANES] = res[:, :LANES].astype(BF16)
        l_ref[0, :, k * LANES:(k + 1) * LANES] = res[:, LANES:].astype(BF16)


def _split_gate_up(w_gate_up):
    n_e, d, f2 = w_gate_up.shape
    tc = min(2 * MOE_TF, f2)
    half = jax.ShapeDtypeStruct((n_e, d, f2 // 2), BF16)
    return pl.pallas_call(
        _split_kernel,
        out_shape=(half, half),
        grid=(n_e, f2 // tc),
        in_specs=[pl.BlockSpec((1, d, tc), lambda e, j: (e, 0, j))],
        out_specs=(pl.BlockSpec((1, d, tc // 2), lambda e, j: (e, 0, j)),
                   pl.BlockSpec((1, d, tc // 2), lambda e, j: (e, 0, j))),
        compiler_params=_cparams(("parallel", "parallel"), 2 * (d * tc * 4 + d * tc * 2) + 6 * d * tc * 2),
        name="moe_split_gate_up",
    )(w_gate_up)


def _moe_kernel(te_ref, nu_ref, idx_hbm, h_hbm, wg_ref, wl_ref, bg_ref, bl_ref, wd_ref, bd_ref, y_hbm,
                idx0_ref, idx1_ref, xf_ref, xb_ref, acc_ref, isem, gsem, ssem):
    t, j = pl.program_id(0), pl.program_id(1)
    n_tiles, nf = pl.num_programs(0), pl.num_programs(1)
    nu = nu_ref[0]
    tm, d = xb_ref.shape
    sub = xf_ref.shape[2]
    slot = lax.rem(t, 2)
    other = 1 - slot
    idx_refs = (idx0_ref, idx1_ref)

    def on_slot(dyn_slot, fn):
        for s in range(2):
            pl.when(dyn_slot == s)(functools.partial(fn, s))

    def idx_copy(tile, s):
        return pltpu.make_async_copy(idx_hbm.at[pl.ds(pl.multiple_of(tile * 2 * tm, 2 * tm), 2 * tm)],
                                     idx_refs[s], isem.at[s])

    def row_loop(issue):
        def body(i, carry):
            for k in range(sub):
                issue(i, k)
            return carry
        lax.fori_loop(0, tm // sub, body, 0)

    def gather_start(s):
        def issue(i, k):
            tok = idx_refs[s][i * sub + k]
            pltpu.make_async_copy(h_hbm.at[pl.ds(tok, 1), :], xf_ref.at[s, i, pl.ds(k, 1), :], gsem.at[s]).start()
        row_loop(issue)

    def gather_wait(s):
        pltpu.make_async_copy(xf_ref.at[s], xf_ref.at[s], gsem.at[s]).wait()

    def scatter_start(s):
        def issue(i, k):
            dst = idx_refs[s][tm + i * sub + k]
            pltpu.make_async_copy(acc_ref.at[s, i, pl.ds(k, 1), :], y_hbm.at[pl.ds(dst, 1), :],
                                  ssem.at[s]).start(priority=1)
        row_loop(issue)

    def scatter_wait(s):
        pltpu.make_async_copy(acc_ref.at[s], acc_ref.at[s], ssem.at[s]).wait()

    @pl.when(jnp.logical_and(t == 0, j == 0))
    def _():
        first = idx_copy(0, 0)
        first.start()
        acc_ref[...] = jnp.zeros(acc_ref.shape, F32)
        spill0 = y_hbm.shape[0] - 2 * tm
        for s in range(2):
            def fill(i, carry):
                row = pl.multiple_of(spill0 + s * tm + i * sub, sub)
                pltpu.make_async_copy(acc_ref.at[s, i], y_hbm.at[pl.ds(row, sub), :], ssem.at[s]).start()
                return carry
            lax.fori_loop(0, tm // sub, fill, 0)
            scatter_wait(s)
        first.wait()
        gather_start(0)

    @pl.when(jnp.logical_and(j == 0, t + 1 < nu))
    def _():
        on_slot(other, lambda s: idx_copy(t + 1, s).start())

    @pl.when(jnp.logical_and(j == 0, t < nu))
    def _():
        on_slot(slot, gather_wait)
        xb_ref[...] = xf_ref[slot].reshape(tm, d).astype(BF16)
        acc_ref[slot] = jnp.broadcast_to(bd_ref[0], (tm // sub, sub, d))

    @pl.when(jnp.logical_and(j == jnp.minimum(2, nf - 1), jnp.logical_and(t >= 1, t <= nu)))
    def _():
        on_slot(other, scatter_wait)

    @pl.when(jnp.logical_and(j == 1, t + 1 < nu))
    def _():
        def fetch(s):
            idx_copy(t + 1, s).wait()
            gather_start(s)
        on_slot(other, fetch)

    @pl.when(t < nu)
    def _():
        x = xb_ref[...]
        g = jnp.dot(x, wg_ref[0], preferred_element_type=F32) + bg_ref[0]
        lin = jnp.dot(x, wl_ref[0], preferred_element_type=F32) + bl_ref[0]
        g = jnp.minimum(g, SWIGLU_LIMIT)
        lin = jnp.clip(lin, -SWIGLU_LIMIT, SWIGLU_LIMIT)
        hid = (lin + 1.0) * (g * _sigmoid(g * SWIGLU_ALPHA))
        y = jnp.dot(hid.astype(BF16), wd_ref[0].astype(BF16), preferred_element_type=F32)
        acc_ref[slot] += y.reshape(tm // sub, sub, d)

    @pl.when(jnp.logical_and(j == nf - 1, t < nu))
    def _():
        on_slot(slot, scatter_start)

        @pl.when(t == n_tiles - 1)
        def _():
            on_slot(slot, scatter_wait)


def _moe_experts(h2, tile_idx, tile_expert, n_used, n_out_rows, wg, wl, bg, bl, wd, bd):
    d = h2.shape[1]
    f = wg.shape[2]
    n_tiles, _, tm = tile_idx.shape
    tf = min(MOE_TF, f)
    nf = f // tf
    assert nf >= 2, "the DMA schedule uses hidden-tile steps 0 and 1 of every row tile"

    def tile(t, nu):
        return jnp.minimum(t, nu[0] - 1)

    def ftile(t, j, nu):
        return jnp.where(t < nu[0], j, nf - 1)

    vmem = (2 * (2 * d * tf * 2 + tf * d * 4 + 2 * tf * 4 + d * 4) + 2 * tm * d * 4 + tm * d * 2 + 2 * tm * d * 4
            + 8 * tm * tf * 4 + tm * d * 4)
    any_spec = pl.BlockSpec(memory_space=pl.ANY)
    return pl.pallas_call(
        _moe_kernel,
        out_shape=jax.ShapeDtypeStruct((n_out_rows, d), F32),
        grid_spec=pltpu.PrefetchScalarGridSpec(
            num_scalar_prefetch=2,
            grid=(n_tiles, nf),
            in_specs=[any_spec, any_spec,
                      pl.BlockSpec((1, d, tf), lambda t, j, te, nu: (te[tile(t, nu)], 0, ftile(t, j, nu))),
                      pl.BlockSpec((1, d, tf), lambda t, j, te, nu: (te[tile(t, nu)], 0, ftile(t, j, nu))),
                      pl.BlockSpec((1, 1, tf), lambda t, j, te, nu: (te[tile(t, nu)], 0, ftile(t, j, nu))),
                      pl.BlockSpec((1, 1, tf), lambda t, j, te, nu: (te[tile(t, nu)], 0, ftile(t, j, nu))),
                      pl.BlockSpec((1, tf, d), lambda t, j, te, nu: (te[tile(t, nu)], ftile(t, j, nu), 0)),
                      pl.BlockSpec((1, 1, d), lambda t, j, te, nu: (te[tile(t, nu)], 0, 0))],
            out_specs=any_spec,
            scratch_shapes=[pltpu.SMEM((2 * tm,), I32), pltpu.SMEM((2 * tm,), I32),
                            pltpu.VMEM((2, tm // MOE_ROW_UNROLL, MOE_ROW_UNROLL, d), F32),
                            pltpu.VMEM((tm, d), BF16),
                            pltpu.VMEM((2, tm // MOE_ROW_UNROLL, MOE_ROW_UNROLL, d), F32),
                            pltpu.SemaphoreType.DMA((2,)), pltpu.SemaphoreType.DMA((2,)),
                            pltpu.SemaphoreType.DMA((2,))]),
        compiler_params=_cparams(("arbitrary", "arbitrary"), vmem),
        name="moe_experts",
    )(tile_expert, n_used, tile_idx.reshape(-1), h2, wg, wl, bg, bl, wd, bd)


def _route(top_idx, tm):
    n_tok = top_idx.shape[0]
    n_slots = n_tok * EXPERT_TOPK
    e_flat = top_idx.reshape(-1)
    order = jnp.argsort(e_flat).astype(I32)
    counts = jnp.bincount(e_flat, length=N_EXPERTS).astype(I32)
    padded = (counts + tm - 1) // tm * tm
    pad_end = jnp.cumsum(padded)
    pad_start = pad_end - padded
    start = jnp.cumsum(counts) - counts
    n_tiles = n_slots // tm + N_EXPERTS
    tile_expert = jnp.minimum(
        jnp.searchsorted(pad_end, jnp.arange(n_tiles) * tm, side="right"), N_EXPERTS - 1).astype(I32)
    n_used = (pad_end[-1] // tm).astype(I32).reshape(1)
    rows = jnp.arange(n_tiles * tm, dtype=I32)
    e_row = jnp.repeat(tile_expert, tm)
    within = rows - pad_start[e_row]
    valid = within < counts[e_row]
    slot = order[jnp.clip(start[e_row] + within, 0, n_slots - 1)]
    spill = n_slots + ((rows // tm) % 2) * tm + rows % tm
    row_tok = jnp.where(valid, slot // EXPERT_TOPK, 0)
    row_dst = jnp.where(valid, (slot % EXPERT_TOPK) * n_tok + slot // EXPERT_TOPK, spill)
    tile_idx = jnp.stack([row_tok.reshape(n_tiles, tm), row_dst.reshape(n_tiles, tm)], axis=1).astype(I32)
    return tile_idx, tile_expert, n_used, n_slots + 2 * tm


def _final_kernel(x1_ref, *rest):
    y_refs = rest[:EXPERT_TOPK]
    tg_ref, p_ref, wpp_ref, wpg_ref, gp_ref, gfin_ref, o_ref = rest[EXPERT_TOPK:]
    x2 = x1_ref[...]
    gates = tg_ref[...]
    for r in range(EXPERT_TOPK):
        x2 = x2 + y_refs[r][...] * gates[:, r:r + 1]
    ple = jnp.dot(p_ref[...].astype(BF16), wpp_ref[...], preferred_element_type=F32)
    gate = _sigmoid(jnp.dot(x2.astype(BF16), wpg_ref[...], preferred_element_type=F32))
    x3 = x2 + _rms(gate * ple, gp_ref[...])
    o_ref[...] = _rms(x3, gfin_ref[...])


def _final(x1, y, gates, p2, wpp, wpg, gp, gfin):
    seq, d = x1.shape
    pd = p2.shape[1]
    tm = min(FIN_TM, seq)
    row = lambda i: (i, 0)
    fix = lambda i: (0, 0)
    vmem = (2 * (2 * tm * d * 4 + EXPERT_TOPK * tm * d * 4 + tm * LANES * 4 + tm * pd * 4)
            + pd * d * 2 + d * d * 2 + 8 * tm * d * 4)
    return pl.pallas_call(
        _final_kernel,
        out_shape=jax.ShapeDtypeStruct((seq, d), F32),
        grid=(seq // tm,),
        in_specs=[pl.BlockSpec((tm, d), row),
                  *[pl.BlockSpec((tm, d), functools.partial(lambda i, r: (r * (seq // tm) + i, 0), r=r))
                    for r in range(EXPERT_TOPK)],
                  pl.BlockSpec((tm, LANES), row), pl.BlockSpec((tm, pd), row),
                  _resident((pd, d), fix), _resident((d, d), fix),
                  _resident((1, d), fix), _resident((1, d), fix)],
        out_specs=pl.BlockSpec((tm, d), row),
        compiler_params=_cparams(("parallel",), vmem),
        name="combine_ple_norm",
    )(x1, *([y] * EXPERT_TOPK), gates, p2, wpp, wpg, gp, gfin)


def _layer(x2, p2, g_mix, w_in, g_idx_k, b_idx_k, w_br_a, w_br_b, w_out, g_ffn, w_router, b_router,
           w_gate_up, b_gate_up, w_down, b_down, w_ple_proj, w_ple_gate, g_ple, g_final):
    seq, d = x2.shape
    dsa_w, moba_w = DSA_HEADS * HEAD_DIM, MOBA_HEADS * HEAD_DIM
    sizes = (dsa_w, dsa_w, dsa_w, IDX_HEADS * IDX_DIM, IDX_DIM, IDX_HEADS, moba_w, moba_w, moba_w, d, d)
    off = np.concatenate([[0], np.cumsum(sizes)])
    col = lambda k: w_in[:, off[k]:off[k + 1]]
    row1 = lambda v: v.reshape(1, -1)
    g_mix2 = row1(g_mix)

    w_rope = jnp.concatenate([col(0), col(1), col(6), col(7)], axis=1).astype(BF16)
    w_qi = col(3).astype(BF16)
    w_kiwi = jnp.zeros((d, 2 * LANES), F32).at[:, :IDX_DIM].set(col(4)).at[:, LANES:LANES + IDX_HEADS].set(col(5))
    w_val = jnp.concatenate([col(2), col(8)], axis=1).astype(BF16)
    w_gate = jnp.concatenate([col(9), col(10)], axis=1).astype(BF16)

    rope_tabs = _rope_tables(seq, ROPE_DIM, HEAD_DIM)
    idx_tabs = _rope_tables(seq, IDX_ROPE_DIM, IDX_DIM)
    qk = _project(x2, g_mix2, w_rope, BF16, rope_tabs, ROPE_DIM // 2)
    qi = _project(x2, g_mix2, w_qi, BF16, idx_tabs, IDX_ROPE_DIM // 2)
    pad = lambda v: jnp.zeros((1, LANES), F32).at[0, :IDX_DIM].set(v)
    ki, wi = _project_kiwi(x2, g_mix2, w_kiwi.astype(BF16), pad(g_idx_k), pad(b_idx_k), idx_tabs,
                           (IDX_HEADS ** -0.5) * (IDX_DIM ** -0.5))
    vals = _project(x2, g_mix2, w_val, BF16)
    gates_ab = _project(x2, g_mix2, w_gate, F32)

    bias4 = _dsa_mask(qi, ki, wi, min(DSA_TOPK, seq // 4))
    o_a = _dsa_attention(qk, vals, bias4, 0, 1, 0)
    o_b = _moba_attention(qk, vals, _block_means(qk, 3), 2, 3, 1)

    w_r = jnp.zeros((d, LANES), F32).at[:, :N_EXPERTS].set(w_router).astype(BF16)
    b_r = jnp.full((1, LANES), NEG_INF, F32).at[0, :N_EXPERTS].set(b_router)
    x1, h2, top_idx, top_gate = _post_attention(
        o_a, o_b, gates_ab, x2, w_br_a.astype(BF16), w_br_b.astype(BF16), w_out.astype(BF16),
        row1(g_ffn), w_r, b_r)

    tile_idx, tile_expert, n_used, n_out_rows = _route(top_idx[:, :EXPERT_TOPK], MOE_TM)
    bias_row = lambda b: b.reshape(N_EXPERTS, 1, -1)
    y = _moe_experts(h2, tile_idx, tile_expert, n_used, n_out_rows,
                     *_split_gate_up(w_gate_up),
                     bias_row(b_gate_up[:, 0::2]), bias_row(b_gate_up[:, 1::2]),
                     w_down, bias_row(b_down))

    return _final(x1, y, top_gate, p2, w_ple_proj.astype(BF16), w_ple_gate.astype(BF16), row1(g_ple),
                  row1(g_final))


def kernel(x, p, g_mix, w_in, g_idx_k, b_idx_k, w_br_a, w_br_b, w_out, g_ffn, w_router, b_router,
           w_gate_up, b_gate_up, w_down, b_down, w_ple_proj, w_ple_gate, g_ple, g_final):
    batch, seq, d = x.shape
    depth = w_in.shape[0]
    assert batch == 1 and depth == 1, "kernel handles the single-sequence, single-layer block"
    assert seq % ATT_T == 0 and seq % PROJ_TM == 0
    out = _layer(x[0], p[0, 0], g_mix[0], w_in[0], g_idx_k[0], b_idx_k[0], w_br_a[0], w_br_b[0], w_out[0],
                 g_ffn[0], w_router[0], b_router[0], w_gate_up[0], b_gate_up[0], w_down[0], b_down[0],
                 w_ple_proj[0], w_ple_gate[0], g_ple[0], g_final)
    return out[None]
```

```python
import functools

import numpy as np
import jax
import jax.numpy as jnp
from jax import lax
from jax.experimental import pallas as pl
from jax.experimental.pallas import tpu as pltpu

F32 = jnp.float32
BF16 = jnp.bfloat16
I32 = jnp.int32
NEG_INF = float("-inf")
LOG2_E = 1.4426950408889634

HEAD_DIM = 128
DSA_HEADS = 8
MOBA_HEADS = 8
ROPE_DIM = HEAD_DIM // 4
ROPE_THETA = 500000.0
IDX_HEADS = 16
IDX_DIM = 64
IDX_ROPE_DIM = IDX_DIM // 4
DSA_TOPK = 256
MOBA_BLOCK = 256
MOBA_TOPK = 3
N_EXPERTS = 32
EXPERT_TOPK = 4
SWIGLU_LIMIT = 7.0
SWIGLU_ALPHA = 1.702
EPS = 1e-6

LANES = 128
VMEM_CAP_BYTES = 60000 * 1024

PROJ_TM = 1024
PROJ_TN = 512
IDX_TQ = 128
IDX_CK = 512
ATT_T = 512
POST_TM = 256
MOE_TM = 512
MOE_TF = 512
MOE_ROW_UNROLL = 8
FIN_TM = 256

INT_MIN = -2 ** 31
KEY_NEG_INF = 0x807FFFFF - 2 ** 32


def _cparams(sem, vmem_bytes):
    return pltpu.CompilerParams(dimension_semantics=sem,
                                vmem_limit_bytes=int(min(vmem_bytes, VMEM_CAP_BYTES)))


def _resident(shape, index_map):
    return pl.BlockSpec(shape, index_map, pipeline_mode=pl.Buffered(1))


def _sigmoid(x):
    return 1.0 / (1.0 + jnp.exp(-x))


def _rms(xf, g):
    ms = jnp.mean(xf * xf, axis=-1, keepdims=True)
    return xf * lax.rsqrt(ms + EPS) * g


def _dot_nt(a, b):
    return lax.dot_general(a, b, (((1,), (1,)), ((), ())), preferred_element_type=F32)


def _rope_tables(seq, rot_dim, period):
    half = rot_dim // 2
    inv = 1.0 / (ROPE_THETA ** (jnp.arange(half, dtype=F32) / half))
    ang = jnp.arange(seq).astype(F32)[:, None] * inv[None, :]
    cos, sin = jnp.cos(ang), jnp.sin(ang)
    z = lambda n: jnp.zeros((seq, n), F32)
    c = jnp.concatenate([cos, cos, jnp.ones((seq, period - rot_dim), F32)], axis=-1)
    s1 = jnp.concatenate([-sin, z(period - half)], axis=-1)
    s2 = jnp.concatenate([z(half), sin, z(period - rot_dim)], axis=-1)
    rep = LANES // period
    return tuple(jnp.tile(t, (1, rep)) for t in (c, s1, s2))


def _rope(a, c, s1, s2, half):
    return a * c + pltpu.roll(a, LANES - half, 1) * s1 + pltpu.roll(a, half, 1) * s2


def _proj_kernel(x_ref, g_ref, w_ref, *rest, half, transposed):
    if half is None:
        o_ref, h_ref = rest
    else:
        c_ref, s1_ref, s2_ref, o_ref, h_ref = rest

    @pl.when(pl.program_id(1) == 0)
    def _():
        h_ref[...] = _rms(x_ref[...], g_ref[...]).astype(BF16)

    if transposed:
        o_ref[...] = _dot_nt(w_ref[...], h_ref[...]).astype(o_ref.dtype)
        return
    acc = jnp.dot(h_ref[...], w_ref[...], preferred_element_type=F32)
    if half is None:
        o_ref[...] = acc.astype(o_ref.dtype)
    else:
        c, s1, s2 = c_ref[...], s1_ref[...], s2_ref[...]
        for k in range(acc.shape[1] // LANES):
            sl = slice(k * LANES, (k + 1) * LANES)
            o_ref[:, sl] = _rope(acc[:, sl], c, s1, s2, half).astype(o_ref.dtype)


def _project(x2, g, w, out_dtype, tables=None, half=None, transposed=False):
    seq, d = x2.shape
    n = w.shape[0] if transposed else w.shape[1]
    tm, tn = min(PROJ_TM, seq), min(PROJ_TN, n)
    in_specs = [pl.BlockSpec((tm, d), lambda i, j: (i, 0)),
                pl.BlockSpec((1, d), lambda i, j: (0, 0)),
                pl.BlockSpec((tn, d), lambda i, j: (j, 0)) if transposed
                else pl.BlockSpec((d, tn), lambda i, j: (0, j))]
    args = [x2, g, w]
    if tables is not None:
        in_specs += [pl.BlockSpec((tm, LANES), lambda i, j: (i, 0))] * 3
        args += list(tables)
    vmem = 2 * (tm * d * 4 + d * tn * 2 + tm * tn * 4 + 3 * tm * LANES * 4) + tm * d * 2 + 4 * tm * tn * 4
    return pl.pallas_call(
        functools.partial(_proj_kernel, half=half, transposed=transposed),
        out_shape=jax.ShapeDtypeStruct((n, seq) if transposed else (seq, n), out_dtype),
        grid=(seq // tm, n // tn),
        in_specs=in_specs,
        out_specs=pl.BlockSpec((tn, tm), lambda i, j: (j, i)) if transposed
        else pl.BlockSpec((tm, tn), lambda i, j: (i, j)),
        scratch_shapes=[pltpu.VMEM((tm, d), BF16)],
        compiler_params=_cparams(("parallel", "arbitrary"), vmem),
        name="proj_rope" if half is not None else "proj_plain",
    )(*args)


def _proj_kiwi_kernel(x_ref, g_ref, w_ref, lg_ref, lb_ref, c_ref, s1_ref, s2_ref, ki_ref, wi_ref, *, scale):
    h = _rms(x_ref[...], g_ref[...]).astype(BF16)
    acc = jnp.dot(h, w_ref[...], preferred_element_type=F32)
    a = acc[:, :LANES]
    valid = lax.broadcasted_iota(I32, a.shape, 1) < IDX_DIM
    mu = jnp.sum(jnp.where(valid, a, 0.0), axis=-1, keepdims=True) / IDX_DIM
    dlt = jnp.where(valid, a - mu, 0.0)
    var = jnp.sum(dlt * dlt, axis=-1, keepdims=True) / IDX_DIM
    y = dlt * lax.rsqrt(var + EPS) * lg_ref[...] + lb_ref[...]
    ki_ref[...] = _rope(y, c_ref[...], s1_ref[...], s2_ref[...], IDX_ROPE_DIM // 2).astype(BF16)
    wi_ref[...] = acc[:, LANES:LANES + IDX_HEADS] * scale


def _project_kiwi(x2, g, w_kiwi, lg, lb, tables, scale):
    seq, d = x2.shape
    tm = min(PROJ_TM, seq)
    row = lambda i: (i, 0)
    fix = lambda i: (0, 0)
    vmem = 2 * (tm * d * 4 + d * 2 * LANES * 2 + 5 * tm * LANES * 4) + 8 * tm * 2 * LANES * 4 + tm * d * 6
    return pl.pallas_call(
        functools.partial(_proj_kiwi_kernel, scale=scale),
        out_shape=(jax.ShapeDtypeStruct((seq, LANES), BF16), jax.ShapeDtypeStruct((seq, IDX_HEADS), F32)),
        grid=(seq // tm,),
        in_specs=[pl.BlockSpec((tm, d), row), pl.BlockSpec((1, d), fix), pl.BlockSpec((d, 2 * LANES), fix),
                  pl.BlockSpec((1, LANES), fix), pl.BlockSpec((1, LANES), fix),
                  pl.BlockSpec((tm, LANES), row), pl.BlockSpec((tm, LANES), row), pl.BlockSpec((tm, LANES), row)],
        out_specs=(pl.BlockSpec((tm, LANES), row), pl.BlockSpec((tm, IDX_HEADS), row)),
        compiler_params=_cparams(("parallel",), vmem),
        name="proj_kiwi",
    )(x2, g, w_kiwi, lg, lb, *tables)


def _key_to_float(key):
    bits = key ^ ((key >> 31) & 0x7FFFFFFF)
    return lax.bitcast_convert_type(bits, F32)


def _indexer_kernel(qi_ref, ki_ref, wi_ref, out_ref, s_ref, qh_ref, wb_ref, cst_ref, *, n_sel, idx_bits):
    tq, ck = IDX_TQ, IDX_CK
    nslab = ck // LANES
    n_chunks_total = out_ref.shape[1]
    i = pl.program_id(0)
    nc = ((i + 1) * tq + ck - 1) // ck
    lane = lax.broadcasted_iota(I32, (tq, LANES), 1)
    row = i * tq + lax.broadcasted_iota(I32, (tq, LANES), 0)

    low = lane < IDX_DIM
    w = wi_ref[...]
    for p in range(IDX_HEADS // 2):
        pair = qi_ref[:, p * LANES:(p + 1) * LANES].astype(F32)
        qh_ref[2 * p] = jnp.where(low, pair, 0.0).astype(BF16)
        qh_ref[2 * p + 1] = jnp.where(low, pltpu.roll(pair, IDX_DIM, 1), 0.0).astype(BF16)
    for h in range(IDX_HEADS):
        wb_ref[h] = jnp.broadcast_to(w[:, h:h + 1], (tq, LANES))

    def score_chunk(c, carry):
        kc = ki_ref[pl.ds(pl.multiple_of(c * ck, ck), ck), :]
        slabs = [jnp.zeros((tq, LANES), F32) for _ in range(nslab)]
        for h in range(IDX_HEADS):
            logit = _dot_nt(qh_ref[h], kc)
            wbh = wb_ref[h]
            for k in range(nslab):
                slabs[k] = slabs[k] + jnp.maximum(logit[:, k * LANES:(k + 1) * LANES], 0.0) * wbh
        for k in range(nslab):
            col = c * ck + k * LANES + lane
            slabs[k] = jnp.where(col <= row, slabs[k], NEG_INF)
        s_ref[c] = jnp.concatenate(slabs, axis=1)
        return carry

    lax.fori_loop(0, nc, score_chunk, 0)

    def count(pred):
        def body(c, acc):
            sc = s_ref[c]
            for k in range(nslab):
                col = c * ck + k * LANES + lane
                acc = acc + jnp.where(pred(sc[:, k * LANES:(k + 1) * LANES], col), 1.0, 0.0)
            return acc
        acc = lax.fori_loop(0, nc, body, jnp.zeros((tq, LANES), F32))
        return jnp.broadcast_to(jnp.sum(acc, axis=-1, keepdims=True), (tq, LANES))

    def bit_step(b, carry):
        prefix, cnt_at = carry
        cand = prefix + lax.shift_left(jnp.int32(1), 31 - b)
        cand_f = _key_to_float(cand)
        cnt = count(lambda s, col: s >= cand_f)
        take = cnt >= n_sel
        return jnp.where(take, cand, prefix), jnp.where(take, cnt, cnt_at)

    prefix, cnt_at = lax.fori_loop(
        0, 32, bit_step, (jnp.full((tq, LANES), INT_MIN, I32), jnp.full((tq, LANES), float(n_sel), F32)))
    tau = jnp.where(prefix < KEY_NEG_INF, NEG_INF, _key_to_float(prefix))

    need = jnp.logical_and(cnt_at > n_sel, tau > NEG_INF)
    any_tie = jnp.max(jnp.where(need, 1.0, 0.0)) > 0.0
    idx_all = jnp.full((tq, LANES), 2 ** 30, I32)

    @pl.when(any_tie)
    def _():
        rem = n_sel - count(lambda s, col: s > tau)
        cut = jnp.zeros((tq, LANES), I32)
        for b in range(idx_bits - 1, -1, -1):
            cand = cut + (1 << b)
            below = count(lambda s, col: jnp.logical_and(s == tau, col < cand))
            cut = jnp.where(below < rem, cand, cut)
        cst_ref[...] = jnp.where(need, cut, idx_all)

    @pl.when(jnp.logical_not(any_tie))
    def _():
        cst_ref[...] = idx_all

    cut = cst_ref[...]

    def emit(c, carry):
        sc = s_ref[c]
        outs = []
        for k in range(nslab):
            s = sc[:, k * LANES:(k + 1) * LANES]
            col = c * ck + k * LANES + lane
            tie = jnp.where(col <= cut, 0.0, NEG_INF)
            b = jnp.where(s > tau, 0.0, jnp.where(s == tau, tie, NEG_INF))
            outs.append(jnp.where(col <= row, b, NEG_INF))
        out_ref[0, c] = jnp.concatenate(outs, axis=1).astype(BF16)
        return carry

    lax.fori_loop(0, nc, emit, 0)

    def fill(c, carry):
        out_ref[0, c] = jnp.full((tq, ck), NEG_INF, BF16)
        return carry

    lax.fori_loop(nc, n_chunks_total, fill, 0)


def _dsa_mask(qi, ki, wi, n_sel):
    seq = qi.shape[0]
    tq, ck = IDX_TQ, IDX_CK
    nq, nchunk = seq // tq, seq // ck
    idx_bits = max(1, int(seq - 1).bit_length())
    vmem = (2 * (tq * qi.shape[1] * 2 + tq * LANES * 4 + nchunk * tq * ck * 2) + seq * LANES * 2
            + nchunk * tq * ck * 4 + IDX_HEADS * tq * LANES * 6 + tq * LANES * 4 + 24 * tq * ck * 4)
    return pl.pallas_call(
        functools.partial(_indexer_kernel, n_sel=n_sel, idx_bits=idx_bits),
        out_shape=jax.ShapeDtypeStruct((nq, nchunk, tq, ck), BF16),
        grid=(nq,),
        in_specs=[pl.BlockSpec((tq, qi.shape[1]), lambda i: (i, 0)),
                  _resident((seq, LANES), lambda i: (0, 0)),
                  pl.BlockSpec((tq, IDX_HEADS), lambda i: (i, 0))],
        out_specs=pl.BlockSpec((1, nchunk, tq, ck), lambda i: (i, 0, 0, 0)),
        scratch_shapes=[pltpu.VMEM((nchunk, tq, ck), F32),
                        pltpu.VMEM((IDX_HEADS, tq, LANES), BF16),
                        pltpu.VMEM((IDX_HEADS, tq, LANES), F32),
                        pltpu.VMEM((tq, LANES), I32)],
        compiler_params=_cparams(("parallel",), vmem),
        name="dsa_indexer",
    )(qi, ki, wi)


def _tri_steps(n):
    qs = [i for i in range(n) for _ in range(i + 1)]
    ks = [j for i in range(n) for j in range(i + 1)]
    return jnp.asarray(qs, I32), jnp.asarray(ks, I32)


def _softmax_step(h, s_t, vt_h, m_ref, l_ref, acc_ref):
    m_prev = m_ref[h]
    m_new = jnp.maximum(m_prev, jnp.max(s_t, axis=0, keepdims=True))
    m_safe = jnp.where(m_new == NEG_INF, 0.0, m_new)
    alpha = jnp.exp2(m_prev - m_safe)
    p_t = jnp.exp2(s_t - m_safe)
    l_ref[h] = alpha * l_ref[h] + jnp.sum(p_t, axis=0, keepdims=True)
    acc_ref[h] = alpha * acc_ref[h] + jnp.dot(vt_h, p_t.astype(BF16), preferred_element_type=F32)
    m_ref[h] = m_new


def _attn_init(m_ref, l_ref, acc_ref):
    m_ref[...] = jnp.full(m_ref.shape, NEG_INF, F32)
    l_ref[...] = jnp.zeros(l_ref.shape, F32)
    acc_ref[...] = jnp.zeros(acc_ref.shape, F32)


def _attn_finish(o_ref, l_ref, acc_ref, nh):
    for h in range(nh):
        sl = slice(h * HEAD_DIM, (h + 1) * HEAD_DIM)
        o_ref[:, sl] = (acc_ref[h] / l_ref[h]).T.astype(o_ref.dtype)


def _attn_scratch(nh, t):
    return [pltpu.VMEM((nh, 1, t), F32), pltpu.VMEM((nh, 1, t), F32), pltpu.VMEM((nh, HEAD_DIM, t), F32)]


def _dsa_attn_kernel(qs_ref, ks_ref, q_ref, k_ref, vt_ref, b_ref, o_ref, m_ref, l_ref, acc_ref, *, scale):
    s_id = pl.program_id(0)
    qi, kj = qs_ref[s_id], ks_ref[s_id]
    t = q_ref.shape[0]

    @pl.when(kj == 0)
    def _():
        _attn_init(m_ref, l_ref, acc_ref)

    bias_t = b_ref[...].reshape(t, t).astype(F32).T
    for h in range(DSA_HEADS):
        sl = slice(h * HEAD_DIM, (h + 1) * HEAD_DIM)
        s_t = _dot_nt(k_ref[:, sl], q_ref[:, sl]) * scale + bias_t
        _softmax_step(h, s_t, vt_ref[sl, :], m_ref, l_ref, acc_ref)

    @pl.when(kj == qi)
    def _():
        _attn_finish(o_ref, l_ref, acc_ref, DSA_HEADS)


def _dsa_attention(qk, vals_t, bias4, qcol, kcol, vrow):
    seq, width = qk.shape[0], DSA_HEADS * HEAD_DIM
    t = min(ATT_T, seq)
    nq = seq // t
    qs, ks = _tri_steps(nq)
    sub = t // IDX_TQ
    omap = lambda s, qs, ks: (qs[s], 0)
    qmap = lambda s, qs, ks: (qs[s], qcol)
    kmap = lambda s, qs, ks: (ks[s], kcol)
    vmap = lambda s, qs, ks: (vrow, ks[s])
    vmem = (2 * (3 * t * width * 2 + t * t * 2 + t * width * 2) + t * width * 4 + 28 * t * t * 4)
    return pl.pallas_call(
        functools.partial(_dsa_attn_kernel, scale=HEAD_DIM ** -0.5 * LOG2_E),
        out_shape=jax.ShapeDtypeStruct((seq, width), BF16),
        grid_spec=pltpu.PrefetchScalarGridSpec(
            num_scalar_prefetch=2,
            grid=(int(qs.shape[0]),),
            in_specs=[pl.BlockSpec((t, width), qmap), pl.BlockSpec((t, width), kmap),
                      pl.BlockSpec((width, t), vmap),
                      pl.BlockSpec((sub, 1, IDX_TQ, IDX_CK), lambda s, qs, ks: (qs[s], ks[s], 0, 0))],
            out_specs=pl.BlockSpec((t, width), omap),
            scratch_shapes=_attn_scratch(DSA_HEADS, t)),
        compiler_params=_cparams(("arbitrary",), vmem),
        name="dsa_attention",
    )(qs, ks, qk, qk, vals_t, bias4)


def _kmean_kernel(k_ref, o_ref):
    o_ref[0] = jnp.mean(k_ref[...].astype(F32), axis=0, keepdims=True)


def _block_means(qk, kcol):
    seq, width = qk.shape[0], MOBA_HEADS * HEAD_DIM
    nblk = seq // MOBA_BLOCK
    out = pl.pallas_call(
        _kmean_kernel,
        out_shape=jax.ShapeDtypeStruct((nblk, 1, width), F32),
        grid=(nblk,),
        in_specs=[pl.BlockSpec((MOBA_BLOCK, width), lambda i: (i, kcol))],
        out_specs=pl.BlockSpec((1, 1, width), lambda i: (i, 0, 0)),
        compiler_params=_cparams(("parallel",), 8 * MOBA_BLOCK * width * 4),
        name="moba_block_means",
    )(qk)
    return out.reshape(nblk, width)


def _moba_attn_kernel(qs_ref, ks_ref, q_ref, k_ref, vt_ref, km_ref, o_ref, m_ref, l_ref, acc_ref, sel_ref,
                      *, scale, n_top):
    s_id = pl.program_id(0)
    qi, kj = qs_ref[s_id], ks_ref[s_id]
    t = q_ref.shape[0]
    nblk = km_ref.shape[0]
    per_tile = t // MOBA_BLOCK

    @pl.when(kj == 0)
    def _():
        _attn_init(m_ref, l_ref, acc_ref)
        blk = lax.broadcasted_iota(I32, (nblk, t), 0)
        own = (qi * t + lax.broadcasted_iota(I32, (nblk, t), 1)) // MOBA_BLOCK
        blk_f = blk.astype(F32)
        for h in range(MOBA_HEADS):
            sl = slice(h * HEAD_DIM, (h + 1) * HEAD_DIM)
            g = _dot_nt(km_ref[:, sl].astype(BF16), q_ref[:, sl])
            g = jnp.where(blk < own, g, NEG_INF)
            sel = jnp.full((nblk, t), NEG_INF, F32)
            for _ in range(n_top):
                mx = jnp.max(g, axis=0, keepdims=True)
                is_max = jnp.logical_and(g == mx, mx > NEG_INF)
                first = jnp.min(jnp.where(is_max, blk_f, float(nblk)), axis=0, keepdims=True)
                pick = blk_f == first
                sel = jnp.where(pick, 0.0, sel)
                g = jnp.where(pick, NEG_INF, g)
            sel_ref[h] = sel

    def block_bias(h):
        rows = [jnp.broadcast_to(sel_ref[h, pl.ds(kj * per_tile + b, 1), :], (MOBA_BLOCK, t))
                for b in range(per_tile)]
        return jnp.concatenate(rows, axis=0)

    def run(diag):
        if diag:
            key = lax.broadcasted_iota(I32, (t, t), 0)
            qry = lax.broadcasted_iota(I32, (t, t), 1)
            own_blk = (key // MOBA_BLOCK) == (qry // MOBA_BLOCK)
            causal = jnp.where(key <= qry, 0.0, NEG_INF)
        for h in range(MOBA_HEADS):
            sl = slice(h * HEAD_DIM, (h + 1) * HEAD_DIM)
            bias_t = block_bias(h)
            if diag:
                bias_t = jnp.where(own_blk, causal, bias_t)
            s_t = _dot_nt(k_ref[:, sl], q_ref[:, sl]) * scale + bias_t
            _softmax_step(h, s_t, vt_ref[sl, :], m_ref, l_ref, acc_ref)

    @pl.when(kj < qi)
    def _():
        run(False)

    @pl.when(kj == qi)
    def _():
        run(True)
        _attn_finish(o_ref, l_ref, acc_ref, MOBA_HEADS)


def _moba_attention(qk, vals_t, kmean, qcol, kcol, vrow):
    seq, width = qk.shape[0], MOBA_HEADS * HEAD_DIM
    t = min(ATT_T, seq)
    nq = seq // t
    nblk = kmean.shape[0]
    qs, ks = _tri_steps(nq)
    omap = lambda s, qs, ks: (qs[s], 0)
    qmap = lambda s, qs, ks: (qs[s], qcol)
    kmap = lambda s, qs, ks: (ks[s], kcol)
    vmap = lambda s, qs, ks: (vrow, ks[s])
    vmem = (2 * (4 * t * width * 2) + nblk * width * 4 + t * width * 4 + MOBA_HEADS * nblk * t * 4
            + 28 * t * t * 4)
    return pl.pallas_call(
        functools.partial(_moba_attn_kernel, scale=HEAD_DIM ** -0.5 * LOG2_E, n_top=min(MOBA_TOPK, nblk)),
        out_shape=jax.ShapeDtypeStruct((seq, width), BF16),
        grid_spec=pltpu.PrefetchScalarGridSpec(
            num_scalar_prefetch=2,
            grid=(int(qs.shape[0]),),
            in_specs=[pl.BlockSpec((t, width), qmap), pl.BlockSpec((t, width), kmap),
                      pl.BlockSpec((width, t), vmap),
                      _resident((nblk, width), lambda s, qs, ks: (0, 0))],
            out_specs=pl.BlockSpec((t, width), omap),
            scratch_shapes=_attn_scratch(MOBA_HEADS, t) + [pltpu.VMEM((MOBA_HEADS, nblk, t), F32)]),
        compiler_params=_cparams(("arbitrary",), vmem),
        name="moba_attention",
    )(qs, ks, qk, qk, vals_t, kmean)


def _post_kernel(oa_ref, ob_ref, ga_ref, gb_ref, x_ref, wa_ref, wb_ref, wo_ref, gf_ref, wr_ref, br_ref,
                 x1_ref, h2_ref, ti_ref, tg_ref):
    ta = jnp.dot(oa_ref[...], wa_ref[...], preferred_element_type=F32)
    tb = jnp.dot(ob_ref[...], wb_ref[...], preferred_element_type=F32)
    merged = _sigmoid(ga_ref[...]) * ta + _sigmoid(gb_ref[...]) * tb
    x1 = x_ref[...] + jnp.dot(merged.astype(BF16), wo_ref[...], preferred_element_type=F32)
    x1_ref[...] = x1
    h2 = _rms(x1, gf_ref[...])
    h2_ref[...] = h2
    logits = jnp.dot(h2.astype(BF16), wr_ref[...], preferred_element_type=F32) + br_ref[...]
    lane = lax.broadcasted_iota(I32, logits.shape, 1)
    lane_f = lane.astype(F32)
    idx_out = jnp.zeros(logits.shape, I32)
    val_out = jnp.zeros(logits.shape, F32)
    top = None
    for r in range(EXPERT_TOPK):
        mx = jnp.max(logits, axis=-1, keepdims=True)
        ix = jnp.min(jnp.where(logits == mx, lane_f, float(LANES)), axis=-1, keepdims=True).astype(I32)
        if top is None:
            top = mx
        idx_out = jnp.where(lane == r, ix, idx_out)
        val_out = jnp.where(lane == r, jnp.exp(mx - top), val_out)
        logits = jnp.where(lane == ix, NEG_INF, logits)
    ti_ref[...] = idx_out
    tg_ref[...] = val_out / jnp.sum(val_out, axis=-1, keepdims=True)


def _post_attention(oa, ob, gates_ab, x2, wa, wb, wo, gf, wr, br):
    seq, d = x2.shape
    w = oa.shape[1]
    tm = min(POST_TM, seq)
    row = lambda i: (i, 0)
    fix = lambda i: (0, 0)
    vmem = (2 * (2 * tm * w * 2 + 3 * tm * d * 4 + 2 * tm * d * 4 + 2 * tm * LANES * 4)
            + 2 * w * d * 2 + d * d * 2 + d * LANES * 2 + 8 * tm * d * 4)
    return pl.pallas_call(
        _post_kernel,
        out_shape=(jax.ShapeDtypeStruct((seq, d), F32), jax.ShapeDtypeStruct((seq, d), F32),
                   jax.ShapeDtypeStruct((seq, LANES), I32), jax.ShapeDtypeStruct((seq, LANES), F32)),
        grid=(seq // tm,),
        in_specs=[pl.BlockSpec((tm, w), row), pl.BlockSpec((tm, w), row),
                  pl.BlockSpec((tm, d), row), pl.BlockSpec((tm, d), lambda i: (i, 1)), pl.BlockSpec((tm, d), row),
                  _resident((w, d), fix), _resident((w, d), fix), _resident((d, d), fix),
                  _resident((1, d), fix), _resident((d, LANES), fix), _resident((1, LANES), fix)],
        out_specs=(pl.BlockSpec((tm, d), row), pl.BlockSpec((tm, d), row),
                   pl.BlockSpec((tm, LANES), row), pl.BlockSpec((tm, LANES), row)),
        compiler_params=_cparams(("parallel",), vmem),
        name="merge_outproj_router",
    )(oa, ob, gates_ab, gates_ab, x2, wa, wb, wo, gf, wr, br)


def _split_kernel(w_ref, g_ref, l_ref):
    grp = 2 * LANES
    r = lax.broadcasted_iota(I32, (grp, grp), 0)
    c = lax.broadcasted_iota(I32, (grp, grp), 1)
    src = jnp.where(c < LANES, 2 * c, 2 * (c - LANES) + 1)
    sel = jnp.where(r == src, 1.0, 0.0).astype(BF16)
    for k in range(w_ref.shape[2] // grp):
        res = jnp.dot(w_ref[0, :, k * grp:(k + 1) * grp].astype(BF16), sel, preferred_element_type=F32)
        g_ref[0, :, k * LANES:(k + 1) * LANES] = res[:, :LANES].astype(BF16)
        l_ref[0, :, k * LANES:(k + 1) * LANES] = res[:, LANES:].astype(BF16)


def _split_gate_up(w_gate_up):
    n_e, d, f2 = w_gate_up.shape
    tc = min(2 * MOE_TF, f2)
    half = jax.ShapeDtypeStruct((n_e, d, f2 // 2), BF16)
    return pl.pallas_call(
        _split_kernel,
        out_shape=(half, half),
        grid=(n_e, f2 // tc),
        in_specs=[pl.BlockSpec((1, d, tc), lambda e, j: (e, 0, j))],
        out_specs=(pl.BlockSpec((1, d, tc // 2), lambda e, j: (e, 0, j)),
                   pl.BlockSpec((1, d, tc // 2), lambda e, j: (e, 0, j))),
        compiler_params=_cparams(("parallel", "parallel"), 2 * (d * tc * 4 + d * tc * 2) + 6 * d * tc * 2),
        name="moe_split_gate_up",
    )(w_gate_up)


def _moe_kernel(te_ref, nu_ref, idx_hbm, h_hbm, wg_ref, wl_ref, bg_ref, bl_ref, wd_ref, bd_ref, y_hbm,
                idx0_ref, idx1_ref, xf_ref, xb_ref, acc_ref, isem, gsem, ssem):
    t, j = pl.program_id(0), pl.program_id(1)
    n_tiles, nf = pl.num_programs(0), pl.num_programs(1)
    nu = nu_ref[0]
    tm, d = xb_ref.shape
    sub = xf_ref.shape[2]
    slot = lax.rem(t, 2)
    other = 1 - slot
    idx_refs = (idx0_ref, idx1_ref)

    def on_slot(dyn_slot, fn):
        for s in range(2):
            pl.when(dyn_slot == s)(functools.partial(fn, s))

    def idx_copy(tile, s):
        return pltpu.make_async_copy(idx_hbm.at[pl.ds(pl.multiple_of(tile * 2 * tm, 2 * tm), 2 * tm)],
                                     idx_refs[s], isem.at[s])

    def row_loop(issue):
        def body(i, carry):
            for k in range(sub):
                issue(i, k)
            return carry
        lax.fori_loop(0, tm // sub, body, 0)

    def gather_start(s):
        def issue(i, k):
            tok = idx_refs[s][i * sub + k]
            pltpu.make_async_copy(h_hbm.at[pl.ds(tok, 1), :], xf_ref.at[s, i, pl.ds(k, 1), :], gsem.at[s]).start()
        row_loop(issue)

    def gather_wait(s):
        pltpu.make_async_copy(xf_ref.at[s], xf_ref.at[s], gsem.at[s]).wait()

    def scatter_start(s):
        def issue(i, k):
            dst = idx_refs[s][tm + i * sub + k]
            pltpu.make_async_copy(acc_ref.at[s, i, pl.ds(k, 1), :], y_hbm.at[pl.ds(dst, 1), :], ssem.at[s]).start()
        row_loop(issue)

    def scatter_wait(s):
        pltpu.make_async_copy(acc_ref.at[s], acc_ref.at[s], ssem.at[s]).wait()

    @pl.when(jnp.logical_and(t == 0, j == 0))
    def _():
        first = idx_copy(0, 0)
        first.start()
        acc_ref[...] = jnp.zeros(acc_ref.shape, F32)
        spill0 = y_hbm.shape[0] - 2 * tm
        for s in range(2):
            def fill(i, carry):
                row = pl.multiple_of(spill0 + s * tm + i * sub, sub)
                pltpu.make_async_copy(acc_ref.at[s, i], y_hbm.at[pl.ds(row, sub), :], ssem.at[s]).start()
                return carry
            lax.fori_loop(0, tm // sub, fill, 0)
            scatter_wait(s)
        first.wait()
        gather_start(0)

    @pl.when(jnp.logical_and(j == 0, t + 1 < nu))
    def _():
        on_slot(other, lambda s: idx_copy(t + 1, s).start())

    @pl.when(jnp.logical_and(j == 0, t < nu))
    def _():
        on_slot(slot, gather_wait)
        xb_ref[...] = xf_ref[slot].reshape(tm, d).astype(BF16)
        acc_ref[slot] = jnp.broadcast_to(bd_ref[0], (tm // sub, sub, d))

    @pl.when(jnp.logical_and(j == 1, jnp.logical_and(t >= 1, t <= nu)))
    def _():
        on_slot(other, scatter_wait)

    @pl.when(jnp.logical_and(j == 1, t + 1 < nu))
    def _():
        def fetch(s):
            idx_copy(t + 1, s).wait()
            gather_start(s)
        on_slot(other, fetch)

    @pl.when(t < nu)
    def _():
        x = xb_ref[...]
        g = jnp.dot(x, wg_ref[0], preferred_element_type=F32) + bg_ref[0]
        lin = jnp.dot(x, wl_ref[0], preferred_element_type=F32) + bl_ref[0]
        g = jnp.minimum(g, SWIGLU_LIMIT)
        lin = jnp.clip(lin, -SWIGLU_LIMIT, SWIGLU_LIMIT)
        hid = (lin + 1.0) * (g * _sigmoid(g * SWIGLU_ALPHA))
        y = jnp.dot(hid.astype(BF16), wd_ref[0].astype(BF16), preferred_element_type=F32)
        acc_ref[slot] += y.reshape(tm // sub, sub, d)

    @pl.when(jnp.logical_and(j == nf - 1, t < nu))
    def _():
        on_slot(slot, scatter_start)

        @pl.when(t == n_tiles - 1)
        def _():
            on_slot(slot, scatter_wait)


def _moe_experts(h2, tile_idx, tile_expert, n_used, n_out_rows, wg, wl, bg, bl, wd, bd):
    d = h2.shape[1]
    f = wg.shape[2]
    n_tiles, _, tm = tile_idx.shape
    tf = min(MOE_TF, f)
    nf = f // tf
    assert nf >= 2, "the DMA schedule uses hidden-tile steps 0 and 1 of every row tile"

    def tile(t, nu):
        return jnp.minimum(t, nu[0] - 1)

    def ftile(t, j, nu):
        return jnp.where(t < nu[0], j, nf - 1)

    vmem = (2 * (2 * d * tf * 2 + tf * d * 4 + 2 * tf * 4 + d * 4) + 2 * tm * d * 4 + tm * d * 2 + 2 * tm * d * 4
            + 8 * tm * tf * 4 + tm * d * 4)
    any_spec = pl.BlockSpec(memory_space=pl.ANY)
    return pl.pallas_call(
        _moe_kernel,
        out_shape=jax.ShapeDtypeStruct((n_out_rows, d), F32),
        grid_spec=pltpu.PrefetchScalarGridSpec(
            num_scalar_prefetch=2,
            grid=(n_tiles, nf),
            in_specs=[any_spec, any_spec,
                      pl.BlockSpec((1, d, tf), lambda t, j, te, nu: (te[tile(t, nu)], 0, ftile(t, j, nu))),
                      pl.BlockSpec((1, d, tf), lambda t, j, te, nu: (te[tile(t, nu)], 0, ftile(t, j, nu))),
                      pl.BlockSpec((1, 1, tf), lambda t, j, te, nu: (te[tile(t, nu)], 0, ftile(t, j, nu))),
                      pl.BlockSpec((1, 1, tf), lambda t, j, te, nu: (te[tile(t, nu)], 0, ftile(t, j, nu))),
                      pl.BlockSpec((1, tf, d), lambda t, j, te, nu: (te[tile(t, nu)], ftile(t, j, nu), 0)),
                      pl.BlockSpec((1, 1, d), lambda t, j, te, nu: (te[tile(t, nu)], 0, 0))],
            out_specs=any_spec,
            scratch_shapes=[pltpu.SMEM((2 * tm,), I32), pltpu.SMEM((2 * tm,), I32),
                            pltpu.VMEM((2, tm // MOE_ROW_UNROLL, MOE_ROW_UNROLL, d), F32),
                            pltpu.VMEM((tm, d), BF16),
                            pltpu.VMEM((2, tm // MOE_ROW_UNROLL, MOE_ROW_UNROLL, d), F32),
                            pltpu.SemaphoreType.DMA((2,)), pltpu.SemaphoreType.DMA((2,)),
                            pltpu.SemaphoreType.DMA((2,))]),
        compiler_params=_cparams(("arbitrary", "arbitrary"), vmem),
        name="moe_experts",
    )(tile_expert, n_used, tile_idx.reshape(-1), h2, wg, wl, bg, bl, wd, bd)


def _route(top_idx, tm):
    n_tok = top_idx.shape[0]
    n_slots = n_tok * EXPERT_TOPK
    e_flat = top_idx.reshape(-1)
    order = jnp.argsort(e_flat).astype(I32)
    counts = jnp.bincount(e_flat, length=N_EXPERTS).astype(I32)
    padded = (counts + tm - 1) // tm * tm
    pad_end = jnp.cumsum(padded)
    pad_start = pad_end - padded
    start = jnp.cumsum(counts) - counts
    n_tiles = n_slots // tm + N_EXPERTS
    tile_expert = jnp.minimum(
        jnp.searchsorted(pad_end, jnp.arange(n_tiles) * tm, side="right"), N_EXPERTS - 1).astype(I32)
    n_used = (pad_end[-1] // tm).astype(I32).reshape(1)
    rows = jnp.arange(n_tiles * tm, dtype=I32)
    e_row = jnp.repeat(tile_expert, tm)
    within = rows - pad_start[e_row]
    valid = within < counts[e_row]
    slot = order[jnp.clip(start[e_row] + within, 0, n_slots - 1)]
    spill = n_slots + ((rows // tm) % 2) * tm + rows % tm
    row_tok = jnp.where(valid, slot // EXPERT_TOPK, 0)
    row_dst = jnp.where(valid, (slot % EXPERT_TOPK) * n_tok + slot // EXPERT_TOPK, spill)
    tile_idx = jnp.stack([row_tok.reshape(n_tiles, tm), row_dst.reshape(n_tiles, tm)], axis=1).astype(I32)
    return tile_idx, tile_expert, n_used, n_slots + 2 * tm


def _final_kernel(x1_ref, *rest):
    y_refs = rest[:EXPERT_TOPK]
    tg_ref, p_ref, wpp_ref, wpg_ref, gp_ref, gfin_ref, o_ref = rest[EXPERT_TOPK:]
    x2 = x1_ref[...]
    gates = tg_ref[...]
    for r in range(EXPERT_TOPK):
        x2 = x2 + y_refs[r][...] * gates[:, r:r + 1]
    ple = jnp.dot(p_ref[...].astype(BF16), wpp_ref[...], preferred_element_type=F32)
    gate = _sigmoid(jnp.dot(x2.astype(BF16), wpg_ref[...], preferred_element_type=F32))
    x3 = x2 + _rms(gate * ple, gp_ref[...])
    o_ref[...] = _rms(x3, gfin_ref[...])


def _final(x1, y, gates, p2, wpp, wpg, gp, gfin):
    seq, d = x1.shape
    pd = p2.shape[1]
    tm = min(FIN_TM, seq)
    row = lambda i: (i, 0)
    fix = lambda i: (0, 0)
    vmem = (2 * (2 * tm * d * 4 + EXPERT_TOPK * tm * d * 4 + tm * LANES * 4 + tm * pd * 4)
            + pd * d * 2 + d * d * 2 + 8 * tm * d * 4)
    return pl.pallas_call(
        _final_kernel,
        out_shape=jax.ShapeDtypeStruct((seq, d), F32),
        grid=(seq // tm,),
        in_specs=[pl.BlockSpec((tm, d), row),
                  *[pl.BlockSpec((tm, d), functools.partial(lambda i, r: (r * (seq // tm) + i, 0), r=r))
                    for r in range(EXPERT_TOPK)],
                  pl.BlockSpec((tm, LANES), row), pl.BlockSpec((tm, pd), row),
                  _resident((pd, d), fix), _resident((d, d), fix),
                  _resident((1, d), fix), _resident((1, d), fix)],
        out_specs=pl.BlockSpec((tm, d), row),
        compiler_params=_cparams(("parallel",), vmem),
        name="combine_ple_norm",
    )(x1, *([y] * EXPERT_TOPK), gates, p2, wpp, wpg, gp, gfin)


def _layer(x2, p2, g_mix, w_in, g_idx_k, b_idx_k, w_br_a, w_br_b, w_out, g_ffn, w_router, b_router,
           w_gate_up, b_gate_up, w_down, b_down, w_ple_proj, w_ple_gate, g_ple, g_final):
    seq, d = x2.shape
    dsa_w, moba_w = DSA_HEADS * HEAD_DIM, MOBA_HEADS * HEAD_DIM
    sizes = (dsa_w, dsa_w, dsa_w, IDX_HEADS * IDX_DIM, IDX_DIM, IDX_HEADS, moba_w, moba_w, moba_w, d, d)
    off = np.concatenate([[0], np.cumsum(sizes)])
    col = lambda k: w_in[:, off[k]:off[k + 1]]
    row1 = lambda v: v.reshape(1, -1)
    g_mix2 = row1(g_mix)

    w_rope = jnp.concatenate([col(0), col(1), col(6), col(7)], axis=1).astype(BF16)
    w_qi = col(3).astype(BF16)
    w_kiwi = jnp.zeros((d, 2 * LANES), F32).at[:, :IDX_DIM].set(col(4)).at[:, LANES:LANES + IDX_HEADS].set(col(5))
    w_val_t = jnp.concatenate([col(2), col(8)], axis=1).T.astype(BF16)
    w_gate = jnp.concatenate([col(9), col(10)], axis=1).astype(BF16)

    rope_tabs = _rope_tables(seq, ROPE_DIM, HEAD_DIM)
    idx_tabs = _rope_tables(seq, IDX_ROPE_DIM, IDX_DIM)
    qk = _project(x2, g_mix2, w_rope, BF16, rope_tabs, ROPE_DIM // 2)
    qi = _project(x2, g_mix2, w_qi, BF16, idx_tabs, IDX_ROPE_DIM // 2)
    pad = lambda v: jnp.zeros((1, LANES), F32).at[0, :IDX_DIM].set(v)
    ki, wi = _project_kiwi(x2, g_mix2, w_kiwi.astype(BF16), pad(g_idx_k), pad(b_idx_k), idx_tabs,
                           (IDX_HEADS ** -0.5) * (IDX_DIM ** -0.5))
    vals_t = _project(x2, g_mix2, w_val_t, BF16, transposed=True)
    gates_ab = _project(x2, g_mix2, w_gate, F32)

    bias4 = _dsa_mask(qi, ki, wi, min(DSA_TOPK, seq // 4))
    o_a = _dsa_attention(qk, vals_t, bias4, 0, 1, 0)
    o_b = _moba_attention(qk, vals_t, _block_means(qk, 3), 2, 3, 1)

    w_r = jnp.zeros((d, LANES), F32).at[:, :N_EXPERTS].set(w_router).astype(BF16)
    b_r = jnp.full((1, LANES), NEG_INF, F32).at[0, :N_EXPERTS].set(b_router)
    x1, h2, top_idx, top_gate = _post_attention(
        o_a, o_b, gates_ab, x2, w_br_a.astype(BF16), w_br_b.astype(BF16), w_out.astype(BF16),
        row1(g_ffn), w_r, b_r)

    tile_idx, tile_expert, n_used, n_out_rows = _route(top_idx[:, :EXPERT_TOPK], MOE_TM)
    bias_row = lambda b: b.reshape(N_EXPERTS, 1, -1)
    y = _moe_experts(h2, tile_idx, tile_expert, n_used, n_out_rows,
                     *_split_gate_up(w_gate_up),
                     bias_row(b_gate_up[:, 0::2]), bias_row(b_gate_up[:, 1::2]),
                     w_down, bias_row(b_down))

    return _final(x1, y, top_gate, p2, w_ple_proj.astype(BF16), w_ple_gate.astype(BF16), row1(g_ple),
                  row1(g_final))


def kernel(x, p, g_mix, w_in, g_idx_k, b_idx_k, w_br_a, w_br_b, w_out, g_ffn, w_router, b_router,
           w_gate_up, b_gate_up, w_down, b_down, w_ple_proj, w_ple_gate, g_ple, g_final):
    batch, seq, d = x.shape
    depth = w_in.shape[0]
    assert batch == 1 and depth == 1, "kernel handles the single-sequence, single-layer block"
    assert seq % ATT_T == 0 and seq % PROJ_TM == 0
    out = _layer(x[0], p[0, 0], g_mix[0], w_in[0], g_idx_k[0], b_idx_k[0], w_br_a[0], w_br_b[0], w_out[0],
                 g_ffn[0], w_router[0], b_router[0], w_gate_up[0], b_gate_up[0], w_down[0], b_down[0],
                 w_ple_proj[0], w_ple_gate[0], g_ple[0], g_final)
    return out[None]
```

```python
import functools

import numpy as np
import jax
import jax.numpy as jnp
from jax import lax
from jax.experimental import pallas as pl
from jax.experimental.pallas import tpu as pltpu

F32 = jnp.float32
BF16 = jnp.bfloat16
I32 = jnp.int32
NEG_INF = float("-inf")
LOG2_E = 1.4426950408889634

HEAD_DIM = 128
DSA_HEADS = 8
MOBA_HEADS = 8
ROPE_DIM = HEAD_DIM // 4
ROPE_THETA = 500000.0
IDX_HEADS = 16
IDX_DIM = 64
IDX_ROPE_DIM = IDX_DIM // 4
DSA_TOPK = 256
MOBA_BLOCK = 256
MOBA_TOPK = 3
N_EXPERTS = 32
EXPERT_TOPK = 4
SWIGLU_LIMIT = 7.0
SWIGLU_ALPHA = 1.702
EPS = 1e-6

LANES = 128
VMEM_CAP_BYTES = 60000 * 1024

PROJ_TM = 1024
PROJ_TN = 512
IDX_TQ = 256
IDX_CK = 512
COUNT_ROWS = 32
ATT_T = 512
POST_TM = 256
MOE_TM = 512
MOE_TF = 512
MOE_ROW_UNROLL = 8
FIN_TM = 256

INT_MIN = -2 ** 31
KEY_NEG_INF = 0x807FFFFF - 2 ** 32


def _cparams(sem, vmem_bytes):
    return pltpu.CompilerParams(dimension_semantics=sem,
                                vmem_limit_bytes=int(min(vmem_bytes, VMEM_CAP_BYTES)))


def _resident(shape, index_map):
    return pl.BlockSpec(shape, index_map, pipeline_mode=pl.Buffered(1))


def _sigmoid(x):
    return 1.0 / (1.0 + jnp.exp(-x))


def _rms(xf, g):
    ms = jnp.mean(xf * xf, axis=-1, keepdims=True)
    return xf * lax.rsqrt(ms + EPS) * g


def _dot_nt(a, b):
    return lax.dot_general(a, b, (((1,), (1,)), ((), ())), preferred_element_type=F32)


def _rope_tables(seq, rot_dim, period):
    half = rot_dim // 2
    inv = 1.0 / (ROPE_THETA ** (jnp.arange(half, dtype=F32) / half))
    ang = jnp.arange(seq).astype(F32)[:, None] * inv[None, :]
    cos, sin = jnp.cos(ang), jnp.sin(ang)
    z = lambda n: jnp.zeros((seq, n), F32)
    c = jnp.concatenate([cos, cos, jnp.ones((seq, period - rot_dim), F32)], axis=-1)
    s1 = jnp.concatenate([-sin, z(period - half)], axis=-1)
    s2 = jnp.concatenate([z(half), sin, z(period - rot_dim)], axis=-1)
    rep = LANES // period
    return tuple(jnp.tile(t, (1, rep)) for t in (c, s1, s2))


def _rope(a, c, s1, s2, half):
    return a * c + pltpu.roll(a, LANES - half, 1) * s1 + pltpu.roll(a, half, 1) * s2


def _proj_kernel(x_ref, g_ref, w_ref, *rest, half, transposed):
    if half is None:
        o_ref, h_ref = rest
    else:
        c_ref, s1_ref, s2_ref, o_ref, h_ref = rest

    @pl.when(pl.program_id(1) == 0)
    def _():
        h_ref[...] = _rms(x_ref[...], g_ref[...]).astype(BF16)

    if transposed:
        o_ref[...] = _dot_nt(w_ref[...], h_ref[...]).astype(o_ref.dtype)
        return
    acc = jnp.dot(h_ref[...], w_ref[...], preferred_element_type=F32)
    if half is None:
        o_ref[...] = acc.astype(o_ref.dtype)
    else:
        c, s1, s2 = c_ref[...], s1_ref[...], s2_ref[...]
        for k in range(acc.shape[1] // LANES):
            sl = slice(k * LANES, (k + 1) * LANES)
            o_ref[:, sl] = _rope(acc[:, sl], c, s1, s2, half).astype(o_ref.dtype)


def _project(x2, g, w, out_dtype, tables=None, half=None, transposed=False):
    seq, d = x2.shape
    n = w.shape[0] if transposed else w.shape[1]
    tm, tn = min(PROJ_TM, seq), min(PROJ_TN, n)
    in_specs = [pl.BlockSpec((tm, d), lambda i, j: (i, 0)),
                pl.BlockSpec((1, d), lambda i, j: (0, 0)),
                pl.BlockSpec((tn, d), lambda i, j: (j, 0)) if transposed
                else pl.BlockSpec((d, tn), lambda i, j: (0, j))]
    args = [x2, g, w]
    if tables is not None:
        in_specs += [pl.BlockSpec((tm, LANES), lambda i, j: (i, 0))] * 3
        args += list(tables)
    vmem = 2 * (tm * d * 4 + d * tn * 2 + tm * tn * 4 + 3 * tm * LANES * 4) + tm * d * 2 + 4 * tm * tn * 4
    return pl.pallas_call(
        functools.partial(_proj_kernel, half=half, transposed=transposed),
        out_shape=jax.ShapeDtypeStruct((n, seq) if transposed else (seq, n), out_dtype),
        grid=(seq // tm, n // tn),
        in_specs=in_specs,
        out_specs=pl.BlockSpec((tn, tm), lambda i, j: (j, i)) if transposed
        else pl.BlockSpec((tm, tn), lambda i, j: (i, j)),
        scratch_shapes=[pltpu.VMEM((tm, d), BF16)],
        compiler_params=_cparams(("parallel", "arbitrary"), vmem),
        name="proj_rope" if half is not None else "proj_plain",
    )(*args)


def _proj_kiwi_kernel(x_ref, g_ref, w_ref, lg_ref, lb_ref, c_ref, s1_ref, s2_ref, ki_ref, wi_ref, *, scale):
    h = _rms(x_ref[...], g_ref[...]).astype(BF16)
    acc = jnp.dot(h, w_ref[...], preferred_element_type=F32)
    a = acc[:, :LANES]
    valid = lax.broadcasted_iota(I32, a.shape, 1) < IDX_DIM
    mu = jnp.sum(jnp.where(valid, a, 0.0), axis=-1, keepdims=True) / IDX_DIM
    dlt = jnp.where(valid, a - mu, 0.0)
    var = jnp.sum(dlt * dlt, axis=-1, keepdims=True) / IDX_DIM
    y = dlt * lax.rsqrt(var + EPS) * lg_ref[...] + lb_ref[...]
    ki_ref[...] = _rope(y, c_ref[...], s1_ref[...], s2_ref[...], IDX_ROPE_DIM // 2).astype(BF16)
    wi_ref[...] = acc[:, LANES:LANES + IDX_HEADS] * scale


def _project_kiwi(x2, g, w_kiwi, lg, lb, tables, scale):
    seq, d = x2.shape
    tm = min(PROJ_TM, seq)
    row = lambda i: (i, 0)
    fix = lambda i: (0, 0)
    vmem = 2 * (tm * d * 4 + d * 2 * LANES * 2 + 5 * tm * LANES * 4) + 8 * tm * 2 * LANES * 4 + tm * d * 6
    return pl.pallas_call(
        functools.partial(_proj_kiwi_kernel, scale=scale),
        out_shape=(jax.ShapeDtypeStruct((seq, LANES), BF16), jax.ShapeDtypeStruct((seq, IDX_HEADS), F32)),
        grid=(seq // tm,),
        in_specs=[pl.BlockSpec((tm, d), row), pl.BlockSpec((1, d), fix), pl.BlockSpec((d, 2 * LANES), fix),
                  pl.BlockSpec((1, LANES), fix), pl.BlockSpec((1, LANES), fix),
                  pl.BlockSpec((tm, LANES), row), pl.BlockSpec((tm, LANES), row), pl.BlockSpec((tm, LANES), row)],
        out_specs=(pl.BlockSpec((tm, LANES), row), pl.BlockSpec((tm, IDX_HEADS), row)),
        compiler_params=_cparams(("parallel",), vmem),
        name="proj_kiwi",
    )(x2, g, w_kiwi, lg, lb, *tables)


def _key_to_float(key):
    bits = key ^ ((key >> 31) & 0x7FFFFFFF)
    return lax.bitcast_convert_type(bits, F32)


def _indexer_kernel(qi_ref, ki_ref, wt_ref, out_ref, s_ref, qh_ref, cst_ref, *, n_sel, idx_bits):
    tq, ck = IDX_TQ, IDX_CK
    n_chunks_total = out_ref.shape[1]
    i = pl.program_id(0)
    nc = ((i + 1) * tq + ck - 1) // ck
    qidx = i * tq + lax.broadcasted_iota(I32, (ck, tq), 1)

    def key_idx(c):
        return c * ck + lax.broadcasted_iota(I32, (ck, tq), 0)

    lane = lax.broadcasted_iota(I32, (tq, LANES), 1)
    low = lane < IDX_DIM
    for p in range(IDX_HEADS // 2):
        pair = qi_ref[:, p * LANES:(p + 1) * LANES].astype(F32)
        qh_ref[2 * p] = jnp.where(low, pair, 0.0).astype(BF16)
        qh_ref[2 * p + 1] = jnp.where(low, pltpu.roll(pair, IDX_DIM, 1), 0.0).astype(BF16)

    def score_chunk(c, carry):
        kc = ki_ref[pl.ds(pl.multiple_of(c * ck, ck), ck), :]
        acc = jnp.zeros((ck, tq), F32)
        for h in range(IDX_HEADS):
            acc = acc + jnp.maximum(_dot_nt(kc, qh_ref[h]), 0.0) * wt_ref[h:h + 1, :]
        s_ref[c] = jnp.where(key_idx(c) <= qidx, acc, NEG_INF)
        return carry

    lax.fori_loop(0, nc, score_chunk, 0)

    def count(pred):
        def body(c, acc):
            hit = jnp.where(pred(s_ref[c], lambda: key_idx(c)), 1.0, 0.0)
            return acc + jnp.sum(hit.reshape(ck // COUNT_ROWS, COUNT_ROWS, tq), axis=0)
        acc = lax.fori_loop(0, nc, body, jnp.zeros((COUNT_ROWS, tq), F32))
        return jnp.sum(acc, axis=0, keepdims=True)

    def bit_step(b, carry):
        prefix, cnt_at = carry
        cand = prefix + lax.shift_left(jnp.int32(1), 31 - b)
        cand_f = _key_to_float(cand)
        cnt = count(lambda sc, key: sc >= cand_f)
        take = cnt >= n_sel
        return jnp.where(take, cand, prefix), jnp.where(take, cnt, cnt_at)

    prefix, cnt_at = lax.fori_loop(
        0, 32, bit_step, (jnp.full((1, tq), INT_MIN, I32), jnp.full((1, tq), float(n_sel), F32)))
    tau = jnp.where(prefix < KEY_NEG_INF, NEG_INF, _key_to_float(prefix))

    need = jnp.logical_and(cnt_at > n_sel, tau > NEG_INF)
    any_tie = jnp.max(jnp.where(need, 1.0, 0.0)) > 0.0
    idx_all = jnp.full((1, tq), 2 ** 30, I32)

    @pl.when(any_tie)
    def _():
        rem = n_sel - count(lambda sc, key: sc > tau)
        cut = jnp.zeros((1, tq), I32)
        for b in range(idx_bits - 1, -1, -1):
            cand = cut + (1 << b)
            below = count(lambda sc, key: jnp.logical_and(sc == tau, key() < cand))
            cut = jnp.where(below < rem, cand, cut)
        cst_ref[...] = jnp.where(need, cut, idx_all)

    @pl.when(jnp.logical_not(any_tie))
    def _():
        cst_ref[...] = idx_all

    cut = cst_ref[...]

    def emit(c, carry):
        sc = s_ref[c]
        key = key_idx(c)
        tie = jnp.where(key <= cut, 0.0, NEG_INF)
        b = jnp.where(sc > tau, 0.0, jnp.where(sc == tau, tie, NEG_INF))
        out_ref[0, c] = jnp.where(key <= qidx, b, NEG_INF).astype(BF16)
        return carry

    lax.fori_loop(0, nc, emit, 0)

    def fill(c, carry):
        out_ref[0, c] = jnp.full((ck, tq), NEG_INF, BF16)
        return carry

    lax.fori_loop(nc, n_chunks_total, fill, 0)


def _dsa_mask(qi, ki, wi_t, n_sel):
    seq = qi.shape[0]
    tq, ck = min(IDX_TQ, seq), IDX_CK
    nq, nchunk = seq // tq, seq // ck
    idx_bits = max(1, int(seq - 1).bit_length())
    vmem = (2 * (tq * qi.shape[1] * 2 + IDX_HEADS * tq * 4 + nchunk * tq * ck * 2) + seq * LANES * 2
            + nchunk * tq * ck * 4 + IDX_HEADS * tq * LANES * 2 + 12 * tq * ck * 4)
    return pl.pallas_call(
        functools.partial(_indexer_kernel, n_sel=n_sel, idx_bits=idx_bits),
        out_shape=jax.ShapeDtypeStruct((nq, nchunk, ck, tq), BF16),
        grid=(nq,),
        in_specs=[pl.BlockSpec((tq, qi.shape[1]), lambda i: (i, 0)),
                  _resident((seq, LANES), lambda i: (0, 0)),
                  pl.BlockSpec((IDX_HEADS, tq), lambda i: (0, i))],
        out_specs=pl.BlockSpec((1, nchunk, ck, tq), lambda i: (i, 0, 0, 0)),
        scratch_shapes=[pltpu.VMEM((nchunk, ck, tq), F32),
                        pltpu.VMEM((IDX_HEADS, tq, LANES), BF16),
                        pltpu.VMEM((1, tq), I32)],
        compiler_params=_cparams(("parallel",), vmem),
        name="dsa_indexer",
    )(qi, ki, wi_t)


def _tri_steps(n):
    qs = [i for i in range(n) for _ in range(i + 1)]
    ks = [j for i in range(n) for j in range(i + 1)]
    return jnp.asarray(qs, I32), jnp.asarray(ks, I32)


def _softmax_step(h, s_t, vt_h, m_ref, l_ref, acc_ref):
    m_prev = m_ref[h]
    m_new = jnp.maximum(m_prev, jnp.max(s_t, axis=0, keepdims=True))
    m_safe = jnp.where(m_new == NEG_INF, 0.0, m_new)
    alpha = jnp.exp2(m_prev - m_safe)
    p_t = jnp.exp2(s_t - m_safe)
    l_ref[h] = alpha * l_ref[h] + jnp.sum(p_t, axis=0, keepdims=True)
    acc_ref[h] = alpha * acc_ref[h] + jnp.dot(vt_h, p_t.astype(BF16), preferred_element_type=F32)
    m_ref[h] = m_new


def _attn_init(m_ref, l_ref, acc_ref):
    m_ref[...] = jnp.full(m_ref.shape, NEG_INF, F32)
    l_ref[...] = jnp.zeros(l_ref.shape, F32)
    acc_ref[...] = jnp.zeros(acc_ref.shape, F32)


def _attn_finish(o_ref, l_ref, acc_ref, nh):
    for h in range(nh):
        sl = slice(h * HEAD_DIM, (h + 1) * HEAD_DIM)
        o_ref[:, sl] = (acc_ref[h] / l_ref[h]).T.astype(o_ref.dtype)


def _attn_scratch(nh, t):
    return [pltpu.VMEM((nh, 1, t), F32), pltpu.VMEM((nh, 1, t), F32), pltpu.VMEM((nh, HEAD_DIM, t), F32)]


def _dsa_attn_kernel(qs_ref, ks_ref, q_ref, k_ref, vt_ref, b_ref, o_ref, m_ref, l_ref, acc_ref, *, scale):
    s_id = pl.program_id(0)
    qi, kj = qs_ref[s_id], ks_ref[s_id]
    t = q_ref.shape[0]

    @pl.when(kj == 0)
    def _():
        _attn_init(m_ref, l_ref, acc_ref)

    bias_t = jnp.concatenate([b_ref[u, 0] for u in range(b_ref.shape[0])], axis=1).astype(F32)
    for h in range(DSA_HEADS):
        sl = slice(h * HEAD_DIM, (h + 1) * HEAD_DIM)
        s_t = _dot_nt(k_ref[:, sl], q_ref[:, sl]) * scale + bias_t
        _softmax_step(h, s_t, vt_ref[sl, :], m_ref, l_ref, acc_ref)

    @pl.when(kj == qi)
    def _():
        _attn_finish(o_ref, l_ref, acc_ref, DSA_HEADS)


def _dsa_attention(qk, vals_t, bias4, qcol, kcol, vrow):
    seq, width = qk.shape[0], DSA_HEADS * HEAD_DIM
    t = min(ATT_T, seq)
    nq = seq // t
    qs, ks = _tri_steps(nq)
    sub = t // IDX_TQ
    omap = lambda s, qs, ks: (qs[s], 0)
    qmap = lambda s, qs, ks: (qs[s], qcol)
    kmap = lambda s, qs, ks: (ks[s], kcol)
    vmap = lambda s, qs, ks: (vrow, ks[s])
    vmem = (2 * (3 * t * width * 2 + t * t * 2 + t * width * 2) + t * width * 4 + 28 * t * t * 4)
    return pl.pallas_call(
        functools.partial(_dsa_attn_kernel, scale=HEAD_DIM ** -0.5 * LOG2_E),
        out_shape=jax.ShapeDtypeStruct((seq, width), BF16),
        grid_spec=pltpu.PrefetchScalarGridSpec(
            num_scalar_prefetch=2,
            grid=(int(qs.shape[0]),),
            in_specs=[pl.BlockSpec((t, width), qmap), pl.BlockSpec((t, width), kmap),
                      pl.BlockSpec((width, t), vmap),
                      pl.BlockSpec((sub, 1, IDX_CK, IDX_TQ), lambda s, qs, ks: (qs[s], ks[s], 0, 0))],
            out_specs=pl.BlockSpec((t, width), omap),
            scratch_shapes=_attn_scratch(DSA_HEADS, t)),
        compiler_params=_cparams(("arbitrary",), vmem),
        name="dsa_attention",
    )(qs, ks, qk, qk, vals_t, bias4)


def _kmean_kernel(k_ref, o_ref):
    o_ref[0] = jnp.mean(k_ref[...].astype(F32), axis=0, keepdims=True)


def _block_means(qk, kcol):
    seq, width = qk.shape[0], MOBA_HEADS * HEAD_DIM
    nblk = seq // MOBA_BLOCK
    out = pl.pallas_call(
        _kmean_kernel,
        out_shape=jax.ShapeDtypeStruct((nblk, 1, width), F32),
        grid=(nblk,),
        in_specs=[pl.BlockSpec((MOBA_BLOCK, width), lambda i: (i, kcol))],
        out_specs=pl.BlockSpec((1, 1, width), lambda i: (i, 0, 0)),
        compiler_params=_cparams(("parallel",), 8 * MOBA_BLOCK * width * 4),
        name="moba_block_means",
    )(qk)
    return out.reshape(nblk, width)


def _moba_attn_kernel(qs_ref, ks_ref, q_ref, k_ref, vt_ref, km_ref, o_ref, m_ref, l_ref, acc_ref, sel_ref,
                      *, scale, n_top):
    s_id = pl.program_id(0)
    qi, kj = qs_ref[s_id], ks_ref[s_id]
    t = q_ref.shape[0]
    nblk = km_ref.shape[0]
    per_tile = t // MOBA_BLOCK

    @pl.when(kj == 0)
    def _():
        _attn_init(m_ref, l_ref, acc_ref)
        blk = lax.broadcasted_iota(I32, (nblk, t), 0)
        own = (qi * t + lax.broadcasted_iota(I32, (nblk, t), 1)) // MOBA_BLOCK
        blk_f = blk.astype(F32)
        for h in range(MOBA_HEADS):
            sl = slice(h * HEAD_DIM, (h + 1) * HEAD_DIM)
            g = _dot_nt(km_ref[:, sl].astype(BF16), q_ref[:, sl])
            g = jnp.where(blk < own, g, NEG_INF)
            sel = jnp.full((nblk, t), NEG_INF, F32)
            for _ in range(n_top):
                mx = jnp.max(g, axis=0, keepdims=True)
                is_max = jnp.logical_and(g == mx, mx > NEG_INF)
                first = jnp.min(jnp.where(is_max, blk_f, float(nblk)), axis=0, keepdims=True)
                pick = blk_f == first
                sel = jnp.where(pick, 0.0, sel)
                g = jnp.where(pick, NEG_INF, g)
            sel_ref[h] = sel

    def block_bias(h):
        rows = [jnp.broadcast_to(sel_ref[h, pl.ds(kj * per_tile + b, 1), :], (MOBA_BLOCK, t))
                for b in range(per_tile)]
        return jnp.concatenate(rows, axis=0)

    def run(diag):
        if diag:
            key = lax.broadcasted_iota(I32, (t, t), 0)
            qry = lax.broadcasted_iota(I32, (t, t), 1)
            own_blk = (key // MOBA_BLOCK) == (qry // MOBA_BLOCK)
            causal = jnp.where(key <= qry, 0.0, NEG_INF)
        for h in range(MOBA_HEADS):
            sl = slice(h * HEAD_DIM, (h + 1) * HEAD_DIM)
            bias_t = block_bias(h)
            if diag:
                bias_t = jnp.where(own_blk, causal, bias_t)
            s_t = _dot_nt(k_ref[:, sl], q_ref[:, sl]) * scale + bias_t
            _softmax_step(h, s_t, vt_ref[sl, :], m_ref, l_ref, acc_ref)

    @pl.when(kj < qi)
    def _():
        run(False)

    @pl.when(kj == qi)
    def _():
        run(True)
        _attn_finish(o_ref, l_ref, acc_ref, MOBA_HEADS)


def _moba_attention(qk, vals_t, kmean, qcol, kcol, vrow):
    seq, width = qk.shape[0], MOBA_HEADS * HEAD_DIM
    t = min(ATT_T, seq)
    nq = seq // t
    nblk = kmean.shape[0]
    qs, ks = _tri_steps(nq)
    omap = lambda s, qs, ks: (qs[s], 0)
    qmap = lambda s, qs, ks: (qs[s], qcol)
    kmap = lambda s, qs, ks: (ks[s], kcol)
    vmap = lambda s, qs, ks: (vrow, ks[s])
    vmem = (2 * (4 * t * width * 2) + nblk * width * 4 + t * width * 4 + MOBA_HEADS * nblk * t * 4
            + 28 * t * t * 4)
    return pl.pallas_call(
        functools.partial(_moba_attn_kernel, scale=HEAD_DIM ** -0.5 * LOG2_E, n_top=min(MOBA_TOPK, nblk)),
        out_shape=jax.ShapeDtypeStruct((seq, width), BF16),
        grid_spec=pltpu.PrefetchScalarGridSpec(
            num_scalar_prefetch=2,
            grid=(int(qs.shape[0]),),
            in_specs=[pl.BlockSpec((t, width), qmap), pl.BlockSpec((t, width), kmap),
                      pl.BlockSpec((width, t), vmap),
                      _resident((nblk, width), lambda s, qs, ks: (0, 0))],
            out_specs=pl.BlockSpec((t, width), omap),
            scratch_shapes=_attn_scratch(MOBA_HEADS, t) + [pltpu.VMEM((MOBA_HEADS, nblk, t), F32)]),
        compiler_params=_cparams(("arbitrary",), vmem),
        name="moba_attention",
    )(qs, ks, qk, qk, vals_t, kmean)


def _post_kernel(oa_ref, ob_ref, ga_ref, gb_ref, x_ref, wa_ref, wb_ref, wo_ref, gf_ref, wr_ref, br_ref,
                 x1_ref, h2_ref, ti_ref, tg_ref):
    ta = jnp.dot(oa_ref[...], wa_ref[...], preferred_element_type=F32)
    tb = jnp.dot(ob_ref[...], wb_ref[...], preferred_element_type=F32)
    merged = _sigmoid(ga_ref[...]) * ta + _sigmoid(gb_ref[...]) * tb
    x1 = x_ref[...] + jnp.dot(merged.astype(BF16), wo_ref[...], preferred_element_type=F32)
    x1_ref[...] = x1
    h2 = _rms(x1, gf_ref[...])
    h2_ref[...] = h2
    logits = jnp.dot(h2.astype(BF16), wr_ref[...], preferred_element_type=F32) + br_ref[...]
    lane = lax.broadcasted_iota(I32, logits.shape, 1)
    lane_f = lane.astype(F32)
    idx_out = jnp.zeros(logits.shape, I32)
    val_out = jnp.zeros(logits.shape, F32)
    top = None
    for r in range(EXPERT_TOPK):
        mx = jnp.max(logits, axis=-1, keepdims=True)
        ix = jnp.min(jnp.where(logits == mx, lane_f, float(LANES)), axis=-1, keepdims=True).astype(I32)
        if top is None:
            top = mx
        idx_out = jnp.where(lane == r, ix, idx_out)
        val_out = jnp.where(lane == r, jnp.exp(mx - top), val_out)
        logits = jnp.where(lane == ix, NEG_INF, logits)
    ti_ref[...] = idx_out
    tg_ref[...] = val_out / jnp.sum(val_out, axis=-1, keepdims=True)


def _post_attention(oa, ob, gates_ab, x2, wa, wb, wo, gf, wr, br):
    seq, d = x2.shape
    w = oa.shape[1]
    tm = min(POST_TM, seq)
    row = lambda i: (i, 0)
    fix = lambda i: (0, 0)
    vmem = (2 * (2 * tm * w * 2 + 3 * tm * d * 4 + 2 * tm * d * 4 + 2 * tm * LANES * 4)
            + 2 * w * d * 2 + d * d * 2 + d * LANES * 2 + 8 * tm * d * 4)
    return pl.pallas_call(
        _post_kernel,
        out_shape=(jax.ShapeDtypeStruct((seq, d), F32), jax.ShapeDtypeStruct((seq, d), F32),
                   jax.ShapeDtypeStruct((seq, LANES), I32), jax.ShapeDtypeStruct((seq, LANES), F32)),
        grid=(seq // tm,),
        in_specs=[pl.BlockSpec((tm, w), row), pl.BlockSpec((tm, w), row),
                  pl.BlockSpec((tm, d), row), pl.BlockSpec((tm, d), lambda i: (i, 1)), pl.BlockSpec((tm, d), row),
                  _resident((w, d), fix), _resident((w, d), fix), _resident((d, d), fix),
                  _resident((1, d), fix), _resident((d, LANES), fix), _resident((1, LANES), fix)],
        out_specs=(pl.BlockSpec((tm, d), row), pl.BlockSpec((tm, d), row),
                   pl.BlockSpec((tm, LANES), row), pl.BlockSpec((tm, LANES), row)),
        compiler_params=_cparams(("parallel",), vmem),
        name="merge_outproj_router",
    )(oa, ob, gates_ab, gates_ab, x2, wa, wb, wo, gf, wr, br)


def _split_kernel(w_ref, g_ref, l_ref):
    grp = 2 * LANES
    r = lax.broadcasted_iota(I32, (grp, grp), 0)
    c = lax.broadcasted_iota(I32, (grp, grp), 1)
    src = jnp.where(c < LANES, 2 * c, 2 * (c - LANES) + 1)
    sel = jnp.where(r == src, 1.0, 0.0).astype(BF16)
    for k in range(w_ref.shape[2] // grp):
        res = jnp.dot(w_ref[0, :, k * grp:(k + 1) * grp].astype(BF16), sel, preferred_element_type=F32)
        g_ref[0, :, k * LANES:(k + 1) * LANES] = res[:, :LANES].astype(BF16)
        l_ref[0, :, k * LANES:(k + 1) * LANES] = res[:, LANES:].astype(BF16)


def _split_gate_up(w_gate_up):
    n_e, d, f2 = w_gate_up.shape
    tc = min(2 * MOE_TF, f2)
    half = jax.ShapeDtypeStruct((n_e, d, f2 // 2), BF16)
    return pl.pallas_call(
        _split_kernel,
        out_shape=(half, half),
        grid=(n_e, f2 // tc),
        in_specs=[pl.BlockSpec((1, d, tc), lambda e, j: (e, 0, j))],
        out_specs=(pl.BlockSpec((1, d, tc // 2), lambda e, j: (e, 0, j)),
                   pl.BlockSpec((1, d, tc // 2), lambda e, j: (e, 0, j))),
        compiler_params=_cparams(("parallel", "parallel"), 2 * (d * tc * 4 + d * tc * 2) + 6 * d * tc * 2),
        name="moe_split_gate_up",
    )(w_gate_up)


def _moe_kernel(te_ref, nu_ref, idx_hbm, h_hbm, wg_ref, wl_ref, bg_ref, bl_ref, wd_ref, bd_ref, y_hbm,
                idx0_ref, idx1_ref, xf_ref, xb_ref, acc_ref, isem, gsem, ssem):
    t, j = pl.program_id(0), pl.program_id(1)
    n_tiles, nf = pl.num_programs(0), pl.num_programs(1)
    nu = nu_ref[0]
    tm, d = xb_ref.shape
    sub = xf_ref.shape[2]
    slot = lax.rem(t, 2)
    other = 1 - slot
    idx_refs = (idx0_ref, idx1_ref)

    def on_slot(dyn_slot, fn):
        for s in range(2):
            pl.when(dyn_slot == s)(functools.partial(fn, s))

    def idx_copy(tile, s):
        return pltpu.make_async_copy(idx_hbm.at[pl.ds(pl.multiple_of(tile * 2 * tm, 2 * tm), 2 * tm)],
                                     idx_refs[s], isem.at[s])

    def row_loop(issue):
        def body(i, carry):
            for k in range(sub):
                issue(i, k)
            return carry
        lax.fori_loop(0, tm // sub, body, 0)

    def gather_start(s):
        def issue(i, k):
            tok = idx_refs[s][i * sub + k]
            pltpu.make_async_copy(h_hbm.at[pl.ds(tok, 1), :], xf_ref.at[s, i, pl.ds(k, 1), :], gsem.at[s]).start()
        row_loop(issue)

    def gather_wait(s):
        pltpu.make_async_copy(xf_ref.at[s], xf_ref.at[s], gsem.at[s]).wait()

    def scatter_start(s):
        def issue(i, k):
            dst = idx_refs[s][tm + i * sub + k]
            pltpu.make_async_copy(acc_ref.at[s, i, pl.ds(k, 1), :], y_hbm.at[pl.ds(dst, 1), :], ssem.at[s]).start()
        row_loop(issue)

    def scatter_wait(s):
        pltpu.make_async_copy(acc_ref.at[s], acc_ref.at[s], ssem.at[s]).wait()

    @pl.when(jnp.logical_and(t == 0, j == 0))
    def _():
        first = idx_copy(0, 0)
        first.start()
        acc_ref[...] = jnp.zeros(acc_ref.shape, F32)
        spill0 = y_hbm.shape[0] - 2 * tm
        for s in range(2):
            def fill(i, carry):
                row = pl.multiple_of(spill0 + s * tm + i * sub, sub)
                pltpu.make_async_copy(acc_ref.at[s, i], y_hbm.at[pl.ds(row, sub), :], ssem.at[s]).start()
                return carry
            lax.fori_loop(0, tm // sub, fill, 0)
            scatter_wait(s)
        first.wait()
        gather_start(0)

    @pl.when(jnp.logical_and(j == 0, t + 1 < nu))
    def _():
        on_slot(other, lambda s: idx_copy(t + 1, s).start())

    @pl.when(jnp.logical_and(j == 0, t < nu))
    def _():
        on_slot(slot, gather_wait)
        xb_ref[...] = xf_ref[slot].reshape(tm, d).astype(BF16)
        acc_ref[slot] = jnp.broadcast_to(bd_ref[0], (tm // sub, sub, d))

    @pl.when(jnp.logical_and(j == 1, jnp.logical_and(t >= 1, t <= nu)))
    def _():
        on_slot(other, scatter_wait)

    @pl.when(jnp.logical_and(j == 1, t + 1 < nu))
    def _():
        def fetch(s):
            idx_copy(t + 1, s).wait()
            gather_start(s)
        on_slot(other, fetch)

    @pl.when(t < nu)
    def _():
        x = xb_ref[...]
        g = jnp.dot(x, wg_ref[0], preferred_element_type=F32) + bg_ref[0]
        lin = jnp.dot(x, wl_ref[0], preferred_element_type=F32) + bl_ref[0]
        g = jnp.minimum(g, SWIGLU_LIMIT)
        lin = jnp.clip(lin, -SWIGLU_LIMIT, SWIGLU_LIMIT)
        hid = (lin + 1.0) * (g * _sigmoid(g * SWIGLU_ALPHA))
        y = jnp.dot(hid.astype(BF16), wd_ref[0].astype(BF16), preferred_element_type=F32)
        acc_ref[slot] += y.reshape(tm // sub, sub, d)

    @pl.when(jnp.logical_and(j == nf - 1, t < nu))
    def _():
        on_slot(slot, scatter_start)

        @pl.when(t == n_tiles - 1)
        def _():
            on_slot(slot, scatter_wait)


def _moe_experts(h2, tile_idx, tile_expert, n_used, n_out_rows, wg, wl, bg, bl, wd, bd):
    d = h2.shape[1]
    f = wg.shape[2]
    n_tiles, _, tm = tile_idx.shape
    tf = min(MOE_TF, f)
    nf = f // tf
    assert nf >= 2, "the DMA schedule uses hidden-tile steps 0 and 1 of every row tile"

    def tile(t, nu):
        return jnp.minimum(t, nu[0] - 1)

    def ftile(t, j, nu):
        return jnp.where(t < nu[0], j, nf - 1)

    vmem = (2 * (2 * d * tf * 2 + tf * d * 4 + 2 * tf * 4 + d * 4) + 2 * tm * d * 4 + tm * d * 2 + 2 * tm * d * 4
            + 8 * tm * tf * 4 + tm * d * 4)
    any_spec = pl.BlockSpec(memory_space=pl.ANY)
    return pl.pallas_call(
        _moe_kernel,
        out_shape=jax.ShapeDtypeStruct((n_out_rows, d), F32),
        grid_spec=pltpu.PrefetchScalarGridSpec(
            num_scalar_prefetch=2,
            grid=(n_tiles, nf),
            in_specs=[any_spec, any_spec,
                      pl.BlockSpec((1, d, tf), lambda t, j, te, nu: (te[tile(t, nu)], 0, ftile(t, j, nu))),
                      pl.BlockSpec((1, d, tf), lambda t, j, te, nu: (te[tile(t, nu)], 0, ftile(t, j, nu))),
                      pl.BlockSpec((1, 1, tf), lambda t, j, te, nu: (te[tile(t, nu)], 0, ftile(t, j, nu))),
                      pl.BlockSpec((1, 1, tf), lambda t, j, te, nu: (te[tile(t, nu)], 0, ftile(t, j, nu))),
                      pl.BlockSpec((1, tf, d), lambda t, j, te, nu: (te[tile(t, nu)], ftile(t, j, nu), 0)),
                      pl.BlockSpec((1, 1, d), lambda t, j, te, nu: (te[tile(t, nu)], 0, 0))],
            out_specs=any_spec,
            scratch_shapes=[pltpu.SMEM((2 * tm,), I32), pltpu.SMEM((2 * tm,), I32),
                            pltpu.VMEM((2, tm // MOE_ROW_UNROLL, MOE_ROW_UNROLL, d), F32),
                            pltpu.VMEM((tm, d), BF16),
                            pltpu.VMEM((2, tm // MOE_ROW_UNROLL, MOE_ROW_UNROLL, d), F32),
                            pltpu.SemaphoreType.DMA((2,)), pltpu.SemaphoreType.DMA((2,)),
                            pltpu.SemaphoreType.DMA((2,))]),
        compiler_params=_cparams(("arbitrary", "arbitrary"), vmem),
        name="moe_experts",
    )(tile_expert, n_used, tile_idx.reshape(-1), h2, wg, wl, bg, bl, wd, bd)


def _route(top_idx, tm):
    n_tok = top_idx.shape[0]
    n_slots = n_tok * EXPERT_TOPK
    e_flat = top_idx.reshape(-1)
    order = jnp.argsort(e_flat).astype(I32)
    counts = jnp.bincount(e_flat, length=N_EXPERTS).astype(I32)
    padded = (counts + tm - 1) // tm * tm
    pad_end = jnp.cumsum(padded)
    pad_start = pad_end - padded
    start = jnp.cumsum(counts) - counts
    n_tiles = n_slots // tm + N_EXPERTS
    tile_expert = jnp.minimum(
        jnp.searchsorted(pad_end, jnp.arange(n_tiles) * tm, side="right"), N_EXPERTS - 1).astype(I32)
    n_used = (pad_end[-1] // tm).astype(I32).reshape(1)
    rows = jnp.arange(n_tiles * tm, dtype=I32)
    e_row = jnp.repeat(tile_expert, tm)
    within = rows - pad_start[e_row]
    valid = within < counts[e_row]
    slot = order[jnp.clip(start[e_row] + within, 0, n_slots - 1)]
    spill = n_slots + ((rows // tm) % 2) * tm + rows % tm
    row_tok = jnp.where(valid, slot // EXPERT_TOPK, 0)
    row_dst = jnp.where(valid, (slot % EXPERT_TOPK) * n_tok + slot // EXPERT_TOPK, spill)
    tile_idx = jnp.stack([row_tok.reshape(n_tiles, tm), row_dst.reshape(n_tiles, tm)], axis=1).astype(I32)
    return tile_idx, tile_expert, n_used, n_slots + 2 * tm


def _final_kernel(x1_ref, *rest):
    y_refs = rest[:EXPERT_TOPK]
    tg_ref, p_ref, wpp_ref, wpg_ref, gp_ref, gfin_ref, o_ref = rest[EXPERT_TOPK:]
    x2 = x1_ref[...]
    gates = tg_ref[...]
    for r in range(EXPERT_TOPK):
        x2 = x2 + y_refs[r][...] * gates[:, r:r + 1]
    ple = jnp.dot(p_ref[...].astype(BF16), wpp_ref[...], preferred_element_type=F32)
    gate = _sigmoid(jnp.dot(x2.astype(BF16), wpg_ref[...], preferred_element_type=F32))
    x3 = x2 + _rms(gate * ple, gp_ref[...])
    o_ref[...] = _rms(x3, gfin_ref[...])


def _final(x1, y, gates, p2, wpp, wpg, gp, gfin):
    seq, d = x1.shape
    pd = p2.shape[1]
    tm = min(FIN_TM, seq)
    row = lambda i: (i, 0)
    fix = lambda i: (0, 0)
    vmem = (2 * (2 * tm * d * 4 + EXPERT_TOPK * tm * d * 4 + tm * LANES * 4 + tm * pd * 4)
            + pd * d * 2 + d * d * 2 + 8 * tm * d * 4)
    return pl.pallas_call(
        _final_kernel,
        out_shape=jax.ShapeDtypeStruct((seq, d), F32),
        grid=(seq // tm,),
        in_specs=[pl.BlockSpec((tm, d), row),
                  *[pl.BlockSpec((tm, d), functools.partial(lambda i, r: (r * (seq // tm) + i, 0), r=r))
                    for r in range(EXPERT_TOPK)],
                  pl.BlockSpec((tm, LANES), row), pl.BlockSpec((tm, pd), row),
                  _resident((pd, d), fix), _resident((d, d), fix),
                  _resident((1, d), fix), _resident((1, d), fix)],
        out_specs=pl.BlockSpec((tm, d), row),
        compiler_params=_cparams(("parallel",), vmem),
        name="combine_ple_norm",
    )(x1, *([y] * EXPERT_TOPK), gates, p2, wpp, wpg, gp, gfin)


def _layer(x2, p2, g_mix, w_in, g_idx_k, b_idx_k, w_br_a, w_br_b, w_out, g_ffn, w_router, b_router,
           w_gate_up, b_gate_up, w_down, b_down, w_ple_proj, w_ple_gate, g_ple, g_final):
    seq, d = x2.shape
    dsa_w, moba_w = DSA_HEADS * HEAD_DIM, MOBA_HEADS * HEAD_DIM
    sizes = (dsa_w, dsa_w, dsa_w, IDX_HEADS * IDX_DIM, IDX_DIM, IDX_HEADS, moba_w, moba_w, moba_w, d, d)
    off = np.concatenate([[0], np.cumsum(sizes)])
    col = lambda k: w_in[:, off[k]:off[k + 1]]
    row1 = lambda v: v.reshape(1, -1)
    g_mix2 = row1(g_mix)

    w_rope = jnp.concatenate([col(0), col(1), col(6), col(7)], axis=1).astype(BF16)
    w_qi = col(3).astype(BF16)
    w_kiwi = jnp.zeros((d, 2 * LANES), F32).at[:, :IDX_DIM].set(col(4)).at[:, LANES:LANES + IDX_HEADS].set(col(5))
    w_val_t = jnp.concatenate([col(2), col(8)], axis=1).T.astype(BF16)
    w_gate = jnp.concatenate([col(9), col(10)], axis=1).astype(BF16)

    rope_tabs = _rope_tables(seq, ROPE_DIM, HEAD_DIM)
    idx_tabs = _rope_tables(seq, IDX_ROPE_DIM, IDX_DIM)
    qk = _project(x2, g_mix2, w_rope, BF16, rope_tabs, ROPE_DIM // 2)
    qi = _project(x2, g_mix2, w_qi, BF16, idx_tabs, IDX_ROPE_DIM // 2)
    pad = lambda v: jnp.zeros((1, LANES), F32).at[0, :IDX_DIM].set(v)
    ki, wi = _project_kiwi(x2, g_mix2, w_kiwi.astype(BF16), pad(g_idx_k), pad(b_idx_k), idx_tabs,
                           (IDX_HEADS ** -0.5) * (IDX_DIM ** -0.5))
    vals_t = _project(x2, g_mix2, w_val_t, BF16, transposed=True)
    gates_ab = _project(x2, g_mix2, w_gate, F32)

    bias4 = _dsa_mask(qi, ki, wi.T, min(DSA_TOPK, seq // 4))
    o_a = _dsa_attention(qk, vals_t, bias4, 0, 1, 0)
    o_b = _moba_attention(qk, vals_t, _block_means(qk, 3), 2, 3, 1)

    w_r = jnp.zeros((d, LANES), F32).at[:, :N_EXPERTS].set(w_router).astype(BF16)
    b_r = jnp.full((1, LANES), NEG_INF, F32).at[0, :N_EXPERTS].set(b_router)
    x1, h2, top_idx, top_gate = _post_attention(
        o_a, o_b, gates_ab, x2, w_br_a.astype(BF16), w_br_b.astype(BF16), w_out.astype(BF16),
        row1(g_ffn), w_r, b_r)

    tile_idx, tile_expert, n_used, n_out_rows = _route(top_idx[:, :EXPERT_TOPK], MOE_TM)
    bias_row = lambda b: b.reshape(N_EXPERTS, 1, -1)
    y = _moe_experts(h2, tile_idx, tile_expert, n_used, n_out_rows,
                     *_split_gate_up(w_gate_up),
                     bias_row(b_gate_up[:, 0::2]), bias_row(b_gate_up[:, 1::2]),
                     w_down, bias_row(b_down))

    return _final(x1, y, top_gate, p2, w_ple_proj.astype(BF16), w_ple_gate.astype(BF16), row1(g_ple),
                  row1(g_final))


def kernel(x, p, g_mix, w_in, g_idx_k, b_idx_k, w_br_a, w_br_b, w_out, g_ffn, w_router, b_router,
           w_gate_up, b_gate_up, w_down, b_down, w_ple_proj, w_ple_gate, g_ple, g_final):
    batch, seq, d = x.shape
    depth = w_in.shape[0]
    assert batch == 1 and depth == 1, "kernel handles the single-sequence, single-layer block"
    assert seq % ATT_T == 0 and seq % PROJ_TM == 0
    out = _layer(x[0], p[0, 0], g_mix[0], w_in[0], g_idx_k[0], b_idx_k[0], w_br_a[0], w_br_b[0], w_out[0],
                 g_ffn[0], w_router[0], b_router[0], w_gate_up[0], b_gate_up[0], w_down[0], b_down[0],
                 w_ple_proj[0], w_ple_gate[0], g_ple[0], g_final)
    return out[None]
```

```python
import functools

import numpy as np
import jax
import jax.numpy as jnp
from jax import lax
from jax.experimental import pallas as pl
from jax.experimental.pallas import tpu as pltpu

F32 = jnp.float32
BF16 = jnp.bfloat16
I32 = jnp.int32
NEG_INF = float("-inf")
LOG2_E = 1.4426950408889634

HEAD_DIM = 128
DSA_HEADS = 8
MOBA_HEADS = 8
ROPE_DIM = HEAD_DIM // 4
ROPE_THETA = 500000.0
IDX_HEADS = 16
IDX_DIM = 64
IDX_ROPE_DIM = IDX_DIM // 4
DSA_TOPK = 256
MOBA_BLOCK = 256
MOBA_TOPK = 3
N_EXPERTS = 32
EXPERT_TOPK = 4
SWIGLU_LIMIT = 7.0
SWIGLU_ALPHA = 1.702
EPS = 1e-6

LANES = 128
VMEM_CAP_BYTES = 60000 * 1024

PROJ_TM = 1024
PROJ_TN = 512
IDX_TQ = 256
IDX_CK = 512
COUNT_ROWS = 32
ATT_T = 512
POST_TM = 256
MOE_TS = 256
MOE_NSUB = 6
MOE_TF = 256
MOE_ROW_UNROLL = 8
FIN_TM = 256

INT_MIN = -2 ** 31
KEY_NEG_INF = 0x807FFFFF - 2 ** 32


def _cparams(sem, vmem_bytes):
    return pltpu.CompilerParams(dimension_semantics=sem,
                                vmem_limit_bytes=int(min(vmem_bytes, VMEM_CAP_BYTES)))


def _resident(shape, index_map):
    return pl.BlockSpec(shape, index_map, pipeline_mode=pl.Buffered(1))


def _sigmoid(x):
    return 1.0 / (1.0 + jnp.exp(-x))


def _rms(xf, g):
    ms = jnp.mean(xf * xf, axis=-1, keepdims=True)
    return xf * lax.rsqrt(ms + EPS) * g


def _dot_nt(a, b):
    return lax.dot_general(a, b, (((1,), (1,)), ((), ())), preferred_element_type=F32)


def _rope_tables(seq, rot_dim, period):
    half = rot_dim // 2
    inv = 1.0 / (ROPE_THETA ** (jnp.arange(half, dtype=F32) / half))
    ang = jnp.arange(seq).astype(F32)[:, None] * inv[None, :]
    cos, sin = jnp.cos(ang), jnp.sin(ang)
    z = lambda n: jnp.zeros((seq, n), F32)
    c = jnp.concatenate([cos, cos, jnp.ones((seq, period - rot_dim), F32)], axis=-1)
    s1 = jnp.concatenate([-sin, z(period - half)], axis=-1)
    s2 = jnp.concatenate([z(half), sin, z(period - rot_dim)], axis=-1)
    rep = LANES // period
    return tuple(jnp.tile(t, (1, rep)) for t in (c, s1, s2))


def _rope(a, c, s1, s2, half):
    return a * c + pltpu.roll(a, LANES - half, 1) * s1 + pltpu.roll(a, half, 1) * s2


def _proj_kernel(x_ref, g_ref, w_ref, *rest, half, transposed):
    if half is None:
        o_ref, h_ref = rest
    else:
        c_ref, s1_ref, s2_ref, o_ref, h_ref = rest

    @pl.when(pl.program_id(1) == 0)
    def _():
        h_ref[...] = _rms(x_ref[...], g_ref[...]).astype(BF16)

    if transposed:
        o_ref[...] = _dot_nt(w_ref[...], h_ref[...]).astype(o_ref.dtype)
        return
    acc = jnp.dot(h_ref[...], w_ref[...], preferred_element_type=F32)
    if half is None:
        o_ref[...] = acc.astype(o_ref.dtype)
    else:
        c, s1, s2 = c_ref[...], s1_ref[...], s2_ref[...]
        for k in range(acc.shape[1] // LANES):
            sl = slice(k * LANES, (k + 1) * LANES)
            o_ref[:, sl] = _rope(acc[:, sl], c, s1, s2, half).astype(o_ref.dtype)


def _project(x2, g, w, out_dtype, tables=None, half=None, transposed=False):
    seq, d = x2.shape
    n = w.shape[0] if transposed else w.shape[1]
    tm, tn = min(PROJ_TM, seq), min(PROJ_TN, n)
    in_specs = [pl.BlockSpec((tm, d), lambda i, j: (i, 0)),
                pl.BlockSpec((1, d), lambda i, j: (0, 0)),
                pl.BlockSpec((tn, d), lambda i, j: (j, 0)) if transposed
                else pl.BlockSpec((d, tn), lambda i, j: (0, j))]
    args = [x2, g, w]
    if tables is not None:
        in_specs += [pl.BlockSpec((tm, LANES), lambda i, j: (i, 0))] * 3
        args += list(tables)
    vmem = 2 * (tm * d * 4 + d * tn * 2 + tm * tn * 4 + 3 * tm * LANES * 4) + tm * d * 2 + 4 * tm * tn * 4
    return pl.pallas_call(
        functools.partial(_proj_kernel, half=half, transposed=transposed),
        out_shape=jax.ShapeDtypeStruct((n, seq) if transposed else (seq, n), out_dtype),
        grid=(seq // tm, n // tn),
        in_specs=in_specs,
        out_specs=pl.BlockSpec((tn, tm), lambda i, j: (j, i)) if transposed
        else pl.BlockSpec((tm, tn), lambda i, j: (i, j)),
        scratch_shapes=[pltpu.VMEM((tm, d), BF16)],
        compiler_params=_cparams(("parallel", "arbitrary"), vmem),
        name="proj_rope" if half is not None else "proj_plain",
    )(*args)


def _proj_kiwi_kernel(x_ref, g_ref, w_ref, lg_ref, lb_ref, c_ref, s1_ref, s2_ref, ki_ref, wi_ref, *, scale):
    h = _rms(x_ref[...], g_ref[...]).astype(BF16)
    acc = jnp.dot(h, w_ref[...], preferred_element_type=F32)
    a = acc[:, :LANES]
    valid = lax.broadcasted_iota(I32, a.shape, 1) < IDX_DIM
    mu = jnp.sum(jnp.where(valid, a, 0.0), axis=-1, keepdims=True) / IDX_DIM
    dlt = jnp.where(valid, a - mu, 0.0)
    var = jnp.sum(dlt * dlt, axis=-1, keepdims=True) / IDX_DIM
    y = dlt * lax.rsqrt(var + EPS) * lg_ref[...] + lb_ref[...]
    ki_ref[...] = _rope(y, c_ref[...], s1_ref[...], s2_ref[...], IDX_ROPE_DIM // 2).astype(BF16)
    wi_ref[...] = acc[:, LANES:LANES + IDX_HEADS] * scale


def _project_kiwi(x2, g, w_kiwi, lg, lb, tables, scale):
    seq, d = x2.shape
    tm = min(PROJ_TM, seq)
    row = lambda i: (i, 0)
    fix = lambda i: (0, 0)
    vmem = 2 * (tm * d * 4 + d * 2 * LANES * 2 + 5 * tm * LANES * 4) + 8 * tm * 2 * LANES * 4 + tm * d * 6
    return pl.pallas_call(
        functools.partial(_proj_kiwi_kernel, scale=scale),
        out_shape=(jax.ShapeDtypeStruct((seq, LANES), BF16), jax.ShapeDtypeStruct((seq, IDX_HEADS), F32)),
        grid=(seq // tm,),
        in_specs=[pl.BlockSpec((tm, d), row), pl.BlockSpec((1, d), fix), pl.BlockSpec((d, 2 * LANES), fix),
                  pl.BlockSpec((1, LANES), fix), pl.BlockSpec((1, LANES), fix),
                  pl.BlockSpec((tm, LANES), row), pl.BlockSpec((tm, LANES), row), pl.BlockSpec((tm, LANES), row)],
        out_specs=(pl.BlockSpec((tm, LANES), row), pl.BlockSpec((tm, IDX_HEADS), row)),
        compiler_params=_cparams(("parallel",), vmem),
        name="proj_kiwi",
    )(x2, g, w_kiwi, lg, lb, *tables)


def _key_to_float(key):
    bits = key ^ ((key >> 31) & 0x7FFFFFFF)
    return lax.bitcast_convert_type(bits, F32)


def _indexer_kernel(qi_ref, ki_ref, wt_ref, out_ref, s_ref, qh_ref, cst_ref, *, n_sel, idx_bits):
    tq, ck = IDX_TQ, IDX_CK
    n_chunks_total = out_ref.shape[1]
    i = pl.program_id(0)
    nc = ((i + 1) * tq + ck - 1) // ck
    qidx = i * tq + lax.broadcasted_iota(I32, (ck, tq), 1)

    def key_idx(c):
        return c * ck + lax.broadcasted_iota(I32, (ck, tq), 0)

    lane = lax.broadcasted_iota(I32, (tq, LANES), 1)
    low = lane < IDX_DIM
    for p in range(IDX_HEADS // 2):
        pair = qi_ref[:, p * LANES:(p + 1) * LANES].astype(F32)
        qh_ref[2 * p] = jnp.where(low, pair, 0.0).astype(BF16)
        qh_ref[2 * p + 1] = jnp.where(low, pltpu.roll(pair, IDX_DIM, 1), 0.0).astype(BF16)

    def score_chunk(c, carry):
        kc = ki_ref[pl.ds(pl.multiple_of(c * ck, ck), ck), :]
        acc = jnp.zeros((ck, tq), F32)
        for h in range(IDX_HEADS):
            acc = acc + jnp.maximum(_dot_nt(kc, qh_ref[h]), 0.0) * wt_ref[h:h + 1, :]
        s_ref[c] = jnp.where(key_idx(c) <= qidx, acc, NEG_INF)
        return carry

    lax.fori_loop(0, nc, score_chunk, 0)

    def count(pred):
        def body(c, acc):
            hit = jnp.where(pred(s_ref[c], lambda: key_idx(c)), 1.0, 0.0)
            return acc + jnp.sum(hit.reshape(ck // COUNT_ROWS, COUNT_ROWS, tq), axis=0)
        acc = lax.fori_loop(0, nc, body, jnp.zeros((COUNT_ROWS, tq), F32))
        return jnp.sum(acc, axis=0, keepdims=True)

    def bit_step(b, carry):
        prefix, cnt_at = carry
        cand = prefix + lax.shift_left(jnp.int32(1), 31 - b)
        cand_f = _key_to_float(cand)
        cnt = count(lambda sc, key: sc >= cand_f)
        take = cnt >= n_sel
        return jnp.where(take, cand, prefix), jnp.where(take, cnt, cnt_at)

    prefix, cnt_at = lax.fori_loop(
        0, 32, bit_step, (jnp.full((1, tq), INT_MIN, I32), jnp.full((1, tq), float(n_sel), F32)))
    tau = jnp.where(prefix < KEY_NEG_INF, NEG_INF, _key_to_float(prefix))

    need = jnp.logical_and(cnt_at > n_sel, tau > NEG_INF)
    any_tie = jnp.max(jnp.where(need, 1.0, 0.0)) > 0.0
    idx_all = jnp.full((1, tq), 2 ** 30, I32)

    @pl.when(any_tie)
    def _():
        rem = n_sel - count(lambda sc, key: sc > tau)
        cut = jnp.zeros((1, tq), I32)
        for b in range(idx_bits - 1, -1, -1):
            cand = cut + (1 << b)
            below = count(lambda sc, key: jnp.logical_and(sc == tau, key() < cand))
            cut = jnp.where(below < rem, cand, cut)
        cst_ref[...] = jnp.where(need, cut, idx_all)

    @pl.when(jnp.logical_not(any_tie))
    def _():
        cst_ref[...] = idx_all

    cut = cst_ref[...]

    def emit(c, carry):
        sc = s_ref[c]
        key = key_idx(c)
        tie = jnp.where(key <= cut, 0.0, NEG_INF)
        b = jnp.where(sc > tau, 0.0, jnp.where(sc == tau, tie, NEG_INF))
        out_ref[0, c] = jnp.where(key <= qidx, b, NEG_INF).astype(BF16)
        return carry

    lax.fori_loop(0, nc, emit, 0)

    def fill(c, carry):
        out_ref[0, c] = jnp.full((ck, tq), NEG_INF, BF16)
        return carry

    lax.fori_loop(nc, n_chunks_total, fill, 0)


def _dsa_mask(qi, ki, wi_t, n_sel):
    seq = qi.shape[0]
    tq, ck = min(IDX_TQ, seq), IDX_CK
    nq, nchunk = seq // tq, seq // ck
    idx_bits = max(1, int(seq - 1).bit_length())
    vmem = (2 * (tq * qi.shape[1] * 2 + IDX_HEADS * tq * 4 + nchunk * tq * ck * 2) + seq * LANES * 2
            + nchunk * tq * ck * 4 + IDX_HEADS * tq * LANES * 2 + 12 * tq * ck * 4)
    return pl.pallas_call(
        functools.partial(_indexer_kernel, n_sel=n_sel, idx_bits=idx_bits),
        out_shape=jax.ShapeDtypeStruct((nq, nchunk, ck, tq), BF16),
        grid=(nq,),
        in_specs=[pl.BlockSpec((tq, qi.shape[1]), lambda i: (i, 0)),
                  _resident((seq, LANES), lambda i: (0, 0)),
                  pl.BlockSpec((IDX_HEADS, tq), lambda i: (0, i))],
        out_specs=pl.BlockSpec((1, nchunk, ck, tq), lambda i: (i, 0, 0, 0)),
        scratch_shapes=[pltpu.VMEM((nchunk, ck, tq), F32),
                        pltpu.VMEM((IDX_HEADS, tq, LANES), BF16),
                        pltpu.VMEM((1, tq), I32)],
        compiler_params=_cparams(("parallel",), vmem),
        name="dsa_indexer",
    )(qi, ki, wi_t)


def _tri_steps(n):
    qs = [i for i in range(n) for _ in range(i + 1)]
    ks = [j for i in range(n) for j in range(i + 1)]
    return jnp.asarray(qs, I32), jnp.asarray(ks, I32)


def _softmax_step(h, s_t, vt_h, m_ref, l_ref, acc_ref):
    m_prev = m_ref[h]
    m_new = jnp.maximum(m_prev, jnp.max(s_t, axis=0, keepdims=True))
    m_safe = jnp.where(m_new == NEG_INF, 0.0, m_new)
    alpha = jnp.exp2(m_prev - m_safe)
    p_t = jnp.exp2(s_t - m_safe)
    l_ref[h] = alpha * l_ref[h] + jnp.sum(p_t, axis=0, keepdims=True)
    acc_ref[h] = alpha * acc_ref[h] + jnp.dot(vt_h, p_t.astype(BF16), preferred_element_type=F32)
    m_ref[h] = m_new


def _attn_init(m_ref, l_ref, acc_ref):
    m_ref[...] = jnp.full(m_ref.shape, NEG_INF, F32)
    l_ref[...] = jnp.zeros(l_ref.shape, F32)
    acc_ref[...] = jnp.zeros(acc_ref.shape, F32)


def _attn_finish(o_ref, l_ref, acc_ref, nh):
    for h in range(nh):
        sl = slice(h * HEAD_DIM, (h + 1) * HEAD_DIM)
        o_ref[:, sl] = (acc_ref[h] / l_ref[h]).T.astype(o_ref.dtype)


def _attn_scratch(nh, t):
    return [pltpu.VMEM((nh, 1, t), F32), pltpu.VMEM((nh, 1, t), F32), pltpu.VMEM((nh, HEAD_DIM, t), F32)]


def _dsa_attn_kernel(qs_ref, ks_ref, q_ref, k_ref, vt_ref, b_ref, o_ref, m_ref, l_ref, acc_ref, *, scale):
    s_id = pl.program_id(0)
    qi, kj = qs_ref[s_id], ks_ref[s_id]
    t = q_ref.shape[0]

    @pl.when(kj == 0)
    def _():
        _attn_init(m_ref, l_ref, acc_ref)

    bias_t = jnp.concatenate([b_ref[u, 0] for u in range(b_ref.shape[0])], axis=1).astype(F32)
    for h in range(DSA_HEADS):
        sl = slice(h * HEAD_DIM, (h + 1) * HEAD_DIM)
        s_t = _dot_nt(k_ref[:, sl], q_ref[:, sl]) * scale + bias_t
        _softmax_step(h, s_t, vt_ref[sl, :], m_ref, l_ref, acc_ref)

    @pl.when(kj == qi)
    def _():
        _attn_finish(o_ref, l_ref, acc_ref, DSA_HEADS)


def _dsa_attention(qk, vals_t, bias4, qcol, kcol, vrow):
    seq, width = qk.shape[0], DSA_HEADS * HEAD_DIM
    t = min(ATT_T, seq)
    nq = seq // t
    qs, ks = _tri_steps(nq)
    sub = t // IDX_TQ
    omap = lambda s, qs, ks: (qs[s], 0)
    qmap = lambda s, qs, ks: (qs[s], qcol)
    kmap = lambda s, qs, ks: (ks[s], kcol)
    vmap = lambda s, qs, ks: (vrow, ks[s])
    vmem = (2 * (3 * t * width * 2 + t * t * 2 + t * width * 2) + t * width * 4 + 28 * t * t * 4)
    return pl.pallas_call(
        functools.partial(_dsa_attn_kernel, scale=HEAD_DIM ** -0.5 * LOG2_E),
        out_shape=jax.ShapeDtypeStruct((seq, width), BF16),
        grid_spec=pltpu.PrefetchScalarGridSpec(
            num_scalar_prefetch=2,
            grid=(int(qs.shape[0]),),
            in_specs=[pl.BlockSpec((t, width), qmap), pl.BlockSpec((t, width), kmap),
                      pl.BlockSpec((width, t), vmap),
                      pl.BlockSpec((sub, 1, IDX_CK, IDX_TQ), lambda s, qs, ks: (qs[s], ks[s], 0, 0))],
            out_specs=pl.BlockSpec((t, width), omap),
            scratch_shapes=_attn_scratch(DSA_HEADS, t)),
        compiler_params=_cparams(("arbitrary",), vmem),
        name="dsa_attention",
    )(qs, ks, qk, qk, vals_t, bias4)


def _kmean_kernel(k_ref, o_ref):
    o_ref[0] = jnp.mean(k_ref[...].astype(F32), axis=0, keepdims=True)


def _block_means(qk, kcol):
    seq, width = qk.shape[0], MOBA_HEADS * HEAD_DIM
    nblk = seq // MOBA_BLOCK
    out = pl.pallas_call(
        _kmean_kernel,
        out_shape=jax.ShapeDtypeStruct((nblk, 1, width), F32),
        grid=(nblk,),
        in_specs=[pl.BlockSpec((MOBA_BLOCK, width), lambda i: (i, kcol))],
        out_specs=pl.BlockSpec((1, 1, width), lambda i: (i, 0, 0)),
        compiler_params=_cparams(("parallel",), 8 * MOBA_BLOCK * width * 4),
        name="moba_block_means",
    )(qk)
    return out.reshape(nblk, width)


def _moba_attn_kernel(qs_ref, ks_ref, q_ref, k_ref, vt_ref, km_ref, o_ref, m_ref, l_ref, acc_ref, sel_ref,
                      *, scale, n_top):
    s_id = pl.program_id(0)
    qi, kj = qs_ref[s_id], ks_ref[s_id]
    t = q_ref.shape[0]
    nblk = km_ref.shape[0]
    per_tile = t // MOBA_BLOCK

    @pl.when(kj == 0)
    def _():
        _attn_init(m_ref, l_ref, acc_ref)
        blk = lax.broadcasted_iota(I32, (nblk, t), 0)
        own = (qi * t + lax.broadcasted_iota(I32, (nblk, t), 1)) // MOBA_BLOCK
        blk_f = blk.astype(F32)
        for h in range(MOBA_HEADS):
            sl = slice(h * HEAD_DIM, (h + 1) * HEAD_DIM)
            g = _dot_nt(km_ref[:, sl].astype(BF16), q_ref[:, sl])
            g = jnp.where(blk < own, g, NEG_INF)
            sel = jnp.full((nblk, t), NEG_INF, F32)
            for _ in range(n_top):
                mx = jnp.max(g, axis=0, keepdims=True)
                is_max = jnp.logical_and(g == mx, mx > NEG_INF)
                first = jnp.min(jnp.where(is_max, blk_f, float(nblk)), axis=0, keepdims=True)
                pick = blk_f == first
                sel = jnp.where(pick, 0.0, sel)
                g = jnp.where(pick, NEG_INF, g)
            sel_ref[h] = sel

    def block_bias(h):
        rows = [jnp.broadcast_to(sel_ref[h, pl.ds(kj * per_tile + b, 1), :], (MOBA_BLOCK, t))
                for b in range(per_tile)]
        return jnp.concatenate(rows, axis=0)

    def run(diag):
        if diag:
            key = lax.broadcasted_iota(I32, (t, t), 0)
            qry = lax.broadcasted_iota(I32, (t, t), 1)
            own_blk = (key // MOBA_BLOCK) == (qry // MOBA_BLOCK)
            causal = jnp.where(key <= qry, 0.0, NEG_INF)
        for h in range(MOBA_HEADS):
            sl = slice(h * HEAD_DIM, (h + 1) * HEAD_DIM)
            bias_t = block_bias(h)
            if diag:
                bias_t = jnp.where(own_blk, causal, bias_t)
            s_t = _dot_nt(k_ref[:, sl], q_ref[:, sl]) * scale + bias_t
            _softmax_step(h, s_t, vt_ref[sl, :], m_ref, l_ref, acc_ref)

    @pl.when(kj < qi)
    def _():
        run(False)

    @pl.when(kj == qi)
    def _():
        run(True)
        _attn_finish(o_ref, l_ref, acc_ref, MOBA_HEADS)


def _moba_attention(qk, vals_t, kmean, qcol, kcol, vrow):
    seq, width = qk.shape[0], MOBA_HEADS * HEAD_DIM
    t = min(ATT_T, seq)
    nq = seq // t
    nblk = kmean.shape[0]
    qs, ks = _tri_steps(nq)
    omap = lambda s, qs, ks: (qs[s], 0)
    qmap = lambda s, qs, ks: (qs[s], qcol)
    kmap = lambda s, qs, ks: (ks[s], kcol)
    vmap = lambda s, qs, ks: (vrow, ks[s])
    vmem = (2 * (4 * t * width * 2) + nblk * width * 4 + t * width * 4 + MOBA_HEADS * nblk * t * 4
            + 28 * t * t * 4)
    return pl.pallas_call(
        functools.partial(_moba_attn_kernel, scale=HEAD_DIM ** -0.5 * LOG2_E, n_top=min(MOBA_TOPK, nblk)),
        out_shape=jax.ShapeDtypeStruct((seq, width), BF16),
        grid_spec=pltpu.PrefetchScalarGridSpec(
            num_scalar_prefetch=2,
            grid=(int(qs.shape[0]),),
            in_specs=[pl.BlockSpec((t, width), qmap), pl.BlockSpec((t, width), kmap),
                      pl.BlockSpec((width, t), vmap),
                      _resident((nblk, width), lambda s, qs, ks: (0, 0))],
            out_specs=pl.BlockSpec((t, width), omap),
            scratch_shapes=_attn_scratch(MOBA_HEADS, t) + [pltpu.VMEM((MOBA_HEADS, nblk, t), F32)]),
        compiler_params=_cparams(("arbitrary",), vmem),
        name="moba_attention",
    )(qs, ks, qk, qk, vals_t, kmean)


def _post_kernel(oa_ref, ob_ref, ga_ref, gb_ref, x_ref, wa_ref, wb_ref, wo_ref, gf_ref, wr_ref, br_ref,
                 x1_ref, h2_ref, ti_ref, tg_ref):
    ta = jnp.dot(oa_ref[...], wa_ref[...], preferred_element_type=F32)
    tb = jnp.dot(ob_ref[...], wb_ref[...], preferred_element_type=F32)
    merged = _sigmoid(ga_ref[...]) * ta + _sigmoid(gb_ref[...]) * tb
    x1 = x_ref[...] + jnp.dot(merged.astype(BF16), wo_ref[...], preferred_element_type=F32)
    x1_ref[...] = x1
    h2 = _rms(x1, gf_ref[...])
    h2_ref[...] = h2
    logits = jnp.dot(h2.astype(BF16), wr_ref[...], preferred_element_type=F32) + br_ref[...]
    lane = lax.broadcasted_iota(I32, logits.shape, 1)
    lane_f = lane.astype(F32)
    idx_out = jnp.zeros(logits.shape, I32)
    val_out = jnp.zeros(logits.shape, F32)
    top = None
    for r in range(EXPERT_TOPK):
        mx = jnp.max(logits, axis=-1, keepdims=True)
        ix = jnp.min(jnp.where(logits == mx, lane_f, float(LANES)), axis=-1, keepdims=True).astype(I32)
        if top is None:
            top = mx
        idx_out = jnp.where(lane == r, ix, idx_out)
        val_out = jnp.where(lane == r, jnp.exp(mx - top), val_out)
        logits = jnp.where(lane == ix, NEG_INF, logits)
    ti_ref[...] = idx_out
    tg_ref[...] = val_out / jnp.sum(val_out, axis=-1, keepdims=True)


def _post_attention(oa, ob, gates_ab, x2, wa, wb, wo, gf, wr, br):
    seq, d = x2.shape
    w = oa.shape[1]
    tm = min(POST_TM, seq)
    row = lambda i: (i, 0)
    fix = lambda i: (0, 0)
    vmem = (2 * (2 * tm * w * 2 + 3 * tm * d * 4 + 2 * tm * d * 4 + 2 * tm * LANES * 4)
            + 2 * w * d * 2 + d * d * 2 + d * LANES * 2 + 8 * tm * d * 4)
    return pl.pallas_call(
        _post_kernel,
        out_shape=(jax.ShapeDtypeStruct((seq, d), F32), jax.ShapeDtypeStruct((seq, d), F32),
                   jax.ShapeDtypeStruct((seq, LANES), I32), jax.ShapeDtypeStruct((seq, LANES), F32)),
        grid=(seq // tm,),
        in_specs=[pl.BlockSpec((tm, w), row), pl.BlockSpec((tm, w), row),
                  pl.BlockSpec((tm, d), row), pl.BlockSpec((tm, d), lambda i: (i, 1)), pl.BlockSpec((tm, d), row),
                  _resident((w, d), fix), _resident((w, d), fix), _resident((d, d), fix),
                  _resident((1, d), fix), _resident((d, LANES), fix), _resident((1, LANES), fix)],
        out_specs=(pl.BlockSpec((tm, d), row), pl.BlockSpec((tm, d), row),
                   pl.BlockSpec((tm, LANES), row), pl.BlockSpec((tm, LANES), row)),
        compiler_params=_cparams(("parallel",), vmem),
        name="merge_outproj_router",
    )(oa, ob, gates_ab, gates_ab, x2, wa, wb, wo, gf, wr, br)


def _split_even_odd(w):
    grp = 2 * LANES
    r = lax.broadcasted_iota(I32, (grp, grp), 0)
    c = lax.broadcasted_iota(I32, (grp, grp), 1)
    src = jnp.where(c < LANES, 2 * c, 2 * (c - LANES) + 1)
    sel = jnp.where(r == src, 1.0, 0.0).astype(BF16)
    parts = [jnp.dot(w[:, k * grp:(k + 1) * grp], sel, preferred_element_type=F32).astype(BF16)
             for k in range(w.shape[1] // grp)]
    even = jnp.concatenate([p[:, :LANES] for p in parts], axis=1)
    odd = jnp.concatenate([p[:, LANES:] for p in parts], axis=1)
    return even, odd


def _moe_kernel(ge_ref, nt_ref, ng_ref, tok_hbm, dst_hbm, h_hbm, wgu_ref, bg_ref, bl_ref, wd_ref, bd_ref,
                y_hbm, tok_ref, dst_ref, stage_ref, xall_ref, acc_ref, isem, gsem, ssem):
    g, j = pl.program_id(0), pl.program_id(1)
    n_grp_max, nf = pl.num_programs(0), pl.num_programs(1)
    ng = ng_ref[0]
    ns, ts = acc_ref.shape[0], tok_ref.shape[2]
    sub = stage_ref.shape[2]
    d = stage_ref.shape[3]
    cur = lax.rem(g, 2)
    nxt = 1 - cur
    nt_cur = nt_ref[g]
    nt_next = nt_ref[jnp.minimum(g + 1, n_grp_max - 1)]

    def on_slot(dyn_slot, fn):
        for s in range(2):
            pl.when(dyn_slot == s)(functools.partial(fn, s))

    def idx_copies(grp, s):
        return (pltpu.make_async_copy(tok_hbm.at[grp], tok_ref.at[s], isem.at[s]),
                pltpu.make_async_copy(dst_hbm.at[grp], dst_ref.at[s], isem.at[s]))

    def idx_start(grp, s):
        for cp in idx_copies(grp, s):
            cp.start()

    def idx_wait(s):
        for cp in idx_copies(0, s):
            cp.wait()

    def row_loop(issue):
        def body(i, carry):
            for k in range(sub):
                issue(i, k)
            return carry
        lax.fori_loop(0, ts // sub, body, 0)

    def gather_start(s, r):
        st = r % 2

        def issue(i, k):
            tok = tok_ref[s, r, i * sub + k]
            pltpu.make_async_copy(h_hbm.at[pl.ds(tok, 1), :], stage_ref.at[st, i, pl.ds(k, 1), :],
                                  gsem.at[st]).start()
        row_loop(issue)

    def gather_finish(r, xslot):
        st = r % 2
        pltpu.make_async_copy(stage_ref.at[st], stage_ref.at[st], gsem.at[st]).wait()
        xall_ref[xslot, r] = stage_ref[st].reshape(ts, d).astype(BF16)

    def scatter_start(s, r):
        def issue(i, k):
            dst = dst_ref[s, r, i * sub + k]
            pltpu.make_async_copy(acc_ref.at[r, i, pl.ds(k, 1), :], y_hbm.at[pl.ds(dst, 1), :], ssem.at[0]).start()
        row_loop(issue)

    def scatter_wait(n_sub):
        for r in range(ns):
            @pl.when(r < n_sub)
            def _():
                pltpu.make_async_copy(acc_ref.at[0], acc_ref.at[0], ssem.at[0]).wait()

    @pl.when(jnp.logical_and(g == 0, j == 0))
    def _():
        idx_start(0, 0)
        acc_ref[...] = jnp.zeros(acc_ref.shape, F32)
        spill0 = y_hbm.shape[0] - ns * ts
        for r in range(ns):
            def fill(i, carry):
                row = pl.multiple_of(spill0 + r * ts + i * sub, sub)
                pltpu.make_async_copy(acc_ref.at[r, i], y_hbm.at[pl.ds(row, sub), :], ssem.at[0]).start()
                return carry
            lax.fori_loop(0, ts // sub, fill, 0)
        scatter_wait(ns)
        idx_wait(0)
        for r in range(ns):
            @pl.when(r < nt_ref[0])
            def _():
                gather_start(0, r)
                gather_finish(r, 0)

        @pl.when(1 < ng)
        def _():
            idx_start(1, 1)

    @pl.when(jnp.logical_and(j == 0, jnp.logical_and(g >= 1, g <= ng)))
    def _():
        scatter_wait(nt_ref[jnp.maximum(g - 1, 0)])

    @pl.when(jnp.logical_and(j == 0, g < ng))
    def _():
        for r in range(ns):
            @pl.when(r < nt_cur)
            def _():
                acc_ref[r] = jnp.broadcast_to(bd_ref[0], (ts // sub, sub, d))

    @pl.when(g + 1 < ng)
    def _():
        def prefetch(s):
            @pl.when(j == 0)
            def _():
                idx_wait(s)
                gather_start(s, 0)
            for step in range(1, ns + 1):
                @pl.when(jnp.logical_and(j == step, step - 1 < nt_next))
                def _():
                    gather_finish(step - 1, nxt)
                if step < ns:
                    @pl.when(jnp.logical_and(j == step, step < nt_next))
                    def _():
                        gather_start(s, step)
        on_slot(nxt, prefetch)

    @pl.when(g < ng)
    def _():
        wg, wl = _split_even_odd(wgu_ref[0].astype(BF16))
        wd = wd_ref[0].astype(BF16)
        for r in range(ns):
            @pl.when(r < nt_cur)
            def _():
                x = xall_ref[cur, r]
                gate = jnp.dot(x, wg, preferred_element_type=F32) + bg_ref[0]
                lin = jnp.dot(x, wl, preferred_element_type=F32) + bl_ref[0]
                gate = jnp.minimum(gate, SWIGLU_LIMIT)
                lin = jnp.clip(lin, -SWIGLU_LIMIT, SWIGLU_LIMIT)
                hid = (lin + 1.0) * (gate * _sigmoid(gate * SWIGLU_ALPHA))
                y = jnp.dot(hid.astype(BF16), wd, preferred_element_type=F32)
                acc_ref[r] += y.reshape(ts // sub, sub, d)

    @pl.when(jnp.logical_and(j == nf - 1, g < ng))
    def _():
        def finish(s):
            for r in range(ns):
                @pl.when(r < nt_cur)
                def _():
                    scatter_start(s, r)

            @pl.when(g + 2 < ng)
            def _():
                idx_start(g + 2, s)
        on_slot(cur, finish)

        @pl.when(g == n_grp_max - 1)
        def _():
            scatter_wait(nt_cur)


def _moe_experts(h2, route, w_gate_up, bg, bl, w_down, bd):
    grp_tok, grp_dst, grp_expert, grp_nt, n_grp, n_out_rows = route
    d = h2.shape[1]
    f = w_down.shape[1]
    ns, ts = MOE_NSUB, grp_tok.shape[2]
    n_grp_max = grp_expert.shape[0]
    tf = min(MOE_TF, f)
    nf = f // tf
    assert nf >= ns + 1, "one sub-tile of the next group is gathered per hidden-tile step"

    def grp(g, ng):
        return jnp.minimum(g, ng[0] - 1)

    def ftile(g, j, ng):
        return jnp.where(g < ng[0], j, nf - 1)

    w_map = lambda g, j, ge, nt, ng: (ge[grp(g, ng)], 0, ftile(g, j, ng))
    vmem = (2 * (d * 2 * tf * 4 + tf * d * 4 + 2 * tf * 4 + d * 4) + 2 * ts * d * 4 + 2 * ns * ts * d * 2
            + ns * ts * d * 4 + d * 2 * tf * (2 + 4 + 2) + tf * d * 2 + 3 * ts * d * 4)
    any_spec = pl.BlockSpec(memory_space=pl.ANY)
    return pl.pallas_call(
        _moe_kernel,
        out_shape=jax.ShapeDtypeStruct((n_out_rows, d), F32),
        grid_spec=pltpu.PrefetchScalarGridSpec(
            num_scalar_prefetch=3,
            grid=(n_grp_max, nf),
            in_specs=[any_spec, any_spec, any_spec,
                      pl.BlockSpec((1, d, 2 * tf), w_map),
                      pl.BlockSpec((1, 1, tf), w_map),
                      pl.BlockSpec((1, 1, tf), w_map),
                      pl.BlockSpec((1, tf, d), lambda g, j, ge, nt, ng: (ge[grp(g, ng)], ftile(g, j, ng), 0)),
                      pl.BlockSpec((1, 1, d), lambda g, j, ge, nt, ng: (ge[grp(g, ng)], 0, 0))],
            out_specs=any_spec,
            scratch_shapes=[pltpu.SMEM((2,) + grp_tok.shape[1:], I32), pltpu.SMEM((2,) + grp_tok.shape[1:], I32),
                            pltpu.VMEM((2, ts // MOE_ROW_UNROLL, MOE_ROW_UNROLL, d), F32),
                            pltpu.VMEM((2, ns, ts, d), BF16),
                            pltpu.VMEM((ns, ts // MOE_ROW_UNROLL, MOE_ROW_UNROLL, d), F32),
                            pltpu.SemaphoreType.DMA((2,)), pltpu.SemaphoreType.DMA((2,)),
                            pltpu.SemaphoreType.DMA((1,))]),
        compiler_params=_cparams(("arbitrary", "arbitrary"), vmem),
        name="moe_experts",
    )(grp_expert, grp_nt, n_grp, grp_tok, grp_dst, h2, w_gate_up, bg, bl, w_down, bd)


def _route(top_idx):
    ts, ns = MOE_TS, MOE_NSUB
    n_tok = top_idx.shape[0]
    n_slots = n_tok * EXPERT_TOPK
    e_flat = top_idx.reshape(-1)
    order = jnp.argsort(e_flat).astype(I32)
    counts = jnp.bincount(e_flat, length=N_EXPERTS).astype(I32)
    start = jnp.cumsum(counts) - counts
    nsub = (counts + ts - 1) // ts
    sub_end = jnp.cumsum(nsub)
    sub_start = sub_end - nsub
    n_sub_max = n_slots // ts + N_EXPERTS
    last = N_EXPERTS - 1
    sub_expert = jnp.minimum(jnp.searchsorted(sub_end, jnp.arange(n_sub_max), side="right"), last)
    rows = jnp.arange(n_sub_max * ts, dtype=I32)
    e_row = jnp.repeat(sub_expert, ts)
    within = rows - ts * sub_start[e_row]
    valid = within < counts[e_row]
    slot = order[jnp.clip(start[e_row] + within, 0, n_slots - 1)]
    spare = n_slots + rows % (ts * ns)
    row_tok = jnp.where(valid, slot // EXPERT_TOPK, 0)
    row_dst = jnp.where(valid, (slot % EXPERT_TOPK) * n_tok + slot // EXPERT_TOPK, spare)
    ngrp = (nsub + ns - 1) // ns
    grp_end = jnp.cumsum(ngrp)
    grp_start = grp_end - ngrp
    n_grp = grp_end[-1]
    gidx = jnp.arange(N_EXPERTS + n_slots // (ts * ns), dtype=I32)
    used = gidx < n_grp
    grp_expert = jnp.minimum(jnp.searchsorted(grp_end, gidx, side="right"), last).astype(I32)
    k = gidx - grp_start[grp_expert]
    grp_sub0 = jnp.where(used, sub_start[grp_expert] + k * ns, 0)
    grp_nt = jnp.where(used, jnp.clip(nsub[grp_expert] - k * ns, 0, ns), 0).astype(I32)
    ns_pad = -(-ns // 8) * 8
    sub_ids = jnp.minimum(grp_sub0[:, None] + jnp.arange(ns_pad)[None, :], n_sub_max - 1)
    grp_tok = row_tok.reshape(n_sub_max, ts).astype(I32)[sub_ids]
    grp_dst = row_dst.reshape(n_sub_max, ts).astype(I32)[sub_ids]
    return grp_tok, grp_dst, grp_expert, grp_nt, n_grp.astype(I32).reshape(1), n_slots + ts * ns


def _final_kernel(x1_ref, *rest):
    y_refs = rest[:EXPERT_TOPK]
    tg_ref, p_ref, wpp_ref, wpg_ref, gp_ref, gfin_ref, o_ref = rest[EXPERT_TOPK:]
    x2 = x1_ref[...]
    gates = tg_ref[...]
    for r in range(EXPERT_TOPK):
        x2 = x2 + y_refs[r][...] * gates[:, r:r + 1]
    ple = jnp.dot(p_ref[...].astype(BF16), wpp_ref[...], preferred_element_type=F32)
    gate = _sigmoid(jnp.dot(x2.astype(BF16), wpg_ref[...], preferred_element_type=F32))
    x3 = x2 + _rms(gate * ple, gp_ref[...])
    o_ref[...] = _rms(x3, gfin_ref[...])


def _final(x1, y, gates, p2, wpp, wpg, gp, gfin):
    seq, d = x1.shape
    pd = p2.shape[1]
    tm = min(FIN_TM, seq)
    row = lambda i: (i, 0)
    fix = lambda i: (0, 0)
    vmem = (2 * (2 * tm * d * 4 + EXPERT_TOPK * tm * d * 4 + tm * LANES * 4 + tm * pd * 4)
            + pd * d * 2 + d * d * 2 + 8 * tm * d * 4)
    return pl.pallas_call(
        _final_kernel,
        out_shape=jax.ShapeDtypeStruct((seq, d), F32),
        grid=(seq // tm,),
        in_specs=[pl.BlockSpec((tm, d), row),
                  *[pl.BlockSpec((tm, d), functools.partial(lambda i, r: (r * (seq // tm) + i, 0), r=r))
                    for r in range(EXPERT_TOPK)],
                  pl.BlockSpec((tm, LANES), row), pl.BlockSpec((tm, pd), row),
                  _resident((pd, d), fix), _resident((d, d), fix),
                  _resident((1, d), fix), _resident((1, d), fix)],
        out_specs=pl.BlockSpec((tm, d), row),
        compiler_params=_cparams(("parallel",), vmem),
        name="combine_ple_norm",
    )(x1, *([y] * EXPERT_TOPK), gates, p2, wpp, wpg, gp, gfin)


def _layer(x2, p2, g_mix, w_in, g_idx_k, b_idx_k, w_br_a, w_br_b, w_out, g_ffn, w_router, b_router,
           w_gate_up, b_gate_up, w_down, b_down, w_ple_proj, w_ple_gate, g_ple, g_final):
    seq, d = x2.shape
    dsa_w, moba_w = DSA_HEADS * HEAD_DIM, MOBA_HEADS * HEAD_DIM
    sizes = (dsa_w, dsa_w, dsa_w, IDX_HEADS * IDX_DIM, IDX_DIM, IDX_HEADS, moba_w, moba_w, moba_w, d, d)
    off = np.concatenate([[0], np.cumsum(sizes)])
    col = lambda k: w_in[:, off[k]:off[k + 1]]
    row1 = lambda v: v.reshape(1, -1)
    g_mix2 = row1(g_mix)

    w_rope = jnp.concatenate([col(0), col(1), col(6), col(7)], axis=1).astype(BF16)
    w_qi = col(3).astype(BF16)
    w_kiwi = jnp.zeros((d, 2 * LANES), F32).at[:, :IDX_DIM].set(col(4)).at[:, LANES:LANES + IDX_HEADS].set(col(5))
    w_val_t = jnp.concatenate([col(2), col(8)], axis=1).T.astype(BF16)
    w_gate = jnp.concatenate([col(9), col(10)], axis=1).astype(BF16)

    rope_tabs = _rope_tables(seq, ROPE_DIM, HEAD_DIM)
    idx_tabs = _rope_tables(seq, IDX_ROPE_DIM, IDX_DIM)
    qk = _project(x2, g_mix2, w_rope, BF16, rope_tabs, ROPE_DIM // 2)
    qi = _project(x2, g_mix2, w_qi, BF16, idx_tabs, IDX_ROPE_DIM // 2)
    pad = lambda v: jnp.zeros((1, LANES), F32).at[0, :IDX_DIM].set(v)
    ki, wi = _project_kiwi(x2, g_mix2, w_kiwi.astype(BF16), pad(g_idx_k), pad(b_idx_k), idx_tabs,
                           (IDX_HEADS ** -0.5) * (IDX_DIM ** -0.5))
    vals_t = _project(x2, g_mix2, w_val_t, BF16, transposed=True)
    gates_ab = _project(x2, g_mix2, w_gate, F32)

    bias4 = _dsa_mask(qi, ki, wi.T, min(DSA_TOPK, seq // 4))
    o_a = _dsa_attention(qk, vals_t, bias4, 0, 1, 0)
    o_b = _moba_attention(qk, vals_t, _block_means(qk, 3), 2, 3, 1)

    w_r = jnp.zeros((d, LANES), F32).at[:, :N_EXPERTS].set(w_router).astype(BF16)
    b_r = jnp.full((1, LANES), NEG_INF, F32).at[0, :N_EXPERTS].set(b_router)
    x1, h2, top_idx, top_gate = _post_attention(
        o_a, o_b, gates_ab, x2, w_br_a.astype(BF16), w_br_b.astype(BF16), w_out.astype(BF16),
        row1(g_ffn), w_r, b_r)

    bias_row = lambda b: b.reshape(N_EXPERTS, 1, -1)
    y = _moe_experts(h2, _route(top_idx[:, :EXPERT_TOPK]), w_gate_up,
                     bias_row(b_gate_up[:, 0::2]), bias_row(b_gate_up[:, 1::2]), w_down, bias_row(b_down))

    return _final(x1, y, top_gate, p2, w_ple_proj.astype(BF16), w_ple_gate.astype(BF16), row1(g_ple),
                  row1(g_final))


def kernel(x, p, g_mix, w_in, g_idx_k, b_idx_k, w_br_a, w_br_b, w_out, g_ffn, w_router, b_router,
           w_gate_up, b_gate_up, w_down, b_down, w_ple_proj, w_ple_gate, g_ple, g_final):
    batch, seq, d = x.shape
    depth = w_in.shape[0]
    assert batch == 1 and depth == 1, "kernel handles the single-sequence, single-layer block"
    assert seq % ATT_T == 0 and seq % PROJ_TM == 0
    out = _layer(x[0], p[0, 0], g_mix[0], w_in[0], g_idx_k[0], b_idx_k[0], w_br_a[0], w_br_b[0], w_out[0],
                 g_ffn[0], w_router[0], b_router[0], w_gate_up[0], b_gate_up[0], w_down[0], b_down[0],
                 w_ple_proj[0], w_ple_gate[0], g_ple[0], g_final)
    return out[None]
```

```python
import functools

import numpy as np
import jax
import jax.numpy as jnp
from jax import lax
from jax.experimental import pallas as pl
from jax.experimental.pallas import tpu as pltpu

F32 = jnp.float32
BF16 = jnp.bfloat16
I32 = jnp.int32
NEG_INF = float("-inf")
LOG2_E = 1.4426950408889634

HEAD_DIM = 128
DSA_HEADS = 8
MOBA_HEADS = 8
ROPE_DIM = HEAD_DIM // 4
ROPE_THETA = 500000.0
IDX_HEADS = 16
IDX_DIM = 64
IDX_ROPE_DIM = IDX_DIM // 4
DSA_TOPK = 256
MOBA_BLOCK = 256
MOBA_TOPK = 3
N_EXPERTS = 32
EXPERT_TOPK = 4
SWIGLU_LIMIT = 7.0
SWIGLU_ALPHA = 1.702
EPS = 1e-6

LANES = 128
VMEM_CAP_BYTES = 60000 * 1024

PROJ_TM = 1024
PROJ_TN = 512
IDX_TQ = 256
IDX_CK = 512
COUNT_ROWS = 32
ATT_T = 512
POST_TM = 256
MOE_TS = 256
MOE_NSUB = 6
MOE_TF = 256
MOE_ROW_UNROLL = 8
FIN_TM = 256

INT_MIN = -2 ** 31
KEY_NEG_INF = 0x807FFFFF - 2 ** 32


def _cparams(sem, vmem_bytes):
    return pltpu.CompilerParams(dimension_semantics=sem,
                                vmem_limit_bytes=int(min(vmem_bytes, VMEM_CAP_BYTES)))


def _resident(shape, index_map):
    return pl.BlockSpec(shape, index_map, pipeline_mode=pl.Buffered(1))


def _sigmoid(x):
    return 1.0 / (1.0 + jnp.exp(-x))


def _rms(xf, g):
    ms = jnp.mean(xf * xf, axis=-1, keepdims=True)
    return xf * lax.rsqrt(ms + EPS) * g


def _dot_nt(a, b):
    return lax.dot_general(a, b, (((1,), (1,)), ((), ())), preferred_element_type=F32)


def _rope_tables(seq, rot_dim, period):
    half = rot_dim // 2
    inv = 1.0 / (ROPE_THETA ** (jnp.arange(half, dtype=F32) / half))
    ang = jnp.arange(seq).astype(F32)[:, None] * inv[None, :]
    cos, sin = jnp.cos(ang), jnp.sin(ang)
    z = lambda n: jnp.zeros((seq, n), F32)
    c = jnp.concatenate([cos, cos, jnp.ones((seq, period - rot_dim), F32)], axis=-1)
    s1 = jnp.concatenate([-sin, z(period - half)], axis=-1)
    s2 = jnp.concatenate([z(half), sin, z(period - rot_dim)], axis=-1)
    rep = LANES // period
    return tuple(jnp.tile(t, (1, rep)) for t in (c, s1, s2))


def _rope(a, c, s1, s2, half):
    return a * c + pltpu.roll(a, LANES - half, 1) * s1 + pltpu.roll(a, half, 1) * s2


def _proj_kernel(x_ref, g_ref, w_ref, *rest, half, transposed):
    if half is None:
        o_ref, h_ref = rest
    else:
        c_ref, s1_ref, s2_ref, o_ref, h_ref = rest

    @pl.when(pl.program_id(1) == 0)
    def _():
        h_ref[...] = _rms(x_ref[...], g_ref[...]).astype(BF16)

    if transposed:
        o_ref[...] = _dot_nt(w_ref[...], h_ref[...]).astype(o_ref.dtype)
        return
    acc = jnp.dot(h_ref[...], w_ref[...], preferred_element_type=F32)
    if half is None:
        o_ref[...] = acc.astype(o_ref.dtype)
    else:
        c, s1, s2 = c_ref[...], s1_ref[...], s2_ref[...]
        for k in range(acc.shape[1] // LANES):
            sl = slice(k * LANES, (k + 1) * LANES)
            o_ref[:, sl] = _rope(acc[:, sl], c, s1, s2, half).astype(o_ref.dtype)


def _project(x2, g, w, out_dtype, tables=None, half=None, transposed=False):
    seq, d = x2.shape
    n = w.shape[0] if transposed else w.shape[1]
    tm, tn = min(PROJ_TM, seq), min(PROJ_TN, n)
    in_specs = [pl.BlockSpec((tm, d), lambda i, j: (i, 0)),
                pl.BlockSpec((1, d), lambda i, j: (0, 0)),
                pl.BlockSpec((tn, d), lambda i, j: (j, 0)) if transposed
                else pl.BlockSpec((d, tn), lambda i, j: (0, j))]
    args = [x2, g, w]
    if tables is not None:
        in_specs += [pl.BlockSpec((tm, LANES), lambda i, j: (i, 0))] * 3
        args += list(tables)
    vmem = 2 * (tm * d * 4 + d * tn * 2 + tm * tn * 4 + 3 * tm * LANES * 4) + tm * d * 2 + 4 * tm * tn * 4
    return pl.pallas_call(
        functools.partial(_proj_kernel, half=half, transposed=transposed),
        out_shape=jax.ShapeDtypeStruct((n, seq) if transposed else (seq, n), out_dtype),
        grid=(seq // tm, n // tn),
        in_specs=in_specs,
        out_specs=pl.BlockSpec((tn, tm), lambda i, j: (j, i)) if transposed
        else pl.BlockSpec((tm, tn), lambda i, j: (i, j)),
        scratch_shapes=[pltpu.VMEM((tm, d), BF16)],
        compiler_params=_cparams(("parallel", "arbitrary"), vmem),
        name="proj_rope" if half is not None else "proj_plain",
    )(*args)


def _proj_kiwi_kernel(x_ref, g_ref, w_ref, lg_ref, lb_ref, c_ref, s1_ref, s2_ref, ki_ref, wi_ref, *, scale):
    h = _rms(x_ref[...], g_ref[...]).astype(BF16)
    acc = jnp.dot(h, w_ref[...], preferred_element_type=F32)
    a = acc[:, :LANES]
    valid = lax.broadcasted_iota(I32, a.shape, 1) < IDX_DIM
    mu = jnp.sum(jnp.where(valid, a, 0.0), axis=-1, keepdims=True) / IDX_DIM
    dlt = jnp.where(valid, a - mu, 0.0)
    var = jnp.sum(dlt * dlt, axis=-1, keepdims=True) / IDX_DIM
    y = dlt * lax.rsqrt(var + EPS) * lg_ref[...] + lb_ref[...]
    ki_ref[...] = _rope(y, c_ref[...], s1_ref[...], s2_ref[...], IDX_ROPE_DIM // 2).astype(BF16)
    wi_ref[...] = acc[:, LANES:LANES + IDX_HEADS] * scale


def _project_kiwi(x2, g, w_kiwi, lg, lb, tables, scale):
    seq, d = x2.shape
    tm = min(PROJ_TM, seq)
    row = lambda i: (i, 0)
    fix = lambda i: (0, 0)
    vmem = 2 * (tm * d * 4 + d * 2 * LANES * 2 + 5 * tm * LANES * 4) + 8 * tm * 2 * LANES * 4 + tm * d * 6
    return pl.pallas_call(
        functools.partial(_proj_kiwi_kernel, scale=scale),
        out_shape=(jax.ShapeDtypeStruct((seq, LANES), BF16), jax.ShapeDtypeStruct((seq, IDX_HEADS), F32)),
        grid=(seq // tm,),
        in_specs=[pl.BlockSpec((tm, d), row), pl.BlockSpec((1, d), fix), pl.BlockSpec((d, 2 * LANES), fix),
                  pl.BlockSpec((1, LANES), fix), pl.BlockSpec((1, LANES), fix),
                  pl.BlockSpec((tm, LANES), row), pl.BlockSpec((tm, LANES), row), pl.BlockSpec((tm, LANES), row)],
        out_specs=(pl.BlockSpec((tm, LANES), row), pl.BlockSpec((tm, IDX_HEADS), row)),
        compiler_params=_cparams(("parallel",), vmem),
        name="proj_kiwi",
    )(x2, g, w_kiwi, lg, lb, *tables)


def _key_to_float(key):
    bits = key ^ ((key >> 31) & 0x7FFFFFFF)
    return lax.bitcast_convert_type(bits, F32)


def _indexer_kernel(qi_ref, ki_ref, wt_ref, out_ref, s_ref, qh_ref, cst_ref, *, n_sel, idx_bits):
    tq, ck = IDX_TQ, IDX_CK
    n_chunks_total = out_ref.shape[1]
    i = pl.program_id(0)
    nc = ((i + 1) * tq + ck - 1) // ck
    qidx = i * tq + lax.broadcasted_iota(I32, (ck, tq), 1)

    def key_idx(c):
        return c * ck + lax.broadcasted_iota(I32, (ck, tq), 0)

    lane = lax.broadcasted_iota(I32, (tq, LANES), 1)
    low = lane < IDX_DIM
    for p in range(IDX_HEADS // 2):
        pair = qi_ref[:, p * LANES:(p + 1) * LANES].astype(F32)
        qh_ref[2 * p] = jnp.where(low, pair, 0.0).astype(BF16)
        qh_ref[2 * p + 1] = jnp.where(low, pltpu.roll(pair, IDX_DIM, 1), 0.0).astype(BF16)

    def score_chunk(c, carry):
        kc = ki_ref[pl.ds(pl.multiple_of(c * ck, ck), ck), :]
        acc = jnp.zeros((ck, tq), F32)
        for h in range(IDX_HEADS):
            acc = acc + jnp.maximum(_dot_nt(kc, qh_ref[h]), 0.0) * wt_ref[h:h + 1, :]
        s_ref[c] = jnp.where(key_idx(c) <= qidx, acc, NEG_INF)
        return carry

    lax.fori_loop(0, nc, score_chunk, 0)

    def count(pred):
        def body(c, acc):
            hit = jnp.where(pred(s_ref[c], lambda: key_idx(c)), 1.0, 0.0)
            return acc + jnp.sum(hit.reshape(ck // COUNT_ROWS, COUNT_ROWS, tq), axis=0)
        acc = lax.fori_loop(0, nc, body, jnp.zeros((COUNT_ROWS, tq), F32))
        return jnp.sum(acc, axis=0, keepdims=True)

    def bit_step(b, carry):
        prefix, cnt_at = carry
        cand = prefix + lax.shift_left(jnp.int32(1), 31 - b)
        cand_f = _key_to_float(cand)
        cnt = count(lambda sc, key: sc >= cand_f)
        take = cnt >= n_sel
        return jnp.where(take, cand, prefix), jnp.where(take, cnt, cnt_at)

    prefix, cnt_at = lax.fori_loop(
        0, 32, bit_step, (jnp.full((1, tq), INT_MIN, I32), jnp.full((1, tq), float(n_sel), F32)))
    tau = jnp.where(prefix < KEY_NEG_INF, NEG_INF, _key_to_float(prefix))

    need = jnp.logical_and(cnt_at > n_sel, tau > NEG_INF)
    any_tie = jnp.max(jnp.where(need, 1.0, 0.0)) > 0.0
    idx_all = jnp.full((1, tq), 2 ** 30, I32)

    @pl.when(any_tie)
    def _():
        rem = n_sel - count(lambda sc, key: sc > tau)
        cut = jnp.zeros((1, tq), I32)
        for b in range(idx_bits - 1, -1, -1):
            cand = cut + (1 << b)
            below = count(lambda sc, key: jnp.logical_and(sc == tau, key() < cand))
            cut = jnp.where(below < rem, cand, cut)
        cst_ref[...] = jnp.where(need, cut, idx_all)

    @pl.when(jnp.logical_not(any_tie))
    def _():
        cst_ref[...] = idx_all

    cut = cst_ref[...]

    def emit(c, carry):
        sc = s_ref[c]
        key = key_idx(c)
        tie = jnp.where(key <= cut, 0.0, NEG_INF)
        b = jnp.where(sc > tau, 0.0, jnp.where(sc == tau, tie, NEG_INF))
        out_ref[0, c] = jnp.where(key <= qidx, b, NEG_INF).astype(BF16)
        return carry

    lax.fori_loop(0, nc, emit, 0)

    def fill(c, carry):
        out_ref[0, c] = jnp.full((ck, tq), NEG_INF, BF16)
        return carry

    lax.fori_loop(nc, n_chunks_total, fill, 0)


def _dsa_mask(qi, ki, wi_t, n_sel):
    seq = qi.shape[0]
    tq, ck = min(IDX_TQ, seq), IDX_CK
    nq, nchunk = seq // tq, seq // ck
    idx_bits = max(1, int(seq - 1).bit_length())
    vmem = (2 * (tq * qi.shape[1] * 2 + IDX_HEADS * tq * 4 + nchunk * tq * ck * 2) + seq * LANES * 2
            + nchunk * tq * ck * 4 + IDX_HEADS * tq * LANES * 2 + 12 * tq * ck * 4)
    return pl.pallas_call(
        functools.partial(_indexer_kernel, n_sel=n_sel, idx_bits=idx_bits),
        out_shape=jax.ShapeDtypeStruct((nq, nchunk, ck, tq), BF16),
        grid=(nq,),
        in_specs=[pl.BlockSpec((tq, qi.shape[1]), lambda i: (i, 0)),
                  _resident((seq, LANES), lambda i: (0, 0)),
                  pl.BlockSpec((IDX_HEADS, tq), lambda i: (0, i))],
        out_specs=pl.BlockSpec((1, nchunk, ck, tq), lambda i: (i, 0, 0, 0)),
        scratch_shapes=[pltpu.VMEM((nchunk, ck, tq), F32),
                        pltpu.VMEM((IDX_HEADS, tq, LANES), BF16),
                        pltpu.VMEM((1, tq), I32)],
        compiler_params=_cparams(("parallel",), vmem),
        name="dsa_indexer",
    )(qi, ki, wi_t)


def _tri_steps(n):
    qs = [i for i in range(n) for _ in range(i + 1)]
    ks = [j for i in range(n) for j in range(i + 1)]
    return jnp.asarray(qs, I32), jnp.asarray(ks, I32)


def _softmax_step(h, s_t, vt_h, m_ref, l_ref, acc_ref):
    m_prev = m_ref[h]
    m_new = jnp.maximum(m_prev, jnp.max(s_t, axis=0, keepdims=True))
    m_safe = jnp.where(m_new == NEG_INF, 0.0, m_new)
    alpha = jnp.exp2(m_prev - m_safe)
    p_t = jnp.exp2(s_t - m_safe)
    l_ref[h] = alpha * l_ref[h] + jnp.sum(p_t, axis=0, keepdims=True)
    acc_ref[h] = alpha * acc_ref[h] + jnp.dot(vt_h, p_t.astype(BF16), preferred_element_type=F32)
    m_ref[h] = m_new


def _attn_init(m_ref, l_ref, acc_ref):
    m_ref[...] = jnp.full(m_ref.shape, NEG_INF, F32)
    l_ref[...] = jnp.zeros(l_ref.shape, F32)
    acc_ref[...] = jnp.zeros(acc_ref.shape, F32)


def _attn_finish(o_ref, l_ref, acc_ref, nh):
    for h in range(nh):
        sl = slice(h * HEAD_DIM, (h + 1) * HEAD_DIM)
        o_ref[:, sl] = (acc_ref[h] / l_ref[h]).T.astype(o_ref.dtype)


def _attn_scratch(nh, t):
    return [pltpu.VMEM((nh, 1, t), F32), pltpu.VMEM((nh, 1, t), F32), pltpu.VMEM((nh, HEAD_DIM, t), F32)]


def _dsa_attn_kernel(qs_ref, ks_ref, q_ref, k_ref, vt_ref, b_ref, o_ref, m_ref, l_ref, acc_ref, *, scale):
    s_id = pl.program_id(0)
    qi, kj = qs_ref[s_id], ks_ref[s_id]
    t = q_ref.shape[0]

    @pl.when(kj == 0)
    def _():
        _attn_init(m_ref, l_ref, acc_ref)

    bias_t = jnp.concatenate([b_ref[u, 0] for u in range(b_ref.shape[0])], axis=1).astype(F32)
    for h in range(DSA_HEADS):
        sl = slice(h * HEAD_DIM, (h + 1) * HEAD_DIM)
        s_t = _dot_nt(k_ref[:, sl], q_ref[:, sl]) * scale + bias_t
        _softmax_step(h, s_t, vt_ref[sl, :], m_ref, l_ref, acc_ref)

    @pl.when(kj == qi)
    def _():
        _attn_finish(o_ref, l_ref, acc_ref, DSA_HEADS)


def _dsa_attention(qk, vals_t, bias4, qcol, kcol, vrow):
    seq, width = qk.shape[0], DSA_HEADS * HEAD_DIM
    t = min(ATT_T, seq)
    nq = seq // t
    qs, ks = _tri_steps(nq)
    sub = t // IDX_TQ
    omap = lambda s, qs, ks: (qs[s], 0)
    qmap = lambda s, qs, ks: (qs[s], qcol)
    kmap = lambda s, qs, ks: (ks[s], kcol)
    vmap = lambda s, qs, ks: (vrow, ks[s])
    vmem = (2 * (3 * t * width * 2 + t * t * 2 + t * width * 2) + t * width * 4 + 28 * t * t * 4)
    return pl.pallas_call(
        functools.partial(_dsa_attn_kernel, scale=HEAD_DIM ** -0.5 * LOG2_E),
        out_shape=jax.ShapeDtypeStruct((seq, width), BF16),
        grid_spec=pltpu.PrefetchScalarGridSpec(
            num_scalar_prefetch=2,
            grid=(int(qs.shape[0]),),
            in_specs=[pl.BlockSpec((t, width), qmap), pl.BlockSpec((t, width), kmap),
                      pl.BlockSpec((width, t), vmap),
                      pl.BlockSpec((sub, 1, IDX_CK, IDX_TQ), lambda s, qs, ks: (qs[s], ks[s], 0, 0))],
            out_specs=pl.BlockSpec((t, width), omap),
            scratch_shapes=_attn_scratch(DSA_HEADS, t)),
        compiler_params=_cparams(("arbitrary",), vmem),
        name="dsa_attention",
    )(qs, ks, qk, qk, vals_t, bias4)


def _kmean_kernel(k_ref, o_ref):
    o_ref[0] = jnp.mean(k_ref[...].astype(F32), axis=0, keepdims=True)


def _block_means(qk, kcol):
    seq, width = qk.shape[0], MOBA_HEADS * HEAD_DIM
    nblk = seq // MOBA_BLOCK
    out = pl.pallas_call(
        _kmean_kernel,
        out_shape=jax.ShapeDtypeStruct((nblk, 1, width), F32),
        grid=(nblk,),
        in_specs=[pl.BlockSpec((MOBA_BLOCK, width), lambda i: (i, kcol))],
        out_specs=pl.BlockSpec((1, 1, width), lambda i: (i, 0, 0)),
        compiler_params=_cparams(("parallel",), 8 * MOBA_BLOCK * width * 4),
        name="moba_block_means",
    )(qk)
    return out.reshape(nblk, width)


def _moba_attn_kernel(qs_ref, ks_ref, q_ref, k_ref, vt_ref, km_ref, o_ref, m_ref, l_ref, acc_ref, sel_ref,
                      *, scale, n_top):
    s_id = pl.program_id(0)
    qi, kj = qs_ref[s_id], ks_ref[s_id]
    t = q_ref.shape[0]
    nblk = km_ref.shape[0]
    per_tile = t // MOBA_BLOCK

    @pl.when(kj == 0)
    def _():
        _attn_init(m_ref, l_ref, acc_ref)
        blk = lax.broadcasted_iota(I32, (nblk, t), 0)
        own = (qi * t + lax.broadcasted_iota(I32, (nblk, t), 1)) // MOBA_BLOCK
        blk_f = blk.astype(F32)
        for h in range(MOBA_HEADS):
            sl = slice(h * HEAD_DIM, (h + 1) * HEAD_DIM)
            g = _dot_nt(km_ref[:, sl].astype(BF16), q_ref[:, sl])
            g = jnp.where(blk < own, g, NEG_INF)
            sel = jnp.full((nblk, t), NEG_INF, F32)
            for _ in range(n_top):
                mx = jnp.max(g, axis=0, keepdims=True)
                is_max = jnp.logical_and(g == mx, mx > NEG_INF)
                first = jnp.min(jnp.where(is_max, blk_f, float(nblk)), axis=0, keepdims=True)
                pick = blk_f == first
                sel = jnp.where(pick, 0.0, sel)
                g = jnp.where(pick, NEG_INF, g)
            sel_ref[h] = sel

    def block_bias(h):
        rows = [jnp.broadcast_to(sel_ref[h, pl.ds(kj * per_tile + b, 1), :], (MOBA_BLOCK, t))
                for b in range(per_tile)]
        return jnp.concatenate(rows, axis=0)

    def run(diag):
        if diag:
            key = lax.broadcasted_iota(I32, (t, t), 0)
            qry = lax.broadcasted_iota(I32, (t, t), 1)
            own_blk = (key // MOBA_BLOCK) == (qry // MOBA_BLOCK)
            causal = jnp.where(key <= qry, 0.0, NEG_INF)
        for h in range(MOBA_HEADS):
            sl = slice(h * HEAD_DIM, (h + 1) * HEAD_DIM)
            bias_t = block_bias(h)
            if diag:
                bias_t = jnp.where(own_blk, causal, bias_t)
            s_t = _dot_nt(k_ref[:, sl], q_ref[:, sl]) * scale + bias_t
            _softmax_step(h, s_t, vt_ref[sl, :], m_ref, l_ref, acc_ref)

    @pl.when(kj < qi)
    def _():
        run(False)

    @pl.when(kj == qi)
    def _():
        run(True)
        _attn_finish(o_ref, l_ref, acc_ref, MOBA_HEADS)


def _moba_attention(qk, vals_t, kmean, qcol, kcol, vrow):
    seq, width = qk.shape[0], MOBA_HEADS * HEAD_DIM
    t = min(ATT_T, seq)
    nq = seq // t
    nblk = kmean.shape[0]
    qs, ks = _tri_steps(nq)
    omap = lambda s, qs, ks: (qs[s], 0)
    qmap = lambda s, qs, ks: (qs[s], qcol)
    kmap = lambda s, qs, ks: (ks[s], kcol)
    vmap = lambda s, qs, ks: (vrow, ks[s])
    vmem = (2 * (4 * t * width * 2) + nblk * width * 4 + t * width * 4 + MOBA_HEADS * nblk * t * 4
            + 28 * t * t * 4)
    return pl.pallas_call(
        functools.partial(_moba_attn_kernel, scale=HEAD_DIM ** -0.5 * LOG2_E, n_top=min(MOBA_TOPK, nblk)),
        out_shape=jax.ShapeDtypeStruct((seq, width), BF16),
        grid_spec=pltpu.PrefetchScalarGridSpec(
            num_scalar_prefetch=2,
            grid=(int(qs.shape[0]),),
            in_specs=[pl.BlockSpec((t, width), qmap), pl.BlockSpec((t, width), kmap),
                      pl.BlockSpec((width, t), vmap),
                      _resident((nblk, width), lambda s, qs, ks: (0, 0))],
            out_specs=pl.BlockSpec((t, width), omap),
            scratch_shapes=_attn_scratch(MOBA_HEADS, t) + [pltpu.VMEM((MOBA_HEADS, nblk, t), F32)]),
        compiler_params=_cparams(("arbitrary",), vmem),
        name="moba_attention",
    )(qs, ks, qk, qk, vals_t, kmean)


def _post_kernel(oa_ref, ob_ref, ga_ref, gb_ref, x_ref, wa_ref, wb_ref, wo_ref, gf_ref, wr_ref, br_ref,
                 x1_ref, h2_ref, ti_ref, tg_ref):
    ta = jnp.dot(oa_ref[...], wa_ref[...], preferred_element_type=F32)
    tb = jnp.dot(ob_ref[...], wb_ref[...], preferred_element_type=F32)
    merged = _sigmoid(ga_ref[...]) * ta + _sigmoid(gb_ref[...]) * tb
    x1 = x_ref[...] + jnp.dot(merged.astype(BF16), wo_ref[...], preferred_element_type=F32)
    x1_ref[...] = x1
    h2 = _rms(x1, gf_ref[...])
    h2_ref[...] = h2
    logits = jnp.dot(h2.astype(BF16), wr_ref[...], preferred_element_type=F32) + br_ref[...]
    lane = lax.broadcasted_iota(I32, logits.shape, 1)
    lane_f = lane.astype(F32)
    idx_out = jnp.zeros(logits.shape, I32)
    val_out = jnp.zeros(logits.shape, F32)
    top = None
    for r in range(EXPERT_TOPK):
        mx = jnp.max(logits, axis=-1, keepdims=True)
        ix = jnp.min(jnp.where(logits == mx, lane_f, float(LANES)), axis=-1, keepdims=True).astype(I32)
        if top is None:
            top = mx
        idx_out = jnp.where(lane == r, ix, idx_out)
        val_out = jnp.where(lane == r, jnp.exp(mx - top), val_out)
        logits = jnp.where(lane == ix, NEG_INF, logits)
    ti_ref[...] = idx_out
    tg_ref[...] = val_out / jnp.sum(val_out, axis=-1, keepdims=True)


def _post_attention(oa, ob, gates_ab, x2, wa, wb, wo, gf, wr, br):
    seq, d = x2.shape
    w = oa.shape[1]
    tm = min(POST_TM, seq)
    row = lambda i: (i, 0)
    fix = lambda i: (0, 0)
    vmem = (2 * (2 * tm * w * 2 + 3 * tm * d * 4 + 2 * tm * d * 4 + 2 * tm * LANES * 4)
            + 2 * w * d * 2 + d * d * 2 + d * LANES * 2 + 8 * tm * d * 4)
    return pl.pallas_call(
        _post_kernel,
        out_shape=(jax.ShapeDtypeStruct((seq, d), F32), jax.ShapeDtypeStruct((seq, d), F32),
                   jax.ShapeDtypeStruct((seq, LANES), I32), jax.ShapeDtypeStruct((seq, LANES), F32)),
        grid=(seq // tm,),
        in_specs=[pl.BlockSpec((tm, w), row), pl.BlockSpec((tm, w), row),
                  pl.BlockSpec((tm, d), row), pl.BlockSpec((tm, d), lambda i: (i, 1)), pl.BlockSpec((tm, d), row),
                  _resident((w, d), fix), _resident((w, d), fix), _resident((d, d), fix),
                  _resident((1, d), fix), _resident((d, LANES), fix), _resident((1, LANES), fix)],
        out_specs=(pl.BlockSpec((tm, d), row), pl.BlockSpec((tm, d), row),
                   pl.BlockSpec((tm, LANES), row), pl.BlockSpec((tm, LANES), row)),
        compiler_params=_cparams(("parallel",), vmem),
        name="merge_outproj_router",
    )(oa, ob, gates_ab, gates_ab, x2, wa, wb, wo, gf, wr, br)


def _split_even_odd(w):
    grp = 2 * LANES
    r = lax.broadcasted_iota(I32, (grp, grp), 0)
    c = lax.broadcasted_iota(I32, (grp, grp), 1)
    src = jnp.where(c < LANES, 2 * c, 2 * (c - LANES) + 1)
    sel = jnp.where(r == src, 1.0, 0.0).astype(BF16)
    parts = [jnp.dot(w[:, k * grp:(k + 1) * grp], sel, preferred_element_type=F32).astype(BF16)
             for k in range(w.shape[1] // grp)]
    even = jnp.concatenate([p[:, :LANES] for p in parts], axis=1)
    odd = jnp.concatenate([p[:, LANES:] for p in parts], axis=1)
    return even, odd


def _moe_kernel(ge_ref, nt_ref, ng_ref, tok_hbm, dst_hbm, h_hbm, wgu_ref, bg_ref, bl_ref, wd_ref, bd_ref,
                y_hbm, tok_ref, dst_ref, stage_ref, xall_ref, acc_ref, isem, gsem, ssem):
    g, j = pl.program_id(0), pl.program_id(1)
    n_grp_max, nf = pl.num_programs(0), pl.num_programs(1)
    ng = ng_ref[0]
    ns, ts = acc_ref.shape[0], tok_ref.shape[2]
    sub = stage_ref.shape[2]
    d = stage_ref.shape[3]
    cur = lax.rem(g, 2)
    nxt = 1 - cur
    nt_cur = nt_ref[g]
    nt_next = nt_ref[jnp.minimum(g + 1, n_grp_max - 1)]

    def on_slot(dyn_slot, fn):
        for s in range(2):
            pl.when(dyn_slot == s)(functools.partial(fn, s))

    def idx_copies(grp, s):
        return (pltpu.make_async_copy(tok_hbm.at[grp], tok_ref.at[s], isem.at[s]),
                pltpu.make_async_copy(dst_hbm.at[grp], dst_ref.at[s], isem.at[s]))

    def idx_start(grp, s):
        for cp in idx_copies(grp, s):
            cp.start()

    def idx_wait(s):
        for cp in idx_copies(0, s):
            cp.wait()

    def row_loop(issue):
        def body(i, carry):
            for k in range(sub):
                issue(i, k)
            return carry
        lax.fori_loop(0, ts // sub, body, 0)

    def gather_start(s, r):
        st = r % 2

        def issue(i, k):
            tok = tok_ref[s, r, i * sub + k]
            pltpu.make_async_copy(h_hbm.at[pl.ds(tok, 1), :], stage_ref.at[st, i, pl.ds(k, 1), :],
                                  gsem.at[st]).start()
        row_loop(issue)

    def gather_finish(r, xslot):
        st = r % 2
        pltpu.make_async_copy(stage_ref.at[st], stage_ref.at[st], gsem.at[st]).wait()
        xall_ref[xslot, r] = stage_ref[st].reshape(ts, d).astype(BF16)

    def scatter_start(s, r):
        def issue(i, k):
            dst = dst_ref[s, r, i * sub + k]
            pltpu.make_async_copy(acc_ref.at[r, i, pl.ds(k, 1), :], y_hbm.at[pl.ds(dst, 1), :], ssem.at[r]).start()
        row_loop(issue)

    def scatter_wait_one(r):
        pltpu.make_async_copy(acc_ref.at[r], acc_ref.at[r], ssem.at[r]).wait()

    def scatter_wait(n_sub):
        for r in range(ns):
            pl.when(r < n_sub)(functools.partial(scatter_wait_one, r))

    @pl.when(jnp.logical_and(g == 0, j == 0))
    def _():
        idx_start(0, 0)
        acc_ref[...] = jnp.zeros(acc_ref.shape, F32)
        spill0 = y_hbm.shape[0] - ns * ts
        for r in range(ns):
            def fill(i, carry):
                row = pl.multiple_of(spill0 + r * ts + i * sub, sub)
                pltpu.make_async_copy(acc_ref.at[r, i], y_hbm.at[pl.ds(row, sub), :], ssem.at[r]).start()
                return carry
            lax.fori_loop(0, ts // sub, fill, 0)
        scatter_wait(ns)
        idx_wait(0)
        for r in range(ns):
            @pl.when(r < nt_ref[0])
            def _():
                gather_start(0, r)
                gather_finish(r, 0)

        @pl.when(1 < ng)
        def _():
            idx_start(1, 1)

    nt_prev = nt_ref[jnp.maximum(g - 1, 0)]

    @pl.when(jnp.logical_and(j == 0, g == ng))
    def _():
        scatter_wait(nt_prev)

    @pl.when(jnp.logical_and(j == 0, jnp.logical_and(g >= 1, g < ng)))
    def _():
        for r in range(ns):
            pl.when(jnp.logical_and(r < nt_prev, r >= nt_cur))(functools.partial(scatter_wait_one, r))

    @pl.when(g + 1 < ng)
    def _():
        def prefetch(s):
            @pl.when(j == 0)
            def _():
                idx_wait(s)
                gather_start(s, 0)
            for step in range(1, ns + 1):
                @pl.when(jnp.logical_and(j == step, step - 1 < nt_next))
                def _():
                    gather_finish(step - 1, nxt)
                if step < ns:
                    @pl.when(jnp.logical_and(j == step, step < nt_next))
                    def _():
                        gather_start(s, step)
        on_slot(nxt, prefetch)

    @pl.when(g < ng)
    def _():
        wg, wl = _split_even_odd(wgu_ref[0].astype(BF16))
        wd = wd_ref[0].astype(BF16)

        def rows(r0, n):
            @pl.when(j == 0)
            def _():
                for r in range(r0, r0 + n):
                    pl.when(jnp.logical_and(g >= 1, r < nt_prev))(functools.partial(scatter_wait_one, r))
                    acc_ref[r] = jnp.broadcast_to(bd_ref[0], (ts // sub, sub, d))

            x = xall_ref[cur, r0:r0 + n].reshape(n * ts, d)
            gate = jnp.dot(x, wg, preferred_element_type=F32) + bg_ref[0]
            lin = jnp.dot(x, wl, preferred_element_type=F32) + bl_ref[0]
            gate = jnp.minimum(gate, SWIGLU_LIMIT)
            lin = jnp.clip(lin, -SWIGLU_LIMIT, SWIGLU_LIMIT)
            hid = (lin + 1.0) * (gate * _sigmoid(gate * SWIGLU_ALPHA))
            y = jnp.dot(hid.astype(BF16), wd, preferred_element_type=F32)
            acc_ref[r0:r0 + n] += y.reshape(n, ts // sub, sub, d)

        for r0 in range(0, ns, 2):
            pl.when(r0 + 1 < nt_cur)(functools.partial(rows, r0, 2))
            pl.when(r0 + 1 == nt_cur)(functools.partial(rows, r0, 1))

    @pl.when(jnp.logical_and(j == nf - 1, g < ng))
    def _():
        def finish(s):
            for r in range(ns):
                @pl.when(r < nt_cur)
                def _():
                    scatter_start(s, r)

            @pl.when(g + 2 < ng)
            def _():
                idx_start(g + 2, s)
        on_slot(cur, finish)

        @pl.when(g == n_grp_max - 1)
        def _():
            scatter_wait(nt_cur)


def _moe_experts(h2, route, w_gate_up, bg, bl, w_down, bd):
    grp_tok, grp_dst, grp_expert, grp_nt, n_grp, n_out_rows = route
    d = h2.shape[1]
    f = w_down.shape[1]
    ns, ts = MOE_NSUB, grp_tok.shape[2]
    n_grp_max = grp_expert.shape[0]
    tf = min(MOE_TF, f)
    nf = f // tf
    assert nf >= ns + 1, "one sub-tile of the next group is gathered per hidden-tile step"

    def grp(g, ng):
        return jnp.minimum(g, ng[0] - 1)

    def ftile(g, j, ng):
        return jnp.where(g < ng[0], j, nf - 1)

    w_map = lambda g, j, ge, nt, ng: (ge[grp(g, ng)], 0, ftile(g, j, ng))
    vmem = (2 * (d * 2 * tf * 4 + tf * d * 4 + 2 * tf * 4 + d * 4) + 2 * ts * d * 4 + 2 * ns * ts * d * 2
            + ns * ts * d * 4 + d * 2 * tf * (2 + 4 + 2) + tf * d * 2 + 3 * ts * d * 4)
    any_spec = pl.BlockSpec(memory_space=pl.ANY)
    return pl.pallas_call(
        _moe_kernel,
        out_shape=jax.ShapeDtypeStruct((n_out_rows, d), F32),
        grid_spec=pltpu.PrefetchScalarGridSpec(
            num_scalar_prefetch=3,
            grid=(n_grp_max, nf),
            in_specs=[any_spec, any_spec, any_spec,
                      pl.BlockSpec((1, d, 2 * tf), w_map),
                      pl.BlockSpec((1, 1, tf), w_map),
                      pl.BlockSpec((1, 1, tf), w_map),
                      pl.BlockSpec((1, tf, d), lambda g, j, ge, nt, ng: (ge[grp(g, ng)], ftile(g, j, ng), 0)),
                      pl.BlockSpec((1, 1, d), lambda g, j, ge, nt, ng: (ge[grp(g, ng)], 0, 0))],
            out_specs=any_spec,
            scratch_shapes=[pltpu.SMEM((2,) + grp_tok.shape[1:], I32), pltpu.SMEM((2,) + grp_tok.shape[1:], I32),
                            pltpu.VMEM((2, ts // MOE_ROW_UNROLL, MOE_ROW_UNROLL, d), F32),
                            pltpu.VMEM((2, ns, ts, d), BF16),
                            pltpu.VMEM((ns, ts // MOE_ROW_UNROLL, MOE_ROW_UNROLL, d), F32),
                            pltpu.SemaphoreType.DMA((2,)), pltpu.SemaphoreType.DMA((2,)),
                            pltpu.SemaphoreType.DMA((ns,))]),
        compiler_params=_cparams(("arbitrary", "arbitrary"), vmem),
        name="moe_experts",
    )(grp_expert, grp_nt, n_grp, grp_tok, grp_dst, h2, w_gate_up, bg, bl, w_down, bd)


def _route(top_idx):
    ts, ns = MOE_TS, MOE_NSUB
    n_tok = top_idx.shape[0]
    n_slots = n_tok * EXPERT_TOPK
    e_flat = top_idx.reshape(-1)
    order = jnp.argsort(e_flat).astype(I32)
    counts = jnp.bincount(e_flat, length=N_EXPERTS).astype(I32)
    start = jnp.cumsum(counts) - counts
    nsub = (counts + ts - 1) // ts
    last = N_EXPERTS - 1
    ngrp = (nsub + ns - 1) // ns
    grp_end = jnp.cumsum(ngrp)
    grp_start = grp_end - ngrp
    n_grp = grp_end[-1]
    gidx = jnp.arange(N_EXPERTS + n_slots // (ts * ns), dtype=I32)
    used = gidx < n_grp
    grp_expert = jnp.minimum(jnp.searchsorted(grp_end, gidx, side="right"), last).astype(I32)
    k = gidx - grp_start[grp_expert]
    grp_nt = jnp.where(used, jnp.clip(nsub[grp_expert] - k * ns, 0, ns), 0).astype(I32)
    ns_pad = -(-ns // 8) * 8
    r_idx = jnp.arange(ns_pad, dtype=I32)[None, :, None]
    c_idx = jnp.arange(ts, dtype=I32)[None, None, :]
    e_g = grp_expert[:, None, None]
    within = (k[:, None, None] * ns + r_idx) * ts + c_idx
    valid = used[:, None, None] & (r_idx < ns) & (within < counts[e_g])
    slot = order[jnp.clip(start[e_g] + within, 0, n_slots - 1)]
    spare = n_slots + (r_idx % ns) * ts + c_idx
    grp_tok = jnp.where(valid, slot // EXPERT_TOPK, 0).astype(I32)
    grp_dst = jnp.where(valid, (slot % EXPERT_TOPK) * n_tok + slot // EXPERT_TOPK, spare).astype(I32)
    return grp_tok, grp_dst, grp_expert, grp_nt, n_grp.astype(I32).reshape(1), n_slots + ts * ns


def _final_kernel(x1_ref, *rest):
    y_refs = rest[:EXPERT_TOPK]
    tg_ref, p_ref, wpp_ref, wpg_ref, gp_ref, gfin_ref, o_ref = rest[EXPERT_TOPK:]
    x2 = x1_ref[...]
    gates = tg_ref[...]
    for r in range(EXPERT_TOPK):
        x2 = x2 + y_refs[r][...] * gates[:, r:r + 1]
    ple = jnp.dot(p_ref[...].astype(BF16), wpp_ref[...], preferred_element_type=F32)
    gate = _sigmoid(jnp.dot(x2.astype(BF16), wpg_ref[...], preferred_element_type=F32))
    x3 = x2 + _rms(gate * ple, gp_ref[...])
    o_ref[...] = _rms(x3, gfin_ref[...])


def _final(x1, y, gates, p2, wpp, wpg, gp, gfin):
    seq, d = x1.shape
    pd = p2.shape[1]
    tm = min(FIN_TM, seq)
    row = lambda i: (i, 0)
    fix = lambda i: (0, 0)
    vmem = (2 * (2 * tm * d * 4 + EXPERT_TOPK * tm * d * 4 + tm * LANES * 4 + tm * pd * 4)
            + pd * d * 2 + d * d * 2 + 8 * tm * d * 4)
    return pl.pallas_call(
        _final_kernel,
        out_shape=jax.ShapeDtypeStruct((seq, d), F32),
        grid=(seq // tm,),
        in_specs=[pl.BlockSpec((tm, d), row),
                  *[pl.BlockSpec((tm, d), functools.partial(lambda i, r: (r * (seq // tm) + i, 0), r=r))
                    for r in range(EXPERT_TOPK)],
                  pl.BlockSpec((tm, LANES), row), pl.BlockSpec((tm, pd), row),
                  _resident((pd, d), fix), _resident((d, d), fix),
                  _resident((1, d), fix), _resident((1, d), fix)],
        out_specs=pl.BlockSpec((tm, d), row),
        compiler_params=_cparams(("parallel",), vmem),
        name="combine_ple_norm",
    )(x1, *([y] * EXPERT_TOPK), gates, p2, wpp, wpg, gp, gfin)


def _layer(x2, p2, g_mix, w_in, g_idx_k, b_idx_k, w_br_a, w_br_b, w_out, g_ffn, w_router, b_router,
           w_gate_up, b_gate_up, w_down, b_down, w_ple_proj, w_ple_gate, g_ple, g_final):
    seq, d = x2.shape
    dsa_w, moba_w = DSA_HEADS * HEAD_DIM, MOBA_HEADS * HEAD_DIM
    sizes = (dsa_w, dsa_w, dsa_w, IDX_HEADS * IDX_DIM, IDX_DIM, IDX_HEADS, moba_w, moba_w, moba_w, d, d)
    off = np.concatenate([[0], np.cumsum(sizes)])
    col = lambda k: w_in[:, off[k]:off[k + 1]]
    row1 = lambda v: v.reshape(1, -1)
    g_mix2 = row1(g_mix)

    w_rope = jnp.concatenate([col(0), col(1), col(6), col(7)], axis=1).astype(BF16)
    w_qi = col(3).astype(BF16)
    w_kiwi = jnp.zeros((d, 2 * LANES), F32).at[:, :IDX_DIM].set(col(4)).at[:, LANES:LANES + IDX_HEADS].set(col(5))
    w_val_t = jnp.concatenate([col(2), col(8)], axis=1).T.astype(BF16)
    w_gate = jnp.concatenate([col(9), col(10)], axis=1).astype(BF16)

    rope_tabs = _rope_tables(seq, ROPE_DIM, HEAD_DIM)
    idx_tabs = _rope_tables(seq, IDX_ROPE_DIM, IDX_DIM)
    qk = _project(x2, g_mix2, w_rope, BF16, rope_tabs, ROPE_DIM // 2)
    qi = _project(x2, g_mix2, w_qi, BF16, idx_tabs, IDX_ROPE_DIM // 2)
    pad = lambda v: jnp.zeros((1, LANES), F32).at[0, :IDX_DIM].set(v)
    ki, wi = _project_kiwi(x2, g_mix2, w_kiwi.astype(BF16), pad(g_idx_k), pad(b_idx_k), idx_tabs,
                           (IDX_HEADS ** -0.5) * (IDX_DIM ** -0.5))
    vals_t = _project(x2, g_mix2, w_val_t, BF16, transposed=True)
    gates_ab = _project(x2, g_mix2, w_gate, F32)

    bias4 = _dsa_mask(qi, ki, wi.T, min(DSA_TOPK, seq // 4))
    o_a = _dsa_attention(qk, vals_t, bias4, 0, 1, 0)
    o_b = _moba_attention(qk, vals_t, _block_means(qk, 3), 2, 3, 1)

    w_r = jnp.zeros((d, LANES), F32).at[:, :N_EXPERTS].set(w_router).astype(BF16)
    b_r = jnp.full((1, LANES), NEG_INF, F32).at[0, :N_EXPERTS].set(b_router)
    x1, h2, top_idx, top_gate = _post_attention(
        o_a, o_b, gates_ab, x2, w_br_a.astype(BF16), w_br_b.astype(BF16), w_out.astype(BF16),
        row1(g_ffn), w_r, b_r)

    bias_row = lambda b: b.reshape(N_EXPERTS, 1, -1)
    y = _moe_experts(h2, _route(top_idx[:, :EXPERT_TOPK]), w_gate_up,
                     bias_row(b_gate_up[:, 0::2]), bias_row(b_gate_up[:, 1::2]), w_down, bias_row(b_down))

    return _final(x1, y, top_gate, p2, w_ple_proj.astype(BF16), w_ple_gate.astype(BF16), row1(g_ple),
                  row1(g_final))


def kernel(x, p, g_mix, w_in, g_idx_k, b_idx_k, w_br_a, w_br_b, w_out, g_ffn, w_router, b_router,
           w_gate_up, b_gate_up, w_down, b_down, w_ple_proj, w_ple_gate, g_ple, g_final):
    batch, seq, d = x.shape
    depth = w_in.shape[0]
    assert batch == 1 and depth == 1, "kernel handles the single-sequence, single-layer block"
    assert seq % ATT_T == 0 and seq % PROJ_TM == 0
    out = _layer(x[0], p[0, 0], g_mix[0], w_in[0], g_idx_k[0], b_idx_k[0], w_br_a[0], w_br_b[0], w_out[0],
                 g_ffn[0], w_router[0], b_router[0], w_gate_up[0], b_gate_up[0], w_down[0], b_down[0],
                 w_ple_proj[0], w_ple_gate[0], g_ple[0], g_final)
    return out[None]
```

```python
import functools

import numpy as np
import jax
import jax.numpy as jnp
from jax import lax
from jax.experimental import pallas as pl
from jax.experimental.pallas import tpu as pltpu

F32 = jnp.float32
BF16 = jnp.bfloat16
I32 = jnp.int32
NEG_INF = float("-inf")
LOG2_E = 1.4426950408889634

HEAD_DIM = 128
DSA_HEADS = 8
MOBA_HEADS = 8
ROPE_DIM = HEAD_DIM // 4
ROPE_THETA = 500000.0
IDX_HEADS = 16
IDX_DIM = 64
IDX_ROPE_DIM = IDX_DIM // 4
DSA_TOPK = 256
MOBA_BLOCK = 256
MOBA_TOPK = 3
N_EXPERTS = 32
EXPERT_TOPK = 4
SWIGLU_LIMIT = 7.0
SWIGLU_ALPHA = 1.702
EPS = 1e-6

LANES = 128
VMEM_CAP_BYTES = 60000 * 1024

PROJ_TM = 1024
PROJ_TN = 512
IDX_TQ = 256
IDX_CK = 512
COUNT_ROWS = 32
ATT_T = 512
POST_TM = 256
MOE_TS = 256
MOE_NSUB = 6
MOE_TF = 256
MOE_ROW_UNROLL = 8
FIN_TM = 256

INT_MIN = -2 ** 31
KEY_NEG_INF = 0x807FFFFF - 2 ** 32


def _cparams(sem, vmem_bytes):
    return pltpu.CompilerParams(dimension_semantics=sem,
                                vmem_limit_bytes=int(min(vmem_bytes, VMEM_CAP_BYTES)))


def _resident(shape, index_map):
    return pl.BlockSpec(shape, index_map, pipeline_mode=pl.Buffered(1))


def _sigmoid(x):
    return 1.0 / (1.0 + jnp.exp(-x))


def _rms(xf, g):
    ms = jnp.mean(xf * xf, axis=-1, keepdims=True)
    return xf * lax.rsqrt(ms + EPS) * g


def _dot_nt(a, b):
    return lax.dot_general(a, b, (((1,), (1,)), ((), ())), preferred_element_type=F32)


def _rope_tables(seq, rot_dim, period):
    half = rot_dim // 2
    inv = 1.0 / (ROPE_THETA ** (jnp.arange(half, dtype=F32) / half))
    ang = jnp.arange(seq).astype(F32)[:, None] * inv[None, :]
    cos, sin = jnp.cos(ang), jnp.sin(ang)
    z = lambda n: jnp.zeros((seq, n), F32)
    c = jnp.concatenate([cos, cos, jnp.ones((seq, period - rot_dim), F32)], axis=-1)
    s1 = jnp.concatenate([-sin, z(period - half)], axis=-1)
    s2 = jnp.concatenate([z(half), sin, z(period - rot_dim)], axis=-1)
    rep = LANES // period
    return tuple(jnp.tile(t, (1, rep)) for t in (c, s1, s2))


def _rope(a, c, s1, s2, half):
    return a * c + pltpu.roll(a, LANES - half, 1) * s1 + pltpu.roll(a, half, 1) * s2


def _proj_kernel(x_ref, g_ref, w_ref, *rest, half, transposed):
    if half is None:
        o_ref, h_ref = rest
    else:
        c_ref, s1_ref, s2_ref, o_ref, h_ref = rest

    @pl.when(pl.program_id(1) == 0)
    def _():
        h_ref[...] = _rms(x_ref[...], g_ref[...]).astype(BF16)

    if transposed:
        o_ref[...] = _dot_nt(w_ref[...], h_ref[...]).astype(o_ref.dtype)
        return
    acc = jnp.dot(h_ref[...], w_ref[...], preferred_element_type=F32)
    if half is None:
        o_ref[...] = acc.astype(o_ref.dtype)
    else:
        c, s1, s2 = c_ref[...], s1_ref[...], s2_ref[...]
        for k in range(acc.shape[1] // LANES):
            sl = slice(k * LANES, (k + 1) * LANES)
            o_ref[:, sl] = _rope(acc[:, sl], c, s1, s2, half).astype(o_ref.dtype)


def _project(x2, g, w, out_dtype, tables=None, half=None, transposed=False):
    seq, d = x2.shape
    n = w.shape[0] if transposed else w.shape[1]
    tm, tn = min(PROJ_TM, seq), min(PROJ_TN, n)
    in_specs = [pl.BlockSpec((tm, d), lambda i, j: (i, 0)),
                pl.BlockSpec((1, d), lambda i, j: (0, 0)),
                pl.BlockSpec((tn, d), lambda i, j: (j, 0)) if transposed
                else pl.BlockSpec((d, tn), lambda i, j: (0, j))]
    args = [x2, g, w]
    if tables is not None:
        in_specs += [pl.BlockSpec((tm, LANES), lambda i, j: (i, 0))] * 3
        args += list(tables)
    vmem = 2 * (tm * d * 4 + d * tn * 2 + tm * tn * 4 + 3 * tm * LANES * 4) + tm * d * 2 + 4 * tm * tn * 4
    return pl.pallas_call(
        functools.partial(_proj_kernel, half=half, transposed=transposed),
        out_shape=jax.ShapeDtypeStruct((n, seq) if transposed else (seq, n), out_dtype),
        grid=(seq // tm, n // tn),
        in_specs=in_specs,
        out_specs=pl.BlockSpec((tn, tm), lambda i, j: (j, i)) if transposed
        else pl.BlockSpec((tm, tn), lambda i, j: (i, j)),
        scratch_shapes=[pltpu.VMEM((tm, d), BF16)],
        compiler_params=_cparams(("parallel", "arbitrary"), vmem),
        name="proj_rope" if half is not None else "proj_plain",
    )(*args)


def _proj_kiwi_kernel(x_ref, g_ref, w_ref, lg_ref, lb_ref, c_ref, s1_ref, s2_ref, ki_ref, wi_ref, *, scale):
    h = _rms(x_ref[...], g_ref[...]).astype(BF16)
    acc = jnp.dot(h, w_ref[...], preferred_element_type=F32)
    a = acc[:, :LANES]
    valid = lax.broadcasted_iota(I32, a.shape, 1) < IDX_DIM
    mu = jnp.sum(jnp.where(valid, a, 0.0), axis=-1, keepdims=True) / IDX_DIM
    dlt = jnp.where(valid, a - mu, 0.0)
    var = jnp.sum(dlt * dlt, axis=-1, keepdims=True) / IDX_DIM
    y = dlt * lax.rsqrt(var + EPS) * lg_ref[...] + lb_ref[...]
    ki_ref[...] = _rope(y, c_ref[...], s1_ref[...], s2_ref[...], IDX_ROPE_DIM // 2).astype(BF16)
    wi_ref[...] = acc[:, LANES:LANES + IDX_HEADS] * scale


def _project_kiwi(x2, g, w_kiwi, lg, lb, tables, scale):
    seq, d = x2.shape
    tm = min(PROJ_TM, seq)
    row = lambda i: (i, 0)
    fix = lambda i: (0, 0)
    vmem = 2 * (tm * d * 4 + d * 2 * LANES * 2 + 5 * tm * LANES * 4) + 8 * tm * 2 * LANES * 4 + tm * d * 6
    return pl.pallas_call(
        functools.partial(_proj_kiwi_kernel, scale=scale),
        out_shape=(jax.ShapeDtypeStruct((seq, LANES), BF16), jax.ShapeDtypeStruct((seq, IDX_HEADS), F32)),
        grid=(seq // tm,),
        in_specs=[pl.BlockSpec((tm, d), row), pl.BlockSpec((1, d), fix), pl.BlockSpec((d, 2 * LANES), fix),
                  pl.BlockSpec((1, LANES), fix), pl.BlockSpec((1, LANES), fix),
                  pl.BlockSpec((tm, LANES), row), pl.BlockSpec((tm, LANES), row), pl.BlockSpec((tm, LANES), row)],
        out_specs=(pl.BlockSpec((tm, LANES), row), pl.BlockSpec((tm, IDX_HEADS), row)),
        compiler_params=_cparams(("parallel",), vmem),
        name="proj_kiwi",
    )(x2, g, w_kiwi, lg, lb, *tables)


def _key_to_float(key):
    bits = key ^ ((key >> 31) & 0x7FFFFFFF)
    return lax.bitcast_convert_type(bits, F32)


def _indexer_kernel(qi_ref, ki_ref, wt_ref, out_ref, s_ref, qh_ref, cst_ref, *, n_sel, idx_bits):
    tq, ck = IDX_TQ, IDX_CK
    n_chunks_total = out_ref.shape[1]
    i = pl.program_id(0)
    nc = ((i + 1) * tq + ck - 1) // ck
    qidx = i * tq + lax.broadcasted_iota(I32, (ck, tq), 1)

    def key_idx(c):
        return c * ck + lax.broadcasted_iota(I32, (ck, tq), 0)

    lane = lax.broadcasted_iota(I32, (tq, LANES), 1)
    low = lane < IDX_DIM
    for p in range(IDX_HEADS // 2):
        pair = qi_ref[:, p * LANES:(p + 1) * LANES].astype(F32)
        qh_ref[2 * p] = jnp.where(low, pair, 0.0).astype(BF16)
        qh_ref[2 * p + 1] = jnp.where(low, pltpu.roll(pair, IDX_DIM, 1), 0.0).astype(BF16)

    def score_chunk(c, carry):
        kc = ki_ref[pl.ds(pl.multiple_of(c * ck, ck), ck), :]
        acc = jnp.zeros((ck, tq), F32)
        for h in range(IDX_HEADS):
            acc = acc + jnp.maximum(_dot_nt(kc, qh_ref[h]), 0.0) * wt_ref[h:h + 1, :]
        s_ref[c] = jnp.where(key_idx(c) <= qidx, acc, NEG_INF)
        return carry

    lax.fori_loop(0, nc, score_chunk, 0)

    def count(pred):
        def body(c, acc):
            hit = jnp.where(pred(s_ref[c], lambda: key_idx(c)), 1.0, 0.0)
            return acc + jnp.sum(hit.reshape(ck // COUNT_ROWS, COUNT_ROWS, tq), axis=0)
        acc = lax.fori_loop(0, nc, body, jnp.zeros((COUNT_ROWS, tq), F32))
        return jnp.sum(acc, axis=0, keepdims=True)

    def bit_step(b, carry):
        prefix, cnt_at = carry
        cand = prefix + lax.shift_left(jnp.int32(1), 31 - b)
        cand_f = _key_to_float(cand)
        cnt = count(lambda sc, key: sc >= cand_f)
        take = cnt >= n_sel
        return jnp.where(take, cand, prefix), jnp.where(take, cnt, cnt_at)

    prefix, cnt_at = lax.fori_loop(
        0, 32, bit_step, (jnp.full((1, tq), INT_MIN, I32), jnp.full((1, tq), float(n_sel), F32)))
    tau = jnp.where(prefix < KEY_NEG_INF, NEG_INF, _key_to_float(prefix))

    need = jnp.logical_and(cnt_at > n_sel, tau > NEG_INF)
    any_tie = jnp.max(jnp.where(need, 1.0, 0.0)) > 0.0
    idx_all = jnp.full((1, tq), 2 ** 30, I32)

    @pl.when(any_tie)
    def _():
        rem = n_sel - count(lambda sc, key: sc > tau)
        cut = jnp.zeros((1, tq), I32)
        for b in range(idx_bits - 1, -1, -1):
            cand = cut + (1 << b)
            below = count(lambda sc, key: jnp.logical_and(sc == tau, key() < cand))
            cut = jnp.where(below < rem, cand, cut)
        cst_ref[...] = jnp.where(need, cut, idx_all)

    @pl.when(jnp.logical_not(any_tie))
    def _():
        cst_ref[...] = idx_all

    cut = cst_ref[...]

    def emit(c, carry):
        sc = s_ref[c]
        key = key_idx(c)
        tie = jnp.where(key <= cut, 0.0, NEG_INF)
        b = jnp.where(sc > tau, 0.0, jnp.where(sc == tau, tie, NEG_INF))
        out_ref[0, c] = jnp.where(key <= qidx, b, NEG_INF).astype(BF16)
        return carry

    lax.fori_loop(0, nc, emit, 0)

    def fill(c, carry):
        out_ref[0, c] = jnp.full((ck, tq), NEG_INF, BF16)
        return carry

    lax.fori_loop(nc, n_chunks_total, fill, 0)


def _dsa_mask(qi, ki, wi_t, n_sel):
    seq = qi.shape[0]
    tq, ck = min(IDX_TQ, seq), IDX_CK
    nq, nchunk = seq // tq, seq // ck
    idx_bits = max(1, int(seq - 1).bit_length())
    vmem = (2 * (tq * qi.shape[1] * 2 + IDX_HEADS * tq * 4 + nchunk * tq * ck * 2) + seq * LANES * 2
            + nchunk * tq * ck * 4 + IDX_HEADS * tq * LANES * 2 + 12 * tq * ck * 4)
    return pl.pallas_call(
        functools.partial(_indexer_kernel, n_sel=n_sel, idx_bits=idx_bits),
        out_shape=jax.ShapeDtypeStruct((nq, nchunk, ck, tq), BF16),
        grid=(nq,),
        in_specs=[pl.BlockSpec((tq, qi.shape[1]), lambda i: (i, 0)),
                  _resident((seq, LANES), lambda i: (0, 0)),
                  pl.BlockSpec((IDX_HEADS, tq), lambda i: (0, i))],
        out_specs=pl.BlockSpec((1, nchunk, ck, tq), lambda i: (i, 0, 0, 0)),
        scratch_shapes=[pltpu.VMEM((nchunk, ck, tq), F32),
                        pltpu.VMEM((IDX_HEADS, tq, LANES), BF16),
                        pltpu.VMEM((1, tq), I32)],
        compiler_params=_cparams(("parallel",), vmem),
        name="dsa_indexer",
    )(qi, ki, wi_t)


def _tri_steps(n):
    qs = [i for i in range(n) for _ in range(i + 1)]
    ks = [j for i in range(n) for j in range(i + 1)]
    return jnp.asarray(qs, I32), jnp.asarray(ks, I32)


def _softmax_step(h, s_t, vt_h, m_ref, l_ref, acc_ref):
    m_prev = m_ref[h]
    m_new = jnp.maximum(m_prev, jnp.max(s_t, axis=0, keepdims=True))
    m_safe = jnp.where(m_new == NEG_INF, 0.0, m_new)
    alpha = jnp.exp2(m_prev - m_safe)
    p_t = jnp.exp2(s_t - m_safe)
    l_ref[h] = alpha * l_ref[h] + jnp.sum(p_t, axis=0, keepdims=True)
    acc_ref[h] = alpha * acc_ref[h] + jnp.dot(vt_h, p_t.astype(BF16), preferred_element_type=F32)
    m_ref[h] = m_new


def _attn_init(m_ref, l_ref, acc_ref):
    m_ref[...] = jnp.full(m_ref.shape, NEG_INF, F32)
    l_ref[...] = jnp.zeros(l_ref.shape, F32)
    acc_ref[...] = jnp.zeros(acc_ref.shape, F32)


def _attn_finish(o_ref, l_ref, acc_ref, nh):
    for h in range(nh):
        sl = slice(h * HEAD_DIM, (h + 1) * HEAD_DIM)
        o_ref[:, sl] = (acc_ref[h] / l_ref[h]).T.astype(o_ref.dtype)


def _attn_scratch(nh, t):
    return [pltpu.VMEM((nh, 1, t), F32), pltpu.VMEM((nh, 1, t), F32), pltpu.VMEM((nh, HEAD_DIM, t), F32)]


def _dsa_attn_kernel(qs_ref, ks_ref, q_ref, k_ref, vt_ref, b_ref, o_ref, m_ref, l_ref, acc_ref, *, scale):
    s_id = pl.program_id(0)
    qi, kj = qs_ref[s_id], ks_ref[s_id]
    t = q_ref.shape[0]

    @pl.when(kj == 0)
    def _():
        _attn_init(m_ref, l_ref, acc_ref)

    bias_t = jnp.concatenate([b_ref[u, 0] for u in range(b_ref.shape[0])], axis=1).astype(F32)
    for h in range(DSA_HEADS):
        sl = slice(h * HEAD_DIM, (h + 1) * HEAD_DIM)
        s_t = _dot_nt(k_ref[:, sl], q_ref[:, sl]) * scale + bias_t
        _softmax_step(h, s_t, vt_ref[sl, :], m_ref, l_ref, acc_ref)

    @pl.when(kj == qi)
    def _():
        _attn_finish(o_ref, l_ref, acc_ref, DSA_HEADS)


def _dsa_attention(qk, vals_t, bias4, qcol, kcol, vrow):
    seq, width = qk.shape[0], DSA_HEADS * HEAD_DIM
    t = min(ATT_T, seq)
    nq = seq // t
    qs, ks = _tri_steps(nq)
    sub = t // IDX_TQ
    omap = lambda s, qs, ks: (qs[s], 0)
    qmap = lambda s, qs, ks: (qs[s], qcol)
    kmap = lambda s, qs, ks: (ks[s], kcol)
    vmap = lambda s, qs, ks: (vrow, ks[s])
    vmem = (2 * (3 * t * width * 2 + t * t * 2 + t * width * 2) + t * width * 4 + 28 * t * t * 4)
    return pl.pallas_call(
        functools.partial(_dsa_attn_kernel, scale=HEAD_DIM ** -0.5 * LOG2_E),
        out_shape=jax.ShapeDtypeStruct((seq, width), BF16),
        grid_spec=pltpu.PrefetchScalarGridSpec(
            num_scalar_prefetch=2,
            grid=(int(qs.shape[0]),),
            in_specs=[pl.BlockSpec((t, width), qmap), pl.BlockSpec((t, width), kmap),
                      pl.BlockSpec((width, t), vmap),
                      pl.BlockSpec((sub, 1, IDX_CK, IDX_TQ), lambda s, qs, ks: (qs[s], ks[s], 0, 0))],
            out_specs=pl.BlockSpec((t, width), omap),
            scratch_shapes=_attn_scratch(DSA_HEADS, t)),
        compiler_params=_cparams(("arbitrary",), vmem),
        name="dsa_attention",
    )(qs, ks, qk, qk, vals_t, bias4)


def _kmean_kernel(k_ref, o_ref):
    o_ref[0] = jnp.mean(k_ref[...].astype(F32), axis=0, keepdims=True)


def _block_means(qk, kcol):
    seq, width = qk.shape[0], MOBA_HEADS * HEAD_DIM
    nblk = seq // MOBA_BLOCK
    out = pl.pallas_call(
        _kmean_kernel,
        out_shape=jax.ShapeDtypeStruct((nblk, 1, width), F32),
        grid=(nblk,),
        in_specs=[pl.BlockSpec((MOBA_BLOCK, width), lambda i: (i, kcol))],
        out_specs=pl.BlockSpec((1, 1, width), lambda i: (i, 0, 0)),
        compiler_params=_cparams(("parallel",), 8 * MOBA_BLOCK * width * 4),
        name="moba_block_means",
    )(qk)
    return out.reshape(nblk, width)


def _moba_attn_kernel(qs_ref, ks_ref, q_ref, k_ref, vt_ref, km_ref, o_ref, m_ref, l_ref, acc_ref, sel_ref,
                      *, scale, n_top):
    s_id = pl.program_id(0)
    qi, kj = qs_ref[s_id], ks_ref[s_id]
    t = q_ref.shape[0]
    nblk = km_ref.shape[0]
    per_tile = t // MOBA_BLOCK

    @pl.when(kj == 0)
    def _():
        _attn_init(m_ref, l_ref, acc_ref)
        blk = lax.broadcasted_iota(I32, (nblk, t), 0)
        own = (qi * t + lax.broadcasted_iota(I32, (nblk, t), 1)) // MOBA_BLOCK
        blk_f = blk.astype(F32)
        for h in range(MOBA_HEADS):
            sl = slice(h * HEAD_DIM, (h + 1) * HEAD_DIM)
            g = _dot_nt(km_ref[:, sl].astype(BF16), q_ref[:, sl])
            g = jnp.where(blk < own, g, NEG_INF)
            sel = jnp.full((nblk, t), NEG_INF, F32)
            for _ in range(n_top):
                mx = jnp.max(g, axis=0, keepdims=True)
                is_max = jnp.logical_and(g == mx, mx > NEG_INF)
                first = jnp.min(jnp.where(is_max, blk_f, float(nblk)), axis=0, keepdims=True)
                pick = blk_f == first
                sel = jnp.where(pick, 0.0, sel)
                g = jnp.where(pick, NEG_INF, g)
            sel_ref[h] = sel

    def block_bias(h):
        rows = [jnp.broadcast_to(sel_ref[h, pl.ds(kj * per_tile + b, 1), :], (MOBA_BLOCK, t))
                for b in range(per_tile)]
        return jnp.concatenate(rows, axis=0)

    def run(diag):
        if diag:
            key = lax.broadcasted_iota(I32, (t, t), 0)
            qry = lax.broadcasted_iota(I32, (t, t), 1)
            own_blk = (key // MOBA_BLOCK) == (qry // MOBA_BLOCK)
            causal = jnp.where(key <= qry, 0.0, NEG_INF)
        for h in range(MOBA_HEADS):
            sl = slice(h * HEAD_DIM, (h + 1) * HEAD_DIM)
            bias_t = block_bias(h)
            if diag:
                bias_t = jnp.where(own_blk, causal, bias_t)
            s_t = _dot_nt(k_ref[:, sl], q_ref[:, sl]) * scale + bias_t
            _softmax_step(h, s_t, vt_ref[sl, :], m_ref, l_ref, acc_ref)

    @pl.when(kj < qi)
    def _():
        run(False)

    @pl.when(kj == qi)
    def _():
        run(True)
        _attn_finish(o_ref, l_ref, acc_ref, MOBA_HEADS)


def _moba_attention(qk, vals_t, kmean, qcol, kcol, vrow):
    seq, width = qk.shape[0], MOBA_HEADS * HEAD_DIM
    t = min(ATT_T, seq)
    nq = seq // t
    nblk = kmean.shape[0]
    qs, ks = _tri_steps(nq)
    omap = lambda s, qs, ks: (qs[s], 0)
    qmap = lambda s, qs, ks: (qs[s], qcol)
    kmap = lambda s, qs, ks: (ks[s], kcol)
    vmap = lambda s, qs, ks: (vrow, ks[s])
    vmem = (2 * (4 * t * width * 2) + nblk * width * 4 + t * width * 4 + MOBA_HEADS * nblk * t * 4
            + 28 * t * t * 4)
    return pl.pallas_call(
        functools.partial(_moba_attn_kernel, scale=HEAD_DIM ** -0.5 * LOG2_E, n_top=min(MOBA_TOPK, nblk)),
        out_shape=jax.ShapeDtypeStruct((seq, width), BF16),
        grid_spec=pltpu.PrefetchScalarGridSpec(
            num_scalar_prefetch=2,
            grid=(int(qs.shape[0]),),
            in_specs=[pl.BlockSpec((t, width), qmap), pl.BlockSpec((t, width), kmap),
                      pl.BlockSpec((width, t), vmap),
                      _resident((nblk, width), lambda s, qs, ks: (0, 0))],
            out_specs=pl.BlockSpec((t, width), omap),
            scratch_shapes=_attn_scratch(MOBA_HEADS, t) + [pltpu.VMEM((MOBA_HEADS, nblk, t), F32)]),
        compiler_params=_cparams(("arbitrary",), vmem),
        name="moba_attention",
    )(qs, ks, qk, qk, vals_t, kmean)


def _post_kernel(oa_ref, ob_ref, ga_ref, gb_ref, x_ref, wa_ref, wb_ref, wo_ref, gf_ref, wr_ref, br_ref,
                 x1_ref, h2_ref, ti_ref, tg_ref):
    ta = jnp.dot(oa_ref[...], wa_ref[...], preferred_element_type=F32)
    tb = jnp.dot(ob_ref[...], wb_ref[...], preferred_element_type=F32)
    merged = _sigmoid(ga_ref[...]) * ta + _sigmoid(gb_ref[...]) * tb
    x1 = x_ref[...] + jnp.dot(merged.astype(BF16), wo_ref[...], preferred_element_type=F32)
    x1_ref[...] = x1
    h2 = _rms(x1, gf_ref[...])
    h2_ref[...] = h2
    logits = jnp.dot(h2.astype(BF16), wr_ref[...], preferred_element_type=F32) + br_ref[...]
    lane = lax.broadcasted_iota(I32, logits.shape, 1)
    lane_f = lane.astype(F32)
    idx_out = jnp.zeros(logits.shape, I32)
    val_out = jnp.zeros(logits.shape, F32)
    top = None
    for r in range(EXPERT_TOPK):
        mx = jnp.max(logits, axis=-1, keepdims=True)
        ix = jnp.min(jnp.where(logits == mx, lane_f, float(LANES)), axis=-1, keepdims=True).astype(I32)
        if top is None:
            top = mx
        idx_out = jnp.where(lane == r, ix, idx_out)
        val_out = jnp.where(lane == r, jnp.exp(mx - top), val_out)
        logits = jnp.where(lane == ix, NEG_INF, logits)
    ti_ref[...] = idx_out
    tg_ref[...] = val_out / jnp.sum(val_out, axis=-1, keepdims=True)


def _post_attention(oa, ob, gates_ab, x2, wa, wb, wo, gf, wr, br):
    seq, d = x2.shape
    w = oa.shape[1]
    tm = min(POST_TM, seq)
    row = lambda i: (i, 0)
    fix = lambda i: (0, 0)
    vmem = (2 * (2 * tm * w * 2 + 3 * tm * d * 4 + 2 * tm * d * 4 + 2 * tm * LANES * 4)
            + 2 * w * d * 2 + d * d * 2 + d * LANES * 2 + 8 * tm * d * 4)
    return pl.pallas_call(
        _post_kernel,
        out_shape=(jax.ShapeDtypeStruct((seq, d), F32), jax.ShapeDtypeStruct((seq, d), F32),
                   jax.ShapeDtypeStruct((seq, LANES), I32), jax.ShapeDtypeStruct((seq, LANES), F32)),
        grid=(seq // tm,),
        in_specs=[pl.BlockSpec((tm, w), row), pl.BlockSpec((tm, w), row),
                  pl.BlockSpec((tm, d), row), pl.BlockSpec((tm, d), lambda i: (i, 1)), pl.BlockSpec((tm, d), row),
                  _resident((w, d), fix), _resident((w, d), fix), _resident((d, d), fix),
                  _resident((1, d), fix), _resident((d, LANES), fix), _resident((1, LANES), fix)],
        out_specs=(pl.BlockSpec((tm, d), row), pl.BlockSpec((tm, d), row),
                   pl.BlockSpec((tm, LANES), row), pl.BlockSpec((tm, LANES), row)),
        compiler_params=_cparams(("parallel",), vmem),
        name="merge_outproj_router",
    )(oa, ob, gates_ab, gates_ab, x2, wa, wb, wo, gf, wr, br)


def _split_even_odd(w):
    grp = 2 * LANES
    r = lax.broadcasted_iota(I32, (grp, grp), 0)
    c = lax.broadcasted_iota(I32, (grp, grp), 1)
    src = jnp.where(c < LANES, 2 * c, 2 * (c - LANES) + 1)
    sel = jnp.where(r == src, 1.0, 0.0).astype(BF16)
    parts = [jnp.dot(w[:, k * grp:(k + 1) * grp], sel, preferred_element_type=F32).astype(BF16)
             for k in range(w.shape[1] // grp)]
    even = jnp.concatenate([p[:, :LANES] for p in parts], axis=1)
    odd = jnp.concatenate([p[:, LANES:] for p in parts], axis=1)
    return even, odd


def _moe_kernel(ge_ref, sub0_ref, nt_ref, ng_ref, tok_hbm, dst_hbm, h_hbm, wgu_ref, bg_ref, bl_ref, wd_ref, bd_ref,
                y_hbm, tok_ref, dst_ref, stage_ref, xall_ref, acc_ref, isem, gsem, ssem):
    g, j = pl.program_id(0), pl.program_id(1)
    n_grp_max, nf = pl.num_programs(0), pl.num_programs(1)
    ng = ng_ref[0]
    ns, ts = acc_ref.shape[0], tok_ref.shape[3]
    sub = stage_ref.shape[2]
    d = stage_ref.shape[3]
    cur = lax.rem(g, 2)
    nxt = 1 - cur
    nt_cur = nt_ref[g]
    nt_next = nt_ref[jnp.minimum(g + 1, n_grp_max - 1)]

    def on_slot(dyn_slot, fn):
        for s in range(2):
            pl.when(dyn_slot == s)(functools.partial(fn, s))

    def idx_copies(grp, s):
        first = sub0_ref[grp]
        return ([pltpu.make_async_copy(tok_hbm.at[first + r], tok_ref.at[s, r], isem.at[s]) for r in range(ns)]
                + [pltpu.make_async_copy(dst_hbm.at[first + r], dst_ref.at[s, r], isem.at[s]) for r in range(ns)])

    def idx_start(grp, s):
        for cp in idx_copies(grp, s):
            cp.start()

    def idx_wait(s):
        for cp in idx_copies(0, s):
            cp.wait()

    def row_loop(issue):
        def body(i, carry):
            for k in range(sub):
                issue(i, k)
            return carry
        lax.fori_loop(0, ts // sub, body, 0)

    def gather_start(s, r):
        st = r % 2

        def issue(i, k):
            tok = tok_ref[s, r, 0, i * sub + k]
            pltpu.make_async_copy(h_hbm.at[pl.ds(tok, 1), :], stage_ref.at[st, i, pl.ds(k, 1), :],
                                  gsem.at[st]).start()
        row_loop(issue)

    def gather_finish(r, xslot):
        st = r % 2
        pltpu.make_async_copy(stage_ref.at[st], stage_ref.at[st], gsem.at[st]).wait()
        xall_ref[xslot, r] = stage_ref[st].reshape(ts, d).astype(BF16)

    def scatter_start(s, r):
        def issue(i, k):
            dst = dst_ref[s, r, 0, i * sub + k]
            pltpu.make_async_copy(acc_ref.at[r, i, pl.ds(k, 1), :], y_hbm.at[pl.ds(dst, 1), :], ssem.at[r]).start()
        row_loop(issue)

    def scatter_wait_one(r):
        pltpu.make_async_copy(acc_ref.at[r], acc_ref.at[r], ssem.at[r]).wait()

    def scatter_wait(n_sub):
        for r in range(ns):
            pl.when(r < n_sub)(functools.partial(scatter_wait_one, r))

    @pl.when(jnp.logical_and(g == 0, j == 0))
    def _():
        idx_start(0, 0)
        acc_ref[...] = jnp.zeros(acc_ref.shape, F32)
        spill0 = y_hbm.shape[0] - ns * ts
        for r in range(ns):
            def fill(i, carry):
                row = pl.multiple_of(spill0 + r * ts + i * sub, sub)
                pltpu.make_async_copy(acc_ref.at[r, i], y_hbm.at[pl.ds(row, sub), :], ssem.at[r]).start()
                return carry
            lax.fori_loop(0, ts // sub, fill, 0)
        scatter_wait(ns)
        idx_wait(0)
        for r in range(ns):
            @pl.when(r < nt_ref[0])
            def _():
                gather_start(0, r)
                gather_finish(r, 0)

        @pl.when(1 < ng)
        def _():
            idx_start(1, 1)

    nt_prev = nt_ref[jnp.maximum(g - 1, 0)]

    @pl.when(jnp.logical_and(j == 0, g == ng))
    def _():
        scatter_wait(nt_prev)

    @pl.when(jnp.logical_and(j == 0, jnp.logical_and(g >= 1, g < ng)))
    def _():
        for r in range(ns):
            pl.when(jnp.logical_and(r < nt_prev, r >= nt_cur))(functools.partial(scatter_wait_one, r))

    @pl.when(g + 1 < ng)
    def _():
        def prefetch(s):
            @pl.when(j == 0)
            def _():
                idx_wait(s)
                gather_start(s, 0)
            for step in range(1, ns + 1):
                @pl.when(jnp.logical_and(j == step, step - 1 < nt_next))
                def _():
                    gather_finish(step - 1, nxt)
                if step < ns:
                    @pl.when(jnp.logical_and(j == step, step < nt_next))
                    def _():
                        gather_start(s, step)
        on_slot(nxt, prefetch)

    @pl.when(g < ng)
    def _():
        wg, wl = _split_even_odd(wgu_ref[0].astype(BF16))
        wd = wd_ref[0].astype(BF16)

        def rows(r0, n):
            @pl.when(j == 0)
            def _():
                for r in range(r0, r0 + n):
                    pl.when(jnp.logical_and(g >= 1, r < nt_prev))(functools.partial(scatter_wait_one, r))
                    acc_ref[r] = jnp.broadcast_to(bd_ref[0], (ts // sub, sub, d))

            x = xall_ref[cur, r0:r0 + n].reshape(n * ts, d)
            gate = jnp.dot(x, wg, preferred_element_type=F32) + bg_ref[0]
            lin = jnp.dot(x, wl, preferred_element_type=F32) + bl_ref[0]
            gate = jnp.minimum(gate, SWIGLU_LIMIT)
            lin = jnp.clip(lin, -SWIGLU_LIMIT, SWIGLU_LIMIT)
            hid = (lin + 1.0) * (gate * _sigmoid(gate * SWIGLU_ALPHA))
            y = jnp.dot(hid.astype(BF16), wd, preferred_element_type=F32)
            acc_ref[r0:r0 + n] += y.reshape(n, ts // sub, sub, d)

        for r0 in range(0, ns, 2):
            pl.when(r0 + 1 < nt_cur)(functools.partial(rows, r0, 2))
            pl.when(r0 + 1 == nt_cur)(functools.partial(rows, r0, 1))

    @pl.when(jnp.logical_and(j == nf - 1, g < ng))
    def _():
        def finish(s):
            for r in range(ns):
                @pl.when(r < nt_cur)
                def _():
                    scatter_start(s, r)

            @pl.when(g + 2 < ng)
            def _():
                idx_start(g + 2, s)
        on_slot(cur, finish)

        @pl.when(g == n_grp_max - 1)
        def _():
            scatter_wait(nt_cur)


def _moe_experts(h2, route, w_gate_up, bg, bl, w_down, bd):
    sub_tok, sub_dst, grp_expert, grp_sub0, grp_nt, n_grp, n_out_rows = route
    d = h2.shape[1]
    f = w_down.shape[1]
    ns, ts = MOE_NSUB, sub_tok.shape[2]
    n_grp_max = grp_expert.shape[0]
    tf = min(MOE_TF, f)
    nf = f // tf
    assert nf >= ns + 1, "one sub-tile of the next group is gathered per hidden-tile step"

    def grp(g, ng):
        return jnp.minimum(g, ng[0] - 1)

    def ftile(g, j, ng):
        return jnp.where(g < ng[0], j, nf - 1)

    w_map = lambda g, j, ge, s0, nt, ng: (ge[grp(g, ng)], 0, ftile(g, j, ng))
    vmem = (2 * (d * 2 * tf * 4 + tf * d * 4 + 2 * tf * 4 + d * 4) + 2 * ts * d * 4 + 2 * ns * ts * d * 2
            + ns * ts * d * 4 + d * 2 * tf * (2 + 4 + 2) + tf * d * 2 + 3 * ts * d * 4)
    any_spec = pl.BlockSpec(memory_space=pl.ANY)
    return pl.pallas_call(
        _moe_kernel,
        out_shape=jax.ShapeDtypeStruct((n_out_rows, d), F32),
        grid_spec=pltpu.PrefetchScalarGridSpec(
            num_scalar_prefetch=4,
            grid=(n_grp_max, nf),
            in_specs=[any_spec, any_spec, any_spec,
                      pl.BlockSpec((1, d, 2 * tf), w_map),
                      pl.BlockSpec((1, 1, tf), w_map),
                      pl.BlockSpec((1, 1, tf), w_map),
                      pl.BlockSpec((1, tf, d), lambda g, j, ge, s0, nt, ng: (ge[grp(g, ng)], ftile(g, j, ng), 0)),
                      pl.BlockSpec((1, 1, d), lambda g, j, ge, s0, nt, ng: (ge[grp(g, ng)], 0, 0))],
            out_specs=any_spec,
            scratch_shapes=[pltpu.SMEM((2, ns, 1, ts), I32), pltpu.SMEM((2, ns, 1, ts), I32),
                            pltpu.VMEM((2, ts // MOE_ROW_UNROLL, MOE_ROW_UNROLL, d), F32),
                            pltpu.VMEM((2, ns, ts, d), BF16),
                            pltpu.VMEM((ns, ts // MOE_ROW_UNROLL, MOE_ROW_UNROLL, d), F32),
                            pltpu.SemaphoreType.DMA((2,)), pltpu.SemaphoreType.DMA((2,)),
                            pltpu.SemaphoreType.DMA((ns,))]),
        compiler_params=_cparams(("arbitrary", "arbitrary"), vmem),
        name="moe_experts",
    )(grp_expert, grp_sub0, grp_nt, n_grp, sub_tok, sub_dst, h2, w_gate_up, bg, bl, w_down, bd)


def _route(top_idx):
    ts, ns = MOE_TS, MOE_NSUB
    n_tok = top_idx.shape[0]
    n_slots = n_tok * EXPERT_TOPK
    e_flat = top_idx.reshape(-1)
    order = jnp.argsort(e_flat).astype(I32)
    counts = jnp.bincount(e_flat, length=N_EXPERTS).astype(I32)
    start = jnp.cumsum(counts) - counts
    nsub = (counts + ts - 1) // ts
    sub_end = jnp.cumsum(nsub)
    sub_start = sub_end - nsub
    n_sub_max = n_slots // ts + N_EXPERTS
    last = N_EXPERTS - 1
    sub_expert = jnp.minimum(jnp.searchsorted(sub_end, jnp.arange(n_sub_max), side="right"), last)
    rows = jnp.arange(n_sub_max * ts, dtype=I32)
    e_row = jnp.repeat(sub_expert, ts)
    within = rows - ts * sub_start[e_row]
    valid = within < counts[e_row]
    slot = order[jnp.clip(start[e_row] + within, 0, n_slots - 1)]
    row_tok = jnp.where(valid, slot // EXPERT_TOPK, 0)
    row_dst = jnp.where(valid, (slot % EXPERT_TOPK) * n_tok + slot // EXPERT_TOPK, n_slots + rows % (ts * ns))
    tail = jnp.zeros((ns, 1, ts), I32)
    sub_tok = jnp.concatenate([row_tok.reshape(n_sub_max, 1, ts).astype(I32), tail])
    sub_dst = jnp.concatenate([row_dst.reshape(n_sub_max, 1, ts).astype(I32), tail + n_slots])
    ngrp = (nsub + ns - 1) // ns
    grp_end = jnp.cumsum(ngrp)
    grp_start = grp_end - ngrp
    n_grp = grp_end[-1]
    gidx = jnp.arange(N_EXPERTS + n_slots // (ts * ns), dtype=I32)
    used = gidx < n_grp
    grp_expert = jnp.minimum(jnp.searchsorted(grp_end, gidx, side="right"), last).astype(I32)
    k = gidx - grp_start[grp_expert]
    grp_sub0 = jnp.where(used, sub_start[grp_expert] + k * ns, 0).astype(I32)
    grp_nt = jnp.where(used, jnp.clip(nsub[grp_expert] - k * ns, 0, ns), 0).astype(I32)
    return sub_tok, sub_dst, grp_expert, grp_sub0, grp_nt, n_grp.astype(I32).reshape(1), n_slots + ts * ns


def _final_kernel(x1_ref, *rest):
    y_refs = rest[:EXPERT_TOPK]
    tg_ref, p_ref, wpp_ref, wpg_ref, gp_ref, gfin_ref, o_ref = rest[EXPERT_TOPK:]
    x2 = x1_ref[...]
    gates = tg_ref[...]
    for r in range(EXPERT_TOPK):
        x2 = x2 + y_refs[r][...] * gates[:, r:r + 1]
    ple = jnp.dot(p_ref[...].astype(BF16), wpp_ref[...], preferred_element_type=F32)
    gate = _sigmoid(jnp.dot(x2.astype(BF16), wpg_ref[...], preferred_element_type=F32))
    x3 = x2 + _rms(gate * ple, gp_ref[...])
    o_ref[...] = _rms(x3, gfin_ref[...])


def _final(x1, y, gates, p2, wpp, wpg, gp, gfin):
    seq, d = x1.shape
    pd = p2.shape[1]
    tm = min(FIN_TM, seq)
    row = lambda i: (i, 0)
    fix = lambda i: (0, 0)
    vmem = (2 * (2 * tm * d * 4 + EXPERT_TOPK * tm * d * 4 + tm * LANES * 4 + tm * pd * 4)
            + pd * d * 2 + d * d * 2 + 8 * tm * d * 4)
    return pl.pallas_call(
        _final_kernel,
        out_shape=jax.ShapeDtypeStruct((seq, d), F32),
        grid=(seq // tm,),
        in_specs=[pl.BlockSpec((tm, d), row),
                  *[pl.BlockSpec((tm, d), functools.partial(lambda i, r: (r * (seq // tm) + i, 0), r=r))
                    for r in range(EXPERT_TOPK)],
                  pl.BlockSpec((tm, LANES), row), pl.BlockSpec((tm, pd), row),
                  _resident((pd, d), fix), _resident((d, d), fix),
                  _resident((1, d), fix), _resident((1, d), fix)],
        out_specs=pl.BlockSpec((tm, d), row),
        compiler_params=_cparams(("parallel",), vmem),
        name="combine_ple_norm",
    )(x1, *([y] * EXPERT_TOPK), gates, p2, wpp, wpg, gp, gfin)


def _layer(x2, p2, g_mix, w_in, g_idx_k, b_idx_k, w_br_a, w_br_b, w_out, g_ffn, w_router, b_router,
           w_gate_up, b_gate_up, w_down, b_down, w_ple_proj, w_ple_gate, g_ple, g_final):
    seq, d = x2.shape
    dsa_w, moba_w = DSA_HEADS * HEAD_DIM, MOBA_HEADS * HEAD_DIM
    sizes = (dsa_w, dsa_w, dsa_w, IDX_HEADS * IDX_DIM, IDX_DIM, IDX_HEADS, moba_w, moba_w, moba_w, d, d)
    off = np.concatenate([[0], np.cumsum(sizes)])
    col = lambda k: w_in[:, off[k]:off[k + 1]]
    row1 = lambda v: v.reshape(1, -1)
    g_mix2 = row1(g_mix)

    w_rope = jnp.concatenate([col(0), col(1), col(6), col(7)], axis=1).astype(BF16)
    w_qi = col(3).astype(BF16)
    w_kiwi = jnp.zeros((d, 2 * LANES), F32).at[:, :IDX_DIM].set(col(4)).at[:, LANES:LANES + IDX_HEADS].set(col(5))
    w_val_t = jnp.concatenate([col(2), col(8)], axis=1).T.astype(BF16)
    w_gate = jnp.concatenate([col(9), col(10)], axis=1).astype(BF16)

    rope_tabs = _rope_tables(seq, ROPE_DIM, HEAD_DIM)
    idx_tabs = _rope_tables(seq, IDX_ROPE_DIM, IDX_DIM)
    qk = _project(x2, g_mix2, w_rope, BF16, rope_tabs, ROPE_DIM // 2)
    qi = _project(x2, g_mix2, w_qi, BF16, idx_tabs, IDX_ROPE_DIM // 2)
    pad = lambda v: jnp.zeros((1, LANES), F32).at[0, :IDX_DIM].set(v)
    ki, wi = _project_kiwi(x2, g_mix2, w_kiwi.astype(BF16), pad(g_idx_k), pad(b_idx_k), idx_tabs,
                           (IDX_HEADS ** -0.5) * (IDX_DIM ** -0.5))
    vals_t = _project(x2, g_mix2, w_val_t, BF16, transposed=True)
    gates_ab = _project(x2, g_mix2, w_gate, F32)

    bias4 = _dsa_mask(qi, ki, wi.T, min(DSA_TOPK, seq // 4))
    o_a = _dsa_attention(qk, vals_t, bias4, 0, 1, 0)
    o_b = _moba_attention(qk, vals_t, _block_means(qk, 3), 2, 3, 1)

    w_r = jnp.zeros((d, LANES), F32).at[:, :N_EXPERTS].set(w_router).astype(BF16)
    b_r = jnp.full((1, LANES), NEG_INF, F32).at[0, :N_EXPERTS].set(b_router)
    x1, h2, top_idx, top_gate = _post_attention(
        o_a, o_b, gates_ab, x2, w_br_a.astype(BF16), w_br_b.astype(BF16), w_out.astype(BF16),
        row1(g_ffn), w_r, b_r)

    bias_row = lambda b: b.reshape(N_EXPERTS, 1, -1)
    y = _moe_experts(h2, _route(top_idx[:, :EXPERT_TOPK]), w_gate_up,
                     bias_row(b_gate_up[:, 0::2]), bias_row(b_gate_up[:, 1::2]), w_down, bias_row(b_down))

    return _final(x1, y, top_gate, p2, w_ple_proj.astype(BF16), w_ple_gate.astype(BF16), row1(g_ple),
                  row1(g_final))


def kernel(x, p, g_mix, w_in, g_idx_k, b_idx_k, w_br_a, w_br_b, w_out, g_ffn, w_router, b_router,
           w_gate_up, b_gate_up, w_down, b_down, w_ple_proj, w_ple_gate, g_ple, g_final):
    batch, seq, d = x.shape
    depth = w_in.shape[0]
    assert batch == 1 and depth == 1, "kernel handles the single-sequence, single-layer block"
    assert seq % ATT_T == 0 and seq % PROJ_TM == 0
    out = _layer(x[0], p[0, 0], g_mix[0], w_in[0], g_idx_k[0], b_idx_k[0], w_br_a[0], w_br_b[0], w_out[0],
                 g_ffn[0], w_router[0], b_router[0], w_gate_up[0], b_gate_up[0], w_down[0], b_down[0],
                 w_ple_proj[0], w_ple_gate[0], g_ple[0], g_final)
    return out[None]
```

```python
import functools

import numpy as np
import jax
import jax.numpy as jnp
from jax import lax
from jax.experimental import pallas as pl
from jax.experimental.pallas import tpu as pltpu

F32 = jnp.float32
BF16 = jnp.bfloat16
I32 = jnp.int32
NEG_INF = float("-inf")
LOG2_E = 1.4426950408889634

HEAD_DIM = 128
DSA_HEADS = 8
MOBA_HEADS = 8
ROPE_DIM = HEAD_DIM // 4
ROPE_THETA = 500000.0
IDX_HEADS = 16
IDX_DIM = 64
IDX_ROPE_DIM = IDX_DIM // 4
DSA_TOPK = 256
MOBA_BLOCK = 256
MOBA_TOPK = 3
N_EXPERTS = 32
EXPERT_TOPK = 4
SWIGLU_LIMIT = 7.0
SWIGLU_ALPHA = 1.702
EPS = 1e-6

LANES = 128
VMEM_CAP_BYTES = 60000 * 1024

PROJ_TM = 1024
PROJ_TN = 512
IDX_TQ = 256
IDX_CK = 512
COUNT_ROWS = 32
ATT_T = 512
POST_TM = 256
MOE_TS = 256
MOE_NSUB = 6
MOE_TF = 256
MOE_ROW_UNROLL = 8
FIN_TM = 256

INT_MIN = -2 ** 31
KEY_NEG_INF = 0x807FFFFF - 2 ** 32


def _cparams(sem, vmem_bytes):
    return pltpu.CompilerParams(dimension_semantics=sem,
                                vmem_limit_bytes=int(min(vmem_bytes, VMEM_CAP_BYTES)))


def _resident(shape, index_map):
    return pl.BlockSpec(shape, index_map, pipeline_mode=pl.Buffered(1))


def _sigmoid(x):
    return 1.0 / (1.0 + jnp.exp(-x))


def _rms(xf, g):
    ms = jnp.mean(xf * xf, axis=-1, keepdims=True)
    return xf * lax.rsqrt(ms + EPS) * g


def _dot_nt(a, b):
    return lax.dot_general(a, b, (((1,), (1,)), ((), ())), preferred_element_type=F32)


def _rope_tables(seq, rot_dim, period):
    half = rot_dim // 2
    inv = 1.0 / (ROPE_THETA ** (jnp.arange(half, dtype=F32) / half))
    ang = jnp.arange(seq).astype(F32)[:, None] * inv[None, :]
    cos, sin = jnp.cos(ang), jnp.sin(ang)
    z = lambda n: jnp.zeros((seq, n), F32)
    c = jnp.concatenate([cos, cos, jnp.ones((seq, period - rot_dim), F32)], axis=-1)
    s1 = jnp.concatenate([-sin, z(period - half)], axis=-1)
    s2 = jnp.concatenate([z(half), sin, z(period - rot_dim)], axis=-1)
    rep = LANES // period
    return tuple(jnp.tile(t, (1, rep)) for t in (c, s1, s2))


def _rope(a, c, s1, s2, half):
    return a * c + pltpu.roll(a, LANES - half, 1) * s1 + pltpu.roll(a, half, 1) * s2


def _proj_kernel(x_ref, g_ref, w_ref, *rest, half, transposed):
    if half is None:
        o_ref, h_ref = rest
    else:
        c_ref, s1_ref, s2_ref, o_ref, h_ref = rest

    @pl.when(pl.program_id(1) == 0)
    def _():
        h_ref[...] = _rms(x_ref[...], g_ref[...]).astype(BF16)

    if transposed:
        o_ref[...] = _dot_nt(w_ref[...], h_ref[...]).astype(o_ref.dtype)
        return
    acc = jnp.dot(h_ref[...], w_ref[...], preferred_element_type=F32)
    if half is None:
        o_ref[...] = acc.astype(o_ref.dtype)
    else:
        c, s1, s2 = c_ref[...], s1_ref[...], s2_ref[...]
        for k in range(acc.shape[1] // LANES):
            sl = slice(k * LANES, (k + 1) * LANES)
            o_ref[:, sl] = _rope(acc[:, sl], c, s1, s2, half).astype(o_ref.dtype)


def _project(x2, g, w, out_dtype, tables=None, half=None, transposed=False):
    seq, d = x2.shape
    n = w.shape[0] if transposed else w.shape[1]
    tm, tn = min(PROJ_TM, seq), min(PROJ_TN, n)
    in_specs = [pl.BlockSpec((tm, d), lambda i, j: (i, 0)),
                pl.BlockSpec((1, d), lambda i, j: (0, 0)),
                pl.BlockSpec((tn, d), lambda i, j: (j, 0)) if transposed
                else pl.BlockSpec((d, tn), lambda i, j: (0, j))]
    args = [x2, g, w]
    if tables is not None:
        in_specs += [pl.BlockSpec((tm, LANES), lambda i, j: (i, 0))] * 3
        args += list(tables)
    vmem = 2 * (tm * d * 4 + d * tn * 2 + tm * tn * 4 + 3 * tm * LANES * 4) + tm * d * 2 + 4 * tm * tn * 4
    return pl.pallas_call(
        functools.partial(_proj_kernel, half=half, transposed=transposed),
        out_shape=jax.ShapeDtypeStruct((n, seq) if transposed else (seq, n), out_dtype),
        grid=(seq // tm, n // tn),
        in_specs=in_specs,
        out_specs=pl.BlockSpec((tn, tm), lambda i, j: (j, i)) if transposed
        else pl.BlockSpec((tm, tn), lambda i, j: (i, j)),
        scratch_shapes=[pltpu.VMEM((tm, d), BF16)],
        compiler_params=_cparams(("parallel", "arbitrary"), vmem),
        name="proj_rope" if half is not None else "proj_plain",
    )(*args)


def _proj_kiwi_kernel(x_ref, g_ref, w_ref, lg_ref, lb_ref, c_ref, s1_ref, s2_ref, ki_ref, wi_ref, *, scale):
    h = _rms(x_ref[...], g_ref[...]).astype(BF16)
    acc = jnp.dot(h, w_ref[...], preferred_element_type=F32)
    a = acc[:, :LANES]
    valid = lax.broadcasted_iota(I32, a.shape, 1) < IDX_DIM
    mu = jnp.sum(jnp.where(valid, a, 0.0), axis=-1, keepdims=True) / IDX_DIM
    dlt = jnp.where(valid, a - mu, 0.0)
    var = jnp.sum(dlt * dlt, axis=-1, keepdims=True) / IDX_DIM
    y = dlt * lax.rsqrt(var + EPS) * lg_ref[...] + lb_ref[...]
    ki_ref[...] = _rope(y, c_ref[...], s1_ref[...], s2_ref[...], IDX_ROPE_DIM // 2).astype(BF16)
    wi_ref[...] = acc[:, LANES:LANES + IDX_HEADS] * scale


def _project_kiwi(x2, g, w_kiwi, lg, lb, tables, scale):
    seq, d = x2.shape
    tm = min(PROJ_TM, seq)
    row = lambda i: (i, 0)
    fix = lambda i: (0, 0)
    vmem = 2 * (tm * d * 4 + d * 2 * LANES * 2 + 5 * tm * LANES * 4) + 8 * tm * 2 * LANES * 4 + tm * d * 6
    return pl.pallas_call(
        functools.partial(_proj_kiwi_kernel, scale=scale),
        out_shape=(jax.ShapeDtypeStruct((seq, LANES), BF16), jax.ShapeDtypeStruct((seq, IDX_HEADS), F32)),
        grid=(seq // tm,),
        in_specs=[pl.BlockSpec((tm, d), row), pl.BlockSpec((1, d), fix), pl.BlockSpec((d, 2 * LANES), fix),
                  pl.BlockSpec((1, LANES), fix), pl.BlockSpec((1, LANES), fix),
                  pl.BlockSpec((tm, LANES), row), pl.BlockSpec((tm, LANES), row), pl.BlockSpec((tm, LANES), row)],
        out_specs=(pl.BlockSpec((tm, LANES), row), pl.BlockSpec((tm, IDX_HEADS), row)),
        compiler_params=_cparams(("parallel",), vmem),
        name="proj_kiwi",
    )(x2, g, w_kiwi, lg, lb, *tables)


def _key_to_float(key):
    bits = key ^ ((key >> 31) & 0x7FFFFFFF)
    return lax.bitcast_convert_type(bits, F32)


def _indexer_kernel(qi_ref, ki_ref, wt_ref, out_ref, s_ref, qh_ref, cst_ref, *, n_sel, idx_bits):
    tq, ck = IDX_TQ, IDX_CK
    n_chunks_total = out_ref.shape[1]
    i = pl.program_id(0)
    nc = ((i + 1) * tq + ck - 1) // ck
    qidx = i * tq + lax.broadcasted_iota(I32, (ck, tq), 1)

    def key_idx(c):
        return c * ck + lax.broadcasted_iota(I32, (ck, tq), 0)

    lane = lax.broadcasted_iota(I32, (tq, LANES), 1)
    low = lane < IDX_DIM
    for p in range(IDX_HEADS // 2):
        pair = qi_ref[:, p * LANES:(p + 1) * LANES].astype(F32)
        qh_ref[2 * p] = jnp.where(low, pair, 0.0).astype(BF16)
        qh_ref[2 * p + 1] = jnp.where(low, pltpu.roll(pair, IDX_DIM, 1), 0.0).astype(BF16)

    def score_chunk(c, carry):
        kc = ki_ref[pl.ds(pl.multiple_of(c * ck, ck), ck), :]
        acc = jnp.zeros((ck, tq), F32)
        for h in range(IDX_HEADS):
            acc = acc + jnp.maximum(_dot_nt(kc, qh_ref[h]), 0.0) * wt_ref[h:h + 1, :]
        s_ref[c] = jnp.where(key_idx(c) <= qidx, acc, NEG_INF)
        return carry

    lax.fori_loop(0, nc, score_chunk, 0)

    def count(pred):
        def body(c, acc):
            hit = jnp.where(pred(s_ref[c], lambda: key_idx(c)), 1.0, 0.0)
            return acc + jnp.sum(hit.reshape(ck // COUNT_ROWS, COUNT_ROWS, tq), axis=0)
        acc = lax.fori_loop(0, nc, body, jnp.zeros((COUNT_ROWS, tq), F32))
        return jnp.sum(acc, axis=0, keepdims=True)

    def bit_step(b, carry):
        prefix, cnt_at = carry
        cand = prefix + lax.shift_left(jnp.int32(1), 31 - b)
        cand_f = _key_to_float(cand)
        cnt = count(lambda sc, key: sc >= cand_f)
        take = cnt >= n_sel
        return jnp.where(take, cand, prefix), jnp.where(take, cnt, cnt_at)

    prefix, cnt_at = lax.fori_loop(
        0, 32, bit_step, (jnp.full((1, tq), INT_MIN, I32), jnp.full((1, tq), float(n_sel), F32)))
    tau = jnp.where(prefix < KEY_NEG_INF, NEG_INF, _key_to_float(prefix))

    need = jnp.logical_and(cnt_at > n_sel, tau > NEG_INF)
    any_tie = jnp.max(jnp.where(need, 1.0, 0.0)) > 0.0
    idx_all = jnp.full((1, tq), 2 ** 30, I32)

    @pl.when(any_tie)
    def _():
        rem = n_sel - count(lambda sc, key: sc > tau)
        cut = jnp.zeros((1, tq), I32)
        for b in range(idx_bits - 1, -1, -1):
            cand = cut + (1 << b)
            below = count(lambda sc, key: jnp.logical_and(sc == tau, key() < cand))
            cut = jnp.where(below < rem, cand, cut)
        cst_ref[...] = jnp.where(need, cut, idx_all)

    @pl.when(jnp.logical_not(any_tie))
    def _():
        cst_ref[...] = idx_all

    cut = cst_ref[...]

    def emit(c, carry):
        sc = s_ref[c]
        key = key_idx(c)
        tie = jnp.where(key <= cut, 0.0, NEG_INF)
        b = jnp.where(sc > tau, 0.0, jnp.where(sc == tau, tie, NEG_INF))
        out_ref[0, c] = jnp.where(key <= qidx, b, NEG_INF).astype(BF16)
        return carry

    lax.fori_loop(0, nc, emit, 0)

    def fill(c, carry):
        out_ref[0, c] = jnp.full((ck, tq), NEG_INF, BF16)
        return carry

    lax.fori_loop(nc, n_chunks_total, fill, 0)


def _dsa_mask(qi, ki, wi_t, n_sel):
    seq = qi.shape[0]
    tq, ck = min(IDX_TQ, seq), IDX_CK
    nq, nchunk = seq // tq, seq // ck
    idx_bits = max(1, int(seq - 1).bit_length())
    vmem = (2 * (tq * qi.shape[1] * 2 + IDX_HEADS * tq * 4 + nchunk * tq * ck * 2) + seq * LANES * 2
            + nchunk * tq * ck * 4 + IDX_HEADS * tq * LANES * 2 + 12 * tq * ck * 4)
    return pl.pallas_call(
        functools.partial(_indexer_kernel, n_sel=n_sel, idx_bits=idx_bits),
        out_shape=jax.ShapeDtypeStruct((nq, nchunk, ck, tq), BF16),
        grid=(nq,),
        in_specs=[pl.BlockSpec((tq, qi.shape[1]), lambda i: (i, 0)),
                  _resident((seq, LANES), lambda i: (0, 0)),
                  pl.BlockSpec((IDX_HEADS, tq), lambda i: (0, i))],
        out_specs=pl.BlockSpec((1, nchunk, ck, tq), lambda i: (i, 0, 0, 0)),
        scratch_shapes=[pltpu.VMEM((nchunk, ck, tq), F32),
                        pltpu.VMEM((IDX_HEADS, tq, LANES), BF16),
                        pltpu.VMEM((1, tq), I32)],
        compiler_params=_cparams(("parallel",), vmem),
        name="dsa_indexer",
    )(qi, ki, wi_t)


def _tri_steps(n):
    qs = [i for i in range(n) for _ in range(i + 1)]
    ks = [j for i in range(n) for j in range(i + 1)]
    return jnp.asarray(qs, I32), jnp.asarray(ks, I32)


def _softmax_step(h, s_t, vt_h, m_ref, l_ref, acc_ref):
    m_prev = m_ref[h]
    m_new = jnp.maximum(m_prev, jnp.max(s_t, axis=0, keepdims=True))
    m_safe = jnp.where(m_new == NEG_INF, 0.0, m_new)
    alpha = jnp.exp2(m_prev - m_safe)
    p_t = jnp.exp2(s_t - m_safe)
    l_ref[h] = alpha * l_ref[h] + jnp.sum(p_t, axis=0, keepdims=True)
    acc_ref[h] = alpha * acc_ref[h] + jnp.dot(vt_h, p_t.astype(BF16), preferred_element_type=F32)
    m_ref[h] = m_new


def _attn_init(m_ref, l_ref, acc_ref):
    m_ref[...] = jnp.full(m_ref.shape, NEG_INF, F32)
    l_ref[...] = jnp.zeros(l_ref.shape, F32)
    acc_ref[...] = jnp.zeros(acc_ref.shape, F32)


def _attn_finish(o_ref, l_ref, acc_ref, nh):
    for h in range(nh):
        sl = slice(h * HEAD_DIM, (h + 1) * HEAD_DIM)
        o_ref[:, sl] = (acc_ref[h] / l_ref[h]).T.astype(o_ref.dtype)


def _attn_scratch(nh, t):
    return [pltpu.VMEM((nh, 1, t), F32), pltpu.VMEM((nh, 1, t), F32), pltpu.VMEM((nh, HEAD_DIM, t), F32)]


def _dsa_attn_kernel(qs_ref, ks_ref, q_ref, k_ref, vt_ref, b_ref, o_ref, m_ref, l_ref, acc_ref, *, scale):
    s_id = pl.program_id(0)
    qi, kj = qs_ref[s_id], ks_ref[s_id]
    t = q_ref.shape[0]

    @pl.when(kj == 0)
    def _():
        _attn_init(m_ref, l_ref, acc_ref)

    bias_t = jnp.concatenate([b_ref[u, 0] for u in range(b_ref.shape[0])], axis=1).astype(F32)
    for h in range(DSA_HEADS):
        sl = slice(h * HEAD_DIM, (h + 1) * HEAD_DIM)
        s_t = _dot_nt(k_ref[:, sl], q_ref[:, sl]) * scale + bias_t
        _softmax_step(h, s_t, vt_ref[sl, :], m_ref, l_ref, acc_ref)

    @pl.when(kj == qi)
    def _():
        _attn_finish(o_ref, l_ref, acc_ref, DSA_HEADS)


def _dsa_attention(qk, vals_t, bias4, qcol, kcol, vrow):
    seq, width = qk.shape[0], DSA_HEADS * HEAD_DIM
    t = min(ATT_T, seq)
    nq = seq // t
    qs, ks = _tri_steps(nq)
    sub = t // IDX_TQ
    omap = lambda s, qs, ks: (qs[s], 0)
    qmap = lambda s, qs, ks: (qs[s], qcol)
    kmap = lambda s, qs, ks: (ks[s], kcol)
    vmap = lambda s, qs, ks: (vrow, ks[s])
    vmem = (2 * (3 * t * width * 2 + t * t * 2 + t * width * 2) + t * width * 4 + 28 * t * t * 4)
    return pl.pallas_call(
        functools.partial(_dsa_attn_kernel, scale=HEAD_DIM ** -0.5 * LOG2_E),
        out_shape=jax.ShapeDtypeStruct((seq, width), BF16),
        grid_spec=pltpu.PrefetchScalarGridSpec(
            num_scalar_prefetch=2,
            grid=(int(qs.shape[0]),),
            in_specs=[pl.BlockSpec((t, width), qmap), pl.BlockSpec((t, width), kmap),
                      pl.BlockSpec((width, t), vmap),
                      pl.BlockSpec((sub, 1, IDX_CK, IDX_TQ), lambda s, qs, ks: (qs[s], ks[s], 0, 0))],
            out_specs=pl.BlockSpec((t, width), omap),
            scratch_shapes=_attn_scratch(DSA_HEADS, t)),
        compiler_params=_cparams(("arbitrary",), vmem),
        name="dsa_attention",
    )(qs, ks, qk, qk, vals_t, bias4)


def _kmean_kernel(k_ref, o_ref):
    o_ref[0] = jnp.mean(k_ref[...].astype(F32), axis=0, keepdims=True)


def _block_means(qk, kcol):
    seq, width = qk.shape[0], MOBA_HEADS * HEAD_DIM
    nblk = seq // MOBA_BLOCK
    out = pl.pallas_call(
        _kmean_kernel,
        out_shape=jax.ShapeDtypeStruct((nblk, 1, width), F32),
        grid=(nblk,),
        in_specs=[pl.BlockSpec((MOBA_BLOCK, width), lambda i: (i, kcol))],
        out_specs=pl.BlockSpec((1, 1, width), lambda i: (i, 0, 0)),
        compiler_params=_cparams(("parallel",), 8 * MOBA_BLOCK * width * 4),
        name="moba_block_means",
    )(qk)
    return out.reshape(nblk, width)


def _moba_attn_kernel(qs_ref, ks_ref, q_ref, k_ref, vt_ref, km_ref, o_ref, m_ref, l_ref, acc_ref, sel_ref,
                      *, scale, n_top):
    s_id = pl.program_id(0)
    qi, kj = qs_ref[s_id], ks_ref[s_id]
    t = q_ref.shape[0]
    nblk = km_ref.shape[0]
    per_tile = t // MOBA_BLOCK

    @pl.when(kj == 0)
    def _():
        _attn_init(m_ref, l_ref, acc_ref)
        blk = lax.broadcasted_iota(I32, (nblk, t), 0)
        own = (qi * t + lax.broadcasted_iota(I32, (nblk, t), 1)) // MOBA_BLOCK
        blk_f = blk.astype(F32)
        for h in range(MOBA_HEADS):
            sl = slice(h * HEAD_DIM, (h + 1) * HEAD_DIM)
            g = _dot_nt(km_ref[:, sl].astype(BF16), q_ref[:, sl])
            g = jnp.where(blk < own, g, NEG_INF)
            sel = jnp.full((nblk, t), NEG_INF, F32)
            for _ in range(n_top):
                mx = jnp.max(g, axis=0, keepdims=True)
                is_max = jnp.logical_and(g == mx, mx > NEG_INF)
                first = jnp.min(jnp.where(is_max, blk_f, float(nblk)), axis=0, keepdims=True)
                pick = blk_f == first
                sel = jnp.where(pick, 0.0, sel)
                g = jnp.where(pick, NEG_INF, g)
            sel_ref[h] = sel

    def block_bias(h):
        rows = [jnp.broadcast_to(sel_ref[h, pl.ds(kj * per_tile + b, 1), :], (MOBA_BLOCK, t))
                for b in range(per_tile)]
        return jnp.concatenate(rows, axis=0)

    def run(diag):
        if diag:
            key = lax.broadcasted_iota(I32, (t, t), 0)
            qry = lax.broadcasted_iota(I32, (t, t), 1)
            own_blk = (key // MOBA_BLOCK) == (qry // MOBA_BLOCK)
            causal = jnp.where(key <= qry, 0.0, NEG_INF)
        for h in range(MOBA_HEADS):
            sl = slice(h * HEAD_DIM, (h + 1) * HEAD_DIM)
            bias_t = block_bias(h)
            if diag:
                bias_t = jnp.where(own_blk, causal, bias_t)
            s_t = _dot_nt(k_ref[:, sl], q_ref[:, sl]) * scale + bias_t
            _softmax_step(h, s_t, vt_ref[sl, :], m_ref, l_ref, acc_ref)

    @pl.when(kj < qi)
    def _():
        run(False)

    @pl.when(kj == qi)
    def _():
        run(True)
        _attn_finish(o_ref, l_ref, acc_ref, MOBA_HEADS)


def _moba_attention(qk, vals_t, kmean, qcol, kcol, vrow):
    seq, width = qk.shape[0], MOBA_HEADS * HEAD_DIM
    t = min(ATT_T, seq)
    nq = seq // t
    nblk = kmean.shape[0]
    qs, ks = _tri_steps(nq)
    omap = lambda s, qs, ks: (qs[s], 0)
    qmap = lambda s, qs, ks: (qs[s], qcol)
    kmap = lambda s, qs, ks: (ks[s], kcol)
    vmap = lambda s, qs, ks: (vrow, ks[s])
    vmem = (2 * (4 * t * width * 2) + nblk * width * 4 + t * width * 4 + MOBA_HEADS * nblk * t * 4
            + 28 * t * t * 4)
    return pl.pallas_call(
        functools.partial(_moba_attn_kernel, scale=HEAD_DIM ** -0.5 * LOG2_E, n_top=min(MOBA_TOPK, nblk)),
        out_shape=jax.ShapeDtypeStruct((seq, width), BF16),
        grid_spec=pltpu.PrefetchScalarGridSpec(
            num_scalar_prefetch=2,
            grid=(int(qs.shape[0]),),
            in_specs=[pl.BlockSpec((t, width), qmap), pl.BlockSpec((t, width), kmap),
                      pl.BlockSpec((width, t), vmap),
                      _resident((nblk, width), lambda s, qs, ks: (0, 0))],
            out_specs=pl.BlockSpec((t, width), omap),
            scratch_shapes=_attn_scratch(MOBA_HEADS, t) + [pltpu.VMEM((MOBA_HEADS, nblk, t), F32)]),
        compiler_params=_cparams(("arbitrary",), vmem),
        name="moba_attention",
    )(qs, ks, qk, qk, vals_t, kmean)


def _post_kernel(oa_ref, ob_ref, ga_ref, gb_ref, x_ref, wa_ref, wb_ref, wo_ref, gf_ref, wr_ref, br_ref,
                 x1_ref, h2_ref, ti_ref, tg_ref):
    ta = jnp.dot(oa_ref[...], wa_ref[...], preferred_element_type=F32)
    tb = jnp.dot(ob_ref[...], wb_ref[...], preferred_element_type=F32)
    merged = _sigmoid(ga_ref[...]) * ta + _sigmoid(gb_ref[...]) * tb
    x1 = x_ref[...] + jnp.dot(merged.astype(BF16), wo_ref[...], preferred_element_type=F32)
    x1_ref[...] = x1
    h2 = _rms(x1, gf_ref[...])
    h2_ref[...] = h2
    logits = jnp.dot(h2.astype(BF16), wr_ref[...], preferred_element_type=F32) + br_ref[...]
    lane = lax.broadcasted_iota(I32, logits.shape, 1)
    lane_f = lane.astype(F32)
    idx_out = jnp.zeros(logits.shape, I32)
    val_out = jnp.zeros(logits.shape, F32)
    top = None
    for r in range(EXPERT_TOPK):
        mx = jnp.max(logits, axis=-1, keepdims=True)
        ix = jnp.min(jnp.where(logits == mx, lane_f, float(LANES)), axis=-1, keepdims=True).astype(I32)
        if top is None:
            top = mx
        idx_out = jnp.where(lane == r, ix, idx_out)
        val_out = jnp.where(lane == r, jnp.exp(mx - top), val_out)
        logits = jnp.where(lane == ix, NEG_INF, logits)
    ti_ref[...] = idx_out
    tg_ref[...] = val_out / jnp.sum(val_out, axis=-1, keepdims=True)


def _post_attention(oa, ob, gates_ab, x2, wa, wb, wo, gf, wr, br):
    seq, d = x2.shape
    w = oa.shape[1]
    tm = min(POST_TM, seq)
    row = lambda i: (i, 0)
    fix = lambda i: (0, 0)
    vmem = (2 * (2 * tm * w * 2 + 3 * tm * d * 4 + 2 * tm * d * 4 + 2 * tm * LANES * 4)
            + 2 * w * d * 2 + d * d * 2 + d * LANES * 2 + 8 * tm * d * 4)
    return pl.pallas_call(
        _post_kernel,
        out_shape=(jax.ShapeDtypeStruct((seq, d), F32), jax.ShapeDtypeStruct((seq, d), F32),
                   jax.ShapeDtypeStruct((seq, LANES), I32), jax.ShapeDtypeStruct((seq, LANES), F32)),
        grid=(seq // tm,),
        in_specs=[pl.BlockSpec((tm, w), row), pl.BlockSpec((tm, w), row),
                  pl.BlockSpec((tm, d), row), pl.BlockSpec((tm, d), lambda i: (i, 1)), pl.BlockSpec((tm, d), row),
                  _resident((w, d), fix), _resident((w, d), fix), _resident((d, d), fix),
                  _resident((1, d), fix), _resident((d, LANES), fix), _resident((1, LANES), fix)],
        out_specs=(pl.BlockSpec((tm, d), row), pl.BlockSpec((tm, d), row),
                   pl.BlockSpec((tm, LANES), row), pl.BlockSpec((tm, LANES), row)),
        compiler_params=_cparams(("parallel",), vmem),
        name="merge_outproj_router",
    )(oa, ob, gates_ab, gates_ab, x2, wa, wb, wo, gf, wr, br)


def _split_even_odd(w):
    grp = 2 * LANES
    r = lax.broadcasted_iota(I32, (grp, grp), 0)
    c = lax.broadcasted_iota(I32, (grp, grp), 1)
    src = jnp.where(c < LANES, 2 * c, 2 * (c - LANES) + 1)
    sel = jnp.where(r == src, 1.0, 0.0).astype(BF16)
    parts = [jnp.dot(w[:, k * grp:(k + 1) * grp], sel, preferred_element_type=F32).astype(BF16)
             for k in range(w.shape[1] // grp)]
    even = jnp.concatenate([p[:, :LANES] for p in parts], axis=1)
    odd = jnp.concatenate([p[:, LANES:] for p in parts], axis=1)
    return even, odd


def _moe_kernel(ge_ref, sub0_ref, nt_ref, ng_ref, tok_hbm, dst_hbm, h_hbm, wgu_ref, bg_ref, bl_ref, wd_ref, bd_ref,
                y_hbm, tok_ref, dst_ref, stage_ref, xall_ref, acc_ref, isem, gsem, ssem):
    g, j = pl.program_id(0), pl.program_id(1)
    n_grp_max, nf = pl.num_programs(0), pl.num_programs(1)
    ng = ng_ref[0]
    ns, ts = acc_ref.shape[0], tok_ref.shape[3]
    sub = stage_ref.shape[2]
    d = stage_ref.shape[3]
    cur = lax.rem(g, 2)
    nxt = 1 - cur
    nt_cur = nt_ref[g]
    nt_next = nt_ref[jnp.minimum(g + 1, n_grp_max - 1)]

    def on_slot(dyn_slot, fn):
        for s in range(2):
            pl.when(dyn_slot == s)(functools.partial(fn, s))

    def idx_copies(grp, s):
        first = sub0_ref[grp]
        return ([pltpu.make_async_copy(tok_hbm.at[first + r], tok_ref.at[s, r], isem.at[s]) for r in range(ns)]
                + [pltpu.make_async_copy(dst_hbm.at[first + r], dst_ref.at[s, r], isem.at[s]) for r in range(ns)])

    def idx_start(grp, s):
        for cp in idx_copies(grp, s):
            cp.start()

    def idx_wait(s):
        for cp in idx_copies(0, s):
            cp.wait()

    def row_loop(issue):
        def body(i, carry):
            for k in range(sub):
                issue(i, k)
            return carry
        lax.fori_loop(0, ts // sub, body, 0)

    def gather_start(s, r):
        st = r % 2

        def issue(i, k):
            tok = tok_ref[s, r, 0, i * sub + k]
            pltpu.make_async_copy(h_hbm.at[pl.ds(tok, 1), :], stage_ref.at[st, i, pl.ds(k, 1), :],
                                  gsem.at[st]).start()
        row_loop(issue)

    def gather_finish(r, xslot):
        st = r % 2
        pltpu.make_async_copy(stage_ref.at[st], stage_ref.at[st], gsem.at[st]).wait()
        xall_ref[xslot, r] = stage_ref[st].reshape(ts, d).astype(BF16)

    def scatter_start(s, r):
        def issue(i, k):
            dst = dst_ref[s, r, 0, i * sub + k]
            pltpu.make_async_copy(acc_ref.at[r, i, pl.ds(k, 1), :], y_hbm.at[pl.ds(dst, 1), :], ssem.at[r]).start()
        row_loop(issue)

    def scatter_wait_one(r):
        pltpu.make_async_copy(acc_ref.at[r], acc_ref.at[r], ssem.at[r]).wait()

    def scatter_wait(n_sub):
        for r in range(ns):
            pl.when(r < n_sub)(functools.partial(scatter_wait_one, r))

    @pl.when(jnp.logical_and(g == 0, j == 0))
    def _():
        idx_start(0, 0)
        acc_ref[...] = jnp.zeros(acc_ref.shape, F32)
        spill0 = y_hbm.shape[0] - ns * ts
        for r in range(ns):
            def fill(i, carry):
                row = pl.multiple_of(spill0 + r * ts + i * sub, sub)
                pltpu.make_async_copy(acc_ref.at[r, i], y_hbm.at[pl.ds(row, sub), :], ssem.at[r]).start()
                return carry
            lax.fori_loop(0, ts // sub, fill, 0)
        scatter_wait(ns)
        idx_wait(0)
        for r in range(ns):
            @pl.when(r < nt_ref[0])
            def _():
                gather_start(0, r)
                gather_finish(r, 0)

        @pl.when(1 < ng)
        def _():
            idx_start(1, 1)

    nt_prev = nt_ref[jnp.maximum(g - 1, 0)]

    @pl.when(jnp.logical_and(j == 0, g == ng))
    def _():
        scatter_wait(nt_prev)

    @pl.when(jnp.logical_and(j == 0, jnp.logical_and(g >= 1, g < ng)))
    def _():
        for r in range(ns):
            pl.when(jnp.logical_and(r < nt_prev, r >= nt_cur))(functools.partial(scatter_wait_one, r))

    @pl.when(g + 1 < ng)
    def _():
        def prefetch(s):
            @pl.when(j == 0)
            def _():
                idx_wait(s)
                gather_start(s, 0)
            for step in range(1, ns + 1):
                @pl.when(jnp.logical_and(j == step, step - 1 < nt_next))
                def _():
                    gather_finish(step - 1, nxt)
                if step < ns:
                    @pl.when(jnp.logical_and(j == step, step < nt_next))
                    def _():
                        gather_start(s, step)
        on_slot(nxt, prefetch)

    @pl.when(g < ng)
    def _():
        wg, wl = _split_even_odd(wgu_ref[0].astype(BF16))
        wd = wd_ref[0].astype(BF16)

        def rows(r0, n):
            @pl.when(j == 0)
            def _():
                for r in range(r0, r0 + n):
                    pl.when(jnp.logical_and(g >= 1, r < nt_prev))(functools.partial(scatter_wait_one, r))
                    acc_ref[r] = jnp.broadcast_to(bd_ref[0], (ts // sub, sub, d))

            x = xall_ref[cur, r0:r0 + n].reshape(n * ts, d)
            gate = jnp.dot(x, wg, preferred_element_type=F32) + bg_ref[0]
            lin = jnp.dot(x, wl, preferred_element_type=F32) + bl_ref[0]
            gate = jnp.minimum(gate, SWIGLU_LIMIT)
            lin = jnp.clip(lin, -SWIGLU_LIMIT, SWIGLU_LIMIT)
            hid = (lin + 1.0) * (gate * _sigmoid(gate * SWIGLU_ALPHA))
            y = jnp.dot(hid.astype(BF16), wd, preferred_element_type=F32)
            acc_ref[r0:r0 + n] += y.reshape(n, ts // sub, sub, d)

        for r0 in range(0, ns, 2):
            pl.when(r0 + 1 < nt_cur)(functools.partial(rows, r0, 2))
            pl.when(r0 + 1 == nt_cur)(functools.partial(rows, r0, 1))

    @pl.when(jnp.logical_and(j == nf - 1, g < ng))
    def _():
        def finish(s):
            for r in range(ns):
                @pl.when(r < nt_cur)
                def _():
                    scatter_start(s, r)

            @pl.when(g + 2 < ng)
            def _():
                idx_start(g + 2, s)
        on_slot(cur, finish)

        @pl.when(g == n_grp_max - 1)
        def _():
            scatter_wait(nt_cur)


def _moe_experts(h2, route, w_gate_up, bg, bl, w_down, bd):
    sub_tok, sub_dst, grp_expert, grp_sub0, grp_nt, n_grp, n_out_rows = route
    d = h2.shape[1]
    f = w_down.shape[1]
    ns, ts = MOE_NSUB, sub_tok.shape[2]
    n_grp_max = grp_expert.shape[0]
    tf = min(MOE_TF, f)
    nf = f // tf
    assert nf >= ns + 1, "one sub-tile of the next group is gathered per hidden-tile step"

    def grp(g, ng):
        return jnp.minimum(g, ng[0] - 1)

    def ftile(g, j, ng):
        return jnp.where(g < ng[0], j, nf - 1)

    w_map = lambda g, j, ge, s0, nt, ng: (ge[grp(g, ng)], 0, ftile(g, j, ng))
    vmem = (2 * (d * 2 * tf * 4 + tf * d * 4 + 2 * tf * 4 + d * 4) + 2 * ts * d * 4 + 2 * ns * ts * d * 2
            + ns * ts * d * 4 + d * 2 * tf * (2 + 4 + 2) + tf * d * 2 + 3 * ts * d * 4)
    any_spec = pl.BlockSpec(memory_space=pl.ANY)
    return pl.pallas_call(
        _moe_kernel,
        out_shape=jax.ShapeDtypeStruct((n_out_rows, d), F32),
        grid_spec=pltpu.PrefetchScalarGridSpec(
            num_scalar_prefetch=4,
            grid=(n_grp_max, nf),
            in_specs=[any_spec, any_spec, any_spec,
                      pl.BlockSpec((1, d, 2 * tf), w_map),
                      pl.BlockSpec((1, 1, tf), w_map),
                      pl.BlockSpec((1, 1, tf), w_map),
                      pl.BlockSpec((1, tf, d), lambda g, j, ge, s0, nt, ng: (ge[grp(g, ng)], ftile(g, j, ng), 0)),
                      pl.BlockSpec((1, 1, d), lambda g, j, ge, s0, nt, ng: (ge[grp(g, ng)], 0, 0))],
            out_specs=any_spec,
            scratch_shapes=[pltpu.SMEM((2, ns, 1, ts), I32), pltpu.SMEM((2, ns, 1, ts), I32),
                            pltpu.VMEM((2, ts // MOE_ROW_UNROLL, MOE_ROW_UNROLL, d), F32),
                            pltpu.VMEM((2, ns, ts, d), BF16),
                            pltpu.VMEM((ns, ts // MOE_ROW_UNROLL, MOE_ROW_UNROLL, d), F32),
                            pltpu.SemaphoreType.DMA((2,)), pltpu.SemaphoreType.DMA((2,)),
                            pltpu.SemaphoreType.DMA((ns,))]),
        compiler_params=_cparams(("arbitrary", "arbitrary"), vmem),
        name="moe_experts",
    )(grp_expert, grp_sub0, grp_nt, n_grp, sub_tok, sub_dst, h2, w_gate_up, bg, bl, w_down, bd)


def _route(top_idx):
    ts, ns = MOE_TS, MOE_NSUB
    n_tok = top_idx.shape[0]
    n_slots = n_tok * EXPERT_TOPK
    e_flat = top_idx.reshape(-1)
    order = jnp.argsort(e_flat).astype(I32)
    counts = jnp.bincount(e_flat, length=N_EXPERTS).astype(I32)
    start = jnp.cumsum(counts) - counts
    nsub = (counts + ts - 1) // ts
    sub_end = jnp.cumsum(nsub)
    sub_start = sub_end - nsub
    n_sub_max = n_slots // ts + N_EXPERTS
    last = N_EXPERTS - 1
    owner = lambda ends, idx: jnp.minimum(jnp.sum(ends[None, :] <= idx[:, None], axis=1), last).astype(I32)
    sub_expert = owner(sub_end, jnp.arange(n_sub_max))
    rows = jnp.arange(n_sub_max * ts, dtype=I32)
    e_row = jnp.repeat(sub_expert, ts)
    within = rows - ts * sub_start[e_row]
    valid = within < counts[e_row]
    slot = order[jnp.clip(start[e_row] + within, 0, n_slots - 1)]
    row_tok = jnp.where(valid, slot // EXPERT_TOPK, 0)
    row_dst = jnp.where(valid, (slot % EXPERT_TOPK) * n_tok + slot // EXPERT_TOPK, n_slots + rows % (ts * ns))
    tail = jnp.zeros((ns, 1, ts), I32)
    sub_tok = jnp.concatenate([row_tok.reshape(n_sub_max, 1, ts).astype(I32), tail])
    sub_dst = jnp.concatenate([row_dst.reshape(n_sub_max, 1, ts).astype(I32), tail + n_slots])
    ngrp = (nsub + ns - 1) // ns
    grp_end = jnp.cumsum(ngrp)
    grp_start = grp_end - ngrp
    n_grp = grp_end[-1]
    gidx = jnp.arange(N_EXPERTS + n_slots // (ts * ns), dtype=I32)
    used = gidx < n_grp
    grp_expert = owner(grp_end, gidx)
    k = gidx - grp_start[grp_expert]
    grp_sub0 = jnp.where(used, sub_start[grp_expert] + k * ns, 0).astype(I32)
    grp_nt = jnp.where(used, jnp.clip(nsub[grp_expert] - k * ns, 0, ns), 0).astype(I32)
    return sub_tok, sub_dst, grp_expert, grp_sub0, grp_nt, n_grp.astype(I32).reshape(1), n_slots + ts * ns


def _final_kernel(x1_ref, *rest):
    y_refs = rest[:EXPERT_TOPK]
    tg_ref, p_ref, wpp_ref, wpg_ref, gp_ref, gfin_ref, o_ref = rest[EXPERT_TOPK:]
    x2 = x1_ref[...]
    gates = tg_ref[...]
    for r in range(EXPERT_TOPK):
        x2 = x2 + y_refs[r][...] * gates[:, r:r + 1]
    ple = jnp.dot(p_ref[...].astype(BF16), wpp_ref[...], preferred_element_type=F32)
    gate = _sigmoid(jnp.dot(x2.astype(BF16), wpg_ref[...], preferred_element_type=F32))
    x3 = x2 + _rms(gate * ple, gp_ref[...])
    o_ref[...] = _rms(x3, gfin_ref[...])


def _final(x1, y, gates, p2, wpp, wpg, gp, gfin):
    seq, d = x1.shape
    pd = p2.shape[1]
    tm = min(FIN_TM, seq)
    row = lambda i: (i, 0)
    fix = lambda i: (0, 0)
    vmem = (2 * (2 * tm * d * 4 + EXPERT_TOPK * tm * d * 4 + tm * LANES * 4 + tm * pd * 4)
            + pd * d * 2 + d * d * 2 + 8 * tm * d * 4)
    return pl.pallas_call(
        _final_kernel,
        out_shape=jax.ShapeDtypeStruct((seq, d), F32),
        grid=(seq // tm,),
        in_specs=[pl.BlockSpec((tm, d), row),
                  *[pl.BlockSpec((tm, d), functools.partial(lambda i, r: (r * (seq // tm) + i, 0), r=r))
                    for r in range(EXPERT_TOPK)],
                  pl.BlockSpec((tm, LANES), row), pl.BlockSpec((tm, pd), row),
                  _resident((pd, d), fix), _resident((d, d), fix),
                  _resident((1, d), fix), _resident((1, d), fix)],
        out_specs=pl.BlockSpec((tm, d), row),
        compiler_params=_cparams(("parallel",), vmem),
        name="combine_ple_norm",
    )(x1, *([y] * EXPERT_TOPK), gates, p2, wpp, wpg, gp, gfin)


def _layer(x2, p2, g_mix, w_in, g_idx_k, b_idx_k, w_br_a, w_br_b, w_out, g_ffn, w_router, b_router,
           w_gate_up, b_gate_up, w_down, b_down, w_ple_proj, w_ple_gate, g_ple, g_final):
    seq, d = x2.shape
    dsa_w, moba_w = DSA_HEADS * HEAD_DIM, MOBA_HEADS * HEAD_DIM
    sizes = (dsa_w, dsa_w, dsa_w, IDX_HEADS * IDX_DIM, IDX_DIM, IDX_HEADS, moba_w, moba_w, moba_w, d, d)
    off = np.concatenate([[0], np.cumsum(sizes)])
    col = lambda k: w_in[:, off[k]:off[k + 1]]
    row1 = lambda v: v.reshape(1, -1)
    g_mix2 = row1(g_mix)

    w_rope = jnp.concatenate([col(0), col(1), col(6), col(7)], axis=1).astype(BF16)
    w_qi = col(3).astype(BF16)
    w_kiwi = jnp.zeros((d, 2 * LANES), F32).at[:, :IDX_DIM].set(col(4)).at[:, LANES:LANES + IDX_HEADS].set(col(5))
    w_val_t = jnp.concatenate([col(2), col(8)], axis=1).T.astype(BF16)
    w_gate = jnp.concatenate([col(9), col(10)], axis=1).astype(BF16)

    rope_tabs = _rope_tables(seq, ROPE_DIM, HEAD_DIM)
    idx_tabs = _rope_tables(seq, IDX_ROPE_DIM, IDX_DIM)
    qk = _project(x2, g_mix2, w_rope, BF16, rope_tabs, ROPE_DIM // 2)
    qi = _project(x2, g_mix2, w_qi, BF16, idx_tabs, IDX_ROPE_DIM // 2)
    pad = lambda v: jnp.zeros((1, LANES), F32).at[0, :IDX_DIM].set(v)
    ki, wi = _project_kiwi(x2, g_mix2, w_kiwi.astype(BF16), pad(g_idx_k), pad(b_idx_k), idx_tabs,
                           (IDX_HEADS ** -0.5) * (IDX_DIM ** -0.5))
    vals_t = _project(x2, g_mix2, w_val_t, BF16, transposed=True)
    gates_ab = _project(x2, g_mix2, w_gate, F32)

    bias4 = _dsa_mask(qi, ki, wi.T, min(DSA_TOPK, seq // 4))
    o_a = _dsa_attention(qk, vals_t, bias4, 0, 1, 0)
    o_b = _moba_attention(qk, vals_t, _block_means(qk, 3), 2, 3, 1)

    w_r = jnp.zeros((d, LANES), F32).at[:, :N_EXPERTS].set(w_router).astype(BF16)
    b_r = jnp.full((1, LANES), NEG_INF, F32).at[0, :N_EXPERTS].set(b_router)
    x1, h2, top_idx, top_gate = _post_attention(
        o_a, o_b, gates_ab, x2, w_br_a.astype(BF16), w_br_b.astype(BF16), w_out.astype(BF16),
        row1(g_ffn), w_r, b_r)

    bias_row = lambda b: b.reshape(N_EXPERTS, 1, -1)
    y = _moe_experts(h2, _route(top_idx[:, :EXPERT_TOPK]), w_gate_up,
                     bias_row(b_gate_up[:, 0::2]), bias_row(b_gate_up[:, 1::2]), w_down, bias_row(b_down))

    return _final(x1, y, top_gate, p2, w_ple_proj.astype(BF16), w_ple_gate.astype(BF16), row1(g_ple),
                  row1(g_final))


def kernel(x, p, g_mix, w_in, g_idx_k, b_idx_k, w_br_a, w_br_b, w_out, g_ffn, w_router, b_router,
           w_gate_up, b_gate_up, w_down, b_down, w_ple_proj, w_ple_gate, g_ple, g_final):
    batch, seq, d = x.shape
    depth = w_in.shape[0]
    assert batch == 1 and depth == 1, "kernel handles the single-sequence, single-layer block"
    assert seq % ATT_T == 0 and seq % PROJ_TM == 0
    out = _layer(x[0], p[0, 0], g_mix[0], w_in[0], g_idx_k[0], b_idx_k[0], w_br_a[0], w_br_b[0], w_out[0],
                 g_ffn[0], w_router[0], b_router[0], w_gate_up[0], b_gate_up[0], w_down[0], b_down[0],
                 w_ple_proj[0], w_ple_gate[0], g_ple[0], g_final)
    return out[None]
```

```python
import functools

import numpy as np
import jax
import jax.numpy as jnp
from jax import lax
from jax.experimental import pallas as pl
from jax.experimental.pallas import tpu as pltpu

F32 = jnp.float32
BF16 = jnp.bfloat16
I32 = jnp.int32
NEG_INF = float("-inf")
LOG2_E = 1.4426950408889634

HEAD_DIM = 128
DSA_HEADS = 8
MOBA_HEADS = 8
ROPE_DIM = HEAD_DIM // 4
ROPE_THETA = 500000.0
IDX_HEADS = 16
IDX_DIM = 64
IDX_ROPE_DIM = IDX_DIM // 4
DSA_TOPK = 256
MOBA_BLOCK = 256
MOBA_TOPK = 3
N_EXPERTS = 32
EXPERT_TOPK = 4
SWIGLU_LIMIT = 7.0
SWIGLU_ALPHA = 1.702
EPS = 1e-6

LANES = 128
VMEM_CAP_BYTES = 60000 * 1024

PROJ_TM = 1024
PROJ_TN = 512
IDX_TQ = 256
IDX_CK = 512
COUNT_ROWS = 32
ATT_T = 512
POST_TM = 256
MOE_TS = 256
MOE_NSUB = 6
MOE_TF = 256
MOE_ROW_UNROLL = 8
FIN_TM = 256

INT_MIN = -2 ** 31
KEY_NEG_INF = 0x807FFFFF - 2 ** 32


def _cparams(sem, vmem_bytes):
    return pltpu.CompilerParams(dimension_semantics=sem,
                                vmem_limit_bytes=int(min(vmem_bytes, VMEM_CAP_BYTES)))


def _resident(shape, index_map):
    return pl.BlockSpec(shape, index_map, pipeline_mode=pl.Buffered(1))


def _sigmoid(x):
    return 1.0 / (1.0 + jnp.exp(-x))


def _rms(xf, g):
    ms = jnp.mean(xf * xf, axis=-1, keepdims=True)
    return xf * lax.rsqrt(ms + EPS) * g


def _dot_nt(a, b):
    return lax.dot_general(a, b, (((1,), (1,)), ((), ())), preferred_element_type=F32)


def _rope_tables(seq, rot_dim, period):
    half = rot_dim // 2
    inv = 1.0 / (ROPE_THETA ** (jnp.arange(half, dtype=F32) / half))
    ang = jnp.arange(seq).astype(F32)[:, None] * inv[None, :]
    cos, sin = jnp.cos(ang), jnp.sin(ang)
    z = lambda n: jnp.zeros((seq, n), F32)
    c = jnp.concatenate([cos, cos, jnp.ones((seq, period - rot_dim), F32)], axis=-1)
    s1 = jnp.concatenate([-sin, z(period - half)], axis=-1)
    s2 = jnp.concatenate([z(half), sin, z(period - rot_dim)], axis=-1)
    rep = LANES // period
    return tuple(jnp.tile(t, (1, rep)) for t in (c, s1, s2))


def _rope(a, c, s1, s2, half):
    return a * c + pltpu.roll(a, LANES - half, 1) * s1 + pltpu.roll(a, half, 1) * s2


def _proj_kernel(x_ref, g_ref, w_ref, *rest, half, transposed):
    if half is None:
        o_ref, h_ref = rest
    else:
        c_ref, s1_ref, s2_ref, cs_ref, o_ref, h_ref = rest

    @pl.when(pl.program_id(1) == 0)
    def _():
        h_ref[...] = _rms(x_ref[...], g_ref[...]).astype(BF16)

    if transposed:
        o_ref[...] = _dot_nt(w_ref[...], h_ref[...]).astype(o_ref.dtype)
        return
    acc = jnp.dot(h_ref[...], w_ref[...], preferred_element_type=F32)
    if half is None:
        o_ref[...] = acc.astype(o_ref.dtype)
    else:
        c, s1, s2 = c_ref[...], s1_ref[...], s2_ref[...]
        for k in range(acc.shape[1] // LANES):
            sl = slice(k * LANES, (k + 1) * LANES)
            o_ref[:, sl] = (_rope(acc[:, sl], c, s1, s2, half) * cs_ref[:, sl]).astype(o_ref.dtype)


def _project(x2, g, w, out_dtype, tables=None, half=None, col_scale=None, transposed=False):
    seq, d = x2.shape
    n = w.shape[0] if transposed else w.shape[1]
    tm, tn = min(PROJ_TM, seq), min(PROJ_TN, n)
    in_specs = [pl.BlockSpec((tm, d), lambda i, j: (i, 0)),
                pl.BlockSpec((1, d), lambda i, j: (0, 0)),
                pl.BlockSpec((tn, d), lambda i, j: (j, 0)) if transposed
                else pl.BlockSpec((d, tn), lambda i, j: (0, j))]
    args = [x2, g, w]
    if tables is not None:
        in_specs += [pl.BlockSpec((tm, LANES), lambda i, j: (i, 0))] * 3 + [pl.BlockSpec((1, tn), lambda i, j: (0, j))]
        args += list(tables) + [jnp.ones((1, n), F32) if col_scale is None else col_scale]
    vmem = 2 * (tm * d * 4 + d * tn * 2 + tm * tn * 4 + 3 * tm * LANES * 4) + tm * d * 2 + 4 * tm * tn * 4
    return pl.pallas_call(
        functools.partial(_proj_kernel, half=half, transposed=transposed),
        out_shape=jax.ShapeDtypeStruct((n, seq) if transposed else (seq, n), out_dtype),
        grid=(seq // tm, n // tn),
        in_specs=in_specs,
        out_specs=pl.BlockSpec((tn, tm), lambda i, j: (j, i)) if transposed
        else pl.BlockSpec((tm, tn), lambda i, j: (i, j)),
        scratch_shapes=[pltpu.VMEM((tm, d), BF16)],
        compiler_params=_cparams(("parallel", "arbitrary"), vmem),
        name="proj_rope" if half is not None else "proj_plain",
    )(*args)


def _proj_kiwi_kernel(x_ref, g_ref, w_ref, lg_ref, lb_ref, c_ref, s1_ref, s2_ref, ki_ref, wi_ref, *, scale):
    h = _rms(x_ref[...], g_ref[...]).astype(BF16)
    acc = jnp.dot(h, w_ref[...], preferred_element_type=F32)
    a = acc[:, :LANES]
    valid = lax.broadcasted_iota(I32, a.shape, 1) < IDX_DIM
    mu = jnp.sum(jnp.where(valid, a, 0.0), axis=-1, keepdims=True) / IDX_DIM
    dlt = jnp.where(valid, a - mu, 0.0)
    var = jnp.sum(dlt * dlt, axis=-1, keepdims=True) / IDX_DIM
    y = dlt * lax.rsqrt(var + EPS) * lg_ref[...] + lb_ref[...]
    ki_ref[...] = _rope(y, c_ref[...], s1_ref[...], s2_ref[...], IDX_ROPE_DIM // 2).astype(BF16)
    wi_ref[...] = acc[:, LANES:LANES + IDX_HEADS] * scale


def _project_kiwi(x2, g, w_kiwi, lg, lb, tables, scale):
    seq, d = x2.shape
    tm = min(PROJ_TM, seq)
    row = lambda i: (i, 0)
    fix = lambda i: (0, 0)
    vmem = 2 * (tm * d * 4 + d * 2 * LANES * 2 + 5 * tm * LANES * 4) + 8 * tm * 2 * LANES * 4 + tm * d * 6
    return pl.pallas_call(
        functools.partial(_proj_kiwi_kernel, scale=scale),
        out_shape=(jax.ShapeDtypeStruct((seq, LANES), BF16), jax.ShapeDtypeStruct((seq, IDX_HEADS), F32)),
        grid=(seq // tm,),
        in_specs=[pl.BlockSpec((tm, d), row), pl.BlockSpec((1, d), fix), pl.BlockSpec((d, 2 * LANES), fix),
                  pl.BlockSpec((1, LANES), fix), pl.BlockSpec((1, LANES), fix),
                  pl.BlockSpec((tm, LANES), row), pl.BlockSpec((tm, LANES), row), pl.BlockSpec((tm, LANES), row)],
        out_specs=(pl.BlockSpec((tm, LANES), row), pl.BlockSpec((tm, IDX_HEADS), row)),
        compiler_params=_cparams(("parallel",), vmem),
        name="proj_kiwi",
    )(x2, g, w_kiwi, lg, lb, *tables)


def _key_to_float(key):
    bits = key ^ ((key >> 31) & 0x7FFFFFFF)
    return lax.bitcast_convert_type(bits, F32)


def _indexer_kernel(qi_ref, ki_ref, wt_ref, out_ref, s_ref, qh_ref, cst_ref, *, n_sel, idx_bits):
    tq, ck = IDX_TQ, IDX_CK
    n_chunks_total = out_ref.shape[1]
    i = pl.program_id(0)
    nc = ((i + 1) * tq + ck - 1) // ck
    qidx = i * tq + lax.broadcasted_iota(I32, (ck, tq), 1)

    def key_idx(c):
        return c * ck + lax.broadcasted_iota(I32, (ck, tq), 0)

    lane = lax.broadcasted_iota(I32, (tq, LANES), 1)
    low = lane < IDX_DIM
    for p in range(IDX_HEADS // 2):
        pair = qi_ref[:, p * LANES:(p + 1) * LANES].astype(F32)
        qh_ref[2 * p] = jnp.where(low, pair, 0.0).astype(BF16)
        qh_ref[2 * p + 1] = jnp.where(low, pltpu.roll(pair, IDX_DIM, 1), 0.0).astype(BF16)

    def score_chunk(c, carry):
        kc = ki_ref[pl.ds(pl.multiple_of(c * ck, ck), ck), :]
        acc = jnp.zeros((ck, tq), F32)
        for h in range(IDX_HEADS):
            acc = acc + jnp.maximum(_dot_nt(kc, qh_ref[h]), 0.0) * wt_ref[h:h + 1, :]
        s_ref[c] = jnp.where(key_idx(c) <= qidx, acc, NEG_INF)
        return carry

    lax.fori_loop(0, nc, score_chunk, 0)

    def count(pred):
        def body(c, acc):
            hit = jnp.where(pred(s_ref[c], lambda: key_idx(c)), 1.0, 0.0)
            return acc + jnp.sum(hit.reshape(ck // COUNT_ROWS, COUNT_ROWS, tq), axis=0)
        acc = lax.fori_loop(0, nc, body, jnp.zeros((COUNT_ROWS, tq), F32))
        return jnp.sum(acc, axis=0, keepdims=True)

    def bit_step(b, carry):
        prefix, cnt_at = carry
        cand = prefix + lax.shift_left(jnp.int32(1), 31 - b)
        cand_f = _key_to_float(cand)
        cnt = count(lambda sc, key: sc >= cand_f)
        take = cnt >= n_sel
        return jnp.where(take, cand, prefix), jnp.where(take, cnt, cnt_at)

    prefix, cnt_at = lax.fori_loop(
        0, 32, bit_step, (jnp.full((1, tq), INT_MIN, I32), jnp.full((1, tq), float(n_sel), F32)))
    tau = jnp.where(prefix < KEY_NEG_INF, NEG_INF, _key_to_float(prefix))

    need = jnp.logical_and(cnt_at > n_sel, tau > NEG_INF)
    any_tie = jnp.max(jnp.where(need, 1.0, 0.0)) > 0.0
    idx_all = jnp.full((1, tq), 2 ** 30, I32)

    @pl.when(any_tie)
    def _():
        rem = n_sel - count(lambda sc, key: sc > tau)
        cut = jnp.zeros((1, tq), I32)
        for b in range(idx_bits - 1, -1, -1):
            cand = cut + (1 << b)
            below = count(lambda sc, key: jnp.logical_and(sc == tau, key() < cand))
            cut = jnp.where(below < rem, cand, cut)
        cst_ref[...] = jnp.where(need, cut, idx_all)

    @pl.when(jnp.logical_not(any_tie))
    def _():
        cst_ref[...] = idx_all

    cut = cst_ref[...]

    def emit(c, carry):
        sc = s_ref[c]
        key = key_idx(c)
        tie = jnp.where(key <= cut, 0.0, NEG_INF)
        b = jnp.where(sc > tau, 0.0, jnp.where(sc == tau, tie, NEG_INF))
        out_ref[0, c] = jnp.where(key <= qidx, b, NEG_INF).astype(BF16)
        return carry

    lax.fori_loop(0, nc, emit, 0)

    def fill(c, carry):
        out_ref[0, c] = jnp.full((ck, tq), NEG_INF, BF16)
        return carry

    lax.fori_loop(nc, n_chunks_total, fill, 0)


def _dsa_mask(qi, ki, wi_t, n_sel):
    seq = qi.shape[0]
    tq, ck = min(IDX_TQ, seq), IDX_CK
    nq, nchunk = seq // tq, seq // ck
    idx_bits = max(1, int(seq - 1).bit_length())
    vmem = (2 * (tq * qi.shape[1] * 2 + IDX_HEADS * tq * 4 + nchunk * tq * ck * 2) + seq * LANES * 2
            + nchunk * tq * ck * 4 + IDX_HEADS * tq * LANES * 2 + 12 * tq * ck * 4)
    return pl.pallas_call(
        functools.partial(_indexer_kernel, n_sel=n_sel, idx_bits=idx_bits),
        out_shape=jax.ShapeDtypeStruct((nq, nchunk, ck, tq), BF16),
        grid=(nq,),
        in_specs=[pl.BlockSpec((tq, qi.shape[1]), lambda i: (i, 0)),
                  _resident((seq, LANES), lambda i: (0, 0)),
                  pl.BlockSpec((IDX_HEADS, tq), lambda i: (0, i))],
        out_specs=pl.BlockSpec((1, nchunk, ck, tq), lambda i: (i, 0, 0, 0)),
        scratch_shapes=[pltpu.VMEM((nchunk, ck, tq), F32),
                        pltpu.VMEM((IDX_HEADS, tq, LANES), BF16),
                        pltpu.VMEM((1, tq), I32)],
        compiler_params=_cparams(("parallel",), vmem),
        name="dsa_indexer",
    )(qi, ki, wi_t)


def _tri_steps(n):
    qs = [i for i in range(n) for _ in range(i + 1)]
    ks = [j for i in range(n) for j in range(i + 1)]
    return jnp.asarray(qs, I32), jnp.asarray(ks, I32)


def _softmax_step(h, s_t, vt_h, m_ref, l_ref, acc_ref):
    m_prev = m_ref[h]
    m_new = jnp.maximum(m_prev, jnp.max(s_t, axis=0, keepdims=True))
    m_safe = jnp.where(m_new == NEG_INF, 0.0, m_new)
    alpha = jnp.exp2(m_prev - m_safe)
    p_t = jnp.exp2(s_t - m_safe)
    l_ref[h] = alpha * l_ref[h] + jnp.sum(p_t, axis=0, keepdims=True)
    acc_ref[h] = alpha * acc_ref[h] + jnp.dot(vt_h, p_t.astype(BF16), preferred_element_type=F32)
    m_ref[h] = m_new


def _attn_init(m_ref, l_ref, acc_ref):
    m_ref[...] = jnp.full(m_ref.shape, NEG_INF, F32)
    l_ref[...] = jnp.zeros(l_ref.shape, F32)
    acc_ref[...] = jnp.zeros(acc_ref.shape, F32)


def _attn_finish(o_ref, l_ref, acc_ref, nh):
    for h in range(nh):
        sl = slice(h * HEAD_DIM, (h + 1) * HEAD_DIM)
        o_ref[:, sl] = (acc_ref[h] / l_ref[h]).T.astype(o_ref.dtype)


def _attn_scratch(nh, t):
    return [pltpu.VMEM((nh, 1, t), F32), pltpu.VMEM((nh, 1, t), F32), pltpu.VMEM((nh, HEAD_DIM, t), F32)]


def _dsa_attn_kernel(qs_ref, ks_ref, q_ref, k_ref, vt_ref, b_ref, o_ref, m_ref, l_ref, acc_ref):
    s_id = pl.program_id(0)
    qi, kj = qs_ref[s_id], ks_ref[s_id]
    t = q_ref.shape[0]

    @pl.when(kj == 0)
    def _():
        _attn_init(m_ref, l_ref, acc_ref)

    bias_t = jnp.concatenate([b_ref[u, 0] for u in range(b_ref.shape[0])], axis=1).astype(F32)
    for h in range(DSA_HEADS):
        sl = slice(h * HEAD_DIM, (h + 1) * HEAD_DIM)
        s_t = _dot_nt(k_ref[:, sl], q_ref[:, sl]) + bias_t
        _softmax_step(h, s_t, vt_ref[sl, :], m_ref, l_ref, acc_ref)

    @pl.when(kj == qi)
    def _():
        _attn_finish(o_ref, l_ref, acc_ref, DSA_HEADS)


def _dsa_attention(qk, vals_t, bias4, qcol, kcol, vrow):
    seq, width = qk.shape[0], DSA_HEADS * HEAD_DIM
    t = min(ATT_T, seq)
    nq = seq // t
    qs, ks = _tri_steps(nq)
    sub = t // IDX_TQ
    omap = lambda s, qs, ks: (qs[s], 0)
    qmap = lambda s, qs, ks: (qs[s], qcol)
    kmap = lambda s, qs, ks: (ks[s], kcol)
    vmap = lambda s, qs, ks: (vrow, ks[s])
    vmem = (2 * (3 * t * width * 2 + t * t * 2 + t * width * 2) + t * width * 4 + 28 * t * t * 4)
    return pl.pallas_call(
        _dsa_attn_kernel,
        out_shape=jax.ShapeDtypeStruct((seq, width), BF16),
        grid_spec=pltpu.PrefetchScalarGridSpec(
            num_scalar_prefetch=2,
            grid=(int(qs.shape[0]),),
            in_specs=[pl.BlockSpec((t, width), qmap), pl.BlockSpec((t, width), kmap),
                      pl.BlockSpec((width, t), vmap),
                      pl.BlockSpec((sub, 1, IDX_CK, IDX_TQ), lambda s, qs, ks: (qs[s], ks[s], 0, 0))],
            out_specs=pl.BlockSpec((t, width), omap),
            scratch_shapes=_attn_scratch(DSA_HEADS, t)),
        compiler_params=_cparams(("arbitrary",), vmem),
        name="dsa_attention",
    )(qs, ks, qk, qk, vals_t, bias4)


def _kmean_kernel(k_ref, o_ref):
    o_ref[0] = jnp.mean(k_ref[...].astype(F32), axis=0, keepdims=True)


def _block_means(qk, kcol):
    seq, width = qk.shape[0], MOBA_HEADS * HEAD_DIM
    nblk = seq // MOBA_BLOCK
    out = pl.pallas_call(
        _kmean_kernel,
        out_shape=jax.ShapeDtypeStruct((nblk, 1, width), F32),
        grid=(nblk,),
        in_specs=[pl.BlockSpec((MOBA_BLOCK, width), lambda i: (i, kcol))],
        out_specs=pl.BlockSpec((1, 1, width), lambda i: (i, 0, 0)),
        compiler_params=_cparams(("parallel",), 8 * MOBA_BLOCK * width * 4),
        name="moba_block_means",
    )(qk)
    return out.reshape(nblk, width)


def _moba_attn_kernel(qs_ref, ks_ref, q_ref, k_ref, vt_ref, km_ref, o_ref, m_ref, l_ref, acc_ref, sel_ref,
                      *, n_top):
    s_id = pl.program_id(0)
    qi, kj = qs_ref[s_id], ks_ref[s_id]
    t = q_ref.shape[0]
    nblk = km_ref.shape[0]
    per_tile = t // MOBA_BLOCK

    @pl.when(kj == 0)
    def _():
        _attn_init(m_ref, l_ref, acc_ref)
        blk = lax.broadcasted_iota(I32, (nblk, t), 0)
        own = (qi * t + lax.broadcasted_iota(I32, (nblk, t), 1)) // MOBA_BLOCK
        blk_f = blk.astype(F32)
        for h in range(MOBA_HEADS):
            sl = slice(h * HEAD_DIM, (h + 1) * HEAD_DIM)
            g = _dot_nt(km_ref[:, sl].astype(BF16), q_ref[:, sl])
            g = jnp.where(blk < own, g, NEG_INF)
            sel = jnp.full((nblk, t), NEG_INF, F32)
            for _ in range(n_top):
                mx = jnp.max(g, axis=0, keepdims=True)
                is_max = jnp.logical_and(g == mx, mx > NEG_INF)
                first = jnp.min(jnp.where(is_max, blk_f, float(nblk)), axis=0, keepdims=True)
                pick = blk_f == first
                sel = jnp.where(pick, 0.0, sel)
                g = jnp.where(pick, NEG_INF, g)
            sel_ref[h] = sel

    def block_bias(h):
        rows = [jnp.broadcast_to(sel_ref[h, pl.ds(kj * per_tile + b, 1), :], (MOBA_BLOCK, t))
                for b in range(per_tile)]
        return jnp.concatenate(rows, axis=0)

    def run(diag):
        if diag:
            key = lax.broadcasted_iota(I32, (t, t), 0)
            qry = lax.broadcasted_iota(I32, (t, t), 1)
            own_blk = (key // MOBA_BLOCK) == (qry // MOBA_BLOCK)
            causal = jnp.where(key <= qry, 0.0, NEG_INF)
        for h in range(MOBA_HEADS):
            sl = slice(h * HEAD_DIM, (h + 1) * HEAD_DIM)
            bias_t = block_bias(h)
            if diag:
                bias_t = jnp.where(own_blk, causal, bias_t)
            s_t = _dot_nt(k_ref[:, sl], q_ref[:, sl]) + bias_t
            _softmax_step(h, s_t, vt_ref[sl, :], m_ref, l_ref, acc_ref)

    @pl.when(kj < qi)
    def _():
        run(False)

    @pl.when(kj == qi)
    def _():
        run(True)
        _attn_finish(o_ref, l_ref, acc_ref, MOBA_HEADS)


def _moba_attention(qk, vals_t, kmean, qcol, kcol, vrow):
    seq, width = qk.shape[0], MOBA_HEADS * HEAD_DIM
    t = min(ATT_T, seq)
    nq = seq // t
    nblk = kmean.shape[0]
    qs, ks = _tri_steps(nq)
    omap = lambda s, qs, ks: (qs[s], 0)
    qmap = lambda s, qs, ks: (qs[s], qcol)
    kmap = lambda s, qs, ks: (ks[s], kcol)
    vmap = lambda s, qs, ks: (vrow, ks[s])
    vmem = (2 * (4 * t * width * 2) + nblk * width * 4 + t * width * 4 + MOBA_HEADS * nblk * t * 4
            + 28 * t * t * 4)
    return pl.pallas_call(
        functools.partial(_moba_attn_kernel, n_top=min(MOBA_TOPK, nblk)),
        out_shape=jax.ShapeDtypeStruct((seq, width), BF16),
        grid_spec=pltpu.PrefetchScalarGridSpec(
            num_scalar_prefetch=2,
            grid=(int(qs.shape[0]),),
            in_specs=[pl.BlockSpec((t, width), qmap), pl.BlockSpec((t, width), kmap),
                      pl.BlockSpec((width, t), vmap),
                      _resident((nblk, width), lambda s, qs, ks: (0, 0))],
            out_specs=pl.BlockSpec((t, width), omap),
            scratch_shapes=_attn_scratch(MOBA_HEADS, t) + [pltpu.VMEM((MOBA_HEADS, nblk, t), F32)]),
        compiler_params=_cparams(("arbitrary",), vmem),
        name="moba_attention",
    )(qs, ks, qk, qk, vals_t, kmean)


def _post_kernel(oa_ref, ob_ref, ga_ref, gb_ref, x_ref, wa_ref, wb_ref, wo_ref, gf_ref, wr_ref, br_ref,
                 x1_ref, h2_ref, ti_ref, tg_ref):
    ta = jnp.dot(oa_ref[...], wa_ref[...], preferred_element_type=F32)
    tb = jnp.dot(ob_ref[...], wb_ref[...], preferred_element_type=F32)
    merged = _sigmoid(ga_ref[...]) * ta + _sigmoid(gb_ref[...]) * tb
    x1 = x_ref[...] + jnp.dot(merged.astype(BF16), wo_ref[...], preferred_element_type=F32)
    x1_ref[...] = x1
    h2 = _rms(x1, gf_ref[...])
    h2_ref[...] = h2
    logits = jnp.dot(h2.astype(BF16), wr_ref[...], preferred_element_type=F32) + br_ref[...]
    lane = lax.broadcasted_iota(I32, logits.shape, 1)
    lane_f = lane.astype(F32)
    idx_out = jnp.zeros(logits.shape, I32)
    val_out = jnp.zeros(logits.shape, F32)
    top = None
    for r in range(EXPERT_TOPK):
        mx = jnp.max(logits, axis=-1, keepdims=True)
        ix = jnp.min(jnp.where(logits == mx, lane_f, float(LANES)), axis=-1, keepdims=True).astype(I32)
        if top is None:
            top = mx
        idx_out = jnp.where(lane == r, ix, idx_out)
        val_out = jnp.where(lane == r, jnp.exp(mx - top), val_out)
        logits = jnp.where(lane == ix, NEG_INF, logits)
    ti_ref[...] = idx_out
    tg_ref[...] = val_out / jnp.sum(val_out, axis=-1, keepdims=True)


def _post_attention(oa, ob, gates_ab, x2, wa, wb, wo, gf, wr, br):
    seq, d = x2.shape
    w = oa.shape[1]
    tm = min(POST_TM, seq)
    row = lambda i: (i, 0)
    fix = lambda i: (0, 0)
    vmem = (2 * (2 * tm * w * 2 + 3 * tm * d * 4 + 2 * tm * d * 4 + 2 * tm * LANES * 4)
            + 2 * w * d * 2 + d * d * 2 + d * LANES * 2 + 8 * tm * d * 4)
    return pl.pallas_call(
        _post_kernel,
        out_shape=(jax.ShapeDtypeStruct((seq, d), F32), jax.ShapeDtypeStruct((seq, d), F32),
                   jax.ShapeDtypeStruct((seq, LANES), I32), jax.ShapeDtypeStruct((seq, LANES), F32)),
        grid=(seq // tm,),
        in_specs=[pl.BlockSpec((tm, w), row), pl.BlockSpec((tm, w), row),
                  pl.BlockSpec((tm, d), row), pl.BlockSpec((tm, d), lambda i: (i, 1)), pl.BlockSpec((tm, d), row),
                  _resident((w, d), fix), _resident((w, d), fix), _resident((d, d), fix),
                  _resident((1, d), fix), _resident((d, LANES), fix), _resident((1, LANES), fix)],
        out_specs=(pl.BlockSpec((tm, d), row), pl.BlockSpec((tm, d), row),
                   pl.BlockSpec((tm, LANES), row), pl.BlockSpec((tm, LANES), row)),
        compiler_params=_cparams(("parallel",), vmem),
        name="merge_outproj_router",
    )(oa, ob, gates_ab, gates_ab, x2, wa, wb, wo, gf, wr, br)


def _split_even_odd(w):
    grp = 2 * LANES
    r = lax.broadcasted_iota(I32, (grp, grp), 0)
    c = lax.broadcasted_iota(I32, (grp, grp), 1)
    src = jnp.where(c < LANES, 2 * c, 2 * (c - LANES) + 1)
    sel = jnp.where(r == src, 1.0, 0.0).astype(BF16)
    parts = [jnp.dot(w[:, k * grp:(k + 1) * grp], sel, preferred_element_type=F32).astype(BF16)
             for k in range(w.shape[1] // grp)]
    even = jnp.concatenate([p[:, :LANES] for p in parts], axis=1)
    odd = jnp.concatenate([p[:, LANES:] for p in parts], axis=1)
    return even, odd


def _moe_kernel(ge_ref, sub0_ref, nt_ref, ng_ref, tok_hbm, dst_hbm, h_hbm, wgu_ref, bg_ref, bl_ref, wd_ref, bd_ref,
                y_hbm, tok_ref, dst_ref, stage_ref, xall_ref, acc_ref, isem, gsem, ssem):
    g, j = pl.program_id(0), pl.program_id(1)
    n_grp_max, nf = pl.num_programs(0), pl.num_programs(1)
    ng = ng_ref[0]
    ns, ts = acc_ref.shape[0], tok_ref.shape[3]
    sub = stage_ref.shape[2]
    d = stage_ref.shape[3]
    cur = lax.rem(g, 2)
    nxt = 1 - cur
    nt_cur = nt_ref[g]
    nt_next = nt_ref[jnp.minimum(g + 1, n_grp_max - 1)]

    def on_slot(dyn_slot, fn):
        for s in range(2):
            pl.when(dyn_slot == s)(functools.partial(fn, s))

    def idx_copies(grp, s):
        first = sub0_ref[grp]
        return ([pltpu.make_async_copy(tok_hbm.at[first + r], tok_ref.at[s, r], isem.at[s]) for r in range(ns)]
                + [pltpu.make_async_copy(dst_hbm.at[first + r], dst_ref.at[s, r], isem.at[s]) for r in range(ns)])

    def idx_start(grp, s):
        for cp in idx_copies(grp, s):
            cp.start()

    def idx_wait(s):
        for cp in idx_copies(0, s):
            cp.wait()

    def row_loop(issue):
        def body(i, carry):
            for k in range(sub):
                issue(i, k)
            return carry
        lax.fori_loop(0, ts // sub, body, 0)

    def gather_start(s, r):
        st = r % 2

        def issue(i, k):
            tok = tok_ref[s, r, 0, i * sub + k]
            pltpu.make_async_copy(h_hbm.at[pl.ds(tok, 1), :], stage_ref.at[st, i, pl.ds(k, 1), :],
                                  gsem.at[st]).start()
        row_loop(issue)

    def gather_finish(r, xslot):
        st = r % 2
        pltpu.make_async_copy(stage_ref.at[st], stage_ref.at[st], gsem.at[st]).wait()
        xall_ref[xslot, r] = stage_ref[st].reshape(ts, d).astype(BF16)

    def scatter_start(s, r):
        def issue(i, k):
            dst = dst_ref[s, r, 0, i * sub + k]
            pltpu.make_async_copy(acc_ref.at[r, i, pl.ds(k, 1), :], y_hbm.at[pl.ds(dst, 1), :], ssem.at[r]).start()
        row_loop(issue)

    def scatter_wait_one(r):
        pltpu.make_async_copy(acc_ref.at[r], acc_ref.at[r], ssem.at[r]).wait()

    def scatter_wait(n_sub):
        for r in range(ns):
            pl.when(r < n_sub)(functools.partial(scatter_wait_one, r))

    @pl.when(jnp.logical_and(g == 0, j == 0))
    def _():
        idx_start(0, 0)
        acc_ref[...] = jnp.zeros(acc_ref.shape, F32)
        spill0 = y_hbm.shape[0] - ns * ts
        for r in range(ns):
            def fill(i, carry):
                row = pl.multiple_of(spill0 + r * ts + i * sub, sub)
                pltpu.make_async_copy(acc_ref.at[r, i], y_hbm.at[pl.ds(row, sub), :], ssem.at[r]).start()
                return carry
            lax.fori_loop(0, ts // sub, fill, 0)
        scatter_wait(ns)
        idx_wait(0)
        for r in range(ns):
            @pl.when(r < nt_ref[0])
            def _():
                gather_start(0, r)
                gather_finish(r, 0)

        @pl.when(1 < ng)
        def _():
            idx_start(1, 1)

    nt_prev = nt_ref[jnp.maximum(g - 1, 0)]

    @pl.when(jnp.logical_and(j == 0, g == ng))
    def _():
        scatter_wait(nt_prev)

    @pl.when(jnp.logical_and(j == 0, jnp.logical_and(g >= 1, g < ng)))
    def _():
        for r in range(ns):
            pl.when(jnp.logical_and(r < nt_prev, r >= nt_cur))(functools.partial(scatter_wait_one, r))

    @pl.when(g + 1 < ng)
    def _():
        def prefetch(s):
            @pl.when(j == 0)
            def _():
                idx_wait(s)
                gather_start(s, 0)
            for step in range(1, ns + 1):
                @pl.when(jnp.logical_and(j == step, step - 1 < nt_next))
                def _():
                    gather_finish(step - 1, nxt)
                if step < ns:
                    @pl.when(jnp.logical_and(j == step, step < nt_next))
                    def _():
                        gather_start(s, step)
        on_slot(nxt, prefetch)

    @pl.when(g < ng)
    def _():
        wg, wl = _split_even_odd(wgu_ref[0].astype(BF16))
        wd = wd_ref[0].astype(BF16)

        def rows(r0, n):
            @pl.when(j == 0)
            def _():
                for r in range(r0, r0 + n):
                    pl.when(jnp.logical_and(g >= 1, r < nt_prev))(functools.partial(scatter_wait_one, r))
                    acc_ref[r] = jnp.broadcast_to(bd_ref[0], (ts // sub, sub, d))

            x = xall_ref[cur, r0:r0 + n].reshape(n * ts, d)
            gate = jnp.dot(x, wg, preferred_element_type=F32) + bg_ref[0]
            lin = jnp.dot(x, wl, preferred_element_type=F32) + bl_ref[0]
            gate = jnp.minimum(gate, SWIGLU_LIMIT)
            lin = jnp.clip(lin, -SWIGLU_LIMIT, SWIGLU_LIMIT)
            hid = (lin + 1.0) * (gate * _sigmoid(gate * SWIGLU_ALPHA))
            y = jnp.dot(hid.astype(BF16), wd, preferred_element_type=F32)
            acc_ref[r0:r0 + n] += y.reshape(n, ts // sub, sub, d)

        for r0 in range(0, ns, 2):
            pl.when(r0 + 1 < nt_cur)(functools.partial(rows, r0, 2))
            pl.when(r0 + 1 == nt_cur)(functools.partial(rows, r0, 1))

    @pl.when(jnp.logical_and(j == nf - 1, g < ng))
    def _():
        def finish(s):
            for r in range(ns):
                @pl.when(r < nt_cur)
                def _():
                    scatter_start(s, r)

            @pl.when(g + 2 < ng)
            def _():
                idx_start(g + 2, s)
        on_slot(cur, finish)

        @pl.when(g == n_grp_max - 1)
        def _():
            scatter_wait(nt_cur)


def _moe_experts(h2, route, w_gate_up, bg, bl, w_down, bd):
    sub_tok, sub_dst, grp_expert, grp_sub0, grp_nt, n_grp, n_out_rows = route
    d = h2.shape[1]
    f = w_down.shape[1]
    ns, ts = MOE_NSUB, sub_tok.shape[2]
    n_grp_max = grp_expert.shape[0]
    tf = min(MOE_TF, f)
    nf = f // tf
    assert nf >= ns + 1, "one sub-tile of the next group is gathered per hidden-tile step"

    def grp(g, ng):
        return jnp.minimum(g, ng[0] - 1)

    def ftile(g, j, ng):
        return jnp.where(g < ng[0], j, nf - 1)

    w_map = lambda g, j, ge, s0, nt, ng: (ge[grp(g, ng)], 0, ftile(g, j, ng))
    vmem = (2 * (d * 2 * tf * 4 + tf * d * 4 + 2 * tf * 4 + d * 4) + 2 * ts * d * 4 + 2 * ns * ts * d * 2
            + ns * ts * d * 4 + d * 2 * tf * (2 + 4 + 2) + tf * d * 2 + 3 * ts * d * 4)
    any_spec = pl.BlockSpec(memory_space=pl.ANY)
    return pl.pallas_call(
        _moe_kernel,
        out_shape=jax.ShapeDtypeStruct((n_out_rows, d), F32),
        grid_spec=pltpu.PrefetchScalarGridSpec(
            num_scalar_prefetch=4,
            grid=(n_grp_max, nf),
            in_specs=[any_spec, any_spec, any_spec,
                      pl.BlockSpec((1, d, 2 * tf), w_map),
                      pl.BlockSpec((1, 1, tf), w_map),
                      pl.BlockSpec((1, 1, tf), w_map),
                      pl.BlockSpec((1, tf, d), lambda g, j, ge, s0, nt, ng: (ge[grp(g, ng)], ftile(g, j, ng), 0)),
                      pl.BlockSpec((1, 1, d), lambda g, j, ge, s0, nt, ng: (ge[grp(g, ng)], 0, 0))],
            out_specs=any_spec,
            scratch_shapes=[pltpu.SMEM((2, ns, 1, ts), I32), pltpu.SMEM((2, ns, 1, ts), I32),
                            pltpu.VMEM((2, ts // MOE_ROW_UNROLL, MOE_ROW_UNROLL, d), F32),
                            pltpu.VMEM((2, ns, ts, d), BF16),
                            pltpu.VMEM((ns, ts // MOE_ROW_UNROLL, MOE_ROW_UNROLL, d), F32),
                            pltpu.SemaphoreType.DMA((2,)), pltpu.SemaphoreType.DMA((2,)),
                            pltpu.SemaphoreType.DMA((ns,))]),
        compiler_params=_cparams(("arbitrary", "arbitrary"), vmem),
        name="moe_experts",
    )(grp_expert, grp_sub0, grp_nt, n_grp, sub_tok, sub_dst, h2, w_gate_up, bg, bl, w_down, bd)


def _route(top_idx):
    ts, ns = MOE_TS, MOE_NSUB
    n_tok = top_idx.shape[0]
    n_slots = n_tok * EXPERT_TOPK
    e_flat = top_idx.reshape(-1)
    order = jnp.argsort(e_flat).astype(I32)
    counts = jnp.bincount(e_flat, length=N_EXPERTS).astype(I32)
    start = jnp.cumsum(counts) - counts
    nsub = (counts + ts - 1) // ts
    sub_end = jnp.cumsum(nsub)
    sub_start = sub_end - nsub
    n_sub_max = n_slots // ts + N_EXPERTS
    last = N_EXPERTS - 1
    owner = lambda ends, idx: jnp.minimum(jnp.sum(ends[None, :] <= idx[:, None], axis=1), last).astype(I32)
    sub_expert = owner(sub_end, jnp.arange(n_sub_max))
    rows = jnp.arange(n_sub_max * ts, dtype=I32)
    e_row = jnp.repeat(sub_expert, ts)
    within = rows - ts * sub_start[e_row]
    valid = within < counts[e_row]
    slot = order[jnp.clip(start[e_row] + within, 0, n_slots - 1)]
    row_tok = jnp.where(valid, slot // EXPERT_TOPK, 0)
    row_dst = jnp.where(valid, (slot % EXPERT_TOPK) * n_tok + slot // EXPERT_TOPK, n_slots + rows % (ts * ns))
    tail = jnp.zeros((ns, 1, ts), I32)
    sub_tok = jnp.concatenate([row_tok.reshape(n_sub_max, 1, ts).astype(I32), tail])
    sub_dst = jnp.concatenate([row_dst.reshape(n_sub_max, 1, ts).astype(I32), tail + n_slots])
    ngrp = (nsub + ns - 1) // ns
    grp_end = jnp.cumsum(ngrp)
    grp_start = grp_end - ngrp
    n_grp = grp_end[-1]
    gidx = jnp.arange(N_EXPERTS + n_slots // (ts * ns), dtype=I32)
    used = gidx < n_grp
    grp_expert = owner(grp_end, gidx)
    k = gidx - grp_start[grp_expert]
    grp_sub0 = jnp.where(used, sub_start[grp_expert] + k * ns, 0).astype(I32)
    grp_nt = jnp.where(used, jnp.clip(nsub[grp_expert] - k * ns, 0, ns), 0).astype(I32)
    return sub_tok, sub_dst, grp_expert, grp_sub0, grp_nt, n_grp.astype(I32).reshape(1), n_slots + ts * ns


def _final_kernel(x1_ref, *rest):
    y_refs = rest[:EXPERT_TOPK]
    tg_ref, p_ref, wpp_ref, wpg_ref, gp_ref, gfin_ref, o_ref = rest[EXPERT_TOPK:]
    x2 = x1_ref[...]
    gates = tg_ref[...]
    for r in range(EXPERT_TOPK):
        x2 = x2 + y_refs[r][...] * gates[:, r:r + 1]
    ple = jnp.dot(p_ref[...].astype(BF16), wpp_ref[...], preferred_element_type=F32)
    gate = _sigmoid(jnp.dot(x2.astype(BF16), wpg_ref[...], preferred_element_type=F32))
    x3 = x2 + _rms(gate * ple, gp_ref[...])
    o_ref[...] = _rms(x3, gfin_ref[...])


def _final(x1, y, gates, p2, wpp, wpg, gp, gfin):
    seq, d = x1.shape
    pd = p2.shape[1]
    tm = min(FIN_TM, seq)
    row = lambda i: (i, 0)
    fix = lambda i: (0, 0)
    vmem = (2 * (2 * tm * d * 4 + EXPERT_TOPK * tm * d * 4 + tm * LANES * 4 + tm * pd * 4)
            + pd * d * 2 + d * d * 2 + 8 * tm * d * 4)
    return pl.pallas_call(
        _final_kernel,
        out_shape=jax.ShapeDtypeStruct((seq, d), F32),
        grid=(seq // tm,),
        in_specs=[pl.BlockSpec((tm, d), row),
                  *[pl.BlockSpec((tm, d), functools.partial(lambda i, r: (r * (seq // tm) + i, 0), r=r))
                    for r in range(EXPERT_TOPK)],
                  pl.BlockSpec((tm, LANES), row), pl.BlockSpec((tm, pd), row),
                  _resident((pd, d), fix), _resident((d, d), fix),
                  _resident((1, d), fix), _resident((1, d), fix)],
        out_specs=pl.BlockSpec((tm, d), row),
        compiler_params=_cparams(("parallel",), vmem),
        name="combine_ple_norm",
    )(x1, *([y] * EXPERT_TOPK), gates, p2, wpp, wpg, gp, gfin)


def _layer(x2, p2, g_mix, w_in, g_idx_k, b_idx_k, w_br_a, w_br_b, w_out, g_ffn, w_router, b_router,
           w_gate_up, b_gate_up, w_down, b_down, w_ple_proj, w_ple_gate, g_ple, g_final):
    seq, d = x2.shape
    dsa_w, moba_w = DSA_HEADS * HEAD_DIM, MOBA_HEADS * HEAD_DIM
    sizes = (dsa_w, dsa_w, dsa_w, IDX_HEADS * IDX_DIM, IDX_DIM, IDX_HEADS, moba_w, moba_w, moba_w, d, d)
    off = np.concatenate([[0], np.cumsum(sizes)])
    col = lambda k: w_in[:, off[k]:off[k + 1]]
    row1 = lambda v: v.reshape(1, -1)
    g_mix2 = row1(g_mix)

    w_rope = jnp.concatenate([col(0), col(1), col(6), col(7)], axis=1).astype(BF16)
    w_qi = col(3).astype(BF16)
    w_kiwi = jnp.zeros((d, 2 * LANES), F32).at[:, :IDX_DIM].set(col(4)).at[:, LANES:LANES + IDX_HEADS].set(col(5))
    w_val_t = jnp.concatenate([col(2), col(8)], axis=1).T.astype(BF16)
    w_gate = jnp.concatenate([col(9), col(10)], axis=1).astype(BF16)

    rope_tabs = _rope_tables(seq, ROPE_DIM, HEAD_DIM)
    idx_tabs = _rope_tables(seq, IDX_ROPE_DIM, IDX_DIM)
    q_scale = jnp.full((1, dsa_w), HEAD_DIM ** -0.5 * LOG2_E, F32)
    k_scale = jnp.ones((1, dsa_w), F32)
    qk = _project(x2, g_mix2, w_rope, BF16, rope_tabs, ROPE_DIM // 2,
                  col_scale=jnp.concatenate([q_scale, k_scale, q_scale, k_scale], axis=1))
    qi = _project(x2, g_mix2, w_qi, BF16, idx_tabs, IDX_ROPE_DIM // 2)
    pad = lambda v: jnp.zeros((1, LANES), F32).at[0, :IDX_DIM].set(v)
    ki, wi = _project_kiwi(x2, g_mix2, w_kiwi.astype(BF16), pad(g_idx_k), pad(b_idx_k), idx_tabs,
                           (IDX_HEADS ** -0.5) * (IDX_DIM ** -0.5))
    vals_t = _project(x2, g_mix2, w_val_t, BF16, transposed=True)
    gates_ab = _project(x2, g_mix2, w_gate, F32)

    bias4 = _dsa_mask(qi, ki, wi.T, min(DSA_TOPK, seq // 4))
    o_a = _dsa_attention(qk, vals_t, bias4, 0, 1, 0)
    o_b = _moba_attention(qk, vals_t, _block_means(qk, 3), 2, 3, 1)

    w_r = jnp.zeros((d, LANES), F32).at[:, :N_EXPERTS].set(w_router).astype(BF16)
    b_r = jnp.full((1, LANES), NEG_INF, F32).at[0, :N_EXPERTS].set(b_router)
    x1, h2, top_idx, top_gate = _post_attention(
        o_a, o_b, gates_ab, x2, w_br_a.astype(BF16), w_br_b.astype(BF16), w_out.astype(BF16),
        row1(g_ffn), w_r, b_r)

    bias_row = lambda b: b.reshape(N_EXPERTS, 1, -1)
    y = _moe_experts(h2, _route(top_idx[:, :EXPERT_TOPK]), w_gate_up,
                     bias_row(b_gate_up[:, 0::2]), bias_row(b_gate_up[:, 1::2]), w_down, bias_row(b_down))

    return _final(x1, y, top_gate, p2, w_ple_proj.astype(BF16), w_ple_gate.astype(BF16), row1(g_ple),
                  row1(g_final))


def kernel(x, p, g_mix, w_in, g_idx_k, b_idx_k, w_br_a, w_br_b, w_out, g_ffn, w_router, b_router,
           w_gate_up, b_gate_up, w_down, b_down, w_ple_proj, w_ple_gate, g_ple, g_final):
    batch, seq, d = x.shape
    depth = w_in.shape[0]
    assert batch == 1 and depth == 1, "kernel handles the single-sequence, single-layer block"
    assert seq % ATT_T == 0 and seq % PROJ_TM == 0
    out = _layer(x[0], p[0, 0], g_mix[0], w_in[0], g_idx_k[0], b_idx_k[0], w_br_a[0], w_br_b[0], w_out[0],
                 g_ffn[0], w_router[0], b_router[0], w_gate_up[0], b_gate_up[0], w_down[0], b_down[0],
                 w_ple_proj[0], w_ple_gate[0], g_ple[0], g_final)
    return out[None]
```

```python
import functools

import numpy as np
import jax
import jax.numpy as jnp
from jax import lax
from jax.experimental import pallas as pl
from jax.experimental.pallas import tpu as pltpu

F32 = jnp.float32
BF16 = jnp.bfloat16
I32 = jnp.int32
NEG_INF = float("-inf")
LOG2_E = 1.4426950408889634

HEAD_DIM = 128
DSA_HEADS = 8
MOBA_HEADS = 8
ROPE_DIM = HEAD_DIM // 4
ROPE_THETA = 500000.0
IDX_HEADS = 16
IDX_DIM = 64
IDX_ROPE_DIM = IDX_DIM // 4
DSA_TOPK = 256
MOBA_BLOCK = 256
MOBA_TOPK = 3
N_EXPERTS = 32
EXPERT_TOPK = 4
SWIGLU_LIMIT = 7.0
SWIGLU_ALPHA = 1.702
EPS = 1e-6

LANES = 128
VMEM_CAP_BYTES = 60000 * 1024

PROJ_TM = 1024
PROJ_TN = 512
IDX_TQ = 256
IDX_CK = 512
COUNT_ROWS = 32
ATT_T = 512
POST_TM = 256
MOE_TS = 256
MOE_NSUB = 6
MOE_TF = 256
MOE_ROW_UNROLL = 8
FIN_TM = 256

INT_MIN = -2 ** 31
KEY_NEG_INF = 0x807FFFFF - 2 ** 32


def _cparams(sem, vmem_bytes):
    return pltpu.CompilerParams(dimension_semantics=sem,
                                vmem_limit_bytes=int(min(vmem_bytes, VMEM_CAP_BYTES)))


def _resident(shape, index_map):
    return pl.BlockSpec(shape, index_map, pipeline_mode=pl.Buffered(1))


def _sigmoid(x):
    return 1.0 / (1.0 + jnp.exp(-x))


def _rms(xf, g):
    ms = jnp.mean(xf * xf, axis=-1, keepdims=True)
    return xf * lax.rsqrt(ms + EPS) * g


def _dot_nt(a, b):
    return lax.dot_general(a, b, (((1,), (1,)), ((), ())), preferred_element_type=F32)


def _rope_tables(seq, rot_dim, period):
    half = rot_dim // 2
    inv = 1.0 / (ROPE_THETA ** (jnp.arange(half, dtype=F32) / half))
    ang = jnp.arange(seq).astype(F32)[:, None] * inv[None, :]
    cos, sin = jnp.cos(ang), jnp.sin(ang)
    z = lambda n: jnp.zeros((seq, n), F32)
    c = jnp.concatenate([cos, cos, jnp.ones((seq, period - rot_dim), F32)], axis=-1)
    s1 = jnp.concatenate([-sin, z(period - half)], axis=-1)
    s2 = jnp.concatenate([z(half), sin, z(period - rot_dim)], axis=-1)
    rep = LANES // period
    return tuple(jnp.tile(t, (1, rep)) for t in (c, s1, s2))


def _rope(a, c, s1, s2, half):
    return a * c + pltpu.roll(a, LANES - half, 1) * s1 + pltpu.roll(a, half, 1) * s2


def _proj_kernel(x_ref, g_ref, w_ref, *rest, half, transposed):
    if half is None:
        o_ref, h_ref = rest
    else:
        c_ref, s1_ref, s2_ref, cs_ref, o_ref, h_ref = rest

    @pl.when(pl.program_id(1) == 0)
    def _():
        h_ref[...] = _rms(x_ref[...], g_ref[...]).astype(BF16)

    if transposed:
        o_ref[...] = _dot_nt(w_ref[...], h_ref[...]).astype(o_ref.dtype)
        return
    acc = jnp.dot(h_ref[...], w_ref[...], preferred_element_type=F32)
    if half is None:
        o_ref[...] = acc.astype(o_ref.dtype)
    else:
        c, s1, s2 = c_ref[...], s1_ref[...], s2_ref[...]
        for k in range(acc.shape[1] // LANES):
            sl = slice(k * LANES, (k + 1) * LANES)
            o_ref[:, sl] = (_rope(acc[:, sl], c, s1, s2, half) * cs_ref[:, sl]).astype(o_ref.dtype)


def _project(x2, g, w, out_dtype, tables=None, half=None, col_scale=None, transposed=False):
    seq, d = x2.shape
    n = w.shape[0] if transposed else w.shape[1]
    tm, tn = min(PROJ_TM, seq), min(PROJ_TN, n)
    in_specs = [pl.BlockSpec((tm, d), lambda i, j: (i, 0)),
                pl.BlockSpec((1, d), lambda i, j: (0, 0)),
                pl.BlockSpec((tn, d), lambda i, j: (j, 0)) if transposed
                else pl.BlockSpec((d, tn), lambda i, j: (0, j))]
    args = [x2, g, w]
    if tables is not None:
        in_specs += [pl.BlockSpec((tm, LANES), lambda i, j: (i, 0))] * 3 + [pl.BlockSpec((1, tn), lambda i, j: (0, j))]
        args += list(tables) + [jnp.ones((1, n), F32) if col_scale is None else col_scale]
    vmem = 2 * (tm * d * 4 + d * tn * 2 + tm * tn * 4 + 3 * tm * LANES * 4) + tm * d * 2 + 4 * tm * tn * 4
    return pl.pallas_call(
        functools.partial(_proj_kernel, half=half, transposed=transposed),
        out_shape=jax.ShapeDtypeStruct((n, seq) if transposed else (seq, n), out_dtype),
        grid=(seq // tm, n // tn),
        in_specs=in_specs,
        out_specs=pl.BlockSpec((tn, tm), lambda i, j: (j, i)) if transposed
        else pl.BlockSpec((tm, tn), lambda i, j: (i, j)),
        scratch_shapes=[pltpu.VMEM((tm, d), BF16)],
        compiler_params=_cparams(("parallel", "arbitrary"), vmem),
        name="proj_rope" if half is not None else "proj_plain",
    )(*args)


def _proj_kiwi_kernel(x_ref, g_ref, w_ref, lg_ref, lb_ref, c_ref, s1_ref, s2_ref, ki_ref, wi_ref, *, scale):
    h = _rms(x_ref[...], g_ref[...]).astype(BF16)
    acc = jnp.dot(h, w_ref[...], preferred_element_type=F32)
    a = acc[:, :LANES]
    valid = lax.broadcasted_iota(I32, a.shape, 1) < IDX_DIM
    mu = jnp.sum(jnp.where(valid, a, 0.0), axis=-1, keepdims=True) / IDX_DIM
    dlt = jnp.where(valid, a - mu, 0.0)
    var = jnp.sum(dlt * dlt, axis=-1, keepdims=True) / IDX_DIM
    y = dlt * lax.rsqrt(var + EPS) * lg_ref[...] + lb_ref[...]
    ki_ref[...] = _rope(y, c_ref[...], s1_ref[...], s2_ref[...], IDX_ROPE_DIM // 2).astype(BF16)
    wi_ref[...] = acc[:, LANES:LANES + IDX_HEADS] * scale


def _project_kiwi(x2, g, w_kiwi, lg, lb, tables, scale):
    seq, d = x2.shape
    tm = min(PROJ_TM, seq)
    row = lambda i: (i, 0)
    fix = lambda i: (0, 0)
    vmem = 2 * (tm * d * 4 + d * 2 * LANES * 2 + 5 * tm * LANES * 4) + 8 * tm * 2 * LANES * 4 + tm * d * 6
    return pl.pallas_call(
        functools.partial(_proj_kiwi_kernel, scale=scale),
        out_shape=(jax.ShapeDtypeStruct((seq, LANES), BF16), jax.ShapeDtypeStruct((seq, IDX_HEADS), F32)),
        grid=(seq // tm,),
        in_specs=[pl.BlockSpec((tm, d), row), pl.BlockSpec((1, d), fix), pl.BlockSpec((d, 2 * LANES), fix),
                  pl.BlockSpec((1, LANES), fix), pl.BlockSpec((1, LANES), fix),
                  pl.BlockSpec((tm, LANES), row), pl.BlockSpec((tm, LANES), row), pl.BlockSpec((tm, LANES), row)],
        out_specs=(pl.BlockSpec((tm, LANES), row), pl.BlockSpec((tm, IDX_HEADS), row)),
        compiler_params=_cparams(("parallel",), vmem),
        name="proj_kiwi",
    )(x2, g, w_kiwi, lg, lb, *tables)


def _key_to_float(key):
    bits = key ^ ((key >> 31) & 0x7FFFFFFF)
    return lax.bitcast_convert_type(bits, F32)


def _indexer_kernel(qi_ref, ki_ref, wt_ref, out_ref, s_ref, qh_ref, cst_ref, *, n_sel, idx_bits):
    tq, ck = IDX_TQ, IDX_CK
    n_chunks_total = out_ref.shape[1]
    i = pl.program_id(0)
    nc = ((i + 1) * tq + ck - 1) // ck
    qidx = i * tq + lax.broadcasted_iota(I32, (ck, tq), 1)

    def key_idx(c):
        return c * ck + lax.broadcasted_iota(I32, (ck, tq), 0)

    lane = lax.broadcasted_iota(I32, (tq, LANES), 1)
    low = lane < IDX_DIM
    for p in range(IDX_HEADS // 2):
        pair = qi_ref[:, p * LANES:(p + 1) * LANES].astype(F32)
        qh_ref[2 * p] = jnp.where(low, pair, 0.0).astype(BF16)
        qh_ref[2 * p + 1] = jnp.where(low, pltpu.roll(pair, IDX_DIM, 1), 0.0).astype(BF16)

    def score_chunk(c, carry):
        kc = ki_ref[pl.ds(pl.multiple_of(c * ck, ck), ck), :]
        acc = jnp.zeros((ck, tq), F32)
        for h in range(IDX_HEADS):
            acc = acc + jnp.maximum(_dot_nt(kc, qh_ref[h]), 0.0) * wt_ref[h:h + 1, :]
        s_ref[c] = jnp.where(key_idx(c) <= qidx, acc, NEG_INF)
        return carry

    lax.fori_loop(0, nc, score_chunk, 0)

    def count(pred):
        def body(c, acc):
            hit = jnp.where(pred(s_ref[c], lambda: key_idx(c)), 1.0, 0.0)
            return acc + jnp.sum(hit.reshape(ck // COUNT_ROWS, COUNT_ROWS, tq), axis=0)
        acc = lax.fori_loop(0, nc, body, jnp.zeros((COUNT_ROWS, tq), F32))
        return jnp.sum(acc, axis=0, keepdims=True)

    def bit_step(b, carry):
        prefix, cnt_at = carry
        cand = prefix + lax.shift_left(jnp.int32(1), 31 - b)
        cand_f = _key_to_float(cand)
        cnt = count(lambda sc, key: sc >= cand_f)
        take = cnt >= n_sel
        return jnp.where(take, cand, prefix), jnp.where(take, cnt, cnt_at)

    all_keys = (i * tq + lax.broadcasted_iota(I32, (1, tq), 1)) < n_sel

    def unsettled(cnt_at):
        open_q = jnp.logical_and(cnt_at != n_sel, jnp.logical_not(all_keys))
        return jnp.max(jnp.where(open_q, 1, 0))

    def search_step(carry):
        b, prefix, cnt_at, _ = carry
        prefix, cnt_at = bit_step(b, (prefix, cnt_at))
        return b + 1, prefix, cnt_at, unsettled(cnt_at)

    _, prefix, cnt_at, _ = lax.while_loop(
        lambda c: jnp.logical_and(c[0] < 32, c[3] > 0), search_step,
        (jnp.int32(0), jnp.full((1, tq), INT_MIN, I32), jnp.full((1, tq), 0.0, F32), jnp.int32(1)))
    tau = jnp.where(jnp.logical_or(all_keys, prefix < KEY_NEG_INF), NEG_INF, _key_to_float(prefix))

    need = jnp.logical_and(cnt_at > n_sel, tau > NEG_INF)
    any_tie = jnp.max(jnp.where(need, 1.0, 0.0)) > 0.0
    idx_all = jnp.full((1, tq), 2 ** 30, I32)

    @pl.when(any_tie)
    def _():
        rem = n_sel - count(lambda sc, key: sc > tau)
        cut = jnp.zeros((1, tq), I32)
        for b in range(idx_bits - 1, -1, -1):
            cand = cut + (1 << b)
            below = count(lambda sc, key: jnp.logical_and(sc == tau, key() < cand))
            cut = jnp.where(below < rem, cand, cut)
        cst_ref[...] = jnp.where(need, cut, idx_all)

    @pl.when(jnp.logical_not(any_tie))
    def _():
        cst_ref[...] = idx_all

    cut = cst_ref[...]

    def emit(c, carry):
        sc = s_ref[c]
        key = key_idx(c)
        tie = jnp.where(key <= cut, 0.0, NEG_INF)
        b = jnp.where(sc > tau, 0.0, jnp.where(sc == tau, tie, NEG_INF))
        out_ref[0, c] = jnp.where(key <= qidx, b, NEG_INF).astype(BF16)
        return carry

    lax.fori_loop(0, nc, emit, 0)

    def fill(c, carry):
        out_ref[0, c] = jnp.full((ck, tq), NEG_INF, BF16)
        return carry

    lax.fori_loop(nc, n_chunks_total, fill, 0)


def _dsa_mask(qi, ki, wi_t, n_sel):
    seq = qi.shape[0]
    tq, ck = min(IDX_TQ, seq), IDX_CK
    nq, nchunk = seq // tq, seq // ck
    idx_bits = max(1, int(seq - 1).bit_length())
    vmem = (2 * (tq * qi.shape[1] * 2 + IDX_HEADS * tq * 4 + nchunk * tq * ck * 2) + seq * LANES * 2
            + nchunk * tq * ck * 4 + IDX_HEADS * tq * LANES * 2 + 12 * tq * ck * 4)
    return pl.pallas_call(
        functools.partial(_indexer_kernel, n_sel=n_sel, idx_bits=idx_bits),
        out_shape=jax.ShapeDtypeStruct((nq, nchunk, ck, tq), BF16),
        grid=(nq,),
        in_specs=[pl.BlockSpec((tq, qi.shape[1]), lambda i: (i, 0)),
                  _resident((seq, LANES), lambda i: (0, 0)),
                  pl.BlockSpec((IDX_HEADS, tq), lambda i: (0, i))],
        out_specs=pl.BlockSpec((1, nchunk, ck, tq), lambda i: (i, 0, 0, 0)),
        scratch_shapes=[pltpu.VMEM((nchunk, ck, tq), F32),
                        pltpu.VMEM((IDX_HEADS, tq, LANES), BF16),
                        pltpu.VMEM((1, tq), I32)],
        compiler_params=_cparams(("parallel",), vmem),
        name="dsa_indexer",
    )(qi, ki, wi_t)


def _tri_steps(n):
    qs = [i for i in range(n) for _ in range(i + 1)]
    ks = [j for i in range(n) for j in range(i + 1)]
    return jnp.asarray(qs, I32), jnp.asarray(ks, I32)


def _softmax_step(h, s_t, vt_h, m_ref, l_ref, acc_ref):
    m_prev = m_ref[h]
    m_new = jnp.maximum(m_prev, jnp.max(s_t, axis=0, keepdims=True))
    m_safe = jnp.where(m_new == NEG_INF, 0.0, m_new)
    alpha = jnp.exp2(m_prev - m_safe)
    p_t = jnp.exp2(s_t - m_safe)
    l_ref[h] = alpha * l_ref[h] + jnp.sum(p_t, axis=0, keepdims=True)
    acc_ref[h] = alpha * acc_ref[h] + jnp.dot(vt_h, p_t.astype(BF16), preferred_element_type=F32)
    m_ref[h] = m_new


def _attn_init(m_ref, l_ref, acc_ref):
    m_ref[...] = jnp.full(m_ref.shape, NEG_INF, F32)
    l_ref[...] = jnp.zeros(l_ref.shape, F32)
    acc_ref[...] = jnp.zeros(acc_ref.shape, F32)


def _attn_finish(o_ref, l_ref, acc_ref, nh):
    for h in range(nh):
        sl = slice(h * HEAD_DIM, (h + 1) * HEAD_DIM)
        o_ref[:, sl] = (acc_ref[h] / l_ref[h]).T.astype(o_ref.dtype)


def _attn_scratch(nh, t):
    return [pltpu.VMEM((nh, 1, t), F32), pltpu.VMEM((nh, 1, t), F32), pltpu.VMEM((nh, HEAD_DIM, t), F32)]


def _dsa_attn_kernel(qs_ref, ks_ref, q_ref, k_ref, vt_ref, b_ref, o_ref, m_ref, l_ref, acc_ref):
    s_id = pl.program_id(0)
    qi, kj = qs_ref[s_id], ks_ref[s_id]
    t = q_ref.shape[0]

    @pl.when(kj == 0)
    def _():
        _attn_init(m_ref, l_ref, acc_ref)

    bias_t = jnp.concatenate([b_ref[u, 0] for u in range(b_ref.shape[0])], axis=1).astype(F32)
    for h in range(DSA_HEADS):
        sl = slice(h * HEAD_DIM, (h + 1) * HEAD_DIM)
        s_t = _dot_nt(k_ref[:, sl], q_ref[:, sl]) + bias_t
        _softmax_step(h, s_t, vt_ref[sl, :], m_ref, l_ref, acc_ref)

    @pl.when(kj == qi)
    def _():
        _attn_finish(o_ref, l_ref, acc_ref, DSA_HEADS)


def _dsa_attention(qk, vals_t, bias4, qcol, kcol, vrow):
    seq, width = qk.shape[0], DSA_HEADS * HEAD_DIM
    t = min(ATT_T, seq)
    nq = seq // t
    qs, ks = _tri_steps(nq)
    sub = t // IDX_TQ
    omap = lambda s, qs, ks: (qs[s], 0)
    qmap = lambda s, qs, ks: (qs[s], qcol)
    kmap = lambda s, qs, ks: (ks[s], kcol)
    vmap = lambda s, qs, ks: (vrow, ks[s])
    vmem = (2 * (3 * t * width * 2 + t * t * 2 + t * width * 2) + t * width * 4 + 28 * t * t * 4)
    return pl.pallas_call(
        _dsa_attn_kernel,
        out_shape=jax.ShapeDtypeStruct((seq, width), BF16),
        grid_spec=pltpu.PrefetchScalarGridSpec(
            num_scalar_prefetch=2,
            grid=(int(qs.shape[0]),),
            in_specs=[pl.BlockSpec((t, width), qmap), pl.BlockSpec((t, width), kmap),
                      pl.BlockSpec((width, t), vmap),
                      pl.BlockSpec((sub, 1, IDX_CK, IDX_TQ), lambda s, qs, ks: (qs[s], ks[s], 0, 0))],
            out_specs=pl.BlockSpec((t, width), omap),
            scratch_shapes=_attn_scratch(DSA_HEADS, t)),
        compiler_params=_cparams(("arbitrary",), vmem),
        name="dsa_attention",
    )(qs, ks, qk, qk, vals_t, bias4)


def _kmean_kernel(k_ref, o_ref):
    o_ref[0] = jnp.mean(k_ref[...].astype(F32), axis=0, keepdims=True)


def _block_means(qk, kcol):
    seq, width = qk.shape[0], MOBA_HEADS * HEAD_DIM
    nblk = seq // MOBA_BLOCK
    out = pl.pallas_call(
        _kmean_kernel,
        out_shape=jax.ShapeDtypeStruct((nblk, 1, width), F32),
        grid=(nblk,),
        in_specs=[pl.BlockSpec((MOBA_BLOCK, width), lambda i: (i, kcol))],
        out_specs=pl.BlockSpec((1, 1, width), lambda i: (i, 0, 0)),
        compiler_params=_cparams(("parallel",), 8 * MOBA_BLOCK * width * 4),
        name="moba_block_means",
    )(qk)
    return out.reshape(nblk, width)


def _moba_attn_kernel(qs_ref, ks_ref, q_ref, k_ref, vt_ref, km_ref, o_ref, m_ref, l_ref, acc_ref, sel_ref,
                      *, n_top):
    s_id = pl.program_id(0)
    qi, kj = qs_ref[s_id], ks_ref[s_id]
    t = q_ref.shape[0]
    nblk = km_ref.shape[0]
    per_tile = t // MOBA_BLOCK

    @pl.when(kj == 0)
    def _():
        _attn_init(m_ref, l_ref, acc_ref)
        blk = lax.broadcasted_iota(I32, (nblk, t), 0)
        own = (qi * t + lax.broadcasted_iota(I32, (nblk, t), 1)) // MOBA_BLOCK
        blk_f = blk.astype(F32)
        for h in range(MOBA_HEADS):
            sl = slice(h * HEAD_DIM, (h + 1) * HEAD_DIM)
            g = _dot_nt(km_ref[:, sl].astype(BF16), q_ref[:, sl])
            g = jnp.where(blk < own, g, NEG_INF)
            sel = jnp.full((nblk, t), NEG_INF, F32)
            for _ in range(n_top):
                mx = jnp.max(g, axis=0, keepdims=True)
                is_max = jnp.logical_and(g == mx, mx > NEG_INF)
                first = jnp.min(jnp.where(is_max, blk_f, float(nblk)), axis=0, keepdims=True)
                pick = blk_f == first
                sel = jnp.where(pick, 0.0, sel)
                g = jnp.where(pick, NEG_INF, g)
            sel_ref[h] = sel

    def block_bias(h):
        rows = [jnp.broadcast_to(sel_ref[h, pl.ds(kj * per_tile + b, 1), :], (MOBA_BLOCK, t))
                for b in range(per_tile)]
        return jnp.concatenate(rows, axis=0)

    def run(diag):
        if diag:
            key = lax.broadcasted_iota(I32, (t, t), 0)
            qry = lax.broadcasted_iota(I32, (t, t), 1)
            own_blk = (key // MOBA_BLOCK) == (qry // MOBA_BLOCK)
            causal = jnp.where(key <= qry, 0.0, NEG_INF)
        for h in range(MOBA_HEADS):
            sl = slice(h * HEAD_DIM, (h + 1) * HEAD_DIM)
            bias_t = block_bias(h)
            if diag:
                bias_t = jnp.where(own_blk, causal, bias_t)
            s_t = _dot_nt(k_ref[:, sl], q_ref[:, sl]) + bias_t
            _softmax_step(h, s_t, vt_ref[sl, :], m_ref, l_ref, acc_ref)

    @pl.when(kj < qi)
    def _():
        run(False)

    @pl.when(kj == qi)
    def _():
        run(True)
        _attn_finish(o_ref, l_ref, acc_ref, MOBA_HEADS)


def _moba_attention(qk, vals_t, kmean, qcol, kcol, vrow):
    seq, width = qk.shape[0], MOBA_HEADS * HEAD_DIM
    t = min(ATT_T, seq)
    nq = seq // t
    nblk = kmean.shape[0]
    qs, ks = _tri_steps(nq)
    omap = lambda s, qs, ks: (qs[s], 0)
    qmap = lambda s, qs, ks: (qs[s], qcol)
    kmap = lambda s, qs, ks: (ks[s], kcol)
    vmap = lambda s, qs, ks: (vrow, ks[s])
    vmem = (2 * (4 * t * width * 2) + nblk * width * 4 + t * width * 4 + MOBA_HEADS * nblk * t * 4
            + 28 * t * t * 4)
    return pl.pallas_call(
        functools.partial(_moba_attn_kernel, n_top=min(MOBA_TOPK, nblk)),
        out_shape=jax.ShapeDtypeStruct((seq, width), BF16),
        grid_spec=pltpu.PrefetchScalarGridSpec(
            num_scalar_prefetch=2,
            grid=(int(qs.shape[0]),),
            in_specs=[pl.BlockSpec((t, width), qmap), pl.BlockSpec((t, width), kmap),
                      pl.BlockSpec((width, t), vmap),
                      _resident((nblk, width), lambda s, qs, ks: (0, 0))],
            out_specs=pl.BlockSpec((t, width), omap),
            scratch_shapes=_attn_scratch(MOBA_HEADS, t) + [pltpu.VMEM((MOBA_HEADS, nblk, t), F32)]),
        compiler_params=_cparams(("arbitrary",), vmem),
        name="moba_attention",
    )(qs, ks, qk, qk, vals_t, kmean)


def _post_kernel(oa_ref, ob_ref, ga_ref, gb_ref, x_ref, wa_ref, wb_ref, wo_ref, gf_ref, wr_ref, br_ref,
                 x1_ref, h2_ref, ti_ref, tg_ref):
    ta = jnp.dot(oa_ref[...], wa_ref[...], preferred_element_type=F32)
    tb = jnp.dot(ob_ref[...], wb_ref[...], preferred_element_type=F32)
    merged = _sigmoid(ga_ref[...]) * ta + _sigmoid(gb_ref[...]) * tb
    x1 = x_ref[...] + jnp.dot(merged.astype(BF16), wo_ref[...], preferred_element_type=F32)
    x1_ref[...] = x1
    h2 = _rms(x1, gf_ref[...])
    h2_ref[...] = h2
    logits = jnp.dot(h2.astype(BF16), wr_ref[...], preferred_element_type=F32) + br_ref[...]
    lane = lax.broadcasted_iota(I32, logits.shape, 1)
    lane_f = lane.astype(F32)
    idx_out = jnp.zeros(logits.shape, I32)
    val_out = jnp.zeros(logits.shape, F32)
    top = None
    for r in range(EXPERT_TOPK):
        mx = jnp.max(logits, axis=-1, keepdims=True)
        ix = jnp.min(jnp.where(logits == mx, lane_f, float(LANES)), axis=-1, keepdims=True).astype(I32)
        if top is None:
            top = mx
        idx_out = jnp.where(lane == r, ix, idx_out)
        val_out = jnp.where(lane == r, jnp.exp(mx - top), val_out)
        logits = jnp.where(lane == ix, NEG_INF, logits)
    ti_ref[...] = idx_out
    tg_ref[...] = val_out / jnp.sum(val_out, axis=-1, keepdims=True)


def _post_attention(oa, ob, gates_ab, x2, wa, wb, wo, gf, wr, br):
    seq, d = x2.shape
    w = oa.shape[1]
    tm = min(POST_TM, seq)
    row = lambda i: (i, 0)
    fix = lambda i: (0, 0)
    vmem = (2 * (2 * tm * w * 2 + 3 * tm * d * 4 + 2 * tm * d * 4 + 2 * tm * LANES * 4)
            + 2 * w * d * 2 + d * d * 2 + d * LANES * 2 + 8 * tm * d * 4)
    return pl.pallas_call(
        _post_kernel,
        out_shape=(jax.ShapeDtypeStruct((seq, d), F32), jax.ShapeDtypeStruct((seq, d), F32),
                   jax.ShapeDtypeStruct((seq, LANES), I32), jax.ShapeDtypeStruct((seq, LANES), F32)),
        grid=(seq // tm,),
        in_specs=[pl.BlockSpec((tm, w), row), pl.BlockSpec((tm, w), row),
                  pl.BlockSpec((tm, d), row), pl.BlockSpec((tm, d), lambda i: (i, 1)), pl.BlockSpec((tm, d), row),
                  _resident((w, d), fix), _resident((w, d), fix), _resident((d, d), fix),
                  _resident((1, d), fix), _resident((d, LANES), fix), _resident((1, LANES), fix)],
        out_specs=(pl.BlockSpec((tm, d), row), pl.BlockSpec((tm, d), row),
                   pl.BlockSpec((tm, LANES), row), pl.BlockSpec((tm, LANES), row)),
        compiler_params=_cparams(("parallel",), vmem),
        name="merge_outproj_router",
    )(oa, ob, gates_ab, gates_ab, x2, wa, wb, wo, gf, wr, br)


def _split_even_odd(w):
    grp = 2 * LANES
    r = lax.broadcasted_iota(I32, (grp, grp), 0)
    c = lax.broadcasted_iota(I32, (grp, grp), 1)
    src = jnp.where(c < LANES, 2 * c, 2 * (c - LANES) + 1)
    sel = jnp.where(r == src, 1.0, 0.0).astype(BF16)
    parts = [jnp.dot(w[:, k * grp:(k + 1) * grp], sel, preferred_element_type=F32).astype(BF16)
             for k in range(w.shape[1] // grp)]
    even = jnp.concatenate([p[:, :LANES] for p in parts], axis=1)
    odd = jnp.concatenate([p[:, LANES:] for p in parts], axis=1)
    return even, odd


def _moe_kernel(ge_ref, sub0_ref, nt_ref, ng_ref, tok_hbm, dst_hbm, h_hbm, wgu_ref, bg_ref, bl_ref, wd_ref, bd_ref,
                y_hbm, tok_ref, dst_ref, stage_ref, xall_ref, acc_ref, isem, gsem, ssem):
    g, j = pl.program_id(0), pl.program_id(1)
    n_grp_max, nf = pl.num_programs(0), pl.num_programs(1)
    ng = ng_ref[0]
    ns, ts = acc_ref.shape[0], tok_ref.shape[3]
    sub = stage_ref.shape[2]
    d = stage_ref.shape[3]
    cur = lax.rem(g, 2)
    nxt = 1 - cur
    nt_cur = nt_ref[g]
    nt_next = nt_ref[jnp.minimum(g + 1, n_grp_max - 1)]

    def on_slot(dyn_slot, fn):
        for s in range(2):
            pl.when(dyn_slot == s)(functools.partial(fn, s))

    def idx_copies(grp, s):
        first = sub0_ref[grp]
        return ([pltpu.make_async_copy(tok_hbm.at[first + r], tok_ref.at[s, r], isem.at[s]) for r in range(ns)]
                + [pltpu.make_async_copy(dst_hbm.at[first + r], dst_ref.at[s, r], isem.at[s]) for r in range(ns)])

    def idx_start(grp, s):
        for cp in idx_copies(grp, s):
            cp.start()

    def idx_wait(s):
        for cp in idx_copies(0, s):
            cp.wait()

    def row_loop(issue):
        def body(i, carry):
            for k in range(sub):
                issue(i, k)
            return carry
        lax.fori_loop(0, ts // sub, body, 0)

    def gather_start(s, r):
        st = r % 2

        def issue(i, k):
            tok = tok_ref[s, r, 0, i * sub + k]
            pltpu.make_async_copy(h_hbm.at[pl.ds(tok, 1), :], stage_ref.at[st, i, pl.ds(k, 1), :],
                                  gsem.at[st]).start()
        row_loop(issue)

    def gather_finish(r, xslot):
        st = r % 2
        pltpu.make_async_copy(stage_ref.at[st], stage_ref.at[st], gsem.at[st]).wait()
        xall_ref[xslot, r] = stage_ref[st].reshape(ts, d).astype(BF16)

    def scatter_start(s, r):
        def issue(i, k):
            dst = dst_ref[s, r, 0, i * sub + k]
            pltpu.make_async_copy(acc_ref.at[r, i, pl.ds(k, 1), :], y_hbm.at[pl.ds(dst, 1), :], ssem.at[r]).start()
        row_loop(issue)

    def scatter_wait_one(r):
        pltpu.make_async_copy(acc_ref.at[r], acc_ref.at[r], ssem.at[r]).wait()

    def scatter_wait(n_sub):
        for r in range(ns):
            pl.when(r < n_sub)(functools.partial(scatter_wait_one, r))

    @pl.when(jnp.logical_and(g == 0, j == 0))
    def _():
        idx_start(0, 0)
        acc_ref[...] = jnp.zeros(acc_ref.shape, F32)
        spill0 = y_hbm.shape[0] - ns * ts
        for r in range(ns):
            def fill(i, carry):
                row = pl.multiple_of(spill0 + r * ts + i * sub, sub)
                pltpu.make_async_copy(acc_ref.at[r, i], y_hbm.at[pl.ds(row, sub), :], ssem.at[r]).start()
                return carry
            lax.fori_loop(0, ts // sub, fill, 0)
        scatter_wait(ns)
        idx_wait(0)
        for r in range(ns):
            @pl.when(r < nt_ref[0])
            def _():
                gather_start(0, r)
                gather_finish(r, 0)

        @pl.when(1 < ng)
        def _():
            idx_start(1, 1)

    nt_prev = nt_ref[jnp.maximum(g - 1, 0)]

    @pl.when(jnp.logical_and(j == 0, g == ng))
    def _():
        scatter_wait(nt_prev)

    @pl.when(jnp.logical_and(j == 0, jnp.logical_and(g >= 1, g < ng)))
    def _():
        for r in range(ns):
            pl.when(jnp.logical_and(r < nt_prev, r >= nt_cur))(functools.partial(scatter_wait_one, r))

    @pl.when(g + 1 < ng)
    def _():
        def prefetch(s):
            @pl.when(j == 0)
            def _():
                idx_wait(s)
                gather_start(s, 0)
            for step in range(1, ns + 1):
                @pl.when(jnp.logical_and(j == step, step - 1 < nt_next))
                def _():
                    gather_finish(step - 1, nxt)
                if step < ns:
                    @pl.when(jnp.logical_and(j == step, step < nt_next))
                    def _():
                        gather_start(s, step)
        on_slot(nxt, prefetch)

    @pl.when(g < ng)
    def _():
        wg, wl = _split_even_odd(wgu_ref[0].astype(BF16))
        wd = wd_ref[0].astype(BF16)

        def rows(r0, n):
            @pl.when(j == 0)
            def _():
                for r in range(r0, r0 + n):
                    pl.when(jnp.logical_and(g >= 1, r < nt_prev))(functools.partial(scatter_wait_one, r))
                    acc_ref[r] = jnp.broadcast_to(bd_ref[0], (ts // sub, sub, d))

            x = xall_ref[cur, r0:r0 + n].reshape(n * ts, d)
            gate = jnp.dot(x, wg, preferred_element_type=F32) + bg_ref[0]
            lin = jnp.dot(x, wl, preferred_element_type=F32) + bl_ref[0]
            gate = jnp.minimum(gate, SWIGLU_LIMIT)
            lin = jnp.clip(lin, -SWIGLU_LIMIT, SWIGLU_LIMIT)
            hid = (lin + 1.0) * (gate * _sigmoid(gate * SWIGLU_ALPHA))
            y = jnp.dot(hid.astype(BF16), wd, preferred_element_type=F32)
            acc_ref[r0:r0 + n] += y.reshape(n, ts // sub, sub, d)

        for r0 in range(0, ns, 2):
            pl.when(r0 + 1 < nt_cur)(functools.partial(rows, r0, 2))
            pl.when(r0 + 1 == nt_cur)(functools.partial(rows, r0, 1))

    @pl.when(jnp.logical_and(j == nf - 1, g < ng))
    def _():
        def finish(s):
            for r in range(ns):
                @pl.when(r < nt_cur)
                def _():
                    scatter_start(s, r)

            @pl.when(g + 2 < ng)
            def _():
                idx_start(g + 2, s)
        on_slot(cur, finish)

        @pl.when(g == n_grp_max - 1)
        def _():
            scatter_wait(nt_cur)


def _moe_experts(h2, route, w_gate_up, bg, bl, w_down, bd):
    sub_tok, sub_dst, grp_expert, grp_sub0, grp_nt, n_grp, n_out_rows = route
    d = h2.shape[1]
    f = w_down.shape[1]
    ns, ts = MOE_NSUB, sub_tok.shape[2]
    n_grp_max = grp_expert.shape[0]
    tf = min(MOE_TF, f)
    nf = f // tf
    assert nf >= ns + 1, "one sub-tile of the next group is gathered per hidden-tile step"

    def grp(g, ng):
        return jnp.minimum(g, ng[0] - 1)

    def ftile(g, j, ng):
        return jnp.where(g < ng[0], j, nf - 1)

    w_map = lambda g, j, ge, s0, nt, ng: (ge[grp(g, ng)], 0, ftile(g, j, ng))
    vmem = (2 * (d * 2 * tf * 4 + tf * d * 4 + 2 * tf * 4 + d * 4) + 2 * ts * d * 4 + 2 * ns * ts * d * 2
            + ns * ts * d * 4 + d * 2 * tf * (2 + 4 + 2) + tf * d * 2 + 3 * ts * d * 4)
    any_spec = pl.BlockSpec(memory_space=pl.ANY)
    return pl.pallas_call(
        _moe_kernel,
        out_shape=jax.ShapeDtypeStruct((n_out_rows, d), F32),
        grid_spec=pltpu.PrefetchScalarGridSpec(
            num_scalar_prefetch=4,
            grid=(n_grp_max, nf),
            in_specs=[any_spec, any_spec, any_spec,
                      pl.BlockSpec((1, d, 2 * tf), w_map),
                      pl.BlockSpec((1, 1, tf), w_map),
                      pl.BlockSpec((1, 1, tf), w_map),
                      pl.BlockSpec((1, tf, d), lambda g, j, ge, s0, nt, ng: (ge[grp(g, ng)], ftile(g, j, ng), 0)),
                      pl.BlockSpec((1, 1, d), lambda g, j, ge, s0, nt, ng: (ge[grp(g, ng)], 0, 0))],
            out_specs=any_spec,
            scratch_shapes=[pltpu.SMEM((2, ns, 1, ts), I32), pltpu.SMEM((2, ns, 1, ts), I32),
                            pltpu.VMEM((2, ts // MOE_ROW_UNROLL, MOE_ROW_UNROLL, d), F32),
                            pltpu.VMEM((2, ns, ts, d), BF16),
                            pltpu.VMEM((ns, ts // MOE_ROW_UNROLL, MOE_ROW_UNROLL, d), F32),
                            pltpu.SemaphoreType.DMA((2,)), pltpu.SemaphoreType.DMA((2,)),
                            pltpu.SemaphoreType.DMA((ns,))]),
        compiler_params=_cparams(("arbitrary", "arbitrary"), vmem),
        name="moe_experts",
    )(grp_expert, grp_sub0, grp_nt, n_grp, sub_tok, sub_dst, h2, w_gate_up, bg, bl, w_down, bd)


def _route(top_idx):
    ts, ns = MOE_TS, MOE_NSUB
    n_tok = top_idx.shape[0]
    n_slots = n_tok * EXPERT_TOPK
    e_flat = top_idx.reshape(-1)
    order = jnp.argsort(e_flat).astype(I32)
    counts = jnp.bincount(e_flat, length=N_EXPERTS).astype(I32)
    start = jnp.cumsum(counts) - counts
    nsub = (counts + ts - 1) // ts
    sub_end = jnp.cumsum(nsub)
    sub_start = sub_end - nsub
    n_sub_max = n_slots // ts + N_EXPERTS
    last = N_EXPERTS - 1
    owner = lambda ends, idx: jnp.minimum(jnp.sum(ends[None, :] <= idx[:, None], axis=1), last).astype(I32)
    sub_expert = owner(sub_end, jnp.arange(n_sub_max))
    rows = jnp.arange(n_sub_max * ts, dtype=I32)
    e_row = jnp.repeat(sub_expert, ts)
    within = rows - ts * sub_start[e_row]
    valid = within < counts[e_row]
    slot = order[jnp.clip(start[e_row] + within, 0, n_slots - 1)]
    row_tok = jnp.where(valid, slot // EXPERT_TOPK, 0)
    row_dst = jnp.where(valid, (slot % EXPERT_TOPK) * n_tok + slot // EXPERT_TOPK, n_slots + rows % (ts * ns))
    tail = jnp.zeros((ns, 1, ts), I32)
    sub_tok = jnp.concatenate([row_tok.reshape(n_sub_max, 1, ts).astype(I32), tail])
    sub_dst = jnp.concatenate([row_dst.reshape(n_sub_max, 1, ts).astype(I32), tail + n_slots])
    ngrp = (nsub + ns - 1) // ns
    grp_end = jnp.cumsum(ngrp)
    grp_start = grp_end - ngrp
    n_grp = grp_end[-1]
    gidx = jnp.arange(N_EXPERTS + n_slots // (ts * ns), dtype=I32)
    used = gidx < n_grp
    grp_expert = owner(grp_end, gidx)
    k = gidx - grp_start[grp_expert]
    grp_sub0 = jnp.where(used, sub_start[grp_expert] + k * ns, 0).astype(I32)
    grp_nt = jnp.where(used, jnp.clip(nsub[grp_expert] - k * ns, 0, ns), 0).astype(I32)
    return sub_tok, sub_dst, grp_expert, grp_sub0, grp_nt, n_grp.astype(I32).reshape(1), n_slots + ts * ns


def _final_kernel(x1_ref, *rest):
    y_refs = rest[:EXPERT_TOPK]
    tg_ref, p_ref, wpp_ref, wpg_ref, gp_ref, gfin_ref, o_ref = rest[EXPERT_TOPK:]
    x2 = x1_ref[...]
    gates = tg_ref[...]
    for r in range(EXPERT_TOPK):
        x2 = x2 + y_refs[r][...] * gates[:, r:r + 1]
    ple = jnp.dot(p_ref[...].astype(BF16), wpp_ref[...], preferred_element_type=F32)
    gate = _sigmoid(jnp.dot(x2.astype(BF16), wpg_ref[...], preferred_element_type=F32))
    x3 = x2 + _rms(gate * ple, gp_ref[...])
    o_ref[...] = _rms(x3, gfin_ref[...])


def _final(x1, y, gates, p2, wpp, wpg, gp, gfin):
    seq, d = x1.shape
    pd = p2.shape[1]
    tm = min(FIN_TM, seq)
    row = lambda i: (i, 0)
    fix = lambda i: (0, 0)
    vmem = (2 * (2 * tm * d * 4 + EXPERT_TOPK * tm * d * 4 + tm * LANES * 4 + tm * pd * 4)
            + pd * d * 2 + d * d * 2 + 8 * tm * d * 4)
    return pl.pallas_call(
        _final_kernel,
        out_shape=jax.ShapeDtypeStruct((seq, d), F32),
        grid=(seq // tm,),
        in_specs=[pl.BlockSpec((tm, d), row),
                  *[pl.BlockSpec((tm, d), functools.partial(lambda i, r: (r * (seq // tm) + i, 0), r=r))
                    for r in range(EXPERT_TOPK)],
                  pl.BlockSpec((tm, LANES), row), pl.BlockSpec((tm, pd), row),
                  _resident((pd, d), fix), _resident((d, d), fix),
                  _resident((1, d), fix), _resident((1, d), fix)],
        out_specs=pl.BlockSpec((tm, d), row),
        compiler_params=_cparams(("parallel",), vmem),
        name="combine_ple_norm",
    )(x1, *([y] * EXPERT_TOPK), gates, p2, wpp, wpg, gp, gfin)


def _layer(x2, p2, g_mix, w_in, g_idx_k, b_idx_k, w_br_a, w_br_b, w_out, g_ffn, w_router, b_router,
           w_gate_up, b_gate_up, w_down, b_down, w_ple_proj, w_ple_gate, g_ple, g_final):
    seq, d = x2.shape
    dsa_w, moba_w = DSA_HEADS * HEAD_DIM, MOBA_HEADS * HEAD_DIM
    sizes = (dsa_w, dsa_w, dsa_w, IDX_HEADS * IDX_DIM, IDX_DIM, IDX_HEADS, moba_w, moba_w, moba_w, d, d)
    off = np.concatenate([[0], np.cumsum(sizes)])
    col = lambda k: w_in[:, off[k]:off[k + 1]]
    row1 = lambda v: v.reshape(1, -1)
    g_mix2 = row1(g_mix)

    w_rope = jnp.concatenate([col(0), col(1), col(6), col(7)], axis=1).astype(BF16)
    w_qi = col(3).astype(BF16)
    w_kiwi = jnp.zeros((d, 2 * LANES), F32).at[:, :IDX_DIM].set(col(4)).at[:, LANES:LANES + IDX_HEADS].set(col(5))
    w_val_t = jnp.concatenate([col(2), col(8)], axis=1).T.astype(BF16)
    w_gate = jnp.concatenate([col(9), col(10)], axis=1).astype(BF16)

    rope_tabs = _rope_tables(seq, ROPE_DIM, HEAD_DIM)
    idx_tabs = _rope_tables(seq, IDX_ROPE_DIM, IDX_DIM)
    q_scale = jnp.full((1, dsa_w), HEAD_DIM ** -0.5 * LOG2_E, F32)
    k_scale = jnp.ones((1, dsa_w), F32)
    qk = _project(x2, g_mix2, w_rope, BF16, rope_tabs, ROPE_DIM // 2,
                  col_scale=jnp.concatenate([q_scale, k_scale, q_scale, k_scale], axis=1))
    qi = _project(x2, g_mix2, w_qi, BF16, idx_tabs, IDX_ROPE_DIM // 2)
    pad = lambda v: jnp.zeros((1, LANES), F32).at[0, :IDX_DIM].set(v)
    ki, wi = _project_kiwi(x2, g_mix2, w_kiwi.astype(BF16), pad(g_idx_k), pad(b_idx_k), idx_tabs,
                           (IDX_HEADS ** -0.5) * (IDX_DIM ** -0.5))
    vals_t = _project(x2, g_mix2, w_val_t, BF16, transposed=True)
    gates_ab = _project(x2, g_mix2, w_gate, F32)

    bias4 = _dsa_mask(qi, ki, wi.T, min(DSA_TOPK, seq // 4))
    o_a = _dsa_attention(qk, vals_t, bias4, 0, 1, 0)
    o_b = _moba_attention(qk, vals_t, _block_means(qk, 3), 2, 3, 1)

    w_r = jnp.zeros((d, LANES), F32).at[:, :N_EXPERTS].set(w_router).astype(BF16)
    b_r = jnp.full((1, LANES), NEG_INF, F32).at[0, :N_EXPERTS].set(b_router)
    x1, h2, top_idx, top_gate = _post_attention(
        o_a, o_b, gates_ab, x2, w_br_a.astype(BF16), w_br_b.astype(BF16), w_out.astype(BF16),
        row1(g_ffn), w_r, b_r)

    bias_row = lambda b: b.reshape(N_EXPERTS, 1, -1)
    y = _moe_experts(h2, _route(top_idx[:, :EXPERT_TOPK]), w_gate_up,
                     bias_row(b_gate_up[:, 0::2]), bias_row(b_gate_up[:, 1::2]), w_down, bias_row(b_down))

    return _final(x1, y, top_gate, p2, w_ple_proj.astype(BF16), w_ple_gate.astype(BF16), row1(g_ple),
                  row1(g_final))


def kernel(x, p, g_mix, w_in, g_idx_k, b_idx_k, w_br_a, w_br_b, w_out, g_ffn, w_router, b_router,
           w_gate_up, b_gate_up, w_down, b_down, w_ple_proj, w_ple_gate, g_ple, g_final):
    batch, seq, d = x.shape
    depth = w_in.shape[0]
    assert batch == 1 and depth == 1, "kernel handles the single-sequence, single-layer block"
    assert seq % ATT_T == 0 and seq % PROJ_TM == 0
    out = _layer(x[0], p[0, 0], g_mix[0], w_in[0], g_idx_k[0], b_idx_k[0], w_br_a[0], w_br_b[0], w_out[0],
                 g_ffn[0], w_router[0], b_router[0], w_gate_up[0], b_gate_up[0], w_down[0], b_down[0],
                 w_ple_proj[0], w_ple_gate[0], g_ple[0], g_final)
    return out[None]
```

```python
import functools

import numpy as np
import jax
import jax.numpy as jnp
from jax import lax
from jax.experimental import pallas as pl
from jax.experimental.pallas import tpu as pltpu

F32 = jnp.float32
BF16 = jnp.bfloat16
I32 = jnp.int32
NEG_INF = float("-inf")
LOG2_E = 1.4426950408889634

HEAD_DIM = 128
DSA_HEADS = 8
MOBA_HEADS = 8
ROPE_DIM = HEAD_DIM // 4
ROPE_THETA = 500000.0
IDX_HEADS = 16
IDX_DIM = 64
IDX_ROPE_DIM = IDX_DIM // 4
DSA_TOPK = 256
MOBA_BLOCK = 256
MOBA_TOPK = 3
N_EXPERTS = 32
EXPERT_TOPK = 4
SWIGLU_LIMIT = 7.0
SWIGLU_ALPHA = 1.702
EPS = 1e-6

LANES = 128
VMEM_CAP_BYTES = 60000 * 1024

PROJ_TM = 1024
PROJ_TN = 512
IDX_TQ = 256
IDX_CK = 512
COUNT_ROWS = 32
ATT_T = 512
POST_TM = 256
MOE_TS = 256
MOE_NSUB = 6
MOE_TF = 256
MOE_ROW_UNROLL = 8
FIN_TM = 256

INT_MIN = -2 ** 31
KEY_NEG_INF = 0x807FFFFF - 2 ** 32


def _cparams(sem, vmem_bytes):
    return pltpu.CompilerParams(dimension_semantics=sem,
                                vmem_limit_bytes=int(min(vmem_bytes, VMEM_CAP_BYTES)))


def _resident(shape, index_map):
    return pl.BlockSpec(shape, index_map, pipeline_mode=pl.Buffered(1))


def _sigmoid(x):
    return 1.0 / (1.0 + jnp.exp(-x))


def _rms(xf, g):
    ms = jnp.mean(xf * xf, axis=-1, keepdims=True)
    return xf * lax.rsqrt(ms + EPS) * g


def _dot_nt(a, b):
    return lax.dot_general(a, b, (((1,), (1,)), ((), ())), preferred_element_type=F32)


def _rope_tables(seq, rot_dim, period):
    half = rot_dim // 2
    inv = 1.0 / (ROPE_THETA ** (jnp.arange(half, dtype=F32) / half))
    ang = jnp.arange(seq).astype(F32)[:, None] * inv[None, :]
    cos, sin = jnp.cos(ang), jnp.sin(ang)
    z = lambda n: jnp.zeros((seq, n), F32)
    c = jnp.concatenate([cos, cos, jnp.ones((seq, period - rot_dim), F32)], axis=-1)
    s1 = jnp.concatenate([-sin, z(period - half)], axis=-1)
    s2 = jnp.concatenate([z(half), sin, z(period - rot_dim)], axis=-1)
    rep = LANES // period
    return tuple(jnp.tile(t, (1, rep)) for t in (c, s1, s2))


def _rope(a, c, s1, s2, half):
    return a * c + pltpu.roll(a, LANES - half, 1) * s1 + pltpu.roll(a, half, 1) * s2


def _proj_kernel(x_ref, g_ref, w_ref, *rest, half, transposed):
    if half is None:
        o_ref, h_ref = rest
    else:
        c_ref, s1_ref, s2_ref, o_ref, h_ref = rest

    @pl.when(pl.program_id(1) == 0)
    def _():
        h_ref[...] = _rms(x_ref[...], g_ref[...]).astype(BF16)

    if transposed:
        o_ref[...] = _dot_nt(w_ref[...], h_ref[...]).astype(o_ref.dtype)
        return
    acc = jnp.dot(h_ref[...], w_ref[...], preferred_element_type=F32)
    if half is None:
        o_ref[...] = acc.astype(o_ref.dtype)
    else:
        c, s1, s2 = c_ref[...], s1_ref[...], s2_ref[...]
        for k in range(acc.shape[1] // LANES):
            sl = slice(k * LANES, (k + 1) * LANES)
            o_ref[:, sl] = _rope(acc[:, sl], c, s1, s2, half).astype(o_ref.dtype)


def _project(x2, g, w, out_dtype, tables=None, half=None, transposed=False):
    seq, d = x2.shape
    n = w.shape[0] if transposed else w.shape[1]
    tm, tn = min(PROJ_TM, seq), min(PROJ_TN, n)
    in_specs = [pl.BlockSpec((tm, d), lambda i, j: (i, 0)),
                pl.BlockSpec((1, d), lambda i, j: (0, 0)),
                pl.BlockSpec((tn, d), lambda i, j: (j, 0)) if transposed
                else pl.BlockSpec((d, tn), lambda i, j: (0, j))]
    args = [x2, g, w]
    if tables is not None:
        in_specs += [pl.BlockSpec((tm, LANES), lambda i, j: (i, 0))] * 3
        args += list(tables)
    vmem = 2 * (tm * d * 4 + d * tn * 2 + tm * tn * 4 + 3 * tm * LANES * 4) + tm * d * 2 + 4 * tm * tn * 4
    return pl.pallas_call(
        functools.partial(_proj_kernel, half=half, transposed=transposed),
        out_shape=jax.ShapeDtypeStruct((n, seq) if transposed else (seq, n), out_dtype),
        grid=(seq // tm, n // tn),
        in_specs=in_specs,
        out_specs=pl.BlockSpec((tn, tm), lambda i, j: (j, i)) if transposed
        else pl.BlockSpec((tm, tn), lambda i, j: (i, j)),
        scratch_shapes=[pltpu.VMEM((tm, d), BF16)],
        compiler_params=_cparams(("parallel", "arbitrary"), vmem),
        name="proj_rope" if half is not None else "proj_plain",
    )(*args)


def _proj_kiwi_kernel(x_ref, g_ref, w_ref, lg_ref, lb_ref, c_ref, s1_ref, s2_ref, ki_ref, wi_ref, *, scale):
    h = _rms(x_ref[...], g_ref[...]).astype(BF16)
    acc = jnp.dot(h, w_ref[...], preferred_element_type=F32)
    a = acc[:, :LANES]
    valid = lax.broadcasted_iota(I32, a.shape, 1) < IDX_DIM
    mu = jnp.sum(jnp.where(valid, a, 0.0), axis=-1, keepdims=True) / IDX_DIM
    dlt = jnp.where(valid, a - mu, 0.0)
    var = jnp.sum(dlt * dlt, axis=-1, keepdims=True) / IDX_DIM
    y = dlt * lax.rsqrt(var + EPS) * lg_ref[...] + lb_ref[...]
    ki_ref[...] = _rope(y, c_ref[...], s1_ref[...], s2_ref[...], IDX_ROPE_DIM // 2).astype(BF16)
    wi_ref[...] = acc[:, LANES:LANES + IDX_HEADS] * scale


def _project_kiwi(x2, g, w_kiwi, lg, lb, tables, scale):
    seq, d = x2.shape
    tm = min(PROJ_TM, seq)
    row = lambda i: (i, 0)
    fix = lambda i: (0, 0)
    vmem = 2 * (tm * d * 4 + d * 2 * LANES * 2 + 5 * tm * LANES * 4) + 8 * tm * 2 * LANES * 4 + tm * d * 6
    return pl.pallas_call(
        functools.partial(_proj_kiwi_kernel, scale=scale),
        out_shape=(jax.ShapeDtypeStruct((seq, LANES), BF16), jax.ShapeDtypeStruct((seq, IDX_HEADS), F32)),
        grid=(seq // tm,),
        in_specs=[pl.BlockSpec((tm, d), row), pl.BlockSpec((1, d), fix), pl.BlockSpec((d, 2 * LANES), fix),
                  pl.BlockSpec((1, LANES), fix), pl.BlockSpec((1, LANES), fix),
                  pl.BlockSpec((tm, LANES), row), pl.BlockSpec((tm, LANES), row), pl.BlockSpec((tm, LANES), row)],
        out_specs=(pl.BlockSpec((tm, LANES), row), pl.BlockSpec((tm, IDX_HEADS), row)),
        compiler_params=_cparams(("parallel",), vmem),
        name="proj_kiwi",
    )(x2, g, w_kiwi, lg, lb, *tables)


def _key_to_float(key):
    bits = key ^ ((key >> 31) & 0x7FFFFFFF)
    return lax.bitcast_convert_type(bits, F32)


def _half_key_to_float(key16):
    bits16 = (key16 ^ ((key16 >> 15) & 0x7FFF)) & 0xFFFF
    return lax.bitcast_convert_type(lax.shift_left(bits16, 16), F32)


def _indexer_kernel(qi_ref, ki_ref, wt_ref, out_ref, s_ref, r16_ref, qh_ref, cst_ref, *, n_sel, idx_bits):
    tq, ck = IDX_TQ, IDX_CK
    n_chunks_total = out_ref.shape[1]
    i = pl.program_id(0)
    nc = ((i + 1) * tq + ck - 1) // ck
    qidx = i * tq + lax.broadcasted_iota(I32, (ck, tq), 1)

    def key_idx(c):
        return c * ck + lax.broadcasted_iota(I32, (ck, tq), 0)

    lane = lax.broadcasted_iota(I32, (tq, LANES), 1)
    low = lane < IDX_DIM
    for p in range(IDX_HEADS // 2):
        pair = qi_ref[:, p * LANES:(p + 1) * LANES].astype(F32)
        qh_ref[2 * p] = jnp.where(low, pair, 0.0).astype(BF16)
        qh_ref[2 * p + 1] = jnp.where(low, pltpu.roll(pair, IDX_DIM, 1), 0.0).astype(BF16)

    def score_chunk(c, carry):
        kc = ki_ref[pl.ds(pl.multiple_of(c * ck, ck), ck), :]
        acc = jnp.zeros((ck, tq), F32)
        for h in range(IDX_HEADS):
            acc = acc + jnp.maximum(_dot_nt(kc, qh_ref[h]), 0.0) * wt_ref[h:h + 1, :]
        sc = jnp.where(key_idx(c) <= qidx, acc, NEG_INF)
        s_ref[c] = sc
        r16_ref[c] = sc.astype(BF16)
        return carry

    lax.fori_loop(0, nc, score_chunk, 0)

    def count(pred):
        def body(c, acc):
            hit = jnp.where(pred(s_ref[c], lambda: key_idx(c)), 1.0, 0.0)
            return acc + jnp.sum(hit.reshape(ck // COUNT_ROWS, COUNT_ROWS, tq), axis=0)
        acc = lax.fori_loop(0, nc, body, jnp.zeros((COUNT_ROWS, tq), F32))
        return jnp.sum(acc, axis=0, keepdims=True)

    def count_half(cand16):
        one, zero = jnp.ones((), BF16), jnp.zeros((), BF16)
        groups = ck // COUNT_ROWS

        def body(c, acc):
            hit = jnp.where(r16_ref[c] >= cand16, one, zero).reshape(groups, COUNT_ROWS, tq)
            part = hit[0]
            for u in range(1, groups):
                part = part + hit[u]
            return acc + part.astype(F32)
        acc = lax.fori_loop(0, nc, body, jnp.zeros((COUNT_ROWS, tq), F32))
        return jnp.sum(acc, axis=0, keepdims=True)

    def half_step(b, prefix16):
        cand16 = prefix16 + lax.shift_left(jnp.int32(1), 15 - b)
        cnt = count_half(_half_key_to_float(cand16).astype(BF16))
        return jnp.where(cnt >= n_sel, cand16, prefix16)

    key16 = lax.fori_loop(0, 16, half_step, jnp.full((1, tq), -2 ** 15, I32))
    key_c16 = lax.shift_left(key16, 16) + jnp.where(key16 < 0, 0xFFFF, 0)
    base = jnp.where(key_c16 < INT_MIN + 2 ** 16, INT_MIN, key_c16 - 2 ** 16)

    def bit_step(b, carry):
        prefix, cnt_at = carry
        cand = prefix + lax.shift_left(jnp.int32(1), 17 - b)
        cand_f = _key_to_float(cand)
        cnt = count(lambda sc, key: sc >= cand_f)
        take = cnt >= n_sel
        return jnp.where(take, cand, prefix), jnp.where(take, cnt, cnt_at)

    prefix, cnt_at = lax.fori_loop(0, 18, bit_step, (base, jnp.full((1, tq), float(n_sel), F32)))
    tau = jnp.where(prefix < KEY_NEG_INF, NEG_INF, _key_to_float(prefix))

    need = jnp.logical_and(cnt_at > n_sel, tau > NEG_INF)
    any_tie = jnp.max(jnp.where(need, 1.0, 0.0)) > 0.0
    idx_all = jnp.full((1, tq), 2 ** 30, I32)

    @pl.when(any_tie)
    def _():
        rem = n_sel - count(lambda sc, key: sc > tau)
        cut = jnp.zeros((1, tq), I32)
        for b in range(idx_bits - 1, -1, -1):
            cand = cut + (1 << b)
            below = count(lambda sc, key: jnp.logical_and(sc == tau, key() < cand))
            cut = jnp.where(below < rem, cand, cut)
        cst_ref[...] = jnp.where(need, cut, idx_all)

    @pl.when(jnp.logical_not(any_tie))
    def _():
        cst_ref[...] = idx_all

    cut = cst_ref[...]

    def emit(c, carry):
        sc = s_ref[c]
        key = key_idx(c)
        tie = jnp.where(key <= cut, 0.0, NEG_INF)
        b = jnp.where(sc > tau, 0.0, jnp.where(sc == tau, tie, NEG_INF))
        out_ref[0, c] = jnp.where(key <= qidx, b, NEG_INF).astype(BF16)
        return carry

    lax.fori_loop(0, nc, emit, 0)

    def fill(c, carry):
        out_ref[0, c] = jnp.full((ck, tq), NEG_INF, BF16)
        return carry

    lax.fori_loop(nc, n_chunks_total, fill, 0)


def _dsa_mask(qi, ki, wi_t, n_sel):
    seq = qi.shape[0]
    tq, ck = min(IDX_TQ, seq), IDX_CK
    nq, nchunk = seq // tq, seq // ck
    idx_bits = max(1, int(seq - 1).bit_length())
    vmem = (2 * (tq * qi.shape[1] * 2 + IDX_HEADS * tq * 4 + nchunk * tq * ck * 2) + seq * LANES * 2
            + nchunk * tq * ck * 6 + IDX_HEADS * tq * LANES * 2 + 12 * tq * ck * 4)
    return pl.pallas_call(
        functools.partial(_indexer_kernel, n_sel=n_sel, idx_bits=idx_bits),
        out_shape=jax.ShapeDtypeStruct((nq, nchunk, ck, tq), BF16),
        grid=(nq,),
        in_specs=[pl.BlockSpec((tq, qi.shape[1]), lambda i: (i, 0)),
                  _resident((seq, LANES), lambda i: (0, 0)),
                  pl.BlockSpec((IDX_HEADS, tq), lambda i: (0, i))],
        out_specs=pl.BlockSpec((1, nchunk, ck, tq), lambda i: (i, 0, 0, 0)),
        scratch_shapes=[pltpu.VMEM((nchunk, ck, tq), F32), pltpu.VMEM((nchunk, ck, tq), BF16),
                        pltpu.VMEM((IDX_HEADS, tq, LANES), BF16),
                        pltpu.VMEM((1, tq), I32)],
        compiler_params=_cparams(("parallel",), vmem),
        name="dsa_indexer",
    )(qi, ki, wi_t)


def _tri_steps(n):
    qs = [i for i in range(n) for _ in range(i + 1)]
    ks = [j for i in range(n) for j in range(i + 1)]
    return jnp.asarray(qs, I32), jnp.asarray(ks, I32)


def _softmax_step(h, s_t, vt_h, m_ref, l_ref, acc_ref):
    m_prev = m_ref[h]
    m_new = jnp.maximum(m_prev, jnp.max(s_t, axis=0, keepdims=True))
    m_safe = jnp.where(m_new == NEG_INF, 0.0, m_new)
    alpha = jnp.exp2(m_prev - m_safe)
    p_t = jnp.exp2(s_t - m_safe)
    l_ref[h] = alpha * l_ref[h] + jnp.sum(p_t, axis=0, keepdims=True)
    acc_ref[h] = alpha * acc_ref[h] + jnp.dot(vt_h, p_t.astype(BF16), preferred_element_type=F32)
    m_ref[h] = m_new


def _attn_init(m_ref, l_ref, acc_ref):
    m_ref[...] = jnp.full(m_ref.shape, NEG_INF, F32)
    l_ref[...] = jnp.zeros(l_ref.shape, F32)
    acc_ref[...] = jnp.zeros(acc_ref.shape, F32)


def _attn_finish(o_ref, l_ref, acc_ref, nh):
    for h in range(nh):
        sl = slice(h * HEAD_DIM, (h + 1) * HEAD_DIM)
        o_ref[:, sl] = (acc_ref[h] / l_ref[h]).T.astype(o_ref.dtype)


def _attn_scratch(nh, t):
    return [pltpu.VMEM((nh, 1, t), F32), pltpu.VMEM((nh, 1, t), F32), pltpu.VMEM((nh, HEAD_DIM, t), F32)]


def _dsa_attn_kernel(qs_ref, ks_ref, q_ref, k_ref, vt_ref, b_ref, o_ref, m_ref, l_ref, acc_ref, *, scale):
    s_id = pl.program_id(0)
    qi, kj = qs_ref[s_id], ks_ref[s_id]
    t = q_ref.shape[0]

    @pl.when(kj == 0)
    def _():
        _attn_init(m_ref, l_ref, acc_ref)

    bias_t = jnp.concatenate([b_ref[u, 0] for u in range(b_ref.shape[0])], axis=1).astype(F32)
    for h in range(DSA_HEADS):
        sl = slice(h * HEAD_DIM, (h + 1) * HEAD_DIM)
        s_t = _dot_nt(k_ref[:, sl], q_ref[:, sl]) * scale + bias_t
        _softmax_step(h, s_t, vt_ref[sl, :], m_ref, l_ref, acc_ref)

    @pl.when(kj == qi)
    def _():
        _attn_finish(o_ref, l_ref, acc_ref, DSA_HEADS)


def _dsa_attention(qk, vals_t, bias4, qcol, kcol, vrow):
    seq, width = qk.shape[0], DSA_HEADS * HEAD_DIM
    t = min(ATT_T, seq)
    nq = seq // t
    qs, ks = _tri_steps(nq)
    sub = t // IDX_TQ
    omap = lambda s, qs, ks: (qs[s], 0)
    qmap = lambda s, qs, ks: (qs[s], qcol)
    kmap = lambda s, qs, ks: (ks[s], kcol)
    vmap = lambda s, qs, ks: (vrow, ks[s])
    vmem = (2 * (3 * t * width * 2 + t * t * 2 + t * width * 2) + t * width * 4 + 28 * t * t * 4)
    return pl.pallas_call(
        functools.partial(_dsa_attn_kernel, scale=HEAD_DIM ** -0.5 * LOG2_E),
        out_shape=jax.ShapeDtypeStruct((seq, width), BF16),
        grid_spec=pltpu.PrefetchScalarGridSpec(
            num_scalar_prefetch=2,
            grid=(int(qs.shape[0]),),
            in_specs=[pl.BlockSpec((t, width), qmap), pl.BlockSpec((t, width), kmap),
                      pl.BlockSpec((width, t), vmap),
                      pl.BlockSpec((sub, 1, IDX_CK, IDX_TQ), lambda s, qs, ks: (qs[s], ks[s], 0, 0))],
            out_specs=pl.BlockSpec((t, width), omap),
            scratch_shapes=_attn_scratch(DSA_HEADS, t)),
        compiler_params=_cparams(("arbitrary",), vmem),
        name="dsa_attention",
    )(qs, ks, qk, qk, vals_t, bias4)


def _kmean_kernel(k_ref, o_ref):
    o_ref[0] = jnp.mean(k_ref[...].astype(F32), axis=0, keepdims=True)


def _block_means(qk, kcol):
    seq, width = qk.shape[0], MOBA_HEADS * HEAD_DIM
    nblk = seq // MOBA_BLOCK
    out = pl.pallas_call(
        _kmean_kernel,
        out_shape=jax.ShapeDtypeStruct((nblk, 1, width), F32),
        grid=(nblk,),
        in_specs=[pl.BlockSpec((MOBA_BLOCK, width), lambda i: (i, kcol))],
        out_specs=pl.BlockSpec((1, 1, width), lambda i: (i, 0, 0)),
        compiler_params=_cparams(("parallel",), 8 * MOBA_BLOCK * width * 4),
        name="moba_block_means",
    )(qk)
    return out.reshape(nblk, width)


def _moba_attn_kernel(qs_ref, ks_ref, q_ref, k_ref, vt_ref, km_ref, o_ref, m_ref, l_ref, acc_ref, sel_ref,
                      *, scale, n_top):
    s_id = pl.program_id(0)
    qi, kj = qs_ref[s_id], ks_ref[s_id]
    t = q_ref.shape[0]
    nblk = km_ref.shape[0]
    per_tile = t // MOBA_BLOCK

    @pl.when(kj == 0)
    def _():
        _attn_init(m_ref, l_ref, acc_ref)
        blk = lax.broadcasted_iota(I32, (nblk, t), 0)
        own = (qi * t + lax.broadcasted_iota(I32, (nblk, t), 1)) // MOBA_BLOCK
        blk_f = blk.astype(F32)
        for h in range(MOBA_HEADS):
            sl = slice(h * HEAD_DIM, (h + 1) * HEAD_DIM)
            g = _dot_nt(km_ref[:, sl].astype(BF16), q_ref[:, sl])
            g = jnp.where(blk < own, g, NEG_INF)
            sel = jnp.full((nblk, t), NEG_INF, F32)
            for _ in range(n_top):
                mx = jnp.max(g, axis=0, keepdims=True)
                is_max = jnp.logical_and(g == mx, mx > NEG_INF)
                first = jnp.min(jnp.where(is_max, blk_f, float(nblk)), axis=0, keepdims=True)
                pick = blk_f == first
                sel = jnp.where(pick, 0.0, sel)
                g = jnp.where(pick, NEG_INF, g)
            sel_ref[h] = sel

    def block_bias(h):
        rows = [jnp.broadcast_to(sel_ref[h, pl.ds(kj * per_tile + b, 1), :], (MOBA_BLOCK, t))
                for b in range(per_tile)]
        return jnp.concatenate(rows, axis=0)

    def run(diag):
        if diag:
            key = lax.broadcasted_iota(I32, (t, t), 0)
            qry = lax.broadcasted_iota(I32, (t, t), 1)
            own_blk = (key // MOBA_BLOCK) == (qry // MOBA_BLOCK)
            causal = jnp.where(key <= qry, 0.0, NEG_INF)
        for h in range(MOBA_HEADS):
            sl = slice(h * HEAD_DIM, (h + 1) * HEAD_DIM)
            bias_t = block_bias(h)
            if diag:
                bias_t = jnp.where(own_blk, causal, bias_t)
            s_t = _dot_nt(k_ref[:, sl], q_ref[:, sl]) * scale + bias_t
            _softmax_step(h, s_t, vt_ref[sl, :], m_ref, l_ref, acc_ref)

    @pl.when(kj < qi)
    def _():
        run(False)

    @pl.when(kj == qi)
    def _():
        run(True)
        _attn_finish(o_ref, l_ref, acc_ref, MOBA_HEADS)


def _moba_attention(qk, vals_t, kmean, qcol, kcol, vrow):
    seq, width = qk.shape[0], MOBA_HEADS * HEAD_DIM
    t = min(ATT_T, seq)
    nq = seq // t
    nblk = kmean.shape[0]
    qs, ks = _tri_steps(nq)
    omap = lambda s, qs, ks: (qs[s], 0)
    qmap = lambda s, qs, ks: (qs[s], qcol)
    kmap = lambda s, qs, ks: (ks[s], kcol)
    vmap = lambda s, qs, ks: (vrow, ks[s])
    vmem = (2 * (4 * t * width * 2) + nblk * width * 4 + t * width * 4 + MOBA_HEADS * nblk * t * 4
            + 28 * t * t * 4)
    return pl.pallas_call(
        functools.partial(_moba_attn_kernel, scale=HEAD_DIM ** -0.5 * LOG2_E, n_top=min(MOBA_TOPK, nblk)),
        out_shape=jax.ShapeDtypeStruct((seq, width), BF16),
        grid_spec=pltpu.PrefetchScalarGridSpec(
            num_scalar_prefetch=2,
            grid=(int(qs.shape[0]),),
            in_specs=[pl.BlockSpec((t, width), qmap), pl.BlockSpec((t, width), kmap),
                      pl.BlockSpec((width, t), vmap),
                      _resident((nblk, width), lambda s, qs, ks: (0, 0))],
            out_specs=pl.BlockSpec((t, width), omap),
            scratch_shapes=_attn_scratch(MOBA_HEADS, t) + [pltpu.VMEM((MOBA_HEADS, nblk, t), F32)]),
        compiler_params=_cparams(("arbitrary",), vmem),
        name="moba_attention",
    )(qs, ks, qk, qk, vals_t, kmean)


def _post_kernel(oa_ref, ob_ref, ga_ref, gb_ref, x_ref, wa_ref, wb_ref, wo_ref, gf_ref, wr_ref, br_ref,
                 x1_ref, h2_ref, ti_ref, tg_ref):
    ta = jnp.dot(oa_ref[...], wa_ref[...], preferred_element_type=F32)
    tb = jnp.dot(ob_ref[...], wb_ref[...], preferred_element_type=F32)
    merged = _sigmoid(ga_ref[...]) * ta + _sigmoid(gb_ref[...]) * tb
    x1 = x_ref[...] + jnp.dot(merged.astype(BF16), wo_ref[...], preferred_element_type=F32)
    x1_ref[...] = x1
    h2 = _rms(x1, gf_ref[...])
    h2_ref[...] = h2
    logits = jnp.dot(h2.astype(BF16), wr_ref[...], preferred_element_type=F32) + br_ref[...]
    lane = lax.broadcasted_iota(I32, logits.shape, 1)
    lane_f = lane.astype(F32)
    idx_out = jnp.zeros(logits.shape, I32)
    val_out = jnp.zeros(logits.shape, F32)
    top = None
    for r in range(EXPERT_TOPK):
        mx = jnp.max(logits, axis=-1, keepdims=True)
        ix = jnp.min(jnp.where(logits == mx, lane_f, float(LANES)), axis=-1, keepdims=True).astype(I32)
        if top is None:
            top = mx
        idx_out = jnp.where(lane == r, ix, idx_out)
        val_out = jnp.where(lane == r, jnp.exp(mx - top), val_out)
        logits = jnp.where(lane == ix, NEG_INF, logits)
    ti_ref[...] = idx_out
    tg_ref[...] = val_out / jnp.sum(val_out, axis=-1, keepdims=True)


def _post_attention(oa, ob, gates_ab, x2, wa, wb, wo, gf, wr, br):
    seq, d = x2.shape
    w = oa.shape[1]
    tm = min(POST_TM, seq)
    row = lambda i: (i, 0)
    fix = lambda i: (0, 0)
    vmem = (2 * (2 * tm * w * 2 + 3 * tm * d * 4 + 2 * tm * d * 4 + 2 * tm * LANES * 4)
            + 2 * w * d * 2 + d * d * 2 + d * LANES * 2 + 8 * tm * d * 4)
    return pl.pallas_call(
        _post_kernel,
        out_shape=(jax.ShapeDtypeStruct((seq, d), F32), jax.ShapeDtypeStruct((seq, d), F32),
                   jax.ShapeDtypeStruct((seq, LANES), I32), jax.ShapeDtypeStruct((seq, LANES), F32)),
        grid=(seq // tm,),
        in_specs=[pl.BlockSpec((tm, w), row), pl.BlockSpec((tm, w), row),
                  pl.BlockSpec((tm, d), row), pl.BlockSpec((tm, d), lambda i: (i, 1)), pl.BlockSpec((tm, d), row),
                  _resident((w, d), fix), _resident((w, d), fix), _resident((d, d), fix),
                  _resident((1, d), fix), _resident((d, LANES), fix), _resident((1, LANES), fix)],
        out_specs=(pl.BlockSpec((tm, d), row), pl.BlockSpec((tm, d), row),
                   pl.BlockSpec((tm, LANES), row), pl.BlockSpec((tm, LANES), row)),
        compiler_params=_cparams(("parallel",), vmem),
        name="merge_outproj_router",
    )(oa, ob, gates_ab, gates_ab, x2, wa, wb, wo, gf, wr, br)


def _split_even_odd(w):
    grp = 2 * LANES
    r = lax.broadcasted_iota(I32, (grp, grp), 0)
    c = lax.broadcasted_iota(I32, (grp, grp), 1)
    src = jnp.where(c < LANES, 2 * c, 2 * (c - LANES) + 1)
    sel = jnp.where(r == src, 1.0, 0.0).astype(BF16)
    parts = [jnp.dot(w[:, k * grp:(k + 1) * grp], sel, preferred_element_type=F32).astype(BF16)
             for k in range(w.shape[1] // grp)]
    even = jnp.concatenate([p[:, :LANES] for p in parts], axis=1)
    odd = jnp.concatenate([p[:, LANES:] for p in parts], axis=1)
    return even, odd


def _moe_kernel(ge_ref, sub0_ref, nt_ref, ng_ref, tok_hbm, dst_hbm, h_hbm, wgu_ref, bg_ref, bl_ref, wd_ref, bd_ref,
                y_hbm, tok_ref, dst_ref, stage_ref, xall_ref, acc_ref, isem, gsem, ssem):
    g, j = pl.program_id(0), pl.program_id(1)
    n_grp_max, nf = pl.num_programs(0), pl.num_programs(1)
    ng = ng_ref[0]
    ns, ts = acc_ref.shape[0], tok_ref.shape[3]
    sub = stage_ref.shape[2]
    d = stage_ref.shape[3]
    cur = lax.rem(g, 2)
    nxt = 1 - cur
    nt_cur = nt_ref[g]
    nt_next = nt_ref[jnp.minimum(g + 1, n_grp_max - 1)]

    def on_slot(dyn_slot, fn):
        for s in range(2):
            pl.when(dyn_slot == s)(functools.partial(fn, s))

    def idx_copies(grp, s):
        first = sub0_ref[grp]
        return ([pltpu.make_async_copy(tok_hbm.at[first + r], tok_ref.at[s, r], isem.at[s]) for r in range(ns)]
                + [pltpu.make_async_copy(dst_hbm.at[first + r], dst_ref.at[s, r], isem.at[s]) for r in range(ns)])

    def idx_start(grp, s):
        for cp in idx_copies(grp, s):
            cp.start()

    def idx_wait(s):
        for cp in idx_copies(0, s):
            cp.wait()

    def row_loop(issue):
        def body(i, carry):
            for k in range(sub):
                issue(i, k)
            return carry
        lax.fori_loop(0, ts // sub, body, 0)

    def gather_start(s, r):
        st = r % 2

        def issue(i, k):
            tok = tok_ref[s, r, 0, i * sub + k]
            pltpu.make_async_copy(h_hbm.at[pl.ds(tok, 1), :], stage_ref.at[st, i, pl.ds(k, 1), :],
                                  gsem.at[st]).start()
        row_loop(issue)

    def gather_finish(r, xslot):
        st = r % 2
        pltpu.make_async_copy(stage_ref.at[st], stage_ref.at[st], gsem.at[st]).wait()
        xall_ref[xslot, r] = stage_ref[st].reshape(ts, d).astype(BF16)

    def scatter_start(s, r):
        def issue(i, k):
            dst = dst_ref[s, r, 0, i * sub + k]
            pltpu.make_async_copy(acc_ref.at[r, i, pl.ds(k, 1), :], y_hbm.at[pl.ds(dst, 1), :], ssem.at[r]).start()
        row_loop(issue)

    def scatter_wait_one(r):
        pltpu.make_async_copy(acc_ref.at[r], acc_ref.at[r], ssem.at[r]).wait()

    def scatter_wait(n_sub):
        for r in range(ns):
            pl.when(r < n_sub)(functools.partial(scatter_wait_one, r))

    @pl.when(jnp.logical_and(g == 0, j == 0))
    def _():
        idx_start(0, 0)
        acc_ref[...] = jnp.zeros(acc_ref.shape, F32)
        spill0 = y_hbm.shape[0] - ns * ts
        for r in range(ns):
            def fill(i, carry):
                row = pl.multiple_of(spill0 + r * ts + i * sub, sub)
                pltpu.make_async_copy(acc_ref.at[r, i], y_hbm.at[pl.ds(row, sub), :], ssem.at[r]).start()
                return carry
            lax.fori_loop(0, ts // sub, fill, 0)
        scatter_wait(ns)
        idx_wait(0)
        for r in range(ns):
            @pl.when(r < nt_ref[0])
            def _():
                gather_start(0, r)
                gather_finish(r, 0)

        @pl.when(1 < ng)
        def _():
            idx_start(1, 1)

    nt_prev = nt_ref[jnp.maximum(g - 1, 0)]

    @pl.when(jnp.logical_and(j == 0, g == ng))
    def _():
        scatter_wait(nt_prev)

    @pl.when(jnp.logical_and(j == 0, jnp.logical_and(g >= 1, g < ng)))
    def _():
        for r in range(ns):
            pl.when(jnp.logical_and(r < nt_prev, r >= nt_cur))(functools.partial(scatter_wait_one, r))

    @pl.when(g + 1 < ng)
    def _():
        def prefetch(s):
            @pl.when(j == 0)
            def _():
                idx_wait(s)
                gather_start(s, 0)
            for step in range(1, ns + 1):
                @pl.when(jnp.logical_and(j == step, step - 1 < nt_next))
                def _():
                    gather_finish(step - 1, nxt)
                if step < ns:
                    @pl.when(jnp.logical_and(j == step, step < nt_next))
                    def _():
                        gather_start(s, step)
        on_slot(nxt, prefetch)

    @pl.when(g < ng)
    def _():
        wg, wl = _split_even_odd(wgu_ref[0].astype(BF16))
        wd = wd_ref[0].astype(BF16)

        def rows(r0, n):
            @pl.when(j == 0)
            def _():
                for r in range(r0, r0 + n):
                    pl.when(jnp.logical_and(g >= 1, r < nt_prev))(functools.partial(scatter_wait_one, r))
                    acc_ref[r] = jnp.broadcast_to(bd_ref[0], (ts // sub, sub, d))

            x = xall_ref[cur, r0:r0 + n].reshape(n * ts, d)
            gate = jnp.dot(x, wg, preferred_element_type=F32) + bg_ref[0]
            lin = jnp.dot(x, wl, preferred_element_type=F32) + bl_ref[0]
            gate = jnp.minimum(gate, SWIGLU_LIMIT)
            lin = jnp.clip(lin, -SWIGLU_LIMIT, SWIGLU_LIMIT)
            hid = (lin + 1.0) * (gate * _sigmoid(gate * SWIGLU_ALPHA))
            y = jnp.dot(hid.astype(BF16), wd, preferred_element_type=F32)
            acc_ref[r0:r0 + n] += y.reshape(n, ts // sub, sub, d)

        for r0 in range(0, ns, 2):
            pl.when(r0 + 1 < nt_cur)(functools.partial(rows, r0, 2))
            pl.when(r0 + 1 == nt_cur)(functools.partial(rows, r0, 1))

    @pl.when(jnp.logical_and(j == nf - 1, g < ng))
    def _():
        def finish(s):
            for r in range(ns):
                @pl.when(r < nt_cur)
                def _():
                    scatter_start(s, r)

            @pl.when(g + 2 < ng)
            def _():
                idx_start(g + 2, s)
        on_slot(cur, finish)

        @pl.when(g == n_grp_max - 1)
        def _():
            scatter_wait(nt_cur)


def _moe_experts(h2, route, w_gate_up, bg, bl, w_down, bd):
    sub_tok, sub_dst, grp_expert, grp_sub0, grp_nt, n_grp, n_out_rows = route
    d = h2.shape[1]
    f = w_down.shape[1]
    ns, ts = MOE_NSUB, sub_tok.shape[2]
    n_grp_max = grp_expert.shape[0]
    tf = min(MOE_TF, f)
    nf = f // tf
    assert nf >= ns + 1, "one sub-tile of the next group is gathered per hidden-tile step"

    def grp(g, ng):
        return jnp.minimum(g, ng[0] - 1)

    def ftile(g, j, ng):
        return jnp.where(g < ng[0], j, nf - 1)

    w_map = lambda g, j, ge, s0, nt, ng: (ge[grp(g, ng)], 0, ftile(g, j, ng))
    vmem = (2 * (d * 2 * tf * 4 + tf * d * 4 + 2 * tf * 4 + d * 4) + 2 * ts * d * 4 + 2 * ns * ts * d * 2
            + ns * ts * d * 4 + d * 2 * tf * (2 + 4 + 2) + tf * d * 2 + 3 * ts * d * 4)
    any_spec = pl.BlockSpec(memory_space=pl.ANY)
    return pl.pallas_call(
        _moe_kernel,
        out_shape=jax.ShapeDtypeStruct((n_out_rows, d), F32),
        grid_spec=pltpu.PrefetchScalarGridSpec(
            num_scalar_prefetch=4,
            grid=(n_grp_max, nf),
            in_specs=[any_spec, any_spec, any_spec,
                      pl.BlockSpec((1, d, 2 * tf), w_map),
                      pl.BlockSpec((1, 1, tf), w_map),
                      pl.BlockSpec((1, 1, tf), w_map),
                      pl.BlockSpec((1, tf, d), lambda g, j, ge, s0, nt, ng: (ge[grp(g, ng)], ftile(g, j, ng), 0)),
                      pl.BlockSpec((1, 1, d), lambda g, j, ge, s0, nt, ng: (ge[grp(g, ng)], 0, 0))],
            out_specs=any_spec,
            scratch_shapes=[pltpu.SMEM((2, ns, 1, ts), I32), pltpu.SMEM((2, ns, 1, ts), I32),
                            pltpu.VMEM((2, ts // MOE_ROW_UNROLL, MOE_ROW_UNROLL, d), F32),
                            pltpu.VMEM((2, ns, ts, d), BF16),
                            pltpu.VMEM((ns, ts // MOE_ROW_UNROLL, MOE_ROW_UNROLL, d), F32),
                            pltpu.SemaphoreType.DMA((2,)), pltpu.SemaphoreType.DMA((2,)),
                            pltpu.SemaphoreType.DMA((ns,))]),
        compiler_params=_cparams(("arbitrary", "arbitrary"), vmem),
        name="moe_experts",
    )(grp_expert, grp_sub0, grp_nt, n_grp, sub_tok, sub_dst, h2, w_gate_up, bg, bl, w_down, bd)


def _route(top_idx):
    ts, ns = MOE_TS, MOE_NSUB
    n_tok = top_idx.shape[0]
    n_slots = n_tok * EXPERT_TOPK
    e_flat = top_idx.reshape(-1)
    order = jnp.argsort(e_flat).astype(I32)
    counts = jnp.bincount(e_flat, length=N_EXPERTS).astype(I32)
    start = jnp.cumsum(counts) - counts
    nsub = (counts + ts - 1) // ts
    sub_end = jnp.cumsum(nsub)
    sub_start = sub_end - nsub
    n_sub_max = n_slots // ts + N_EXPERTS
    last = N_EXPERTS - 1
    owner = lambda ends, idx: jnp.minimum(jnp.sum(ends[None, :] <= idx[:, None], axis=1), last).astype(I32)
    sub_expert = owner(sub_end, jnp.arange(n_sub_max))
    rows = jnp.arange(n_sub_max * ts, dtype=I32)
    e_row = jnp.repeat(sub_expert, ts)
    within = rows - ts * sub_start[e_row]
    valid = within < counts[e_row]
    slot = order[jnp.clip(start[e_row] + within, 0, n_slots - 1)]
    row_tok = jnp.where(valid, slot // EXPERT_TOPK, 0)
    row_dst = jnp.where(valid, (slot % EXPERT_TOPK) * n_tok + slot // EXPERT_TOPK, n_slots + rows % (ts * ns))
    tail = jnp.zeros((ns, 1, ts), I32)
    sub_tok = jnp.concatenate([row_tok.reshape(n_sub_max, 1, ts).astype(I32), tail])
    sub_dst = jnp.concatenate([row_dst.reshape(n_sub_max, 1, ts).astype(I32), tail + n_slots])
    ngrp = (nsub + ns - 1) // ns
    grp_end = jnp.cumsum(ngrp)
    grp_start = grp_end - ngrp
    n_grp = grp_end[-1]
    gidx = jnp.arange(N_EXPERTS + n_slots // (ts * ns), dtype=I32)
    used = gidx < n_grp
    grp_expert = owner(grp_end, gidx)
    k = gidx - grp_start[grp_expert]
    grp_sub0 = jnp.where(used, sub_start[grp_expert] + k * ns, 0).astype(I32)
    grp_nt = jnp.where(used, jnp.clip(nsub[grp_expert] - k * ns, 0, ns), 0).astype(I32)
    return sub_tok, sub_dst, grp_expert, grp_sub0, grp_nt, n_grp.astype(I32).reshape(1), n_slots + ts * ns


def _final_kernel(x1_ref, *rest):
    y_refs = rest[:EXPERT_TOPK]
    tg_ref, p_ref, wpp_ref, wpg_ref, gp_ref, gfin_ref, o_ref = rest[EXPERT_TOPK:]
    x2 = x1_ref[...]
    gates = tg_ref[...]
    for r in range(EXPERT_TOPK):
        x2 = x2 + y_refs[r][...] * gates[:, r:r + 1]
    ple = jnp.dot(p_ref[...].astype(BF16), wpp_ref[...], preferred_element_type=F32)
    gate = _sigmoid(jnp.dot(x2.astype(BF16), wpg_ref[...], preferred_element_type=F32))
    x3 = x2 + _rms(gate * ple, gp_ref[...])
    o_ref[...] = _rms(x3, gfin_ref[...])


def _final(x1, y, gates, p2, wpp, wpg, gp, gfin):
    seq, d = x1.shape
    pd = p2.shape[1]
    tm = min(FIN_TM, seq)
    row = lambda i: (i, 0)
    fix = lambda i: (0, 0)
    vmem = (2 * (2 * tm * d * 4 + EXPERT_TOPK * tm * d * 4 + tm * LANES * 4 + tm * pd * 4)
            + pd * d * 2 + d * d * 2 + 8 * tm * d * 4)
    return pl.pallas_call(
        _final_kernel,
        out_shape=jax.ShapeDtypeStruct((seq, d), F32),
        grid=(seq // tm,),
        in_specs=[pl.BlockSpec((tm, d), row),
                  *[pl.BlockSpec((tm, d), functools.partial(lambda i, r: (r * (seq // tm) + i, 0), r=r))
                    for r in range(EXPERT_TOPK)],
                  pl.BlockSpec((tm, LANES), row), pl.BlockSpec((tm, pd), row),
                  _resident((pd, d), fix), _resident((d, d), fix),
                  _resident((1, d), fix), _resident((1, d), fix)],
        out_specs=pl.BlockSpec((tm, d), row),
        compiler_params=_cparams(("parallel",), vmem),
        name="combine_ple_norm",
    )(x1, *([y] * EXPERT_TOPK), gates, p2, wpp, wpg, gp, gfin)


def _layer(x2, p2, g_mix, w_in, g_idx_k, b_idx_k, w_br_a, w_br_b, w_out, g_ffn, w_router, b_router,
           w_gate_up, b_gate_up, w_down, b_down, w_ple_proj, w_ple_gate, g_ple, g_final):
    seq, d = x2.shape
    dsa_w, moba_w = DSA_HEADS * HEAD_DIM, MOBA_HEADS * HEAD_DIM
    sizes = (dsa_w, dsa_w, dsa_w, IDX_HEADS * IDX_DIM, IDX_DIM, IDX_HEADS, moba_w, moba_w, moba_w, d, d)
    off = np.concatenate([[0], np.cumsum(sizes)])
    col = lambda k: w_in[:, off[k]:off[k + 1]]
    row1 = lambda v: v.reshape(1, -1)
    g_mix2 = row1(g_mix)

    w_rope = jnp.concatenate([col(0), col(1), col(6), col(7)], axis=1).astype(BF16)
    w_qi = col(3).astype(BF16)
    w_kiwi = jnp.zeros((d, 2 * LANES), F32).at[:, :IDX_DIM].set(col(4)).at[:, LANES:LANES + IDX_HEADS].set(col(5))
    w_val_t = jnp.concatenate([col(2), col(8)], axis=1).T.astype(BF16)
    w_gate = jnp.concatenate([col(9), col(10)], axis=1).astype(BF16)

    rope_tabs = _rope_tables(seq, ROPE_DIM, HEAD_DIM)
    idx_tabs = _rope_tables(seq, IDX_ROPE_DIM, IDX_DIM)
    qk = _project(x2, g_mix2, w_rope, BF16, rope_tabs, ROPE_DIM // 2)
    qi = _project(x2, g_mix2, w_qi, BF16, idx_tabs, IDX_ROPE_DIM // 2)
    pad = lambda v: jnp.zeros((1, LANES), F32).at[0, :IDX_DIM].set(v)
    ki, wi = _project_kiwi(x2, g_mix2, w_kiwi.astype(BF16), pad(g_idx_k), pad(b_idx_k), idx_tabs,
                           (IDX_HEADS ** -0.5) * (IDX_DIM ** -0.5))
    vals_t = _project(x2, g_mix2, w_val_t, BF16, transposed=True)
    gates_ab = _project(x2, g_mix2, w_gate, F32)

    bias4 = _dsa_mask(qi, ki, wi.T, min(DSA_TOPK, seq // 4))
    o_a = _dsa_attention(qk, vals_t, bias4, 0, 1, 0)
    o_b = _moba_attention(qk, vals_t, _block_means(qk, 3), 2, 3, 1)

    w_r = jnp.zeros((d, LANES), F32).at[:, :N_EXPERTS].set(w_router).astype(BF16)
    b_r = jnp.full((1, LANES), NEG_INF, F32).at[0, :N_EXPERTS].set(b_router)
    x1, h2, top_idx, top_gate = _post_attention(
        o_a, o_b, gates_ab, x2, w_br_a.astype(BF16), w_br_b.astype(BF16), w_out.astype(BF16),
        row1(g_ffn), w_r, b_r)

    bias_row = lambda b: b.reshape(N_EXPERTS, 1, -1)
    y = _moe_experts(h2, _route(top_idx[:, :EXPERT_TOPK]), w_gate_up,
                     bias_row(b_gate_up[:, 0::2]), bias_row(b_gate_up[:, 1::2]), w_down, bias_row(b_down))

    return _final(x1, y, top_gate, p2, w_ple_proj.astype(BF16), w_ple_gate.astype(BF16), row1(g_ple),
                  row1(g_final))


def kernel(x, p, g_mix, w_in, g_idx_k, b_idx_k, w_br_a, w_br_b, w_out, g_ffn, w_router, b_router,
           w_gate_up, b_gate_up, w_down, b_down, w_ple_proj, w_ple_gate, g_ple, g_final):
    batch, seq, d = x.shape
    depth = w_in.shape[0]
    assert batch == 1 and depth == 1, "kernel handles the single-sequence, single-layer block"
    assert seq % ATT_T == 0 and seq % PROJ_TM == 0
    out = _layer(x[0], p[0, 0], g_mix[0], w_in[0], g_idx_k[0], b_idx_k[0], w_br_a[0], w_br_b[0], w_out[0],
                 g_ffn[0], w_router[0], b_router[0], w_gate_up[0], b_gate_up[0], w_down[0], b_down[0],
                 w_ple_proj[0], w_ple_gate[0], g_ple[0], g_final)
    return out[None]
```
